```python
import math
import jax, jax.numpy as jnp
from jax import lax
import numpy as np

D_MODEL = 1024
BATCH = 1
SEQ = 16384
DEPTH = 1

PLE_DIM = 256
D_FF = 2816
CONV_CH = 512
CONV_WIDTH = 31
N_HEADS = 8
HEAD_DIM = 64
ATTN_W = N_HEADS * HEAD_DIM
IDX_HEADS = 4
IDX_DIM = 64
TOPK_MAX = 256
Q_BLOCK = 128
NUM_BUCKETS = 32
MAX_DISTANCE = 128
N_BRANCH = 2
EPS = 1e-6
MIX_COLS = (CONV_CH, CONV_CH, ATTN_W, ATTN_W, ATTN_W, IDX_HEADS * IDX_DIM, IDX_DIM, IDX_HEADS, D_MODEL, D_MODEL)
MIX_IN = 2 * CONV_CH + 3 * ATTN_W + IDX_HEADS * IDX_DIM + IDX_DIM + IDX_HEADS + N_BRANCH * D_MODEL

kernel_name = "hybrid_conformer_conv_dsa_macaron_layer"


def rms_norm(x, g):
    xf = x.astype(jnp.float32)
    y = xf * lax.rsqrt(jnp.mean(xf * xf, axis=-1, keepdims=True) + EPS)
    return (y * g.astype(jnp.float32)).astype(x.dtype)


def layer_norm(x, g, b):
    xf = x.astype(jnp.float32)
    mu = jnp.mean(xf, axis=-1, keepdims=True)
    xc = xf - mu
    y = xc * lax.rsqrt(jnp.mean(xc * xc, axis=-1, keepdims=True) + EPS)
    return (y * g.astype(jnp.float32) + b.astype(jnp.float32)).astype(x.dtype)


def swiglu(x, w_in, w_out):
    a, b = jnp.split(x @ w_in, 2, axis=-1)
    return (jax.nn.silu(a) * b) @ w_out


def split_cols(z, sizes):
    outs, start = [], 0
    for n in sizes:
        outs.append(z[..., start:start + n])
        start += n
    return outs


def conv_module(a, gate, dw_w, dw_b, ln_g, ln_b, w_out):
    z = a * jax.nn.sigmoid(gate)
    z = lax.conv_general_dilated(
        z, dw_w[:, None, :].astype(z.dtype), window_strides=(1,),
        padding=[(CONV_WIDTH - 1, 0)], dimension_numbers=('NWC', 'WIO', 'NWC'),
        feature_group_count=CONV_CH) + dw_b
    z = jax.nn.silu(layer_norm(z, ln_g, ln_b))
    return z @ w_out


def t5_bucket(dist):
    n = jnp.maximum(dist, 0)
    max_exact = NUM_BUCKETS // 2
    is_small = n < max_exact
    nf = jnp.maximum(n, 1).astype(jnp.float32)
    large = max_exact + (jnp.log(nf / max_exact) / math.log(MAX_DISTANCE / max_exact)
                         * (NUM_BUCKETS - max_exact)).astype(jnp.int32)
    large = jnp.minimum(large, NUM_BUCKETS - 1)
    return jnp.where(is_small, n, large)


def dsa_attention(q, k, v, qi, ki, wi, rel_bias, k_top):
    B, S = q.shape[0], q.shape[1]
    nb = S // Q_BLOCK
    key_pos = jnp.arange(S, dtype=jnp.int32)
    scale = HEAD_DIM ** -0.5

    def blockify(a):
        return jnp.moveaxis(a.reshape((B, nb, Q_BLOCK) + a.shape[2:]), 1, 0)

    def one_block(args):
        qb, qib, wib, start = args
        q_pos = start + jnp.arange(Q_BLOCK, dtype=jnp.int32)
        s_idx = jax.nn.relu(jnp.einsum('bqhd,bsd->bqhs', qib, ki))
        idx_score = jnp.einsum('bqh,bqhs->bqs', wib, s_idx).astype(jnp.float32)
        causal = key_pos[None, :] <= q_pos[:, None]
        idx_score = jnp.where(causal[None], idx_score, -jnp.inf)
        _, sel = lax.top_k(idx_score, k_top)
        valid = sel <= q_pos[None, :, None]
        k_sel = jax.vmap(lambda kk, ii: kk[ii])(k, sel)
        v_sel = jax.vmap(lambda vv, ii: vv[ii])(v, sel)
        logits = jnp.einsum('bqhd,bqkhd->bqhk', qb, k_sel).astype(jnp.float32) * scale
        bucket = t5_bucket(q_pos[None, :, None] - sel)
        bias = jnp.transpose(rel_bias[bucket].astype(jnp.float32), (0, 1, 3, 2))
        logits = jnp.where(valid[:, :, None, :], logits + bias, -jnp.inf)
        probs = jax.nn.softmax(logits, axis=-1)
        return jnp.einsum('bqhk,bqkhd->bqhd', probs.astype(v.dtype), v_sel)

    starts = jnp.arange(nb, dtype=jnp.int32) * Q_BLOCK
    out = lax.map(one_block, (blockify(q), blockify(qi), blockify(wi), starts))
    return jnp.moveaxis(out, 0, 1).reshape(B, S, N_HEADS, HEAD_DIM)


def setup_inputs(seed: int = 0) -> dict:
    key = jax.random.key(seed)
    ks = jax.random.split(key, 24)

    def nrm(k, shape, scale):
        return jax.random.normal(k, shape, jnp.float32) * scale

    def gain(k, shape):
        return 1.0 + 0.02 * jax.random.normal(k, shape, jnp.float32)

    L = DEPTH
    return {
        "x": nrm(ks[0], (BATCH, SEQ, D_MODEL), 1.0),
        "p": nrm(ks[1], (DEPTH, BATCH, SEQ, PLE_DIM), 1.0),
        "ffn1_norm": gain(ks[2], (L, D_MODEL)),
        "ffn1_w_in": nrm(ks[3], (L, D_MODEL, 2 * D_FF), D_MODEL ** -0.5),
        "ffn1_w_out": nrm(ks[4], (L, D_FF, D_MODEL), D_FF ** -0.5),
        "mix_norm": gain(ks[5], (L, D_MODEL)),
        "mix_w_in": nrm(ks[6], (L, D_MODEL, MIX_IN), D_MODEL ** -0.5),
        "conv_dw_w": nrm(ks[7], (L, CONV_WIDTH, CONV_CH), CONV_WIDTH ** -0.5),
        "conv_dw_b": nrm(ks[8], (L, CONV_CH), 0.02),
        "conv_ln_g": gain(ks[9], (L, CONV_CH)),
        "conv_ln_b": nrm(ks[10], (L, CONV_CH), 0.02),
        "conv_w_out": nrm(ks[11], (L, CONV_CH, D_MODEL), CONV_CH ** -0.5),
        "q_norm": gain(ks[12], (L, HEAD_DIM)),
        "k_norm": gain(ks[13], (L, HEAD_DIM)),
        "attn_w_out": nrm(ks[14], (L, ATTN_W, D_MODEL), ATTN_W ** -0.5),
        "mix_w_out": nrm(ks[15], (L, D_MODEL, D_MODEL), D_MODEL ** -0.5),
        "ffn2_norm": gain(ks[16], (L, D_MODEL)),
        "ffn2_w_in": nrm(ks[17], (L, D_MODEL, 2 * D_FF), D_MODEL ** -0.5),
        "ffn2_w_out": nrm(ks[18], (L, D_FF, D_MODEL), D_FF ** -0.5),
        "ple_norm": gain(ks[19], (L, D_MODEL)),
        "ple_w_gate": nrm(ks[20], (L, D_MODEL, D_MODEL), D_MODEL ** -0.5),
        "ple_w_proj": nrm(ks[21], (L, PLE_DIM, D_MODEL), PLE_DIM ** -0.5),
        "rel_bias": nrm(ks[22], (NUM_BUCKETS, N_HEADS), 0.5),
    }


def reference(x, p, ffn1_norm, ffn1_w_in, ffn1_w_out, mix_norm, mix_w_in, conv_dw_w, conv_dw_b,
              conv_ln_g, conv_ln_b, conv_w_out, q_norm, k_norm, attn_w_out, mix_w_out,
              ffn2_norm, ffn2_w_in, ffn2_w_out, ple_norm, ple_w_gate, ple_w_proj, rel_bias):
    B, S, _ = x.shape
    k_top = min(TOPK_MAX, S // 4)
    idx_w_scale = (IDX_HEADS ** -0.5) * (IDX_DIM ** -0.5)
    h = x
    for i in range(DEPTH):
        h = h + 0.5 * swiglu(rms_norm(h, ffn1_norm[i]), ffn1_w_in[i], ffn1_w_out[i])
        u = rms_norm(h, mix_norm[i])
        z = u @ mix_w_in[i]
        c_a, c_g, q, k, v, qi, ki, wi, g_conv, g_attn = split_cols(z, MIX_COLS)
        conv_out = conv_module(c_a, c_g, conv_dw_w[i], conv_dw_b[i], conv_ln_g[i], conv_ln_b[i], conv_w_out[i])
        q = rms_norm(q.reshape(B, S, N_HEADS, HEAD_DIM), q_norm[i])
        k = rms_norm(k.reshape(B, S, N_HEADS, HEAD_DIM), k_norm[i])
        v = v.reshape(B, S, N_HEADS, HEAD_DIM)
        qi = qi.reshape(B, S, IDX_HEADS, IDX_DIM)
        attn = dsa_attention(q, k, v, qi, ki, wi * idx_w_scale, rel_bias, k_top)
        attn_out = attn.reshape(B, S, ATTN_W) @ attn_w_out[i]
        merged = jax.nn.sigmoid(g_conv) * conv_out + jax.nn.sigmoid(g_attn) * attn_out
        h = h + merged @ mix_w_out[i]
        h = h + 0.5 * swiglu(rms_norm(h, ffn2_norm[i]), ffn2_w_in[i], ffn2_w_out[i])
        gate = jax.nn.sigmoid(rms_norm(h, ple_norm[i]) @ ple_w_gate[i])
        h = h + gate * (p[i] @ ple_w_proj[i])
    return h
```

```python
import functools
import math

import numpy as np
import jax
import jax.numpy as jnp
from jax import lax
from jax.experimental import pallas as pl
from jax.experimental.pallas import tpu as pltpu

F32 = jnp.float32
BF16 = jnp.bfloat16
I32 = jnp.int32

EPS = 1e-6
CONV_WIDTH = 31
N_HEADS = 8
HEAD_DIM = 64
IDX_HEADS = 4
IDX_DIM = 64
TOPK_MAX = 256
NUM_BUCKETS = 32
MAX_DISTANCE = 128

LANES = 128
VMEM_LIMIT = 56 * 1024 * 1024
NEG = -1e30
INT_MIN = -2 ** 31

CONV_HALO = 32
SEL_TQ = 128
SEL_TK = 512
ATT_TQ = 256
ATT_TK = 1024


def _sigmoid(x):
    return 1.0 / (1.0 + jnp.exp(-x))


def _rms(x, g):
    ms = jnp.mean(x * x, axis=-1, keepdims=True)
    return x * lax.rsqrt(ms + EPS) * g


def _dot(a, b):
    return jnp.dot(a, b, preferred_element_type=F32)


def _dot_t(a, b):
    return lax.dot_general(a, b, (((1,), (1,)), ((), ())), preferred_element_type=F32)


def _split_bf16(x):
    hi = x.astype(BF16)
    lo = (x - hi.astype(F32)).astype(BF16)
    return hi, lo


def _ffn_kernel(x_ref, g_ref, wa_ref, wb_ref, wo_ref, o_ref, xn_ref, acc_ref):
    j = pl.program_id(1)

    @pl.when(j == 0)
    def _():
        xn_ref[...] = _rms(x_ref[...], g_ref[...]).astype(BF16)
        acc_ref[...] = jnp.zeros_like(acc_ref)

    xn = xn_ref[...]
    a = _dot(xn, wa_ref[...])
    b = _dot(xn, wb_ref[...])
    hmid = (a * _sigmoid(a) * b).astype(BF16)
    acc_ref[...] += _dot(hmid, wo_ref[...])

    @pl.when(j == pl.num_programs(1) - 1)
    def _():
        o_ref[...] = x_ref[...] + 0.5 * acc_ref[...]


def _ffn(x, g, w_in, w_out, tm=512, fc=1408):
    s, d = x.shape
    dff = w_out.shape[0]
    nj = dff // fc
    assert s % tm == 0 and dff % fc == 0 and fc % LANES == 0
    return pl.pallas_call(
        _ffn_kernel,
        grid=(s // tm, nj),
        in_specs=[
            pl.BlockSpec((tm, d), lambda i, j: (i, 0)),
            pl.BlockSpec((1, d), lambda i, j: (0, 0)),
            pl.BlockSpec((d, fc), lambda i, j: (0, j)),
            pl.BlockSpec((d, fc), lambda i, j: (0, j + nj)),
            pl.BlockSpec((fc, d), lambda i, j: (j, 0)),
        ],
        out_specs=pl.BlockSpec((tm, d), lambda i, j: (i, 0)),
        out_shape=jax.ShapeDtypeStruct((s, d), F32),
        scratch_shapes=[pltpu.VMEM((tm, d), BF16), pltpu.VMEM((tm, d), F32)],
        compiler_params=pltpu.CompilerParams(
            dimension_semantics=("arbitrary", "arbitrary"), vmem_limit_bytes=VMEM_LIMIT),
        name="ffn",
    )(x, g.reshape(1, d), w_in, w_in, w_out)


def _mix_in_kernel(h_ref, g_ref, wc_ref, wqkv_ref, wih_ref, wil_ref, wg_ref, qg_ref, kg_ref, hb_ref,
                   glu_ref, q_ref, k_ref, v_ref, idx_ref, gc_ref, ga_ref):
    u = _rms(h_ref[...], g_ref[...])
    u_hi, u_lo = _split_bf16(u)
    cw = glu_ref.shape[1]
    aw = q_ref.shape[1]
    d = gc_ref.shape[1]

    c = _dot(u_hi, wc_ref[...])
    glu_ref[...] = c[:, :cw] * _sigmoid(c[:, cw:])

    qkv = _dot(u_hi, wqkv_ref[...])
    hb = hb_ref[...]

    def head_norm(t, g):
        t2_hi, t2_lo = _split_bf16(t * t)
        ms = (_dot(t2_hi, hb) + _dot(t2_lo, hb)) * (1.0 / HEAD_DIM)
        return t * lax.rsqrt(ms + EPS) * g

    q_ref[...] = (head_norm(qkv[:, :aw], qg_ref[...]) * (HEAD_DIM ** -0.5)).astype(BF16)
    k_ref[...] = head_norm(qkv[:, aw:2 * aw], kg_ref[...]).astype(BF16)
    v_ref[...] = qkv[:, 2 * aw:].astype(BF16)

    wih = wih_ref[...]
    idx_ref[...] = _dot(u_hi, wih) + _dot(u_lo, wih) + _dot(u_hi, wil_ref[...])

    gates = _sigmoid(_dot(u_hi, wg_ref[...]))
    gc_ref[...] = gates[:, :d]
    ga_ref[...] = gates[:, d:]


def _mix_in(h, g, w_conv, w_qkv, w_idx_hi, w_idx_lo, w_gate, qg, kg, head_blocks, tm=512):
    s, d = h.shape
    cw = w_conv.shape[1] // 2
    aw = w_qkv.shape[1] // 3
    iw = w_idx_hi.shape[1]
    full = lambda a: pl.BlockSpec(a.shape, lambda i: (0,) * a.ndim)
    row = lambda n: pl.BlockSpec((tm, n), lambda i: (i, 0))
    g2 = g.reshape(1, d)
    return pl.pallas_call(
        _mix_in_kernel,
        grid=(s // tm,),
        in_specs=[row(d), full(g2), full(w_conv), full(w_qkv), full(w_idx_hi), full(w_idx_lo), full(w_gate),
                  full(qg), full(kg), full(head_blocks)],
        out_specs=[row(cw), row(aw), row(aw), row(aw), row(iw), row(d), row(d)],
        out_shape=[
            jax.ShapeDtypeStruct((s, cw), F32),
            jax.ShapeDtypeStruct((s, aw), BF16),
            jax.ShapeDtypeStruct((s, aw), BF16),
            jax.ShapeDtypeStruct((s, aw), BF16),
            jax.ShapeDtypeStruct((s, iw), F32),
            jax.ShapeDtypeStruct((s, d), F32),
            jax.ShapeDtypeStruct((s, d), F32),
        ],
        compiler_params=pltpu.CompilerParams(dimension_semantics=("arbitrary",), vmem_limit_bytes=VMEM_LIMIT),
        name="mix_in",
    )(h, g2, w_conv, w_qkv, w_idx_hi, w_idx_lo, w_gate, qg, kg, head_blocks)


def _conv_kernel(z_ref, halo_ref, dw_ref, db_ref, lg_ref, lb_ref, wo_ref, gc_ref, o_ref, zp_ref):
    tm = z_ref.shape[0]
    first = pl.program_id(0) == 0
    halo = halo_ref[...]
    zp_ref[0:CONV_HALO, :] = jnp.where(first, jnp.zeros_like(halo), halo)
    zp_ref[CONV_HALO:, :] = z_ref[...]
    off = CONV_HALO - (CONV_WIDTH - 1)
    acc = jnp.zeros(z_ref.shape, F32) + db_ref[...]
    for j in range(CONV_WIDTH):
        acc = acc + dw_ref[j:j + 1, :] * zp_ref[off + j:off + j + tm, :]
    mu = jnp.mean(acc, axis=-1, keepdims=True)
    xc = acc - mu
    y = xc * lax.rsqrt(jnp.mean(xc * xc, axis=-1, keepdims=True) + EPS)
    y = y * lg_ref[...] + lb_ref[...]
    y = (y * _sigmoid(y)).astype(BF16)
    o_ref[...] = gc_ref[...] * _dot(y, wo_ref[...])


def _conv(z, dw_w, dw_b, ln_g, ln_b, w_out, gate_c, tm=256):
    s, c = z.shape
    d = w_out.shape[1]
    assert tm % CONV_HALO == 0
    r = tm // CONV_HALO
    full = lambda a: pl.BlockSpec(a.shape, lambda i: (0,) * a.ndim)
    vecs = [dw_b.reshape(1, c), ln_g.reshape(1, c), ln_b.reshape(1, c)]
    return pl.pallas_call(
        _conv_kernel,
        grid=(s // tm,),
        in_specs=[
            pl.BlockSpec((tm, c), lambda i: (i, 0)),
            pl.BlockSpec((CONV_HALO, c), lambda i: (jnp.maximum(i * r - 1, 0), 0)),
            full(dw_w), full(vecs[0]), full(vecs[1]), full(vecs[2]), full(w_out),
            pl.BlockSpec((tm, d), lambda i: (i, 0)),
        ],
        out_specs=pl.BlockSpec((tm, d), lambda i: (i, 0)),
        out_shape=jax.ShapeDtypeStruct((s, d), F32),
        scratch_shapes=[pltpu.VMEM((tm + CONV_HALO, c), F32)],
        compiler_params=pltpu.CompilerParams(dimension_semantics=("arbitrary",), vmem_limit_bytes=VMEM_LIMIT),
        name="conv",
    )(z, z, dw_w, *vecs, w_out, gate_c)


def _sortable_key(x):
    bits = pltpu.bitcast(x, I32)
    return bits ^ ((bits >> 31) & 0x7FFFFFFF)


def _select_kernel(qi_ref, w_ref, ki_ref, mask_ref, keys_ref, lim_ref, *, k_top):
    qb = pl.program_id(0)
    n_tiles = mask_ref.shape[1]
    nkt = (qb * SEL_TQ + SEL_TQ - 1) // SEL_TK + 1
    nch = SEL_TK // LANES
    row_pos = qb * SEL_TQ + lax.broadcasted_iota(I32, (SEL_TQ, LANES), 0)
    lane = lax.broadcasted_iota(I32, (SEL_TQ, LANES), 1)
    w = w_ref[...] * ((IDX_HEADS ** -0.5) * (IDX_DIM ** -0.5))

    def score_tile(kt, carry):
        ki = ki_ref[pl.ds(pl.multiple_of(kt * SEL_TK, SEL_TK), SEL_TK), :]
        acc = jnp.zeros((SEL_TQ, SEL_TK), F32)
        for h in range(IDX_HEADS):
            acc = acc + w[:, h:h + 1] * jnp.maximum(_dot_t(qi_ref[h], ki), 0.0)
        for c in range(nch):
            col = kt * SEL_TK + c * LANES + lane
            sc = jnp.where(col <= row_pos, acc[:, c * LANES:(c + 1) * LANES], -jnp.inf)
            keys_ref[kt, :, c * LANES:(c + 1) * LANES] = _sortable_key(sc)
        return carry

    lax.fori_loop(0, nkt, score_tile, 0)

    def count(pred):
        def body(kt, acc):
            for c in range(nch):
                x = keys_ref[kt, :, c * LANES:(c + 1) * LANES]
                col = kt * SEL_TK + c * LANES + lane
                acc = acc + jnp.where(pred(x, col), 1, 0)
            return acc
        acc = lax.fori_loop(0, nkt, body, jnp.zeros((SEL_TQ, LANES), I32))
        return jnp.sum(acc, axis=1, keepdims=True)

    def bcast(v):
        return jnp.broadcast_to(v, (SEL_TQ, LANES))

    def bit_step(i, st):
        t, cnt_t = st
        cand = t ^ lax.shift_left(jnp.int32(1), 31 - i)
        cb = bcast(cand)
        cnt = count(lambda x, col: x >= cb)
        ok = cnt >= k_top
        return jnp.where(ok, cand, t), jnp.where(ok, cnt, cnt_t)

    t0 = jnp.full((SEL_TQ, 1), INT_MIN, I32)
    cnt0 = jnp.broadcast_to(nkt * SEL_TK, (SEL_TQ, 1)).astype(I32)
    thr, cnt_thr = lax.fori_loop(0, 32, bit_step, (t0, cnt0))
    thr_b = bcast(thr)

    n_cols = n_tiles * SEL_TK
    lim_ref[...] = jnp.full((SEL_TQ, LANES), n_cols, I32)

    @pl.when(jnp.max(cnt_thr) > k_top)
    def _():
        need = k_top - count(lambda x, col: x > thr_b)

        def idx_step(i, p):
            cand = p + lax.shift_left(jnp.int32(1), (n_cols.bit_length() - 2) - i)
            cb = bcast(cand)
            f = count(lambda x, col: (x == thr_b) & (col < cb))
            return jnp.where(f < need, cand, p)

        p = lax.fori_loop(0, n_cols.bit_length() - 1, idx_step, jnp.zeros((SEL_TQ, 1), I32))
        lim_ref[...] = bcast(p)

    lim_b = lim_ref[...]

    def write_tile(kt, carry):
        for c in range(nch):
            x = keys_ref[kt, :, c * LANES:(c + 1) * LANES]
            col = kt * SEL_TK + c * LANES + lane
            sel = ((x > thr_b) | ((x == thr_b) & (col <= lim_b))) & (col <= row_pos)
            mask_ref[0, kt, :, c * LANES:(c + 1) * LANES] = jnp.where(sel, 0.0, NEG).astype(BF16)
        return carry

    lax.fori_loop(0, nkt, write_tile, 0)

    def fill_tile(kt, carry):
        mask_ref[0, kt] = jnp.full((SEL_TQ, SEL_TK), NEG, BF16)
        return carry

    lax.fori_loop(nkt, n_tiles, fill_tile, 0)


def _select(qi_ext, w_idx, ki_ext, k_top):
    nh, s, kw = qi_ext.shape
    n_tiles = s // SEL_TK
    return pl.pallas_call(
        functools.partial(_select_kernel, k_top=k_top),
        grid=(s // SEL_TQ,),
        in_specs=[
            pl.BlockSpec((nh, SEL_TQ, kw), lambda i: (0, i, 0)),
            pl.BlockSpec((SEL_TQ, w_idx.shape[1]), lambda i: (i, 0)),
            pl.BlockSpec((s, kw), lambda i: (0, 0)),
        ],
        out_specs=pl.BlockSpec((1, n_tiles, SEL_TQ, SEL_TK), lambda i: (i, 0, 0, 0)),
        out_shape=jax.ShapeDtypeStruct((s // SEL_TQ, n_tiles, SEL_TQ, SEL_TK), BF16),
        scratch_shapes=[pltpu.VMEM((n_tiles, SEL_TQ, SEL_TK), I32), pltpu.VMEM((SEL_TQ, LANES), I32)],
        compiler_params=pltpu.CompilerParams(dimension_semantics=("arbitrary",), vmem_limit_bytes=VMEM_LIMIT),
        name="select",
    )(qi_ext, w_idx, ki_ext)


def _attn_kernel(qb_tab, kt_tab, q_ref, k_ref, v_ref, mask_ref, bias_ref, o_ref, m_ref, acc_ref):
    step = pl.program_id(0)
    qb = qb_tab[step]
    kt = kt_tab[step]
    nq = ATT_TQ // SEL_TQ
    nsub = ATT_TK // LANES
    per_tile = SEL_TK // LANES

    @pl.when(kt == 0)
    def _():
        m_ref[...] = jnp.full(m_ref.shape, NEG, F32)
        acc_ref[...] = jnp.zeros(acc_ref.shape, F32)

    for a in range(nq):
        rows = slice(a * SEL_TQ, (a + 1) * SEL_TQ)
        q0 = qb * ATT_TQ + a * SEL_TQ
        maskf = jnp.concatenate(
            [mask_ref[a, j].astype(F32) for j in range(ATT_TK // SEL_TK)], axis=1)
        tab_idx = [jnp.clip((q0 - (kt * ATT_TK + c * LANES)) // LANES + 1, 0, 3) for c in range(nsub)]
        for h in range(N_HEADS):
            s = _dot_t(q_ref[h, rows, :], k_ref[h]) + maskf
            s = s + jnp.concatenate([bias_ref[tab_idx[c], h] for c in range(nsub)], axis=1)
            m_old = m_ref[h, rows, :]
            m_new = jnp.maximum(m_old, jnp.max(s, axis=1, keepdims=True))
            p = jnp.exp(s - m_new[:, 0:1])
            alpha = jnp.exp(m_old - m_new)
            acc_ref[h, rows, :] = alpha * acc_ref[h, rows, :] + _dot(p.astype(BF16), v_ref[h])
            m_ref[h, rows, :] = m_new

    last = (qb * ATT_TQ + ATT_TQ - 1) // ATT_TK
    @pl.when(kt == last)
    def _():
        for h in range(N_HEADS):
            acc = acc_ref[h]
            o_ref[h] = acc[:, :HEAD_DIM] / acc[:, HEAD_DIM:HEAD_DIM + 1]


def _attn(q, k, v_ext, mask, bias_tab):
    nh, s, hd = q.shape
    pairs = [(qb, kt) for qb in range(s // ATT_TQ) for kt in range((qb * ATT_TQ + ATT_TQ - 1) // ATT_TK + 1)]
    qb_tab = jnp.asarray([p[0] for p in pairs], I32)
    kt_tab = jnp.asarray([p[1] for p in pairs], I32)
    grid_spec = pltpu.PrefetchScalarGridSpec(
        num_scalar_prefetch=2,
        grid=(len(pairs),),
        in_specs=[
            pl.BlockSpec((nh, ATT_TQ, hd), lambda i, qb, kt: (0, qb[i], 0)),
            pl.BlockSpec((nh, ATT_TK, hd), lambda i, qb, kt: (0, kt[i], 0)),
            pl.BlockSpec((nh, ATT_TK, LANES), lambda i, qb, kt: (0, kt[i], 0)),
            pl.BlockSpec((ATT_TQ // SEL_TQ, ATT_TK // SEL_TK, SEL_TQ, SEL_TK),
                         lambda i, qb, kt: (qb[i], kt[i], 0, 0)),
            pl.BlockSpec(bias_tab.shape, lambda i, qb, kt: (0, 0, 0, 0)),
        ],
        out_specs=pl.BlockSpec((nh, ATT_TQ, hd), lambda i, qb, kt: (0, qb[i], 0)),
        scratch_shapes=[pltpu.VMEM((nh, ATT_TQ, LANES), F32), pltpu.VMEM((nh, ATT_TQ, LANES), F32)],
    )
    return pl.pallas_call(
        _attn_kernel,
        grid_spec=grid_spec,
        out_shape=jax.ShapeDtypeStruct((nh, s, hd), F32),
        compiler_params=pltpu.CompilerParams(dimension_semantics=("arbitrary",), vmem_limit_bytes=VMEM_LIMIT),
        name="attn",
    )(qb_tab, kt_tab, q, k, v_ext, mask, bias_tab)


def _merge_kernel(h_ref, cg_ref, ga_ref, at_ref, wao_ref, wmo_ref, o_ref):
    merged = cg_ref[...] + ga_ref[...] * _dot(at_ref[...], wao_ref[...])
    o_ref[...] = h_ref[...] + _dot(merged.astype(BF16), wmo_ref[...])


def _merge(h, conv_g, gate_a, attn, w_ao, w_mo, tm=512):
    s, d = h.shape
    full = lambda a: pl.BlockSpec(a.shape, lambda i: (0,) * a.ndim)
    row = lambda n: pl.BlockSpec((tm, n), lambda i: (i, 0))
    return pl.pallas_call(
        _merge_kernel,
        grid=(s // tm,),
        in_specs=[row(d), row(d), row(d), row(attn.shape[1]), full(w_ao), full(w_mo)],
        out_specs=row(d),
        out_shape=jax.ShapeDtypeStruct((s, d), F32),
        compiler_params=pltpu.CompilerParams(dimension_semantics=("arbitrary",), vmem_limit_bytes=VMEM_LIMIT),
        name="merge",
    )(h, conv_g, gate_a, attn, w_ao, w_mo)


def _ple_kernel(h_ref, g_ref, p_ref, wg_ref, wp_ref, o_ref):
    h = h_ref[...]
    gate = _sigmoid(_dot(_rms(h, g_ref[...]).astype(BF16), wg_ref[...]))
    o_ref[...] = h + gate * _dot(p_ref[...].astype(BF16), wp_ref[...])


def _ple(h, g, p, w_gate, w_proj, tm=512):
    s, d = h.shape
    full = lambda a: pl.BlockSpec(a.shape, lambda i: (0,) * a.ndim)
    row = lambda n: pl.BlockSpec((tm, n), lambda i: (i, 0))
    g2 = g.reshape(1, d)
    return pl.pallas_call(
        _ple_kernel,
        grid=(s // tm,),
        in_specs=[row(d), full(g2), row(p.shape[1]), full(w_gate), full(w_proj)],
        out_specs=row(d),
        out_shape=jax.ShapeDtypeStruct((s, d), F32),
        compiler_params=pltpu.CompilerParams(dimension_semantics=("arbitrary",), vmem_limit_bytes=VMEM_LIMIT),
        name="ple",
    )(h, g2, p, w_gate, w_proj)


def _t5_bucket_table(n_dist):
    n = np.arange(n_dist)
    max_exact = NUM_BUCKETS // 2
    nf = np.maximum(n, 1).astype(np.float32)
    large = max_exact + (np.log(nf / max_exact) / math.log(MAX_DISTANCE / max_exact)
                         * (NUM_BUCKETS - max_exact)).astype(np.int32)
    large = np.minimum(large, NUM_BUCKETS - 1)
    return np.where(n < max_exact, n, large)


def _bias_tables(rel_bias):
    assert MAX_DISTANCE <= LANES
    bucket = _t5_bucket_table(2 * LANES)
    i = np.arange(LANES)[:, None]
    j = np.arange(LANES)[None, :]
    rel = rel_bias.astype(F32) - rel_bias[NUM_BUCKETS - 1].astype(F32)[None, :]
    d0 = rel[bucket[np.maximum(i - j, 0)]]
    d1 = rel[bucket[LANES + i - j]]
    zero = jnp.zeros_like(d0)
    return jnp.transpose(jnp.stack([zero, d0, d1, zero]), (0, 3, 1, 2))


def kernel(x, p, ffn1_norm, ffn1_w_in, ffn1_w_out, mix_norm, mix_w_in, conv_dw_w, conv_dw_b, conv_ln_g,
           conv_ln_b, conv_w_out, q_norm, k_norm, attn_w_out, mix_w_out, ffn2_norm, ffn2_w_in, ffn2_w_out,
           ple_norm, ple_w_gate, ple_w_proj, rel_bias):
    b, s, d = x.shape
    depth = ffn1_norm.shape[0]
    cw = conv_dw_w.shape[2]
    aw = N_HEADS * HEAD_DIM
    iw = IDX_HEADS * IDX_DIM + IDX_DIM + IDX_HEADS
    iw_pad = -(-iw // LANES) * LANES
    k_top = min(TOPK_MAX, s // 4)
    assert b == 1 and s % ATT_TK == 0 and mix_w_in.shape[2] == 2 * cw + 3 * aw + iw + 2 * d

    head_blocks = jnp.asarray(np.kron(np.eye(N_HEADS), np.ones((HEAD_DIM, HEAD_DIM))), BF16)
    bias_tab = _bias_tables(rel_bias)

    h = x[0]
    for i in range(depth):
        h = _ffn(h, ffn1_norm[i], ffn1_w_in[i].astype(BF16), ffn1_w_out[i].astype(BF16))

        w = mix_w_in[i]
        o0 = 2 * cw
        o1 = o0 + 3 * aw
        o2 = o1 + iw
        w_idx = jnp.pad(w[:, o1:o2], ((0, 0), (0, iw_pad - iw)))
        w_idx_hi, w_idx_lo = _split_bf16(w_idx)
        qg = jnp.tile(q_norm[i], N_HEADS).reshape(1, aw)
        kg = jnp.tile(k_norm[i], N_HEADS).reshape(1, aw)
        glu, q, k, v, idx, gate_c, gate_a = _mix_in(
            h, mix_norm[i], w[:, :o0].astype(BF16), w[:, o0:o1].astype(BF16), w_idx_hi, w_idx_lo,
            w[:, o2:].astype(BF16), qg, kg, head_blocks)

        conv_g = _conv(glu, conv_dw_w[i], conv_dw_b[i], conv_ln_g[i], conv_ln_b[i],
                       conv_w_out[i].astype(BF16), gate_c)

        qi = idx[:, :IDX_HEADS * IDX_DIM].reshape(s, IDX_HEADS, IDX_DIM).transpose(1, 0, 2)
        ki = idx[:, IDX_HEADS * IDX_DIM:IDX_HEADS * IDX_DIM + IDX_DIM]
        wi = idx[:, IDX_HEADS * IDX_DIM + IDX_DIM:iw]
        qi_hi, qi_lo = _split_bf16(qi)
        ki_hi, ki_lo = _split_bf16(ki)
        qi_ext = jnp.concatenate([qi_hi, qi_hi, qi_lo, qi_lo], axis=-1)
        ki_ext = jnp.concatenate([ki_hi, ki_lo, ki_hi, ki_lo], axis=-1)
        mask = _select(qi_ext, wi, ki_ext, k_top)

        heads = lambda t: t.reshape(s, N_HEADS, HEAD_DIM).transpose(1, 0, 2)
        ones = jnp.ones((N_HEADS, s, 1), BF16)
        zeros = jnp.zeros((N_HEADS, s, LANES - HEAD_DIM - 1), BF16)
        v_ext = jnp.concatenate([heads(v), ones, zeros], axis=-1)
        attn = _attn(heads(q), heads(k), v_ext, mask, bias_tab)
        attn = attn.transpose(1, 0, 2).reshape(s, aw).astype(BF16)

        h = _merge(h, conv_g, gate_a, attn, attn_w_out[i].astype(BF16), mix_w_out[i].astype(BF16))
        h = _ffn(h, ffn2_norm[i], ffn2_w_in[i].astype(BF16), ffn2_w_out[i].astype(BF16))
        h = _ple(h, ple_norm[i], p[i, 0], ple_w_gate[i].astype(BF16), ple_w_proj[i].astype(BF16))
    return h[None]
```

```python
import functools
import math

import numpy as np
import jax
import jax.numpy as jnp
from jax import lax
from jax.experimental import pallas as pl
from jax.experimental.pallas import tpu as pltpu

F32 = jnp.float32
BF16 = jnp.bfloat16
I32 = jnp.int32

EPS = 1e-6
CONV_WIDTH = 31
N_HEADS = 8
HEAD_DIM = 64
IDX_HEADS = 4
IDX_DIM = 64
TOPK_MAX = 256
NUM_BUCKETS = 32
MAX_DISTANCE = 128

LANES = 128
SUBLANES = 8
VMEM_LIMIT = 56 * 1024 * 1024
NEG = -1e30
INT_MIN = -2 ** 31
LOG2E = math.log2(math.e)

CONV_HALO = 32
SEL_TQ = LANES
SEL_TK = 512
SEL_GROUPS = 16
SEL_DEPTH = 12
ATT_TQ = 512
ATT_TK = 1024
SAFE_TQ = 256
L_MIN = 1e-30


def _sigmoid(x):
    return 1.0 / (1.0 + jnp.exp(-x))


def _rms(x, g):
    ms = jnp.mean(x * x, axis=-1, keepdims=True)
    return x * lax.rsqrt(ms + EPS) * g


def _dot(a, b):
    return jnp.dot(a, b, preferred_element_type=F32)


def _dot_t(a, b):
    return lax.dot_general(a, b, (((1,), (1,)), ((), ())), preferred_element_type=F32)


def _split_bf16(x):
    hi = x.astype(BF16)
    lo = (x - hi.astype(F32)).astype(BF16)
    return hi, lo


def _ffn_kernel(x_ref, g_ref, wa_ref, wb_ref, wo_ref, o_ref, xn_ref, acc_ref):
    j = pl.program_id(1)

    @pl.when(j == 0)
    def _():
        xn_ref[...] = _rms(x_ref[...], g_ref[...]).astype(BF16)
        acc_ref[...] = jnp.zeros_like(acc_ref)

    xn = xn_ref[...]
    a = _dot(xn, wa_ref[...])
    b = _dot(xn, wb_ref[...])
    hmid = (a * _sigmoid(a) * b).astype(BF16)
    acc_ref[...] += _dot(hmid, wo_ref[...])

    @pl.when(j == pl.num_programs(1) - 1)
    def _():
        o_ref[...] = x_ref[...] + 0.5 * acc_ref[...]


def _ffn(x, g, w_in, w_out, tm=512, fc=1408):
    s, d = x.shape
    dff = w_out.shape[0]
    nj = dff // fc
    assert s % tm == 0 and dff % fc == 0 and fc % LANES == 0
    return pl.pallas_call(
        _ffn_kernel,
        grid=(s // tm, nj),
        in_specs=[
            pl.BlockSpec((tm, d), lambda i, j: (i, 0)),
            pl.BlockSpec((1, d), lambda i, j: (0, 0)),
            pl.BlockSpec((d, fc), lambda i, j: (0, j)),
            pl.BlockSpec((d, fc), lambda i, j: (0, j + nj)),
            pl.BlockSpec((fc, d), lambda i, j: (j, 0)),
        ],
        out_specs=pl.BlockSpec((tm, d), lambda i, j: (i, 0)),
        out_shape=jax.ShapeDtypeStruct((s, d), F32),
        scratch_shapes=[pltpu.VMEM((tm, d), BF16), pltpu.VMEM((tm, d), F32)],
        compiler_params=pltpu.CompilerParams(
            dimension_semantics=("arbitrary", "arbitrary"), vmem_limit_bytes=VMEM_LIMIT),
        name="ffn",
    )(x, g.reshape(1, d), w_in, w_in, w_out)


def _mix_in_kernel(h_ref, g_ref, wc_ref, wqkv_ref, wih_ref, wil_ref, wg_ref, qg_ref, kg_ref, hb_ref,
                   glu_ref, q_ref, k_ref, v_ref, idx_ref, gc_ref, ga_ref, ksq_ref):
    u = _rms(h_ref[...], g_ref[...])
    u_hi, u_lo = _split_bf16(u)
    cw = glu_ref.shape[1]
    aw = q_ref.shape[1]
    d = gc_ref.shape[1]

    c = _dot(u_hi, wc_ref[...])
    glu_ref[...] = c[:, :cw] * _sigmoid(c[:, cw:])

    qkv = _dot(u_hi, wqkv_ref[...])
    hb = hb_ref[...]

    def head_sumsq(t):
        t2_hi, t2_lo = _split_bf16(t * t)
        return _dot(t2_hi, hb) + _dot(t2_lo, hb)

    def head_norm(t, g):
        return t * lax.rsqrt(head_sumsq(t) * (1.0 / HEAD_DIM) + EPS) * g

    q_ref[...] = (head_norm(qkv[:, :aw], qg_ref[...]) * (HEAD_DIM ** -0.5 * LOG2E)).astype(BF16)
    k = head_norm(qkv[:, aw:2 * aw], kg_ref[...]).astype(BF16)
    k_ref[...] = k
    v_ref[...] = qkv[:, 2 * aw:].astype(BF16)

    ksq = jnp.max(head_sumsq(k.astype(F32)), axis=0, keepdims=True)

    @pl.when(pl.program_id(0) == 0)
    def _():
        ksq_ref[...] = ksq

    @pl.when(pl.program_id(0) > 0)
    def _():
        ksq_ref[...] = jnp.maximum(ksq_ref[...], ksq)

    wih = wih_ref[...]
    idx_ref[...] = _dot(u_hi, wih) + _dot(u_lo, wih) + _dot(u_hi, wil_ref[...])

    gates = _sigmoid(_dot(u_hi, wg_ref[...]))
    gc_ref[...] = gates[:, :d]
    ga_ref[...] = gates[:, d:]


def _mix_in(h, g, w_conv, w_qkv, w_idx_hi, w_idx_lo, w_gate, qg, kg, head_blocks, tm=512):
    s, d = h.shape
    cw = w_conv.shape[1] // 2
    aw = w_qkv.shape[1] // 3
    iw = w_idx_hi.shape[1]
    full = lambda a: pl.BlockSpec(a.shape, lambda i: (0,) * a.ndim)
    row = lambda n: pl.BlockSpec((tm, n), lambda i: (i, 0))
    g2 = g.reshape(1, d)
    return pl.pallas_call(
        _mix_in_kernel,
        grid=(s // tm,),
        in_specs=[row(d), full(g2), full(w_conv), full(w_qkv), full(w_idx_hi), full(w_idx_lo), full(w_gate),
                  full(qg), full(kg), full(head_blocks)],
        out_specs=[row(cw), row(aw), row(aw), row(aw), row(iw), row(d), row(d),
                   pl.BlockSpec((1, aw), lambda i: (0, 0))],
        out_shape=[
            jax.ShapeDtypeStruct((s, cw), F32),
            jax.ShapeDtypeStruct((s, aw), BF16),
            jax.ShapeDtypeStruct((s, aw), BF16),
            jax.ShapeDtypeStruct((s, aw), BF16),
            jax.ShapeDtypeStruct((s, iw), F32),
            jax.ShapeDtypeStruct((s, d), F32),
            jax.ShapeDtypeStruct((s, d), F32),
            jax.ShapeDtypeStruct((1, aw), F32),
        ],
        compiler_params=pltpu.CompilerParams(dimension_semantics=("arbitrary",), vmem_limit_bytes=VMEM_LIMIT),
        name="mix_in",
    )(h, g2, w_conv, w_qkv, w_idx_hi, w_idx_lo, w_gate, qg, kg, head_blocks)


def _conv_kernel(z_ref, halo_ref, dw_ref, db_ref, lg_ref, lb_ref, wo_ref, gc_ref, o_ref, zp_ref):
    tm = z_ref.shape[0]
    first = pl.program_id(0) == 0
    halo = halo_ref[...]
    zp_ref[0:CONV_HALO, :] = jnp.where(first, jnp.zeros_like(halo), halo)
    zp_ref[CONV_HALO:, :] = z_ref[...]
    off = CONV_HALO - (CONV_WIDTH - 1)
    acc = jnp.zeros(z_ref.shape, F32) + db_ref[...]
    for j in range(CONV_WIDTH):
        acc = acc + dw_ref[j:j + 1, :] * zp_ref[off + j:off + j + tm, :]
    mu = jnp.mean(acc, axis=-1, keepdims=True)
    xc = acc - mu
    y = xc * lax.rsqrt(jnp.mean(xc * xc, axis=-1, keepdims=True) + EPS)
    y = y * lg_ref[...] + lb_ref[...]
    y = (y * _sigmoid(y)).astype(BF16)
    o_ref[...] = gc_ref[...] * _dot(y, wo_ref[...])


def _conv(z, dw_w, dw_b, ln_g, ln_b, w_out, gate_c, tm=256):
    s, c = z.shape
    d = w_out.shape[1]
    assert tm % CONV_HALO == 0
    r = tm // CONV_HALO
    full = lambda a: pl.BlockSpec(a.shape, lambda i: (0,) * a.ndim)
    vecs = [dw_b.reshape(1, c), ln_g.reshape(1, c), ln_b.reshape(1, c)]
    return pl.pallas_call(
        _conv_kernel,
        grid=(s // tm,),
        in_specs=[
            pl.BlockSpec((tm, c), lambda i: (i, 0)),
            pl.BlockSpec((CONV_HALO, c), lambda i: (jnp.maximum(i * r - 1, 0), 0)),
            full(dw_w), full(vecs[0]), full(vecs[1]), full(vecs[2]), full(w_out),
            pl.BlockSpec((tm, d), lambda i: (i, 0)),
        ],
        out_specs=pl.BlockSpec((tm, d), lambda i: (i, 0)),
        out_shape=jax.ShapeDtypeStruct((s, d), F32),
        scratch_shapes=[pltpu.VMEM((tm + CONV_HALO, c), F32)],
        compiler_params=pltpu.CompilerParams(dimension_semantics=("arbitrary",), vmem_limit_bytes=VMEM_LIMIT),
        name="conv",
    )(z, z, dw_w, *vecs, w_out, gate_c)


def _sortable_key(x):
    bits = pltpu.bitcast(x, I32)
    return bits ^ ((bits >> 31) & 0x7FFFFFFF)


def _key_to_float(key):
    return pltpu.bitcast(key ^ ((key >> 31) & 0x7FFFFFFF), F32)


def _select_kernel(qi_ref, w_ref, ki_ref, mask_ref, sc_ref, cand_ref, st_ref, *, k_top):
    qb = pl.program_id(0)
    n_rows = sc_ref.shape[0]
    n_tiles = n_rows // SEL_TK
    q0 = qb * SEL_TQ
    nkt = (q0 + SEL_TQ - 1) // SEL_TK + 1
    vt = SEL_TK // SUBLANES
    vshape = (SUBLANES, LANES)
    qpos = q0 + lax.broadcasted_iota(I32, vshape, 1)
    sub = lax.broadcasted_iota(I32, vshape, 0)
    w = w_ref[...] * ((IDX_HEADS ** -0.5) * (IDX_DIM ** -0.5))

    qpos_t = q0 + lax.broadcasted_iota(I32, (SEL_TK, LANES), 1)
    krow_t = lax.broadcasted_iota(I32, (SEL_TK, LANES), 0)

    def score_tile(kt, carry):
        k0 = pl.multiple_of(kt * SEL_TK, SEL_TK)
        ki = ki_ref[pl.ds(k0, SEL_TK), :]
        sc = jnp.zeros((SEL_TK, LANES), F32)
        for pair in range(IDX_HEADS // 2):
            r = _dot(ki, qi_ref[0, pair])
            for j in range(2):
                h = 2 * pair + j
                sc = sc + w[h:h + 1, :] * jnp.maximum(r[:, j * LANES:(j + 1) * LANES], 0.0)
        sc = jnp.where(k0 + krow_t <= qpos_t, sc, -jnp.inf)
        sc_ref[pl.ds(k0, SEL_TK), :] = sc
        return carry

    lax.fori_loop(0, nkt, score_tile, 0)

    def vreg(ref, row):
        return ref[pl.ds(pl.multiple_of(row, SUBLANES), SUBLANES), :]

    def col_sum(parts):
        tot = parts[0]
        for part in parts[1:]:
            tot = tot + part
        return jnp.broadcast_to(jnp.sum(tot, axis=0, keepdims=True), vshape)

    def group_lists(g, carry):
        def tile(kt, lists):
            lists = list(lists)
            for j in range(vt // SEL_GROUPS):
                x = vreg(sc_ref, kt * SEL_TK + (g + SEL_GROUPS * j) * SUBLANES)
                for i in range(SEL_DEPTH):
                    hi = jnp.maximum(lists[i], x)
                    x = jnp.minimum(lists[i], x)
                    lists[i] = hi
            return tuple(lists)

        lists = lax.fori_loop(0, nkt, tile, tuple(jnp.full(vshape, -jnp.inf, F32) for _ in range(SEL_DEPTH)))
        for i in range(SEL_DEPTH):
            cand_ref[pl.ds(pl.multiple_of((g * SEL_DEPTH + i) * SUBLANES, SUBLANES), SUBLANES), :] = (
                _sortable_key(lists[i]))
        return carry

    lax.fori_loop(0, SEL_GROUPS, group_lists, 0)

    n_cand = SEL_GROUPS * SEL_DEPTH
    n_acc = 4

    def cand_bit(i, t):
        c = t ^ lax.shift_left(jnp.int32(1), 31 - i)
        acc = [jnp.zeros(vshape, I32) for _ in range(n_acc)]
        for v in range(n_cand):
            x = cand_ref[v * SUBLANES:(v + 1) * SUBLANES, :]
            acc[v % n_acc] = acc[v % n_acc] + jnp.where(x >= c, 1, 0)
        return jnp.where(col_sum(acc) >= k_top, c, t)

    st_ref[0] = lax.fori_loop(0, 32, cand_bit, jnp.full(vshape, INT_MIN, I32))

    def count_scores(preds):
        def tile(kt, acc):
            acc = [list(a) for a in acc]
            for v in range(vt):
                x = vreg(sc_ref, kt * SEL_TK + v * SUBLANES)
                kpos = kt * SEL_TK + v * SUBLANES + sub
                for p, pred in enumerate(preds):
                    acc[p][v % n_acc] = acc[p][v % n_acc] + jnp.where(pred(x, kpos), 1, 0)
            return tuple(tuple(a) for a in acc)

        zero = tuple(tuple(jnp.zeros(vshape, I32) for _ in range(n_acc)) for _ in preds)
        return [col_sum(list(a)) for a in lax.fori_loop(0, nkt, tile, zero)]

    def count_around_threshold():
        tf = _key_to_float(st_ref[0])
        gt, ge = count_scores([lambda x, kpos: x > tf, lambda x, kpos: x >= tf])
        st_ref[1] = gt
        st_ref[2] = ge

    count_around_threshold()

    @pl.when(jnp.max(st_ref[1]) >= k_top)
    def _():
        def full_bit(i, t):
            c = t ^ lax.shift_left(jnp.int32(1), 31 - i)
            cnt, = count_scores([lambda x, kpos: _sortable_key(x) >= c])
            return jnp.where(cnt >= k_top, c, t)

        st_ref[0] = lax.fori_loop(0, 32, full_bit, jnp.full(vshape, INT_MIN, I32))
        count_around_threshold()

    st_ref[3] = jnp.full(vshape, n_rows, I32)

    @pl.when(jnp.max(st_ref[2]) > k_top)
    def _():
        tf = _key_to_float(st_ref[0])
        need = k_top - st_ref[1]
        n_bits = n_rows.bit_length() - 1

        def pos_bit(i, p):
            c = p + lax.shift_left(jnp.int32(1), n_bits - 1 - i)
            f, = count_scores([lambda x, kpos: (x == tf) & (kpos < c)])
            return jnp.where(f < need, c, p)

        st_ref[3] = lax.fori_loop(0, n_bits, pos_bit, jnp.zeros(vshape, I32))

    tf_t = jnp.broadcast_to(_key_to_float(st_ref[0])[0:1, :], (SEL_TK, LANES))
    lim_t = jnp.broadcast_to(st_ref[3][0:1, :], (SEL_TK, LANES))

    def write_tile(kt, carry):
        k0 = pl.multiple_of(kt * SEL_TK, SEL_TK)
        x = sc_ref[pl.ds(k0, SEL_TK), :]
        kpos = k0 + krow_t
        sel = ((x > tf_t) | ((x == tf_t) & (kpos <= lim_t))) & (kpos <= qpos_t)
        mask_ref[0, pl.ds(k0, SEL_TK), :] = jnp.where(sel, 1.0, 0.0).astype(BF16)
        return carry

    lax.fori_loop(0, nkt, write_tile, 0)

    def fill_tile(kt, carry):
        mask_ref[0, pl.ds(pl.multiple_of(kt * SEL_TK, SEL_TK), SEL_TK), :] = jnp.zeros((SEL_TK, LANES), BF16)
        return carry

    lax.fori_loop(nkt, n_tiles, fill_tile, 0)


def _select(qi_pairs, w_t, ki_ext, k_top):
    nqb, npair, kw, _ = qi_pairs.shape
    s = ki_ext.shape[0]
    assert s % SEL_TK == 0 and SEL_TK % (SUBLANES * SEL_GROUPS) == 0 and SEL_GROUPS * SEL_DEPTH * SUBLANES >= k_top
    return pl.pallas_call(
        functools.partial(_select_kernel, k_top=k_top),
        grid=(nqb,),
        in_specs=[
            pl.BlockSpec((1, npair, kw, 2 * SEL_TQ), lambda i: (i, 0, 0, 0)),
            pl.BlockSpec((w_t.shape[0], SEL_TQ), lambda i: (0, i)),
            pl.BlockSpec((s, kw), lambda i: (0, 0)),
        ],
        out_specs=pl.BlockSpec((1, s, SEL_TQ), lambda i: (i, 0, 0)),
        out_shape=jax.ShapeDtypeStruct((nqb, s, SEL_TQ), BF16),
        scratch_shapes=[
            pltpu.VMEM((s, SEL_TQ), F32),
            pltpu.VMEM((SEL_GROUPS * SEL_DEPTH * SUBLANES, SEL_TQ), I32),
            pltpu.VMEM((4, SUBLANES, LANES), I32),
        ],
        compiler_params=pltpu.CompilerParams(dimension_semantics=("arbitrary",), vmem_limit_bytes=VMEM_LIMIT),
        name="select",
    )(qi_pairs, w_t, ki_ext)


def _causal_pairs(s, tq, tk):
    pairs = [(qb, kt) for qb in range(s // tq) for kt in range((qb * tq + tq - 1) // tk + 1)]
    return jnp.asarray([p[0] for p in pairs], I32), jnp.asarray([p[1] for p in pairs], I32)


def _attn_kernel(qb_tab, kt_tab, qt_ref, kmax_ref, k_ref, vt_ref, mask_ref, bias_ref, o_ref, l_ref, qx_ref, acc_ref):
    step = pl.program_id(0)
    qb = qb_tab[step]
    kt = kt_tab[step]
    q0 = qb * ATT_TQ
    k0 = kt * ATT_TK
    nq = ATT_TQ // LANES
    nk = ATT_TK // LANES

    @pl.when(kt == 0)
    def _():
        row = lax.broadcasted_iota(I32, (HEAD_DIM, ATT_TQ), 0)
        for h in range(N_HEADS):
            q = qt_ref[h]
            qf = q.astype(F32)
            bound = jnp.sqrt(jnp.sum(qf * qf, axis=0, keepdims=True)) * kmax_ref[h] * 1.02
            qx_ref[h, 0:HEAD_DIM, :] = q
            qx_ref[h, HEAD_DIM:, :] = jnp.where(row == 0, -bound, 0.0).astype(BF16)
        acc_ref[...] = jnp.zeros(acc_ref.shape, F32)

    mask = jnp.concatenate([mask_ref[a] for a in range(nq)], axis=1)

    def heads(with_bias):
        for h in range(N_HEADS):
            s = _dot(k_ref[h], qx_ref[h])
            if with_bias:
                rows = []
                for c in range(nk):
                    tab = [jnp.clip((q0 + a * LANES - k0 - c * LANES) // LANES + 1, 0, 3) for a in range(nq)]
                    rows.append(jnp.concatenate([bias_ref[tab[a], h] for a in range(nq)], axis=1))
                s = s + jnp.concatenate(rows, axis=0)
            p = jnp.exp2(s).astype(BF16) * mask
            acc_ref[h] += _dot(vt_ref[h], p)

    near = k0 + ATT_TK + 2 * LANES > q0

    @pl.when(near)
    def _():
        heads(True)

    @pl.when(jnp.logical_not(near))
    def _():
        heads(False)

    @pl.when(kt == (q0 + ATT_TQ - 1) // ATT_TK)
    def _():
        for h in range(N_HEADS):
            acc = acc_ref[h]
            den = acc[HEAD_DIM:HEAD_DIM + 1, :]
            o_ref[h] = acc[:HEAD_DIM, :] / den
            l_ref[h] = den


def _attn(q_t, kmax_b, k_ext, v_t, mask, bias_tab):
    nh, hd, s = q_t.shape
    qb_tab, kt_tab = _causal_pairs(s, ATT_TQ, ATT_TK)
    grid_spec = pltpu.PrefetchScalarGridSpec(
        num_scalar_prefetch=2,
        grid=(qb_tab.shape[0],),
        in_specs=[
            pl.BlockSpec((nh, hd, ATT_TQ), lambda i, qb, kt: (0, 0, qb[i])),
            pl.BlockSpec((nh, 1, ATT_TQ), lambda i, qb, kt: (0, 0, 0)),
            pl.BlockSpec((nh, ATT_TK, LANES), lambda i, qb, kt: (0, kt[i], 0)),
            pl.BlockSpec((nh, LANES, ATT_TK), lambda i, qb, kt: (0, 0, kt[i])),
            pl.BlockSpec((ATT_TQ // LANES, ATT_TK, LANES), lambda i, qb, kt: (qb[i], kt[i], 0)),
            pl.BlockSpec(bias_tab.shape, lambda i, qb, kt: (0, 0, 0, 0)),
        ],
        out_specs=[
            pl.BlockSpec((nh, hd, ATT_TQ), lambda i, qb, kt: (0, 0, qb[i])),
            pl.BlockSpec((nh, 1, ATT_TQ), lambda i, qb, kt: (0, 0, qb[i])),
        ],
        scratch_shapes=[pltpu.VMEM((nh, LANES, ATT_TQ), BF16), pltpu.VMEM((nh, LANES, ATT_TQ), F32)],
    )
    return pl.pallas_call(
        _attn_kernel,
        grid_spec=grid_spec,
        out_shape=[jax.ShapeDtypeStruct((nh, hd, s), F32), jax.ShapeDtypeStruct((nh, 1, s), F32)],
        compiler_params=pltpu.CompilerParams(dimension_semantics=("arbitrary",), vmem_limit_bytes=VMEM_LIMIT),
        name="attn",
    )(qb_tab, kt_tab, q_t, kmax_b, k_ext, v_t, mask, bias_tab)


def _attn_safe_kernel(qb_tab, kt_tab, q_ref, k_ref, v_ref, mask_ref, bias_ref, o_ref, m_ref, acc_ref):
    step = pl.program_id(0)
    qb = qb_tab[step]
    kt = kt_tab[step]
    nq = SAFE_TQ // LANES
    nsub = ATT_TK // LANES

    @pl.when(kt == 0)
    def _():
        m_ref[...] = jnp.full(m_ref.shape, NEG, F32)
        acc_ref[...] = jnp.zeros(acc_ref.shape, F32)

    for a in range(nq):
        rows = slice(a * LANES, (a + 1) * LANES)
        q0 = qb * SAFE_TQ + a * LANES
        maskf = (mask_ref[a].astype(F32) - 1.0) * (-NEG)
        tab_idx = [jnp.clip((q0 - (kt * ATT_TK + c * LANES)) // LANES + 1, 0, 3) for c in range(nsub)]
        for h in range(N_HEADS):
            s = _dot_t(q_ref[h, rows, :], k_ref[h]) + maskf
            s = s + jnp.concatenate([bias_ref[tab_idx[c], h] for c in range(nsub)], axis=1)
            m_old = m_ref[h, rows, :]
            m_new = jnp.maximum(m_old, jnp.max(s, axis=1, keepdims=True))
            p = jnp.exp2(s - m_new[:, 0:1])
            alpha = jnp.exp2(m_old - m_new)
            acc_ref[h, rows, :] = alpha * acc_ref[h, rows, :] + _dot(p.astype(BF16), v_ref[h])
            m_ref[h, rows, :] = m_new

    @pl.when(kt == (qb * SAFE_TQ + SAFE_TQ - 1) // ATT_TK)
    def _():
        for h in range(N_HEADS):
            acc = acc_ref[h]
            o_ref[h] = acc[:, :HEAD_DIM] / acc[:, HEAD_DIM:HEAD_DIM + 1]


def _attn_safe(q, k, v_ext, mask_qk, bias_tab):
    nh, s, hd = q.shape
    qb_tab, kt_tab = _causal_pairs(s, SAFE_TQ, ATT_TK)
    grid_spec = pltpu.PrefetchScalarGridSpec(
        num_scalar_prefetch=2,
        grid=(qb_tab.shape[0],),
        in_specs=[
            pl.BlockSpec((nh, SAFE_TQ, hd), lambda i, qb, kt: (0, qb[i], 0)),
            pl.BlockSpec((nh, ATT_TK, hd), lambda i, qb, kt: (0, kt[i], 0)),
            pl.BlockSpec((nh, ATT_TK, LANES), lambda i, qb, kt: (0, kt[i], 0)),
            pl.BlockSpec((SAFE_TQ // LANES, LANES, ATT_TK), lambda i, qb, kt: (qb[i], 0, kt[i])),
            pl.BlockSpec(bias_tab.shape, lambda i, qb, kt: (0, 0, 0, 0)),
        ],
        out_specs=pl.BlockSpec((nh, SAFE_TQ, hd), lambda i, qb, kt: (0, qb[i], 0)),
        scratch_shapes=[pltpu.VMEM((nh, SAFE_TQ, LANES), F32), pltpu.VMEM((nh, SAFE_TQ, LANES), F32)],
    )
    return pl.pallas_call(
        _attn_safe_kernel,
        grid_spec=grid_spec,
        out_shape=jax.ShapeDtypeStruct((nh, s, hd), F32),
        compiler_params=pltpu.CompilerParams(dimension_semantics=("arbitrary",), vmem_limit_bytes=VMEM_LIMIT),
        name="attn_safe",
    )(qb_tab, kt_tab, q, k, v_ext, mask_qk, bias_tab)


def _merge_kernel(h_ref, cg_ref, ga_ref, at_ref, wao_ref, wmo_ref, o_ref):
    merged = cg_ref[...] + ga_ref[...] * _dot(at_ref[...], wao_ref[...])
    o_ref[...] = h_ref[...] + _dot(merged.astype(BF16), wmo_ref[...])


def _merge(h, conv_g, gate_a, attn, w_ao, w_mo, tm=512):
    s, d = h.shape
    full = lambda a: pl.BlockSpec(a.shape, lambda i: (0,) * a.ndim)
    row = lambda n: pl.BlockSpec((tm, n), lambda i: (i, 0))
    return pl.pallas_call(
        _merge_kernel,
        grid=(s // tm,),
        in_specs=[row(d), row(d), row(d), row(attn.shape[1]), full(w_ao), full(w_mo)],
        out_specs=row(d),
        out_shape=jax.ShapeDtypeStruct((s, d), F32),
        compiler_params=pltpu.CompilerParams(dimension_semantics=("arbitrary",), vmem_limit_bytes=VMEM_LIMIT),
        name="merge",
    )(h, conv_g, gate_a, attn, w_ao, w_mo)


def _ple_kernel(h_ref, g_ref, p_ref, wg_ref, wp_ref, o_ref):
    h = h_ref[...]
    gate = _sigmoid(_dot(_rms(h, g_ref[...]).astype(BF16), wg_ref[...]))
    o_ref[...] = h + gate * _dot(p_ref[...].astype(BF16), wp_ref[...])


def _ple(h, g, p, w_gate, w_proj, tm=512):
    s, d = h.shape
    full = lambda a: pl.BlockSpec(a.shape, lambda i: (0,) * a.ndim)
    row = lambda n: pl.BlockSpec((tm, n), lambda i: (i, 0))
    g2 = g.reshape(1, d)
    return pl.pallas_call(
        _ple_kernel,
        grid=(s // tm,),
        in_specs=[row(d), full(g2), row(p.shape[1]), full(w_gate), full(w_proj)],
        out_specs=row(d),
        out_shape=jax.ShapeDtypeStruct((s, d), F32),
        compiler_params=pltpu.CompilerParams(dimension_semantics=("arbitrary",), vmem_limit_bytes=VMEM_LIMIT),
        name="ple",
    )(h, g2, p, w_gate, w_proj)


def _t5_bucket_table(n_dist):
    n = np.arange(n_dist)
    max_exact = NUM_BUCKETS // 2
    nf = np.maximum(n, 1).astype(np.float32)
    large = max_exact + (np.log(nf / max_exact) / math.log(MAX_DISTANCE / max_exact)
                         * (NUM_BUCKETS - max_exact)).astype(np.int32)
    large = np.minimum(large, NUM_BUCKETS - 1)
    return np.where(n < max_exact, n, large)


def _bias_tables(rel_bias):
    assert MAX_DISTANCE <= LANES
    bucket = _t5_bucket_table(2 * LANES)
    i = np.arange(LANES)[:, None]
    j = np.arange(LANES)[None, :]
    rel = (rel_bias.astype(F32) - rel_bias[NUM_BUCKETS - 1].astype(F32)[None, :]) * LOG2E
    d0 = rel[bucket[np.maximum(i - j, 0)]]
    d1 = rel[bucket[LANES + i - j]]
    zero = jnp.zeros_like(d0)
    return jnp.transpose(jnp.stack([zero, d0, d1, zero]), (0, 3, 1, 2))


def kernel(x, p, ffn1_norm, ffn1_w_in, ffn1_w_out, mix_norm, mix_w_in, conv_dw_w, conv_dw_b, conv_ln_g,
           conv_ln_b, conv_w_out, q_norm, k_norm, attn_w_out, mix_w_out, ffn2_norm, ffn2_w_in, ffn2_w_out,
           ple_norm, ple_w_gate, ple_w_proj, rel_bias):
    b, s, d = x.shape
    depth = ffn1_norm.shape[0]
    cw = conv_dw_w.shape[2]
    aw = N_HEADS * HEAD_DIM
    nqi = IDX_HEADS * IDX_DIM
    iw = nqi + IDX_DIM + IDX_HEADS
    iw_pad = -(-iw // LANES) * LANES
    k_top = min(TOPK_MAX, s // 4)
    assert b == 1 and s % ATT_TK == 0 and mix_w_in.shape[2] == 2 * cw + 3 * aw + iw + 2 * d

    head_blocks = jnp.asarray(np.kron(np.eye(N_HEADS), np.ones((HEAD_DIM, HEAD_DIM))), BF16)
    bias_qk = _bias_tables(rel_bias)
    bias_kq = jnp.swapaxes(bias_qk, 2, 3)
    heads = lambda t: t.reshape(s, N_HEADS, HEAD_DIM).transpose(1, 0, 2)
    heads_t = lambda t: t.reshape(s, N_HEADS, HEAD_DIM).transpose(1, 2, 0)

    h = x[0]
    for i in range(depth):
        h = _ffn(h, ffn1_norm[i], ffn1_w_in[i].astype(BF16), ffn1_w_out[i].astype(BF16))

        w = mix_w_in[i]
        o0 = 2 * cw
        o1 = o0 + 3 * aw
        o2 = o1 + iw
        w_idx = jnp.pad(w[:, o1:o2], ((0, 0), (0, iw_pad - iw)))
        w_idx_hi, w_idx_lo = _split_bf16(w_idx)
        qg = jnp.tile(q_norm[i], N_HEADS).reshape(1, aw)
        kg = jnp.tile(k_norm[i], N_HEADS).reshape(1, aw)
        glu, q, k, v, idx, gate_c, gate_a, ksq = _mix_in(
            h, mix_norm[i], w[:, :o0].astype(BF16), w[:, o0:o1].astype(BF16), w_idx_hi, w_idx_lo,
            w[:, o2:].astype(BF16), qg, kg, head_blocks)

        conv_g = _conv(glu, conv_dw_w[i], conv_dw_b[i], conv_ln_g[i], conv_ln_b[i],
                       conv_w_out[i].astype(BF16), gate_c)

        qi_hi, qi_lo = _split_bf16(idx[:, :nqi].reshape(s // SEL_TQ, SEL_TQ, IDX_HEADS // 2, 2, IDX_DIM))
        qi_ext = jnp.concatenate([qi_hi, qi_hi, qi_lo, qi_lo], axis=-1)
        qi_pairs = qi_ext.transpose(0, 2, 4, 3, 1).reshape(s // SEL_TQ, IDX_HEADS // 2, 4 * IDX_DIM, 2 * SEL_TQ)
        ki_hi, ki_lo = _split_bf16(idx[:, nqi:nqi + IDX_DIM])
        ki_ext = jnp.concatenate([ki_hi, ki_lo, ki_hi, ki_lo], axis=-1)
        w_t = idx[:, nqi + IDX_DIM:iw].T
        mask = _select(qi_pairs, w_t, ki_ext, k_top)

        def pad_ones(t, axis):
            one = lax.slice_in_dim(t, 0, 1, axis=axis)
            rest = lax.slice_in_dim(t, 0, LANES - HEAD_DIM - 1, axis=axis)
            return jnp.concatenate([t, jnp.ones_like(one), jnp.zeros_like(rest)], axis=axis)

        k_ext = pad_ones(heads(k), 2)
        v_t = pad_ones(heads_t(v), 1)
        kmax = jnp.sqrt(ksq.reshape(N_HEADS, HEAD_DIM)[:, :1])
        kmax_b = jnp.broadcast_to(kmax[:, :, None], (N_HEADS, 1, ATT_TQ))
        attn_t, den = _attn(heads_t(q), kmax_b, k_ext, v_t, mask, bias_kq)
        attn = attn_t.transpose(2, 0, 1).reshape(s, aw)

        def safe_attn():
            mask_qk = jnp.swapaxes(mask, 1, 2)
            v_ext = pad_ones(heads(v), 2)
            return _attn_safe(heads(q), heads(k), v_ext, mask_qk, bias_qk).transpose(1, 0, 2).reshape(s, aw)

        underflow = jnp.logical_not(jnp.min(den) > L_MIN)
        attn = lax.cond(underflow, safe_attn, lambda: attn).astype(BF16)

        h = _merge(h, conv_g, gate_a, attn, attn_w_out[i].astype(BF16), mix_w_out[i].astype(BF16))
        h = _ffn(h, ffn2_norm[i], ffn2_w_in[i].astype(BF16), ffn2_w_out[i].astype(BF16))
        h = _ple(h, ple_norm[i], p[i, 0], ple_w_gate[i].astype(BF16), ple_w_proj[i].astype(BF16))
    return h[None]
```

```python
import functools
import math

import numpy as np
import jax
import jax.numpy as jnp
from jax import lax
from jax.experimental import pallas as pl
from jax.experimental.pallas import tpu as pltpu

F32 = jnp.float32
BF16 = jnp.bfloat16
I32 = jnp.int32

EPS = 1e-6
CONV_WIDTH = 31
N_HEADS = 8
HEAD_DIM = 64
IDX_HEADS = 4
IDX_DIM = 64
TOPK_MAX = 256
NUM_BUCKETS = 32
MAX_DISTANCE = 128

LANES = 128
SUBLANES = 8
VMEM_LIMIT = 56 * 1024 * 1024
NEG = -1e30
INT_MIN = -2 ** 31
LOG2E = math.log2(math.e)

CONV_HALO = 32
SEL_TQ = LANES
SEL_TK = 1024
SEL_GROUPS = 8
SEL_DEPTH = 16
ATT_TQ = 512
ATT_TK = 1024
SAFE_TQ = 256
L_MIN = 1e-30


def _sigmoid(x):
    return 1.0 / (1.0 + jnp.exp(-x))


def _rms(x, g):
    ms = jnp.mean(x * x, axis=-1, keepdims=True)
    return x * lax.rsqrt(ms + EPS) * g


def _dot(a, b):
    return jnp.dot(a, b, preferred_element_type=F32)


def _dot_t(a, b):
    return lax.dot_general(a, b, (((1,), (1,)), ((), ())), preferred_element_type=F32)


def _split_bf16(x):
    hi = x.astype(BF16)
    lo = (x - hi.astype(F32)).astype(BF16)
    return hi, lo


def _ffn_kernel(x_ref, g_ref, wa_ref, wb_ref, wo_ref, o_ref, xn_ref, acc_ref):
    j = pl.program_id(1)

    @pl.when(j == 0)
    def _():
        xn_ref[...] = _rms(x_ref[...], g_ref[...]).astype(BF16)
        acc_ref[...] = jnp.zeros_like(acc_ref)

    xn = xn_ref[...]
    a = _dot(xn, wa_ref[...])
    b = _dot(xn, wb_ref[...])
    hmid = (a * _sigmoid(a) * b).astype(BF16)
    acc_ref[...] += _dot(hmid, wo_ref[...])

    @pl.when(j == pl.num_programs(1) - 1)
    def _():
        o_ref[...] = x_ref[...] + 0.5 * acc_ref[...]


def _ffn(x, g, w_in, w_out, tm=512, fc=1408):
    s, d = x.shape
    dff = w_out.shape[0]
    nj = dff // fc
    assert s % tm == 0 and dff % fc == 0 and fc % LANES == 0
    return pl.pallas_call(
        _ffn_kernel,
        grid=(s // tm, nj),
        in_specs=[
            pl.BlockSpec((tm, d), lambda i, j: (i, 0)),
            pl.BlockSpec((1, d), lambda i, j: (0, 0)),
            pl.BlockSpec((d, fc), lambda i, j: (0, j)),
            pl.BlockSpec((d, fc), lambda i, j: (0, j + nj)),
            pl.BlockSpec((fc, d), lambda i, j: (j, 0)),
        ],
        out_specs=pl.BlockSpec((tm, d), lambda i, j: (i, 0)),
        out_shape=jax.ShapeDtypeStruct((s, d), F32),
        scratch_shapes=[pltpu.VMEM((tm, d), BF16), pltpu.VMEM((tm, d), F32)],
        compiler_params=pltpu.CompilerParams(
            dimension_semantics=("arbitrary", "arbitrary"), vmem_limit_bytes=VMEM_LIMIT),
        name="ffn",
    )(x, g.reshape(1, d), w_in, w_in, w_out)


def _mix_in_kernel(h_ref, g_ref, wc_ref, wqkv_ref, wih_ref, wil_ref, wg_ref, qg_ref, kg_ref, hb_ref,
                   glu_ref, q_ref, k_ref, v_ref, idx_ref, gc_ref, ga_ref, ksq_ref):
    u = _rms(h_ref[...], g_ref[...])
    u_hi, u_lo = _split_bf16(u)
    cw = glu_ref.shape[1]
    aw = q_ref.shape[1]
    d = gc_ref.shape[1]

    c = _dot(u_hi, wc_ref[...])
    glu_ref[...] = c[:, :cw] * _sigmoid(c[:, cw:])

    qkv = _dot(u_hi, wqkv_ref[...])
    hb = hb_ref[...]

    def head_sumsq(t):
        t2_hi, t2_lo = _split_bf16(t * t)
        return _dot(t2_hi, hb) + _dot(t2_lo, hb)

    def head_norm(t, g):
        return t * lax.rsqrt(head_sumsq(t) * (1.0 / HEAD_DIM) + EPS) * g

    q_ref[...] = (head_norm(qkv[:, :aw], qg_ref[...]) * (HEAD_DIM ** -0.5 * LOG2E)).astype(BF16)
    k = head_norm(qkv[:, aw:2 * aw], kg_ref[...]).astype(BF16)
    k_ref[...] = k
    v_ref[...] = qkv[:, 2 * aw:].astype(BF16)

    ksq = jnp.max(head_sumsq(k.astype(F32)), axis=0, keepdims=True)

    @pl.when(pl.program_id(0) == 0)
    def _():
        ksq_ref[...] = ksq

    @pl.when(pl.program_id(0) > 0)
    def _():
        ksq_ref[...] = jnp.maximum(ksq_ref[...], ksq)

    wih = wih_ref[...]
    idx_ref[...] = _dot(u_hi, wih) + _dot(u_lo, wih) + _dot(u_hi, wil_ref[...])

    gates = _sigmoid(_dot(u_hi, wg_ref[...]))
    gc_ref[...] = gates[:, :d]
    ga_ref[...] = gates[:, d:]


def _mix_in(h, g, w_conv, w_qkv, w_idx_hi, w_idx_lo, w_gate, qg, kg, head_blocks, tm=512):
    s, d = h.shape
    cw = w_conv.shape[1] // 2
    aw = w_qkv.shape[1] // 3
    iw = w_idx_hi.shape[1]
    full = lambda a: pl.BlockSpec(a.shape, lambda i: (0,) * a.ndim)
    row = lambda n: pl.BlockSpec((tm, n), lambda i: (i, 0))
    g2 = g.reshape(1, d)
    return pl.pallas_call(
        _mix_in_kernel,
        grid=(s // tm,),
        in_specs=[row(d), full(g2), full(w_conv), full(w_qkv), full(w_idx_hi), full(w_idx_lo), full(w_gate),
                  full(qg), full(kg), full(head_blocks)],
        out_specs=[row(cw), row(aw), row(aw), row(aw), row(iw), row(d), row(d),
                   pl.BlockSpec((1, aw), lambda i: (0, 0))],
        out_shape=[
            jax.ShapeDtypeStruct((s, cw), F32),
            jax.ShapeDtypeStruct((s, aw), BF16),
            jax.ShapeDtypeStruct((s, aw), BF16),
            jax.ShapeDtypeStruct((s, aw), BF16),
            jax.ShapeDtypeStruct((s, iw), F32),
            jax.ShapeDtypeStruct((s, d), F32),
            jax.ShapeDtypeStruct((s, d), F32),
            jax.ShapeDtypeStruct((1, aw), F32),
        ],
        compiler_params=pltpu.CompilerParams(dimension_semantics=("arbitrary",), vmem_limit_bytes=VMEM_LIMIT),
        name="mix_in",
    )(h, g2, w_conv, w_qkv, w_idx_hi, w_idx_lo, w_gate, qg, kg, head_blocks)


def _conv_kernel(z_ref, halo_ref, dw_ref, db_ref, lg_ref, lb_ref, wo_ref, gc_ref, o_ref, zp_ref):
    tm = z_ref.shape[0]
    first = pl.program_id(0) == 0
    halo = halo_ref[...]
    zp_ref[0:CONV_HALO, :] = jnp.where(first, jnp.zeros_like(halo), halo)
    zp_ref[CONV_HALO:, :] = z_ref[...]
    off = CONV_HALO - (CONV_WIDTH - 1)
    acc = jnp.zeros(z_ref.shape, F32) + db_ref[...]
    for j in range(CONV_WIDTH):
        acc = acc + dw_ref[j:j + 1, :] * zp_ref[off + j:off + j + tm, :]
    mu = jnp.mean(acc, axis=-1, keepdims=True)
    xc = acc - mu
    y = xc * lax.rsqrt(jnp.mean(xc * xc, axis=-1, keepdims=True) + EPS)
    y = y * lg_ref[...] + lb_ref[...]
    y = (y * _sigmoid(y)).astype(BF16)
    o_ref[...] = gc_ref[...] * _dot(y, wo_ref[...])


def _conv(z, dw_w, dw_b, ln_g, ln_b, w_out, gate_c, tm=256):
    s, c = z.shape
    d = w_out.shape[1]
    assert tm % CONV_HALO == 0
    r = tm // CONV_HALO
    full = lambda a: pl.BlockSpec(a.shape, lambda i: (0,) * a.ndim)
    vecs = [dw_b.reshape(1, c), ln_g.reshape(1, c), ln_b.reshape(1, c)]
    return pl.pallas_call(
        _conv_kernel,
        grid=(s // tm,),
        in_specs=[
            pl.BlockSpec((tm, c), lambda i: (i, 0)),
            pl.BlockSpec((CONV_HALO, c), lambda i: (jnp.maximum(i * r - 1, 0), 0)),
            full(dw_w), full(vecs[0]), full(vecs[1]), full(vecs[2]), full(w_out),
            pl.BlockSpec((tm, d), lambda i: (i, 0)),
        ],
        out_specs=pl.BlockSpec((tm, d), lambda i: (i, 0)),
        out_shape=jax.ShapeDtypeStruct((s, d), F32),
        scratch_shapes=[pltpu.VMEM((tm + CONV_HALO, c), F32)],
        compiler_params=pltpu.CompilerParams(dimension_semantics=("arbitrary",), vmem_limit_bytes=VMEM_LIMIT),
        name="conv",
    )(z, z, dw_w, *vecs, w_out, gate_c)


def _sortable_key(x):
    bits = pltpu.bitcast(x + 0.0, I32)
    return bits ^ ((bits >> 31) & 0x7FFFFFFF)


def _key_to_float(key):
    return pltpu.bitcast(key ^ ((key >> 31) & 0x7FFFFFFF), F32)


def _sort_network(n):
    pairs = []

    def merge(lo, hi, r):
        step = r * 2
        if step < hi - lo:
            merge(lo, hi, step)
            merge(lo + r, hi, step)
            pairs.extend((i, i + r) for i in range(lo + r, hi - r, step))
        else:
            pairs.append((lo, lo + r))

    def sort(lo, hi):
        if hi > lo:
            mid = lo + (hi - lo) // 2
            sort(lo, mid)
            sort(mid + 1, hi)
            merge(lo, hi, 1)

    sort(0, n - 1)
    return pairs


def _compare_exchange(a, i, j):
    a[i], a[j] = jnp.maximum(a[i], a[j]), jnp.minimum(a[i], a[j])


def _merge_top(top, batch):
    n = len(top)
    out = [jnp.maximum(top[i], batch[n - 1 - i]) for i in range(n)]
    d = n // 2
    while d >= 1:
        for i in range(n):
            if i & d == 0:
                _compare_exchange(out, i, i + d)
        d //= 2
    return out


def _select_kernel(qi_ref, w_ref, ki_ref, mask_ref, sc_ref, cand_ref, ckey_ref, st_ref, *, k_top):
    qb = pl.program_id(0)
    n_rows = sc_ref.shape[0]
    n_tiles = n_rows // SEL_TK
    q0 = qb * SEL_TQ
    nkt = (q0 + SEL_TQ - 1) // SEL_TK + 1
    vt = SEL_TK // SUBLANES
    vshape = (SUBLANES, LANES)
    qpos = q0 + lax.broadcasted_iota(I32, vshape, 1)
    sub = lax.broadcasted_iota(I32, vshape, 0)
    w = w_ref[...] * ((IDX_HEADS ** -0.5) * (IDX_DIM ** -0.5))

    qpos_t = q0 + lax.broadcasted_iota(I32, (SEL_TK, LANES), 1)
    krow_t = lax.broadcasted_iota(I32, (SEL_TK, LANES), 0)

    def score_tile(kt, causal_edge):
        k0 = pl.multiple_of(kt * SEL_TK, SEL_TK)
        ki = ki_ref[pl.ds(k0, SEL_TK), :]
        sc = None
        for pair in range(IDX_HEADS // 2):
            r = _dot(ki, qi_ref[0, pair])
            for j in range(2):
                h = 2 * pair + j
                term = w[h:h + 1, :] * jnp.maximum(r[:, j * LANES:(j + 1) * LANES], 0.0)
                sc = term if sc is None else sc + term
        if causal_edge:
            sc = jnp.where(k0 + krow_t <= qpos_t, sc, -jnp.inf)
        sc_ref[pl.ds(k0, SEL_TK), :] = sc

    def full_score_tile(kt, carry):
        score_tile(kt, False)
        return carry

    lax.fori_loop(0, nkt - 1, full_score_tile, 0)
    score_tile(nkt - 1, True)

    def vreg(ref, row):
        return ref[pl.ds(pl.multiple_of(row, SUBLANES), SUBLANES), :]

    def col_sum(parts):
        tot = parts[0]
        for part in parts[1:]:
            tot = tot + part
        return jnp.broadcast_to(jnp.sum(tot, axis=0, keepdims=True), vshape)

    n_cand = SEL_GROUPS * SEL_DEPTH
    network = _sort_network(SEL_DEPTH)
    cand_ref[...] = jnp.full(cand_ref.shape, -jnp.inf, F32)

    def lists_tile(kt, carry):
        for g in range(SEL_GROUPS):
            batch = [vreg(sc_ref, kt * SEL_TK + (g + SEL_GROUPS * j) * SUBLANES) for j in range(SEL_DEPTH)]
            for i, j in network:
                _compare_exchange(batch, i, j)
            rows = [slice((g * SEL_DEPTH + i) * SUBLANES, (g * SEL_DEPTH + i + 1) * SUBLANES)
                    for i in range(SEL_DEPTH)]
            top = _merge_top([cand_ref[r, :] for r in rows], batch)
            for r, t in zip(rows, top):
                cand_ref[r, :] = t
        return carry

    lax.fori_loop(0, nkt, lists_tile, 0)
    ckey_ref[...] = _sortable_key(cand_ref[...])

    n_acc = 4

    def cand_bit(i, t):
        c = t ^ lax.shift_left(jnp.int32(1), 31 - i)
        acc = [jnp.zeros(vshape, I32) for _ in range(n_acc)]
        for v in range(n_cand):
            x = ckey_ref[v * SUBLANES:(v + 1) * SUBLANES, :]
            acc[v % n_acc] = acc[v % n_acc] + jnp.where(x >= c, 1, 0)
        return jnp.where(col_sum(acc) >= k_top, c, t)

    st_ref[0] = lax.fori_loop(0, 32, cand_bit, jnp.full(vshape, INT_MIN, I32))

    def count_scores(preds):
        def tile(kt, acc):
            acc = [list(a) for a in acc]
            for v in range(vt):
                x = vreg(sc_ref, kt * SEL_TK + v * SUBLANES)
                kpos = kt * SEL_TK + v * SUBLANES + sub
                for p, pred in enumerate(preds):
                    acc[p][v % n_acc] = acc[p][v % n_acc] + jnp.where(pred(x, kpos), 1, 0)
            return tuple(tuple(a) for a in acc)

        zero = tuple(tuple(jnp.zeros(vshape, I32) for _ in range(n_acc)) for _ in preds)
        return [col_sum(list(a)) for a in lax.fori_loop(0, nkt, tile, zero)]

    def count_around_threshold():
        tf = _key_to_float(st_ref[0])
        gt, ge = count_scores([lambda x, kpos: x > tf, lambda x, kpos: x >= tf])
        st_ref[1] = gt
        st_ref[2] = ge

    tf_fast = jnp.broadcast_to(_key_to_float(st_ref[0])[0:1, :], (SEL_TK, LANES))

    def fast_tile(kt, acc):
        k0 = pl.multiple_of(kt * SEL_TK, SEL_TK)
        m = jnp.where(sc_ref[pl.ds(k0, SEL_TK), :] >= tf_fast, 1.0, 0.0)
        mask_ref[0, pl.ds(k0, SEL_TK), :] = m.astype(BF16)
        acc = list(acc)
        for v in range(vt):
            acc[v % n_acc] = acc[v % n_acc] + m[v * SUBLANES:(v + 1) * SUBLANES, :]
        return tuple(acc)

    size = col_sum(list(lax.fori_loop(0, nkt, fast_tile, tuple(jnp.zeros(vshape, F32) for _ in range(n_acc)))))

    @pl.when(jnp.max(jnp.abs(size - k_top)) > 0.0)
    def _():
        count_around_threshold()

        @pl.when(jnp.max(st_ref[1]) >= k_top)
        def _():
            def full_bit(i, t):
                c = t ^ lax.shift_left(jnp.int32(1), 31 - i)
                cnt, = count_scores([lambda x, kpos: _sortable_key(x) >= c])
                return jnp.where(cnt >= k_top, c, t)

            st_ref[0] = lax.fori_loop(0, 32, full_bit, jnp.full(vshape, INT_MIN, I32))
            count_around_threshold()

        st_ref[3] = jnp.full(vshape, n_rows, I32)

        @pl.when(jnp.max(st_ref[2]) > k_top)
        def _():
            tf = _key_to_float(st_ref[0])
            need = k_top - st_ref[1]
            n_bits = n_rows.bit_length() - 1

            def pos_bit(i, p):
                c = p + lax.shift_left(jnp.int32(1), n_bits - 1 - i)
                f, = count_scores([lambda x, kpos: (x == tf) & (kpos < c)])
                return jnp.where(f < need, c, p)

            st_ref[3] = lax.fori_loop(0, n_bits, pos_bit, jnp.zeros(vshape, I32))

        tf_t = jnp.broadcast_to(_key_to_float(st_ref[0])[0:1, :], (SEL_TK, LANES))
        lim_t = jnp.broadcast_to(st_ref[3][0:1, :], (SEL_TK, LANES))

        def write_tile(kt, carry):
            k0 = pl.multiple_of(kt * SEL_TK, SEL_TK)
            x = sc_ref[pl.ds(k0, SEL_TK), :]
            kpos = k0 + krow_t
            sel = ((x > tf_t) | ((x == tf_t) & (kpos <= lim_t))) & (kpos <= qpos_t)
            mask_ref[0, pl.ds(k0, SEL_TK), :] = jnp.where(sel, 1.0, 0.0).astype(BF16)
            return carry

        lax.fori_loop(0, nkt, write_tile, 0)

    def fill_tile(kt, carry):
        mask_ref[0, pl.ds(pl.multiple_of(kt * SEL_TK, SEL_TK), SEL_TK), :] = jnp.zeros((SEL_TK, LANES), BF16)
        return carry

    lax.fori_loop(nkt, n_tiles, fill_tile, 0)


def _select(qi_pairs, w_t, ki_ext, k_top):
    nqb, npair, kw, _ = qi_pairs.shape
    s = ki_ext.shape[0]
    n_cand_rows = SEL_GROUPS * SEL_DEPTH * SUBLANES
    assert s % SEL_TK == 0 and SEL_TK == n_cand_rows and n_cand_rows >= k_top
    return pl.pallas_call(
        functools.partial(_select_kernel, k_top=k_top),
        grid=(nqb,),
        in_specs=[
            pl.BlockSpec((1, npair, kw, 2 * SEL_TQ), lambda i: (i, 0, 0, 0)),
            pl.BlockSpec((w_t.shape[0], SEL_TQ), lambda i: (0, i)),
            pl.BlockSpec((s, kw), lambda i: (0, 0)),
        ],
        out_specs=pl.BlockSpec((1, s, SEL_TQ), lambda i: (i, 0, 0)),
        out_shape=jax.ShapeDtypeStruct((nqb, s, SEL_TQ), BF16),
        scratch_shapes=[
            pltpu.VMEM((s, SEL_TQ), F32),
            pltpu.VMEM((n_cand_rows, SEL_TQ), F32),
            pltpu.VMEM((n_cand_rows, SEL_TQ), I32),
            pltpu.VMEM((4, SUBLANES, LANES), I32),
        ],
        compiler_params=pltpu.CompilerParams(dimension_semantics=("arbitrary",), vmem_limit_bytes=VMEM_LIMIT),
        name="select",
    )(qi_pairs, w_t, ki_ext)


def _causal_pairs(s, tq, tk):
    pairs = [(qb, kt) for qb in range(s // tq) for kt in range((qb * tq + tq - 1) // tk + 1)]
    return jnp.asarray([p[0] for p in pairs], I32), jnp.asarray([p[1] for p in pairs], I32)


def _attn_kernel(qb_tab, kt_tab, qt_ref, kmax_ref, k_ref, vt_ref, mask_ref, bias_ref, o_ref, l_ref, qx_ref, acc_ref):
    step = pl.program_id(0)
    qb = qb_tab[step]
    kt = kt_tab[step]
    q0 = qb * ATT_TQ
    k0 = kt * ATT_TK
    nq = ATT_TQ // LANES
    nk = ATT_TK // LANES

    @pl.when(kt == 0)
    def _():
        row = lax.broadcasted_iota(I32, (HEAD_DIM, ATT_TQ), 0)
        for h in range(N_HEADS):
            q = qt_ref[h]
            qf = q.astype(F32)
            bound = jnp.sqrt(jnp.sum(qf * qf, axis=0, keepdims=True)) * kmax_ref[h] * 1.02
            qx_ref[h, 0:HEAD_DIM, :] = q
            qx_ref[h, HEAD_DIM:, :] = jnp.where(row == 0, -bound, 0.0).astype(BF16)
        acc_ref[...] = jnp.zeros(acc_ref.shape, F32)

    mask = jnp.concatenate([mask_ref[a] for a in range(nq)], axis=1)

    def heads(with_bias):
        for h in range(N_HEADS):
            s = _dot(k_ref[h], qx_ref[h])
            if with_bias:
                rows = []
                for c in range(nk):
                    tab = [jnp.clip((q0 + a * LANES - k0 - c * LANES) // LANES + 1, 0, 3) for a in range(nq)]
                    rows.append(jnp.concatenate([bias_ref[tab[a], h] for a in range(nq)], axis=1))
                s = s + jnp.concatenate(rows, axis=0)
            p = jnp.exp2(s).astype(BF16) * mask
            acc_ref[h] += _dot(vt_ref[h], p)

    near = k0 + ATT_TK + 2 * LANES > q0

    @pl.when(near)
    def _():
        heads(True)

    @pl.when(jnp.logical_not(near))
    def _():
        heads(False)

    @pl.when(kt == (q0 + ATT_TQ - 1) // ATT_TK)
    def _():
        for h in range(N_HEADS):
            acc = acc_ref[h]
            den = acc[HEAD_DIM:HEAD_DIM + 1, :]
            o_ref[h] = acc[:HEAD_DIM, :] / den
            l_ref[h] = den


def _attn(q_t, kmax_b, k_ext, v_t, mask, bias_tab):
    nh, hd, s = q_t.shape
    qb_tab, kt_tab = _causal_pairs(s, ATT_TQ, ATT_TK)
    grid_spec = pltpu.PrefetchScalarGridSpec(
        num_scalar_prefetch=2,
        grid=(qb_tab.shape[0],),
        in_specs=[
            pl.BlockSpec((nh, hd, ATT_TQ), lambda i, qb, kt: (0, 0, qb[i])),
            pl.BlockSpec((nh, 1, ATT_TQ), lambda i, qb, kt: (0, 0, 0)),
            pl.BlockSpec((nh, ATT_TK, LANES), lambda i, qb, kt: (0, kt[i], 0)),
            pl.BlockSpec((nh, LANES, ATT_TK), lambda i, qb, kt: (0, 0, kt[i])),
            pl.BlockSpec((ATT_TQ // LANES, ATT_TK, LANES), lambda i, qb, kt: (qb[i], kt[i], 0)),
            pl.BlockSpec(bias_tab.shape, lambda i, qb, kt: (0, 0, 0, 0)),
        ],
        out_specs=[
            pl.BlockSpec((nh, hd, ATT_TQ), lambda i, qb, kt: (0, 0, qb[i])),
            pl.BlockSpec((nh, 1, ATT_TQ), lambda i, qb, kt: (0, 0, qb[i])),
        ],
        scratch_shapes=[pltpu.VMEM((nh, LANES, ATT_TQ), BF16), pltpu.VMEM((nh, LANES, ATT_TQ), F32)],
    )
    return pl.pallas_call(
        _attn_kernel,
        grid_spec=grid_spec,
        out_shape=[jax.ShapeDtypeStruct((nh, hd, s), F32), jax.ShapeDtypeStruct((nh, 1, s), F32)],
        compiler_params=pltpu.CompilerParams(dimension_semantics=("arbitrary",), vmem_limit_bytes=VMEM_LIMIT),
        name="attn",
    )(qb_tab, kt_tab, q_t, kmax_b, k_ext, v_t, mask, bias_tab)


def _attn_safe_kernel(qb_tab, kt_tab, q_ref, k_ref, v_ref, mask_ref, bias_ref, o_ref, m_ref, acc_ref):
    step = pl.program_id(0)
    qb = qb_tab[step]
    kt = kt_tab[step]
    nq = SAFE_TQ // LANES
    nsub = ATT_TK // LANES

    @pl.when(kt == 0)
    def _():
        m_ref[...] = jnp.full(m_ref.shape, NEG, F32)
        acc_ref[...] = jnp.zeros(acc_ref.shape, F32)

    for a in range(nq):
        rows = slice(a * LANES, (a + 1) * LANES)
        q0 = qb * SAFE_TQ + a * LANES
        maskf = (mask_ref[a].astype(F32) - 1.0) * (-NEG)
        tab_idx = [jnp.clip((q0 - (kt * ATT_TK + c * LANES)) // LANES + 1, 0, 3) for c in range(nsub)]
        for h in range(N_HEADS):
            s = _dot_t(q_ref[h, rows, :], k_ref[h]) + maskf
            s = s + jnp.concatenate([bias_ref[tab_idx[c], h] for c in range(nsub)], axis=1)
            m_old = m_ref[h, rows, :]
            m_new = jnp.maximum(m_old, jnp.max(s, axis=1, keepdims=True))
            p = jnp.exp2(s - m_new[:, 0:1])
            alpha = jnp.exp2(m_old - m_new)
            acc_ref[h, rows, :] = alpha * acc_ref[h, rows, :] + _dot(p.astype(BF16), v_ref[h])
            m_ref[h, rows, :] = m_new

    @pl.when(kt == (qb * SAFE_TQ + SAFE_TQ - 1) // ATT_TK)
    def _():
        for h in range(N_HEADS):
            acc = acc_ref[h]
            o_ref[h] = acc[:, :HEAD_DIM] / acc[:, HEAD_DIM:HEAD_DIM + 1]


def _attn_safe(q, k, v_ext, mask_qk, bias_tab):
    nh, s, hd = q.shape
    qb_tab, kt_tab = _causal_pairs(s, SAFE_TQ, ATT_TK)
    grid_spec = pltpu.PrefetchScalarGridSpec(
        num_scalar_prefetch=2,
        grid=(qb_tab.shape[0],),
        in_specs=[
            pl.BlockSpec((nh, SAFE_TQ, hd), lambda i, qb, kt: (0, qb[i], 0)),
            pl.BlockSpec((nh, ATT_TK, hd), lambda i, qb, kt: (0, kt[i], 0)),
            pl.BlockSpec((nh, ATT_TK, LANES), lambda i, qb, kt: (0, kt[i], 0)),
            pl.BlockSpec((SAFE_TQ // LANES, LANES, ATT_TK), lambda i, qb, kt: (qb[i], 0, kt[i])),
            pl.BlockSpec(bias_tab.shape, lambda i, qb, kt: (0, 0, 0, 0)),
        ],
        out_specs=pl.BlockSpec((nh, SAFE_TQ, hd), lambda i, qb, kt: (0, qb[i], 0)),
        scratch_shapes=[pltpu.VMEM((nh, SAFE_TQ, LANES), F32), pltpu.VMEM((nh, SAFE_TQ, LANES), F32)],
    )
    return pl.pallas_call(
        _attn_safe_kernel,
        grid_spec=grid_spec,
        out_shape=jax.ShapeDtypeStruct((nh, s, hd), F32),
        compiler_params=pltpu.CompilerParams(dimension_semantics=("arbitrary",), vmem_limit_bytes=VMEM_LIMIT),
        name="attn_safe",
    )(qb_tab, kt_tab, q, k, v_ext, mask_qk, bias_tab)


def _merge_kernel(h_ref, cg_ref, ga_ref, at_ref, wao_ref, wmo_ref, o_ref):
    merged = cg_ref[...] + ga_ref[...] * _dot(at_ref[...], wao_ref[...])
    o_ref[...] = h_ref[...] + _dot(merged.astype(BF16), wmo_ref[...])


def _merge(h, conv_g, gate_a, attn, w_ao, w_mo, tm=512):
    s, d = h.shape
    full = lambda a: pl.BlockSpec(a.shape, lambda i: (0,) * a.ndim)
    row = lambda n: pl.BlockSpec((tm, n), lambda i: (i, 0))
    return pl.pallas_call(
        _merge_kernel,
        grid=(s // tm,),
        in_specs=[row(d), row(d), row(d), row(attn.shape[1]), full(w_ao), full(w_mo)],
        out_specs=row(d),
        out_shape=jax.ShapeDtypeStruct((s, d), F32),
        compiler_params=pltpu.CompilerParams(dimension_semantics=("arbitrary",), vmem_limit_bytes=VMEM_LIMIT),
        name="merge",
    )(h, conv_g, gate_a, attn, w_ao, w_mo)


def _ple_kernel(h_ref, g_ref, p_ref, wg_ref, wp_ref, o_ref):
    h = h_ref[...]
    gate = _sigmoid(_dot(_rms(h, g_ref[...]).astype(BF16), wg_ref[...]))
    o_ref[...] = h + gate * _dot(p_ref[...].astype(BF16), wp_ref[...])


def _ple(h, g, p, w_gate, w_proj, tm=512):
    s, d = h.shape
    full = lambda a: pl.BlockSpec(a.shape, lambda i: (0,) * a.ndim)
    row = lambda n: pl.BlockSpec((tm, n), lambda i: (i, 0))
    g2 = g.reshape(1, d)
    return pl.pallas_call(
        _ple_kernel,
        grid=(s // tm,),
        in_specs=[row(d), full(g2), row(p.shape[1]), full(w_gate), full(w_proj)],
        out_specs=row(d),
        out_shape=jax.ShapeDtypeStruct((s, d), F32),
        compiler_params=pltpu.CompilerParams(dimension_semantics=("arbitrary",), vmem_limit_bytes=VMEM_LIMIT),
        name="ple",
    )(h, g2, p, w_gate, w_proj)


def _t5_bucket_table(n_dist):
    n = np.arange(n_dist)
    max_exact = NUM_BUCKETS // 2
    nf = np.maximum(n, 1).astype(np.float32)
    large = max_exact + (np.log(nf / max_exact) / math.log(MAX_DISTANCE / max_exact)
                         * (NUM_BUCKETS - max_exact)).astype(np.int32)
    large = np.minimum(large, NUM_BUCKETS - 1)
    return np.where(n < max_exact, n, large)


def _bias_tables(rel_bias):
    assert MAX_DISTANCE <= LANES
    bucket = _t5_bucket_table(2 * LANES)
    i = np.arange(LANES)[:, None]
    j = np.arange(LANES)[None, :]
    rel = (rel_bias.astype(F32) - rel_bias[NUM_BUCKETS - 1].astype(F32)[None, :]) * LOG2E
    d0 = rel[bucket[np.maximum(i - j, 0)]]
    d1 = rel[bucket[LANES + i - j]]
    zero = jnp.zeros_like(d0)
    return jnp.transpose(jnp.stack([zero, d0, d1, zero]), (0, 3, 1, 2))


def kernel(x, p, ffn1_norm, ffn1_w_in, ffn1_w_out, mix_norm, mix_w_in, conv_dw_w, conv_dw_b, conv_ln_g,
           conv_ln_b, conv_w_out, q_norm, k_norm, attn_w_out, mix_w_out, ffn2_norm, ffn2_w_in, ffn2_w_out,
           ple_norm, ple_w_gate, ple_w_proj, rel_bias):
    b, s, d = x.shape
    depth = ffn1_norm.shape[0]
    cw = conv_dw_w.shape[2]
    aw = N_HEADS * HEAD_DIM
    nqi = IDX_HEADS * IDX_DIM
    iw = nqi + IDX_DIM + IDX_HEADS
    iw_pad = -(-iw // LANES) * LANES
    k_top = min(TOPK_MAX, s // 4)
    assert b == 1 and s % ATT_TK == 0 and mix_w_in.shape[2] == 2 * cw + 3 * aw + iw + 2 * d

    head_blocks = jnp.asarray(np.kron(np.eye(N_HEADS), np.ones((HEAD_DIM, HEAD_DIM))), BF16)
    bias_qk = _bias_tables(rel_bias)
    bias_kq = jnp.swapaxes(bias_qk, 2, 3)
    heads = lambda t: t.reshape(s, N_HEADS, HEAD_DIM).transpose(1, 0, 2)
    heads_t = lambda t: t.reshape(s, N_HEADS, HEAD_DIM).transpose(1, 2, 0)

    h = x[0]
    for i in range(depth):
        h = _ffn(h, ffn1_norm[i], ffn1_w_in[i].astype(BF16), ffn1_w_out[i].astype(BF16))

        w = mix_w_in[i]
        o0 = 2 * cw
        o1 = o0 + 3 * aw
        o2 = o1 + iw
        w_idx = jnp.pad(w[:, o1:o2], ((0, 0), (0, iw_pad - iw)))
        w_idx_hi, w_idx_lo = _split_bf16(w_idx)
        qg = jnp.tile(q_norm[i], N_HEADS).reshape(1, aw)
        kg = jnp.tile(k_norm[i], N_HEADS).reshape(1, aw)
        glu, q, k, v, idx, gate_c, gate_a, ksq = _mix_in(
            h, mix_norm[i], w[:, :o0].astype(BF16), w[:, o0:o1].astype(BF16), w_idx_hi, w_idx_lo,
            w[:, o2:].astype(BF16), qg, kg, head_blocks)

        conv_g = _conv(glu, conv_dw_w[i], conv_dw_b[i], conv_ln_g[i], conv_ln_b[i],
                       conv_w_out[i].astype(BF16), gate_c)

        qi_hi, qi_lo = _split_bf16(idx[:, :nqi].reshape(s // SEL_TQ, SEL_TQ, IDX_HEADS // 2, 2, IDX_DIM))
        qi_ext = jnp.concatenate([qi_hi, qi_hi, qi_lo, qi_lo], axis=-1)
        qi_pairs = qi_ext.transpose(0, 2, 4, 3, 1).reshape(s // SEL_TQ, IDX_HEADS // 2, 4 * IDX_DIM, 2 * SEL_TQ)
        ki_hi, ki_lo = _split_bf16(idx[:, nqi:nqi + IDX_DIM])
        ki_ext = jnp.concatenate([ki_hi, ki_lo, ki_hi, ki_lo], axis=-1)
        w_t = idx[:, nqi + IDX_DIM:iw].T
        mask = _select(qi_pairs, w_t, ki_ext, k_top)

        def pad_ones(t, axis):
            one = lax.slice_in_dim(t, 0, 1, axis=axis)
            rest = lax.slice_in_dim(t, 0, LANES - HEAD_DIM - 1, axis=axis)
            return jnp.concatenate([t, jnp.ones_like(one), jnp.zeros_like(rest)], axis=axis)

        k_ext = pad_ones(heads(k), 2)
        v_t = pad_ones(heads_t(v), 1)
        kmax = jnp.sqrt(ksq.reshape(N_HEADS, HEAD_DIM)[:, :1])
        kmax_b = jnp.broadcast_to(kmax[:, :, None], (N_HEADS, 1, ATT_TQ))
        attn_t, den = _attn(heads_t(q), kmax_b, k_ext, v_t, mask, bias_kq)
        attn = attn_t.transpose(2, 0, 1).reshape(s, aw)

        def safe_attn():
            mask_qk = jnp.swapaxes(mask, 1, 2)
            v_ext = pad_ones(heads(v), 2)
            return _attn_safe(heads(q), heads(k), v_ext, mask_qk, bias_qk).transpose(1, 0, 2).reshape(s, aw)

        underflow = jnp.logical_not(jnp.min(den) > L_MIN)
        attn = lax.cond(underflow, safe_attn, lambda: attn).astype(BF16)

        h = _merge(h, conv_g, gate_a, attn, attn_w_out[i].astype(BF16), mix_w_out[i].astype(BF16))
        h = _ffn(h, ffn2_norm[i], ffn2_w_in[i].astype(BF16), ffn2_w_out[i].astype(BF16))
        h = _ple(h, ple_norm[i], p[i, 0], ple_w_gate[i].astype(BF16), ple_w_proj[i].astype(BF16))
    return h[None]
```

```python
import functools
import math

import numpy as np
import jax
import jax.numpy as jnp
from jax import lax
from jax.experimental import pallas as pl
from jax.experimental.pallas import tpu as pltpu

F32 = jnp.float32
BF16 = jnp.bfloat16
I32 = jnp.int32

EPS = 1e-6
CONV_WIDTH = 31
N_HEADS = 8
HEAD_DIM = 64
IDX_HEADS = 4
IDX_DIM = 64
TOPK_MAX = 256
NUM_BUCKETS = 32
MAX_DISTANCE = 128

LANES = 128
SUBLANES = 8
VMEM_LIMIT = 56 * 1024 * 1024
NEG = -1e30
INT_MIN = -2 ** 31
MIN_NORMAL_BITS = 0x00800000
LOG2E = math.log2(math.e)

CONV_HALO = 32
SEL_TQ = LANES
SEL_TK = 1024
SEL_GROUPS = 8
SEL_DEPTH = 16
ATT_TQ = 512
ATT_TK = 1024
SAFE_TQ = 256
L_MIN = 1e-30


def _sigmoid(x):
    return 1.0 / (1.0 + jnp.exp(-x))


def _rms(x, g):
    ms = jnp.mean(x * x, axis=-1, keepdims=True)
    return x * lax.rsqrt(ms + EPS) * g


def _dot(a, b):
    return jnp.dot(a, b, preferred_element_type=F32)


def _dot_t(a, b):
    return lax.dot_general(a, b, (((1,), (1,)), ((), ())), preferred_element_type=F32)


def _split_bf16(x):
    hi = x.astype(BF16)
    lo = (x - hi.astype(F32)).astype(BF16)
    return hi, lo


def _ffn_kernel(x_ref, g_ref, wa_ref, wb_ref, wo_ref, o_ref, xn_ref, acc_ref):
    j = pl.program_id(1)

    @pl.when(j == 0)
    def _():
        xn_ref[...] = _rms(x_ref[...], g_ref[...]).astype(BF16)
        acc_ref[...] = jnp.zeros_like(acc_ref)

    xn = xn_ref[...]
    a = _dot(xn, wa_ref[...])
    b = _dot(xn, wb_ref[...])
    hmid = (a * _sigmoid(a) * b).astype(BF16)
    acc_ref[...] += _dot(hmid, wo_ref[...])

    @pl.when(j == pl.num_programs(1) - 1)
    def _():
        o_ref[...] = x_ref[...] + 0.5 * acc_ref[...]


def _ffn(x, g, w_in, w_out, tm=512, fc=1408):
    s, d = x.shape
    dff = w_out.shape[0]
    nj = dff // fc
    assert s % tm == 0 and dff % fc == 0 and fc % LANES == 0
    return pl.pallas_call(
        _ffn_kernel,
        grid=(s // tm, nj),
        in_specs=[
            pl.BlockSpec((tm, d), lambda i, j: (i, 0)),
            pl.BlockSpec((1, d), lambda i, j: (0, 0)),
            pl.BlockSpec((d, fc), lambda i, j: (0, j)),
            pl.BlockSpec((d, fc), lambda i, j: (0, j + nj)),
            pl.BlockSpec((fc, d), lambda i, j: (j, 0)),
        ],
        out_specs=pl.BlockSpec((tm, d), lambda i, j: (i, 0)),
        out_shape=jax.ShapeDtypeStruct((s, d), F32),
        scratch_shapes=[pltpu.VMEM((tm, d), BF16), pltpu.VMEM((tm, d), F32)],
        compiler_params=pltpu.CompilerParams(
            dimension_semantics=("arbitrary", "arbitrary"), vmem_limit_bytes=VMEM_LIMIT),
        name="ffn",
    )(x, g.reshape(1, d), w_in, w_in, w_out)


def _mix_in_kernel(h_ref, g_ref, wc_ref, wqkv_ref, wih_ref, wil_ref, wg_ref, qg_ref, kg_ref, hb_ref,
                   glu_ref, q_ref, k_ref, v_ref, idx_ref, gc_ref, ga_ref, ksq_ref):
    u = _rms(h_ref[...], g_ref[...])
    u_hi, u_lo = _split_bf16(u)
    cw = glu_ref.shape[1]
    aw = q_ref.shape[1]
    d = gc_ref.shape[1]

    c = _dot(u_hi, wc_ref[...])
    glu_ref[...] = c[:, :cw] * _sigmoid(c[:, cw:])

    qkv = _dot(u_hi, wqkv_ref[...])
    hb = hb_ref[...]

    def head_sumsq(t):
        t2_hi, t2_lo = _split_bf16(t * t)
        return _dot(t2_hi, hb) + _dot(t2_lo, hb)

    def head_norm(t, g):
        return t * lax.rsqrt(head_sumsq(t) * (1.0 / HEAD_DIM) + EPS) * g

    q_ref[...] = (head_norm(qkv[:, :aw], qg_ref[...]) * (HEAD_DIM ** -0.5 * LOG2E)).astype(BF16)
    k = head_norm(qkv[:, aw:2 * aw], kg_ref[...]).astype(BF16)
    k_ref[...] = k
    v_ref[...] = qkv[:, 2 * aw:].astype(BF16)

    ksq = jnp.max(head_sumsq(k.astype(F32)), axis=0, keepdims=True)

    @pl.when(pl.program_id(0) == 0)
    def _():
        ksq_ref[...] = ksq

    @pl.when(pl.program_id(0) > 0)
    def _():
        ksq_ref[...] = jnp.maximum(ksq_ref[...], ksq)

    wih = wih_ref[...]
    idx_ref[...] = _dot(u_hi, wih) + _dot(u_lo, wih) + _dot(u_hi, wil_ref[...])

    gates = _sigmoid(_dot(u_hi, wg_ref[...]))
    gc_ref[...] = gates[:, :d]
    ga_ref[...] = gates[:, d:]


def _mix_in(h, g, w_conv, w_qkv, w_idx_hi, w_idx_lo, w_gate, qg, kg, head_blocks, tm=512):
    s, d = h.shape
    cw = w_conv.shape[1] // 2
    aw = w_qkv.shape[1] // 3
    iw = w_idx_hi.shape[1]
    full = lambda a: pl.BlockSpec(a.shape, lambda i: (0,) * a.ndim)
    row = lambda n: pl.BlockSpec((tm, n), lambda i: (i, 0))
    g2 = g.reshape(1, d)
    return pl.pallas_call(
        _mix_in_kernel,
        grid=(s // tm,),
        in_specs=[row(d), full(g2), full(w_conv), full(w_qkv), full(w_idx_hi), full(w_idx_lo), full(w_gate),
                  full(qg), full(kg), full(head_blocks)],
        out_specs=[row(cw), row(aw), row(aw), row(aw), row(iw), row(d), row(d),
                   pl.BlockSpec((1, aw), lambda i: (0, 0))],
        out_shape=[
            jax.ShapeDtypeStruct((s, cw), F32),
            jax.ShapeDtypeStruct((s, aw), BF16),
            jax.ShapeDtypeStruct((s, aw), BF16),
            jax.ShapeDtypeStruct((s, aw), BF16),
            jax.ShapeDtypeStruct((s, iw), F32),
            jax.ShapeDtypeStruct((s, d), F32),
            jax.ShapeDtypeStruct((s, d), F32),
            jax.ShapeDtypeStruct((1, aw), F32),
        ],
        compiler_params=pltpu.CompilerParams(dimension_semantics=("arbitrary",), vmem_limit_bytes=VMEM_LIMIT),
        name="mix_in",
    )(h, g2, w_conv, w_qkv, w_idx_hi, w_idx_lo, w_gate, qg, kg, head_blocks)


def _conv_kernel(z_ref, halo_ref, dw_ref, db_ref, lg_ref, lb_ref, wo_ref, gc_ref, o_ref, zp_ref):
    tm = z_ref.shape[0]
    first = pl.program_id(0) == 0
    halo = halo_ref[...]
    zp_ref[0:CONV_HALO, :] = jnp.where(first, jnp.zeros_like(halo), halo)
    zp_ref[CONV_HALO:, :] = z_ref[...]
    off = CONV_HALO - (CONV_WIDTH - 1)
    acc = jnp.zeros(z_ref.shape, F32) + db_ref[...]
    for j in range(CONV_WIDTH):
        acc = acc + dw_ref[j:j + 1, :] * zp_ref[off + j:off + j + tm, :]
    mu = jnp.mean(acc, axis=-1, keepdims=True)
    xc = acc - mu
    y = xc * lax.rsqrt(jnp.mean(xc * xc, axis=-1, keepdims=True) + EPS)
    y = y * lg_ref[...] + lb_ref[...]
    y = (y * _sigmoid(y)).astype(BF16)
    o_ref[...] = gc_ref[...] * _dot(y, wo_ref[...])


def _conv(z, dw_w, dw_b, ln_g, ln_b, w_out, gate_c, tm=256):
    s, c = z.shape
    d = w_out.shape[1]
    assert tm % CONV_HALO == 0
    r = tm // CONV_HALO
    full = lambda a: pl.BlockSpec(a.shape, lambda i: (0,) * a.ndim)
    vecs = [dw_b.reshape(1, c), ln_g.reshape(1, c), ln_b.reshape(1, c)]
    return pl.pallas_call(
        _conv_kernel,
        grid=(s // tm,),
        in_specs=[
            pl.BlockSpec((tm, c), lambda i: (i, 0)),
            pl.BlockSpec((CONV_HALO, c), lambda i: (jnp.maximum(i * r - 1, 0), 0)),
            full(dw_w), full(vecs[0]), full(vecs[1]), full(vecs[2]), full(w_out),
            pl.BlockSpec((tm, d), lambda i: (i, 0)),
        ],
        out_specs=pl.BlockSpec((tm, d), lambda i: (i, 0)),
        out_shape=jax.ShapeDtypeStruct((s, d), F32),
        scratch_shapes=[pltpu.VMEM((tm + CONV_HALO, c), F32)],
        compiler_params=pltpu.CompilerParams(dimension_semantics=("arbitrary",), vmem_limit_bytes=VMEM_LIMIT),
        name="conv",
    )(z, z, dw_w, *vecs, w_out, gate_c)


def _sortable_key(x):
    bits = pltpu.bitcast(x + 0.0, I32)
    return bits ^ ((bits >> 31) & 0x7FFFFFFF)


def _key_to_float(key):
    return pltpu.bitcast(key ^ ((key >> 31) & 0x7FFFFFFF), F32)


def _sort_network(n):
    pairs = []

    def merge(lo, hi, r):
        step = r * 2
        if step < hi - lo:
            merge(lo, hi, step)
            merge(lo + r, hi, step)
            pairs.extend((i, i + r) for i in range(lo + r, hi - r, step))
        else:
            pairs.append((lo, lo + r))

    def sort(lo, hi):
        if hi > lo:
            mid = lo + (hi - lo) // 2
            sort(lo, mid)
            sort(mid + 1, hi)
            merge(lo, hi, 1)

    sort(0, n - 1)
    return pairs


def _compare_exchange(a, i, j):
    a[i], a[j] = jnp.maximum(a[i], a[j]), jnp.minimum(a[i], a[j])


def _merge_top(top, batch):
    n = len(top)
    out = [jnp.maximum(top[i], batch[n - 1 - i]) for i in range(n)]
    d = n // 2
    while d >= 1:
        for i in range(n):
            if i & d == 0:
                _compare_exchange(out, i, i + d)
        d //= 2
    return out


def _select_kernel(qi_ref, w_ref, ki_ref, mask_ref, sc_ref, cand_ref, ckey_ref, st_ref, *, k_top):
    qb = pl.program_id(0)
    n_rows = sc_ref.shape[0]
    n_tiles = n_rows // SEL_TK
    q0 = qb * SEL_TQ
    nkt = (q0 + SEL_TQ - 1) // SEL_TK + 1
    vt = SEL_TK // SUBLANES
    vshape = (SUBLANES, LANES)
    qpos = q0 + lax.broadcasted_iota(I32, vshape, 1)
    sub = lax.broadcasted_iota(I32, vshape, 0)
    w = w_ref[...] * ((IDX_HEADS ** -0.5) * (IDX_DIM ** -0.5))

    qpos_t = q0 + lax.broadcasted_iota(I32, (SEL_TK, LANES), 1)
    krow_t = lax.broadcasted_iota(I32, (SEL_TK, LANES), 0)
    zero_bits_t = (MIN_NORMAL_BITS + n_rows) - krow_t

    def score_tile(kt, causal_edge):
        k0 = pl.multiple_of(kt * SEL_TK, SEL_TK)
        ki = ki_ref[pl.ds(k0, SEL_TK), :]
        sc = None
        for pair in range(IDX_HEADS // 2):
            r = _dot(ki, qi_ref[0, pair])
            for j in range(2):
                h = 2 * pair + j
                term = w[h:h + 1, :] * jnp.maximum(r[:, j * LANES:(j + 1) * LANES], 0.0)
                sc = term if sc is None else sc + term
        sc = jnp.where(sc == 0.0, pltpu.bitcast(zero_bits_t - k0, F32), sc)
        if causal_edge:
            sc = jnp.where(k0 + krow_t <= qpos_t, sc, -jnp.inf)
        sc_ref[pl.ds(k0, SEL_TK), :] = sc

    def full_score_tile(kt, carry):
        score_tile(kt, False)
        return carry

    lax.fori_loop(0, nkt - 1, full_score_tile, 0)
    score_tile(nkt - 1, True)

    def vreg(ref, row):
        return ref[pl.ds(pl.multiple_of(row, SUBLANES), SUBLANES), :]

    def col_sum(parts):
        tot = parts[0]
        for part in parts[1:]:
            tot = tot + part
        return jnp.broadcast_to(jnp.sum(tot, axis=0, keepdims=True), vshape)

    n_cand = SEL_GROUPS * SEL_DEPTH
    network = _sort_network(SEL_DEPTH)
    cand_ref[...] = jnp.full(cand_ref.shape, -jnp.inf, F32)

    def lists_tile(kt, carry):
        for g in range(SEL_GROUPS):
            batch = [vreg(sc_ref, kt * SEL_TK + (g + SEL_GROUPS * j) * SUBLANES) for j in range(SEL_DEPTH)]
            for i, j in network:
                _compare_exchange(batch, i, j)
            rows = [slice((g * SEL_DEPTH + i) * SUBLANES, (g * SEL_DEPTH + i + 1) * SUBLANES)
                    for i in range(SEL_DEPTH)]
            top = _merge_top([cand_ref[r, :] for r in rows], batch)
            for r, t in zip(rows, top):
                cand_ref[r, :] = t
        return carry

    lax.fori_loop(0, nkt, lists_tile, 0)
    ckey_ref[...] = _sortable_key(cand_ref[...])

    n_acc = 4

    def cand_bit(i, t):
        c = t ^ lax.shift_left(jnp.int32(1), 31 - i)
        acc = [jnp.zeros(vshape, I32) for _ in range(n_acc)]
        for v in range(n_cand):
            x = ckey_ref[v * SUBLANES:(v + 1) * SUBLANES, :]
            acc[v % n_acc] = acc[v % n_acc] + jnp.where(x >= c, 1, 0)
        return jnp.where(col_sum(acc) >= k_top, c, t)

    st_ref[0] = lax.fori_loop(0, 32, cand_bit, jnp.full(vshape, INT_MIN, I32))

    def count_scores(preds):
        def tile(kt, acc):
            acc = [list(a) for a in acc]
            for v in range(vt):
                x = vreg(sc_ref, kt * SEL_TK + v * SUBLANES)
                kpos = kt * SEL_TK + v * SUBLANES + sub
                for p, pred in enumerate(preds):
                    acc[p][v % n_acc] = acc[p][v % n_acc] + jnp.where(pred(x, kpos), 1, 0)
            return tuple(tuple(a) for a in acc)

        zero = tuple(tuple(jnp.zeros(vshape, I32) for _ in range(n_acc)) for _ in preds)
        return [col_sum(list(a)) for a in lax.fori_loop(0, nkt, tile, zero)]

    def count_around_threshold():
        tf = _key_to_float(st_ref[0])
        gt, ge = count_scores([lambda x, kpos: x > tf, lambda x, kpos: x >= tf])
        st_ref[1] = gt
        st_ref[2] = ge

    tf_fast = jnp.broadcast_to(_key_to_float(st_ref[0])[0:1, :], (SEL_TK, LANES))

    def fast_tile(kt, acc):
        k0 = pl.multiple_of(kt * SEL_TK, SEL_TK)
        m = jnp.where(sc_ref[pl.ds(k0, SEL_TK), :] >= tf_fast, 1.0, 0.0)
        mask_ref[0, pl.ds(k0, SEL_TK), :] = m.astype(BF16)
        acc = list(acc)
        for v in range(vt):
            acc[v % n_acc] = acc[v % n_acc] + m[v * SUBLANES:(v + 1) * SUBLANES, :]
        return tuple(acc)

    size = col_sum(list(lax.fori_loop(0, nkt, fast_tile, tuple(jnp.zeros(vshape, F32) for _ in range(n_acc)))))

    @pl.when(jnp.max(jnp.abs(size - k_top)) > 0.0)
    def _():
        count_around_threshold()

        @pl.when(jnp.max(st_ref[1]) >= k_top)
        def _():
            def full_bit(i, t):
                c = t ^ lax.shift_left(jnp.int32(1), 31 - i)
                cnt, = count_scores([lambda x, kpos: _sortable_key(x) >= c])
                return jnp.where(cnt >= k_top, c, t)

            st_ref[0] = lax.fori_loop(0, 32, full_bit, jnp.full(vshape, INT_MIN, I32))
            count_around_threshold()

        st_ref[3] = jnp.full(vshape, n_rows, I32)

        @pl.when(jnp.max(st_ref[2]) > k_top)
        def _():
            tf = _key_to_float(st_ref[0])
            need = k_top - st_ref[1]
            n_bits = n_rows.bit_length() - 1

            def pos_bit(i, p):
                c = p + lax.shift_left(jnp.int32(1), n_bits - 1 - i)
                f, = count_scores([lambda x, kpos: (x == tf) & (kpos < c)])
                return jnp.where(f < need, c, p)

            st_ref[3] = lax.fori_loop(0, n_bits, pos_bit, jnp.zeros(vshape, I32))

        tf_t = jnp.broadcast_to(_key_to_float(st_ref[0])[0:1, :], (SEL_TK, LANES))
        lim_t = jnp.broadcast_to(st_ref[3][0:1, :], (SEL_TK, LANES))

        def write_tile(kt, carry):
            k0 = pl.multiple_of(kt * SEL_TK, SEL_TK)
            x = sc_ref[pl.ds(k0, SEL_TK), :]
            kpos = k0 + krow_t
            sel = ((x > tf_t) | ((x == tf_t) & (kpos <= lim_t))) & (kpos <= qpos_t)
            mask_ref[0, pl.ds(k0, SEL_TK), :] = jnp.where(sel, 1.0, 0.0).astype(BF16)
            return carry

        lax.fori_loop(0, nkt, write_tile, 0)

    def fill_tile(kt, carry):
        mask_ref[0, pl.ds(pl.multiple_of(kt * SEL_TK, SEL_TK), SEL_TK), :] = jnp.zeros((SEL_TK, LANES), BF16)
        return carry

    lax.fori_loop(nkt, n_tiles, fill_tile, 0)


def _select(qi_pairs, w_t, ki_ext, k_top):
    nqb, npair, kw, _ = qi_pairs.shape
    s = ki_ext.shape[0]
    n_cand_rows = SEL_GROUPS * SEL_DEPTH * SUBLANES
    assert s % SEL_TK == 0 and SEL_TK == n_cand_rows and n_cand_rows >= k_top
    return pl.pallas_call(
        functools.partial(_select_kernel, k_top=k_top),
        grid=(nqb,),
        in_specs=[
            pl.BlockSpec((1, npair, kw, 2 * SEL_TQ), lambda i: (i, 0, 0, 0)),
            pl.BlockSpec((w_t.shape[0], SEL_TQ), lambda i: (0, i)),
            pl.BlockSpec((s, kw), lambda i: (0, 0)),
        ],
        out_specs=pl.BlockSpec((1, s, SEL_TQ), lambda i: (i, 0, 0)),
        out_shape=jax.ShapeDtypeStruct((nqb, s, SEL_TQ), BF16),
        scratch_shapes=[
            pltpu.VMEM((s, SEL_TQ), F32),
            pltpu.VMEM((n_cand_rows, SEL_TQ), F32),
            pltpu.VMEM((n_cand_rows, SEL_TQ), I32),
            pltpu.VMEM((4, SUBLANES, LANES), I32),
        ],
        compiler_params=pltpu.CompilerParams(dimension_semantics=("arbitrary",), vmem_limit_bytes=VMEM_LIMIT),
        name="select",
    )(qi_pairs, w_t, ki_ext)


def _causal_pairs(s, tq, tk):
    pairs = [(qb, kt) for qb in range(s // tq) for kt in range((qb * tq + tq - 1) // tk + 1)]
    return jnp.asarray([p[0] for p in pairs], I32), jnp.asarray([p[1] for p in pairs], I32)


def _attn_kernel(qb_tab, kt_tab, qt_ref, kmax_ref, k_ref, vt_ref, mask_ref, bias_ref, o_ref, l_ref, qx_ref, acc_ref):
    step = pl.program_id(0)
    qb = qb_tab[step]
    kt = kt_tab[step]
    q0 = qb * ATT_TQ
    k0 = kt * ATT_TK
    nq = ATT_TQ // LANES
    nk = ATT_TK // LANES

    @pl.when(kt == 0)
    def _():
        row = lax.broadcasted_iota(I32, (HEAD_DIM, ATT_TQ), 0)
        for h in range(N_HEADS):
            q = qt_ref[h]
            qf = q.astype(F32)
            bound = jnp.sqrt(jnp.sum(qf * qf, axis=0, keepdims=True)) * kmax_ref[h] * 1.02
            qx_ref[h, 0:HEAD_DIM, :] = q
            qx_ref[h, HEAD_DIM:, :] = jnp.where(row == 0, -bound, 0.0).astype(BF16)
        acc_ref[...] = jnp.zeros(acc_ref.shape, F32)

    mask = jnp.concatenate([mask_ref[a] for a in range(nq)], axis=1)

    def heads(with_bias):
        for h in range(N_HEADS):
            s = _dot(k_ref[h], qx_ref[h])
            if with_bias:
                rows = []
                for c in range(nk):
                    tab = [jnp.clip((q0 + a * LANES - k0 - c * LANES) // LANES + 1, 0, 3) for a in range(nq)]
                    rows.append(jnp.concatenate([bias_ref[tab[a], h] for a in range(nq)], axis=1))
                s = s + jnp.concatenate(rows, axis=0)
            p = jnp.exp2(s).astype(BF16) * mask
            acc_ref[h] += _dot(vt_ref[h], p)

    near = k0 + ATT_TK + 2 * LANES > q0

    @pl.when(near)
    def _():
        heads(True)

    @pl.when(jnp.logical_not(near))
    def _():
        heads(False)

    @pl.when(kt == (q0 + ATT_TQ - 1) // ATT_TK)
    def _():
        for h in range(N_HEADS):
            acc = acc_ref[h]
            den = acc[HEAD_DIM:HEAD_DIM + 1, :]
            o_ref[h] = acc[:HEAD_DIM, :] / den
            l_ref[h] = den


def _attn(q_t, kmax_b, k_ext, v_t, mask, bias_tab):
    nh, hd, s = q_t.shape
    qb_tab, kt_tab = _causal_pairs(s, ATT_TQ, ATT_TK)
    grid_spec = pltpu.PrefetchScalarGridSpec(
        num_scalar_prefetch=2,
        grid=(qb_tab.shape[0],),
        in_specs=[
            pl.BlockSpec((nh, hd, ATT_TQ), lambda i, qb, kt: (0, 0, qb[i])),
            pl.BlockSpec((nh, 1, ATT_TQ), lambda i, qb, kt: (0, 0, 0)),
            pl.BlockSpec((nh, ATT_TK, LANES), lambda i, qb, kt: (0, kt[i], 0)),
            pl.BlockSpec((nh, LANES, ATT_TK), lambda i, qb, kt: (0, 0, kt[i])),
            pl.BlockSpec((ATT_TQ // LANES, ATT_TK, LANES), lambda i, qb, kt: (qb[i], kt[i], 0)),
            pl.BlockSpec(bias_tab.shape, lambda i, qb, kt: (0, 0, 0, 0)),
        ],
        out_specs=[
            pl.BlockSpec((nh, hd, ATT_TQ), lambda i, qb, kt: (0, 0, qb[i])),
            pl.BlockSpec((nh, 1, ATT_TQ), lambda i, qb, kt: (0, 0, qb[i])),
        ],
        scratch_shapes=[pltpu.VMEM((nh, LANES, ATT_TQ), BF16), pltpu.VMEM((nh, LANES, ATT_TQ), F32)],
    )
    return pl.pallas_call(
        _attn_kernel,
        grid_spec=grid_spec,
        out_shape=[jax.ShapeDtypeStruct((nh, hd, s), F32), jax.ShapeDtypeStruct((nh, 1, s), F32)],
        compiler_params=pltpu.CompilerParams(dimension_semantics=("arbitrary",), vmem_limit_bytes=VMEM_LIMIT),
        name="attn",
    )(qb_tab, kt_tab, q_t, kmax_b, k_ext, v_t, mask, bias_tab)


def _attn_safe_kernel(qb_tab, kt_tab, q_ref, k_ref, v_ref, mask_ref, bias_ref, o_ref, m_ref, acc_ref):
    step = pl.program_id(0)
    qb = qb_tab[step]
    kt = kt_tab[step]
    nq = SAFE_TQ // LANES
    nsub = ATT_TK // LANES

    @pl.when(kt == 0)
    def _():
        m_ref[...] = jnp.full(m_ref.shape, NEG, F32)
        acc_ref[...] = jnp.zeros(acc_ref.shape, F32)

    for a in range(nq):
        rows = slice(a * LANES, (a + 1) * LANES)
        q0 = qb * SAFE_TQ + a * LANES
        maskf = (mask_ref[a].astype(F32) - 1.0) * (-NEG)
        tab_idx = [jnp.clip((q0 - (kt * ATT_TK + c * LANES)) // LANES + 1, 0, 3) for c in range(nsub)]
        for h in range(N_HEADS):
            s = _dot_t(q_ref[h, rows, :], k_ref[h]) + maskf
            s = s + jnp.concatenate([bias_ref[tab_idx[c], h] for c in range(nsub)], axis=1)
            m_old = m_ref[h, rows, :]
            m_new = jnp.maximum(m_old, jnp.max(s, axis=1, keepdims=True))
            p = jnp.exp2(s - m_new[:, 0:1])
            alpha = jnp.exp2(m_old - m_new)
            acc_ref[h, rows, :] = alpha * acc_ref[h, rows, :] + _dot(p.astype(BF16), v_ref[h])
            m_ref[h, rows, :] = m_new

    @pl.when(kt == (qb * SAFE_TQ + SAFE_TQ - 1) // ATT_TK)
    def _():
        for h in range(N_HEADS):
            acc = acc_ref[h]
            o_ref[h] = acc[:, :HEAD_DIM] / acc[:, HEAD_DIM:HEAD_DIM + 1]


def _attn_safe(q, k, v_ext, mask_qk, bias_tab):
    nh, s, hd = q.shape
    qb_tab, kt_tab = _causal_pairs(s, SAFE_TQ, ATT_TK)
    grid_spec = pltpu.PrefetchScalarGridSpec(
        num_scalar_prefetch=2,
        grid=(qb_tab.shape[0],),
        in_specs=[
            pl.BlockSpec((nh, SAFE_TQ, hd), lambda i, qb, kt: (0, qb[i], 0)),
            pl.BlockSpec((nh, ATT_TK, hd), lambda i, qb, kt: (0, kt[i], 0)),
            pl.BlockSpec((nh, ATT_TK, LANES), lambda i, qb, kt: (0, kt[i], 0)),
            pl.BlockSpec((SAFE_TQ // LANES, LANES, ATT_TK), lambda i, qb, kt: (qb[i], 0, kt[i])),
            pl.BlockSpec(bias_tab.shape, lambda i, qb, kt: (0, 0, 0, 0)),
        ],
        out_specs=pl.BlockSpec((nh, SAFE_TQ, hd), lambda i, qb, kt: (0, qb[i], 0)),
        scratch_shapes=[pltpu.VMEM((nh, SAFE_TQ, LANES), F32), pltpu.VMEM((nh, SAFE_TQ, LANES), F32)],
    )
    return pl.pallas_call(
        _attn_safe_kernel,
        grid_spec=grid_spec,
        out_shape=jax.ShapeDtypeStruct((nh, s, hd), F32),
        compiler_params=pltpu.CompilerParams(dimension_semantics=("arbitrary",), vmem_limit_bytes=VMEM_LIMIT),
        name="attn_safe",
    )(qb_tab, kt_tab, q, k, v_ext, mask_qk, bias_tab)


def _merge_kernel(h_ref, cg_ref, ga_ref, at_ref, wao_ref, wmo_ref, o_ref):
    merged = cg_ref[...] + ga_ref[...] * _dot(at_ref[...], wao_ref[...])
    o_ref[...] = h_ref[...] + _dot(merged.astype(BF16), wmo_ref[...])


def _merge(h, conv_g, gate_a, attn, w_ao, w_mo, tm=512):
    s, d = h.shape
    full = lambda a: pl.BlockSpec(a.shape, lambda i: (0,) * a.ndim)
    row = lambda n: pl.BlockSpec((tm, n), lambda i: (i, 0))
    return pl.pallas_call(
        _merge_kernel,
        grid=(s // tm,),
        in_specs=[row(d), row(d), row(d), row(attn.shape[1]), full(w_ao), full(w_mo)],
        out_specs=row(d),
        out_shape=jax.ShapeDtypeStruct((s, d), F32),
        compiler_params=pltpu.CompilerParams(dimension_semantics=("arbitrary",), vmem_limit_bytes=VMEM_LIMIT),
        name="merge",
    )(h, conv_g, gate_a, attn, w_ao, w_mo)


def _ple_kernel(h_ref, g_ref, p_ref, wg_ref, wp_ref, o_ref):
    h = h_ref[...]
    gate = _sigmoid(_dot(_rms(h, g_ref[...]).astype(BF16), wg_ref[...]))
    o_ref[...] = h + gate * _dot(p_ref[...].astype(BF16), wp_ref[...])


def _ple(h, g, p, w_gate, w_proj, tm=512):
    s, d = h.shape
    full = lambda a: pl.BlockSpec(a.shape, lambda i: (0,) * a.ndim)
    row = lambda n: pl.BlockSpec((tm, n), lambda i: (i, 0))
    g2 = g.reshape(1, d)
    return pl.pallas_call(
        _ple_kernel,
        grid=(s // tm,),
        in_specs=[row(d), full(g2), row(p.shape[1]), full(w_gate), full(w_proj)],
        out_specs=row(d),
        out_shape=jax.ShapeDtypeStruct((s, d), F32),
        compiler_params=pltpu.CompilerParams(dimension_semantics=("arbitrary",), vmem_limit_bytes=VMEM_LIMIT),
        name="ple",
    )(h, g2, p, w_gate, w_proj)


def _t5_bucket_table(n_dist):
    n = np.arange(n_dist)
    max_exact = NUM_BUCKETS // 2
    nf = np.maximum(n, 1).astype(np.float32)
    large = max_exact + (np.log(nf / max_exact) / math.log(MAX_DISTANCE / max_exact)
                         * (NUM_BUCKETS - max_exact)).astype(np.int32)
    large = np.minimum(large, NUM_BUCKETS - 1)
    return np.where(n < max_exact, n, large)


def _bias_tables(rel_bias):
    assert MAX_DISTANCE <= LANES
    bucket = _t5_bucket_table(2 * LANES)
    i = np.arange(LANES)[:, None]
    j = np.arange(LANES)[None, :]
    rel = (rel_bias.astype(F32) - rel_bias[NUM_BUCKETS - 1].astype(F32)[None, :]) * LOG2E
    d0 = rel[bucket[np.maximum(i - j, 0)]]
    d1 = rel[bucket[LANES + i - j]]
    zero = jnp.zeros_like(d0)
    return jnp.transpose(jnp.stack([zero, d0, d1, zero]), (0, 3, 1, 2))


def kernel(x, p, ffn1_norm, ffn1_w_in, ffn1_w_out, mix_norm, mix_w_in, conv_dw_w, conv_dw_b, conv_ln_g,
           conv_ln_b, conv_w_out, q_norm, k_norm, attn_w_out, mix_w_out, ffn2_norm, ffn2_w_in, ffn2_w_out,
           ple_norm, ple_w_gate, ple_w_proj, rel_bias):
    b, s, d = x.shape
    depth = ffn1_norm.shape[0]
    cw = conv_dw_w.shape[2]
    aw = N_HEADS * HEAD_DIM
    nqi = IDX_HEADS * IDX_DIM
    iw = nqi + IDX_DIM + IDX_HEADS
    iw_pad = -(-iw // LANES) * LANES
    k_top = min(TOPK_MAX, s // 4)
    assert b == 1 and s % ATT_TK == 0 and mix_w_in.shape[2] == 2 * cw + 3 * aw + iw + 2 * d

    head_blocks = jnp.asarray(np.kron(np.eye(N_HEADS), np.ones((HEAD_DIM, HEAD_DIM))), BF16)
    bias_qk = _bias_tables(rel_bias)
    bias_kq = jnp.swapaxes(bias_qk, 2, 3)
    heads = lambda t: t.reshape(s, N_HEADS, HEAD_DIM).transpose(1, 0, 2)
    heads_t = lambda t: t.reshape(s, N_HEADS, HEAD_DIM).transpose(1, 2, 0)

    h = x[0]
    for i in range(depth):
        h = _ffn(h, ffn1_norm[i], ffn1_w_in[i].astype(BF16), ffn1_w_out[i].astype(BF16))

        w = mix_w_in[i]
        o0 = 2 * cw
        o1 = o0 + 3 * aw
        o2 = o1 + iw
        w_idx = jnp.pad(w[:, o1:o2], ((0, 0), (0, iw_pad - iw)))
        w_idx_hi, w_idx_lo = _split_bf16(w_idx)
        qg = jnp.tile(q_norm[i], N_HEADS).reshape(1, aw)
        kg = jnp.tile(k_norm[i], N_HEADS).reshape(1, aw)
        glu, q, k, v, idx, gate_c, gate_a, ksq = _mix_in(
            h, mix_norm[i], w[:, :o0].astype(BF16), w[:, o0:o1].astype(BF16), w_idx_hi, w_idx_lo,
            w[:, o2:].astype(BF16), qg, kg, head_blocks)

        conv_g = _conv(glu, conv_dw_w[i], conv_dw_b[i], conv_ln_g[i], conv_ln_b[i],
                       conv_w_out[i].astype(BF16), gate_c)

        qi_hi, qi_lo = _split_bf16(idx[:, :nqi].reshape(s // SEL_TQ, SEL_TQ, IDX_HEADS // 2, 2, IDX_DIM))
        qi_ext = jnp.concatenate([qi_hi, qi_hi, qi_lo, qi_lo], axis=-1)
        qi_pairs = qi_ext.transpose(0, 2, 4, 3, 1).reshape(s // SEL_TQ, IDX_HEADS // 2, 4 * IDX_DIM, 2 * SEL_TQ)
        ki_hi, ki_lo = _split_bf16(idx[:, nqi:nqi + IDX_DIM])
        ki_ext = jnp.concatenate([ki_hi, ki_lo, ki_hi, ki_lo], axis=-1)
        w_t = idx[:, nqi + IDX_DIM:iw].T
        mask = _select(qi_pairs, w_t, ki_ext, k_top)

        def pad_ones(t, axis):
            one = lax.slice_in_dim(t, 0, 1, axis=axis)
            rest = lax.slice_in_dim(t, 0, LANES - HEAD_DIM - 1, axis=axis)
            return jnp.concatenate([t, jnp.ones_like(one), jnp.zeros_like(rest)], axis=axis)

        k_ext = pad_ones(heads(k), 2)
        v_t = pad_ones(heads_t(v), 1)
        kmax = jnp.sqrt(ksq.reshape(N_HEADS, HEAD_DIM)[:, :1])
        kmax_b = jnp.broadcast_to(kmax[:, :, None], (N_HEADS, 1, ATT_TQ))
        attn_t, den = _attn(heads_t(q), kmax_b, k_ext, v_t, mask, bias_kq)
        attn = attn_t.transpose(2, 0, 1).reshape(s, aw)

        def safe_attn():
            mask_qk = jnp.swapaxes(mask, 1, 2)
            v_ext = pad_ones(heads(v), 2)
            return _attn_safe(heads(q), heads(k), v_ext, mask_qk, bias_qk).transpose(1, 0, 2).reshape(s, aw)

        underflow = jnp.logical_not(jnp.min(den) > L_MIN)
        attn = lax.cond(underflow, safe_attn, lambda: attn).astype(BF16)

        h = _merge(h, conv_g, gate_a, attn, attn_w_out[i].astype(BF16), mix_w_out[i].astype(BF16))
        h = _ffn(h, ffn2_norm[i], ffn2_w_in[i].astype(BF16), ffn2_w_out[i].astype(BF16))
        h = _ple(h, ple_norm[i], p[i, 0], ple_w_gate[i].astype(BF16), ple_w_proj[i].astype(BF16))
    return h[None]
```

```python
import functools
import math

import numpy as np
import jax
import jax.numpy as jnp
from jax import lax
from jax.experimental import pallas as pl
from jax.experimental.pallas import tpu as pltpu

F32 = jnp.float32
BF16 = jnp.bfloat16
I32 = jnp.int32

EPS = 1e-6
CONV_WIDTH = 31
N_HEADS = 8
HEAD_DIM = 64
IDX_HEADS = 4
IDX_DIM = 64
TOPK_MAX = 256
NUM_BUCKETS = 32
MAX_DISTANCE = 128

LANES = 128
SUBLANES = 8
VMEM_LIMIT = 56 * 1024 * 1024
NEG = -1e30
INT_MIN = -2 ** 31
MIN_NORMAL_BITS = 0x00800000
LOG2E = math.log2(math.e)

CONV_HALO = 32
SEL_TQ = LANES
SEL_TK = 1024
SEL_GROUPS = 8
SEL_DEPTH = 16
ATT_TQ = 512
ATT_TK = 1024
SAFE_TQ = 256
L_MIN = 1e-30


def _sigmoid(x):
    return 1.0 / (1.0 + jnp.exp(-x))


def _rms(x, g):
    ms = jnp.mean(x * x, axis=-1, keepdims=True)
    return x * lax.rsqrt(ms + EPS) * g


def _dot(a, b):
    return jnp.dot(a, b, preferred_element_type=F32)


def _dot_t(a, b):
    return lax.dot_general(a, b, (((1,), (1,)), ((), ())), preferred_element_type=F32)


def _split_bf16(x):
    hi = x.astype(BF16)
    lo = (x - hi.astype(F32)).astype(BF16)
    return hi, lo


def _ffn_kernel(x_ref, g_ref, wa_ref, wb_ref, wo_ref, o_ref, xn_ref, acc_ref):
    j = pl.program_id(1)

    @pl.when(j == 0)
    def _():
        xn_ref[...] = _rms(x_ref[...], g_ref[...]).astype(BF16)
        acc_ref[...] = jnp.zeros_like(acc_ref)

    xn = xn_ref[...]
    a = _dot(xn, wa_ref[...])
    b = _dot(xn, wb_ref[...])
    hmid = (a * _sigmoid(a) * b).astype(BF16)
    acc_ref[...] += _dot(hmid, wo_ref[...])

    @pl.when(j == pl.num_programs(1) - 1)
    def _():
        o_ref[...] = x_ref[...] + 0.5 * acc_ref[...]


def _ffn(x, g, w_in, w_out, tm=512, fc=1408):
    s, d = x.shape
    dff = w_out.shape[0]
    nj = dff // fc
    assert s % tm == 0 and dff % fc == 0 and fc % LANES == 0
    return pl.pallas_call(
        _ffn_kernel,
        grid=(s // tm, nj),
        in_specs=[
            pl.BlockSpec((tm, d), lambda i, j: (i, 0)),
            pl.BlockSpec((1, d), lambda i, j: (0, 0)),
            pl.BlockSpec((d, fc), lambda i, j: (0, j)),
            pl.BlockSpec((d, fc), lambda i, j: (0, j + nj)),
            pl.BlockSpec((fc, d), lambda i, j: (j, 0)),
        ],
        out_specs=pl.BlockSpec((tm, d), lambda i, j: (i, 0)),
        out_shape=jax.ShapeDtypeStruct((s, d), F32),
        scratch_shapes=[pltpu.VMEM((tm, d), BF16), pltpu.VMEM((tm, d), F32)],
        compiler_params=pltpu.CompilerParams(
            dimension_semantics=("arbitrary", "arbitrary"), vmem_limit_bytes=VMEM_LIMIT),
        name="ffn",
    )(x, g.reshape(1, d), w_in, w_in, w_out)


def _mix_in_kernel(h_ref, g_ref, wc_ref, wqkv_ref, wih_ref, wil_ref, wg_ref, qg_ref, kg_ref, hb_ref,
                   glu_ref, qt_ref, kx_ref, vt_ref, qip_ref, kix_ref, wi_ref, gc_ref, ga_ref, ksq_ref):
    u = _rms(h_ref[...], g_ref[...])
    u_hi, u_lo = _split_bf16(u)
    tm = h_ref.shape[0]
    cw = glu_ref.shape[1]
    aw = N_HEADS * HEAD_DIM
    nqi = IDX_HEADS * IDX_DIM
    d = gc_ref.shape[1]

    c = _dot(u_hi, wc_ref[...])
    glu_ref[...] = c[:, :cw] * _sigmoid(c[:, cw:])

    qkv = _dot(u_hi, wqkv_ref[...])
    hb = hb_ref[...]

    def head_sumsq(t):
        t2_hi, t2_lo = _split_bf16(t * t)
        return _dot(t2_hi, hb) + _dot(t2_lo, hb)

    def head_norm(t, g):
        return t * lax.rsqrt(head_sumsq(t) * (1.0 / HEAD_DIM) + EPS) * g

    q = head_norm(qkv[:, :aw], qg_ref[...]) * (HEAD_DIM ** -0.5 * LOG2E)
    q_t = q.T
    for hd in range(N_HEADS):
        qt_ref[hd] = q_t[hd * HEAD_DIM:(hd + 1) * HEAD_DIM, :].astype(BF16)
    k = head_norm(qkv[:, aw:2 * aw], kg_ref[...]).astype(BF16).astype(F32)
    one_hot0 = lambda shape, axis: jnp.where(lax.broadcasted_iota(I32, shape, axis) == 0, 1.0, 0.0)
    k_pad = one_hot0((tm, LANES - HEAD_DIM), 1)
    for hd in range(N_HEADS):
        kx_ref[hd] = jnp.concatenate([k[:, hd * HEAD_DIM:(hd + 1) * HEAD_DIM], k_pad], axis=1).astype(BF16)
    v_t = qkv[:, 2 * aw:].T
    v_pad = one_hot0((LANES - HEAD_DIM, tm), 0).astype(BF16)
    for hd in range(N_HEADS):
        vt_ref[hd, 0:HEAD_DIM, :] = v_t[hd * HEAD_DIM:(hd + 1) * HEAD_DIM, :].astype(BF16)
        vt_ref[hd, HEAD_DIM:, :] = v_pad

    ksq = jnp.max(head_sumsq(k), axis=0, keepdims=True)

    @pl.when(pl.program_id(0) == 0)
    def _():
        ksq_ref[...] = ksq

    @pl.when(pl.program_id(0) > 0)
    def _():
        ksq_ref[...] = jnp.maximum(ksq_ref[...], ksq)

    wih = wih_ref[...]
    idx = _dot(u_hi, wih) + _dot(u_lo, wih) + _dot(u_hi, wil_ref[...])
    qi_hi, qi_lo = _split_bf16(idx[:, :nqi].T)
    for blk in range(tm // SEL_TQ):
        cols = slice(blk * SEL_TQ, (blk + 1) * SEL_TQ)
        for hd in range(IDX_HEADS):
            rows = slice(hd * IDX_DIM, (hd + 1) * IDX_DIM)
            out_cols = slice((hd % 2) * SEL_TQ, (hd % 2 + 1) * SEL_TQ)
            for part, src in enumerate((qi_hi, qi_hi, qi_lo, qi_lo)):
                qip_ref[blk, hd // 2, part * IDX_DIM:(part + 1) * IDX_DIM, out_cols] = src[rows, cols]
    ki = idx[:, nqi:nqi + IDX_DIM]
    ki_hi = ki.astype(BF16).astype(F32)
    ki_lo = ki - ki_hi
    kix_ref[...] = jnp.concatenate([ki_hi, ki_lo, ki_hi, ki_lo], axis=1).astype(BF16)
    wi_ref[...] = idx[:, nqi:]

    gates = _sigmoid(_dot(u_hi, wg_ref[...]))
    gc_ref[...] = gates[:, :d]
    ga_ref[...] = gates[:, d:]


def _mix_in(h, g, w_conv, w_qkv, w_idx_hi, w_idx_lo, w_gate, qg, kg, head_blocks, tm=512):
    s, d = h.shape
    cw = w_conv.shape[1] // 2
    aw = w_qkv.shape[1] // 3
    nqi = IDX_HEADS * IDX_DIM
    iw = w_idx_hi.shape[1]
    assert aw == N_HEADS * HEAD_DIM and tm % SEL_TQ == 0 and iw - nqi == LANES
    full = lambda a: pl.BlockSpec(a.shape, lambda i: (0,) * a.ndim)
    row = lambda n: pl.BlockSpec((tm, n), lambda i: (i, 0))
    g2 = g.reshape(1, d)
    return pl.pallas_call(
        _mix_in_kernel,
        grid=(s // tm,),
        in_specs=[row(d), full(g2), full(w_conv), full(w_qkv), full(w_idx_hi), full(w_idx_lo), full(w_gate),
                  full(qg), full(kg), full(head_blocks)],
        out_specs=[
            row(cw),
            pl.BlockSpec((N_HEADS, HEAD_DIM, tm), lambda i: (0, 0, i)),
            pl.BlockSpec((N_HEADS, tm, LANES), lambda i: (0, i, 0)),
            pl.BlockSpec((N_HEADS, LANES, tm), lambda i: (0, 0, i)),
            pl.BlockSpec((tm // SEL_TQ, IDX_HEADS // 2, 4 * IDX_DIM, 2 * SEL_TQ), lambda i: (i, 0, 0, 0)),
            row(4 * IDX_DIM), row(LANES), row(d), row(d),
            pl.BlockSpec((1, aw), lambda i: (0, 0)),
        ],
        out_shape=[
            jax.ShapeDtypeStruct((s, cw), F32),
            jax.ShapeDtypeStruct((N_HEADS, HEAD_DIM, s), BF16),
            jax.ShapeDtypeStruct((N_HEADS, s, LANES), BF16),
            jax.ShapeDtypeStruct((N_HEADS, LANES, s), BF16),
            jax.ShapeDtypeStruct((s // SEL_TQ, IDX_HEADS // 2, 4 * IDX_DIM, 2 * SEL_TQ), BF16),
            jax.ShapeDtypeStruct((s, 4 * IDX_DIM), BF16),
            jax.ShapeDtypeStruct((s, LANES), F32),
            jax.ShapeDtypeStruct((s, d), F32),
            jax.ShapeDtypeStruct((s, d), F32),
            jax.ShapeDtypeStruct((1, aw), F32),
        ],
        compiler_params=pltpu.CompilerParams(dimension_semantics=("arbitrary",), vmem_limit_bytes=VMEM_LIMIT),
        name="mix_in",
    )(h, g2, w_conv, w_qkv, w_idx_hi, w_idx_lo, w_gate, qg, kg, head_blocks)


def _conv_kernel(z_ref, halo_ref, dw_ref, db_ref, lg_ref, lb_ref, wo_ref, gc_ref, o_ref, zp_ref):
    tm = z_ref.shape[0]
    first = pl.program_id(0) == 0
    halo = halo_ref[...]
    zp_ref[0:CONV_HALO, :] = jnp.where(first, jnp.zeros_like(halo), halo)
    zp_ref[CONV_HALO:, :] = z_ref[...]
    off = CONV_HALO - (CONV_WIDTH - 1)
    acc = jnp.zeros(z_ref.shape, F32) + db_ref[...]
    for j in range(CONV_WIDTH):
        acc = acc + dw_ref[j:j + 1, :] * zp_ref[off + j:off + j + tm, :]
    mu = jnp.mean(acc, axis=-1, keepdims=True)
    xc = acc - mu
    y = xc * lax.rsqrt(jnp.mean(xc * xc, axis=-1, keepdims=True) + EPS)
    y = y * lg_ref[...] + lb_ref[...]
    y = (y * _sigmoid(y)).astype(BF16)
    o_ref[...] = gc_ref[...] * _dot(y, wo_ref[...])


def _conv(z, dw_w, dw_b, ln_g, ln_b, w_out, gate_c, tm=256):
    s, c = z.shape
    d = w_out.shape[1]
    assert tm % CONV_HALO == 0
    r = tm // CONV_HALO
    full = lambda a: pl.BlockSpec(a.shape, lambda i: (0,) * a.ndim)
    vecs = [dw_b.reshape(1, c), ln_g.reshape(1, c), ln_b.reshape(1, c)]
    return pl.pallas_call(
        _conv_kernel,
        grid=(s // tm,),
        in_specs=[
            pl.BlockSpec((tm, c), lambda i: (i, 0)),
            pl.BlockSpec((CONV_HALO, c), lambda i: (jnp.maximum(i * r - 1, 0), 0)),
            full(dw_w), full(vecs[0]), full(vecs[1]), full(vecs[2]), full(w_out),
            pl.BlockSpec((tm, d), lambda i: (i, 0)),
        ],
        out_specs=pl.BlockSpec((tm, d), lambda i: (i, 0)),
        out_shape=jax.ShapeDtypeStruct((s, d), F32),
        scratch_shapes=[pltpu.VMEM((tm + CONV_HALO, c), F32)],
        compiler_params=pltpu.CompilerParams(dimension_semantics=("arbitrary",), vmem_limit_bytes=VMEM_LIMIT),
        name="conv",
    )(z, z, dw_w, *vecs, w_out, gate_c)


def _sortable_key(x):
    bits = pltpu.bitcast(x + 0.0, I32)
    return bits ^ ((bits >> 31) & 0x7FFFFFFF)


def _key_to_float(key):
    return pltpu.bitcast(key ^ ((key >> 31) & 0x7FFFFFFF), F32)


def _sort_network(n):
    pairs = []

    def merge(lo, hi, r):
        step = r * 2
        if step < hi - lo:
            merge(lo, hi, step)
            merge(lo + r, hi, step)
            pairs.extend((i, i + r) for i in range(lo + r, hi - r, step))
        else:
            pairs.append((lo, lo + r))

    def sort(lo, hi):
        if hi > lo:
            mid = lo + (hi - lo) // 2
            sort(lo, mid)
            sort(mid + 1, hi)
            merge(lo, hi, 1)

    sort(0, n - 1)
    return pairs


def _compare_exchange(a, i, j):
    a[i], a[j] = jnp.maximum(a[i], a[j]), jnp.minimum(a[i], a[j])


def _merge_top(top, batch):
    n = len(top)
    out = [jnp.maximum(top[i], batch[n - 1 - i]) for i in range(n)]
    d = n // 2
    while d >= 1:
        for i in range(n):
            if i & d == 0:
                _compare_exchange(out, i, i + d)
        d //= 2
    return out


def _select_kernel(qi_ref, w_ref, ki_ref, mask_ref, sc_ref, cand_ref, ckey_ref, st_ref, *, k_top):
    qb = pl.program_id(0)
    n_rows = sc_ref.shape[0]
    n_tiles = n_rows // SEL_TK
    q0 = qb * SEL_TQ
    nkt = (q0 + SEL_TQ - 1) // SEL_TK + 1
    vt = SEL_TK // SUBLANES
    vshape = (SUBLANES, LANES)
    qpos = q0 + lax.broadcasted_iota(I32, vshape, 1)
    sub = lax.broadcasted_iota(I32, vshape, 0)
    w = w_ref[...] * ((IDX_HEADS ** -0.5) * (IDX_DIM ** -0.5))

    qpos_t = q0 + lax.broadcasted_iota(I32, (SEL_TK, LANES), 1)
    krow_t = lax.broadcasted_iota(I32, (SEL_TK, LANES), 0)
    zero_bits_t = (MIN_NORMAL_BITS + n_rows) - krow_t

    def score_tile(kt, causal_edge):
        k0 = pl.multiple_of(kt * SEL_TK, SEL_TK)
        ki = ki_ref[pl.ds(k0, SEL_TK), :]
        sc = None
        for pair in range(IDX_HEADS // 2):
            r = _dot(ki, qi_ref[0, pair])
            for j in range(2):
                h = 2 * pair + j
                term = w[h:h + 1, :] * jnp.maximum(r[:, j * LANES:(j + 1) * LANES], 0.0)
                sc = term if sc is None else sc + term
        sc = jnp.where(sc == 0.0, pltpu.bitcast(zero_bits_t - k0, F32), sc)
        if causal_edge:
            sc = jnp.where(k0 + krow_t <= qpos_t, sc, -jnp.inf)
        sc_ref[pl.ds(k0, SEL_TK), :] = sc

    def full_score_tile(kt, carry):
        score_tile(kt, False)
        return carry

    lax.fori_loop(0, nkt - 1, full_score_tile, 0)
    score_tile(nkt - 1, True)

    def vreg(ref, row):
        return ref[pl.ds(pl.multiple_of(row, SUBLANES), SUBLANES), :]

    def col_sum(parts):
        tot = parts[0]
        for part in parts[1:]:
            tot = tot + part
        return jnp.broadcast_to(jnp.sum(tot, axis=0, keepdims=True), vshape)

    n_cand = SEL_GROUPS * SEL_DEPTH
    network = _sort_network(SEL_DEPTH)
    cand_ref[...] = jnp.full(cand_ref.shape, -jnp.inf, F32)

    def lists_tile(kt, carry):
        for g in range(SEL_GROUPS):
            batch = [vreg(sc_ref, kt * SEL_TK + (g + SEL_GROUPS * j) * SUBLANES) for j in range(SEL_DEPTH)]
            for i, j in network:
                _compare_exchange(batch, i, j)
            rows = [slice((g * SEL_DEPTH + i) * SUBLANES, (g * SEL_DEPTH + i + 1) * SUBLANES)
                    for i in range(SEL_DEPTH)]
            top = _merge_top([cand_ref[r, :] for r in rows], batch)
            for r, t in zip(rows, top):
                cand_ref[r, :] = t
        return carry

    lax.fori_loop(0, nkt, lists_tile, 0)
    ckey_ref[...] = _sortable_key(cand_ref[...])

    n_acc = 4

    def cand_bit(i, t):
        c = t ^ lax.shift_left(jnp.int32(1), 31 - i)
        acc = [jnp.zeros(vshape, I32) for _ in range(n_acc)]
        for v in range(n_cand):
            x = ckey_ref[v * SUBLANES:(v + 1) * SUBLANES, :]
            acc[v % n_acc] = acc[v % n_acc] + jnp.where(x >= c, 1, 0)
        return jnp.where(col_sum(acc) >= k_top, c, t)

    st_ref[0] = lax.fori_loop(0, 32, cand_bit, jnp.full(vshape, INT_MIN, I32))

    def count_scores(preds):
        def tile(kt, acc):
            acc = [list(a) for a in acc]
            for v in range(vt):
                x = vreg(sc_ref, kt * SEL_TK + v * SUBLANES)
                kpos = kt * SEL_TK + v * SUBLANES + sub
                for p, pred in enumerate(preds):
                    acc[p][v % n_acc] = acc[p][v % n_acc] + jnp.where(pred(x, kpos), 1, 0)
            return tuple(tuple(a) for a in acc)

        zero = tuple(tuple(jnp.zeros(vshape, I32) for _ in range(n_acc)) for _ in preds)
        return [col_sum(list(a)) for a in lax.fori_loop(0, nkt, tile, zero)]

    def count_around_threshold():
        tf = _key_to_float(st_ref[0])
        gt, ge = count_scores([lambda x, kpos: x > tf, lambda x, kpos: x >= tf])
        st_ref[1] = gt
        st_ref[2] = ge

    tf_fast = jnp.broadcast_to(_key_to_float(st_ref[0])[0:1, :], (SEL_TK, LANES))

    def fast_tile(kt, acc):
        k0 = pl.multiple_of(kt * SEL_TK, SEL_TK)
        m = jnp.where(sc_ref[pl.ds(k0, SEL_TK), :] >= tf_fast, 1.0, 0.0)
        mask_ref[0, pl.ds(k0, SEL_TK), :] = m.astype(BF16)
        acc = list(acc)
        for v in range(vt):
            acc[v % n_acc] = acc[v % n_acc] + m[v * SUBLANES:(v + 1) * SUBLANES, :]
        return tuple(acc)

    size = col_sum(list(lax.fori_loop(0, nkt, fast_tile, tuple(jnp.zeros(vshape, F32) for _ in range(n_acc)))))

    @pl.when(jnp.max(jnp.abs(size - k_top)) > 0.0)
    def _():
        count_around_threshold()

        @pl.when(jnp.max(st_ref[1]) >= k_top)
        def _():
            def full_bit(i, t):
                c = t ^ lax.shift_left(jnp.int32(1), 31 - i)
                cnt, = count_scores([lambda x, kpos: _sortable_key(x) >= c])
                return jnp.where(cnt >= k_top, c, t)

            st_ref[0] = lax.fori_loop(0, 32, full_bit, jnp.full(vshape, INT_MIN, I32))
            count_around_threshold()

        st_ref[3] = jnp.full(vshape, n_rows, I32)

        @pl.when(jnp.max(st_ref[2]) > k_top)
        def _():
            tf = _key_to_float(st_ref[0])
            need = k_top - st_ref[1]
            n_bits = n_rows.bit_length() - 1

            def pos_bit(i, p):
                c = p + lax.shift_left(jnp.int32(1), n_bits - 1 - i)
                f, = count_scores([lambda x, kpos: (x == tf) & (kpos < c)])
                return jnp.where(f < need, c, p)

            st_ref[3] = lax.fori_loop(0, n_bits, pos_bit, jnp.zeros(vshape, I32))

        tf_t = jnp.broadcast_to(_key_to_float(st_ref[0])[0:1, :], (SEL_TK, LANES))
        lim_t = jnp.broadcast_to(st_ref[3][0:1, :], (SEL_TK, LANES))

        def write_tile(kt, carry):
            k0 = pl.multiple_of(kt * SEL_TK, SEL_TK)
            x = sc_ref[pl.ds(k0, SEL_TK), :]
            kpos = k0 + krow_t
            sel = ((x > tf_t) | ((x == tf_t) & (kpos <= lim_t))) & (kpos <= qpos_t)
            mask_ref[0, pl.ds(k0, SEL_TK), :] = jnp.where(sel, 1.0, 0.0).astype(BF16)
            return carry

        lax.fori_loop(0, nkt, write_tile, 0)

    def fill_tile(kt, carry):
        mask_ref[0, pl.ds(pl.multiple_of(kt * SEL_TK, SEL_TK), SEL_TK), :] = jnp.zeros((SEL_TK, LANES), BF16)
        return carry

    lax.fori_loop(nkt, n_tiles, fill_tile, 0)


def _select(qi_pairs, w_t, ki_ext, k_top):
    nqb, npair, kw, _ = qi_pairs.shape
    s = ki_ext.shape[0]
    n_cand_rows = SEL_GROUPS * SEL_DEPTH * SUBLANES
    assert s % SEL_TK == 0 and SEL_TK == n_cand_rows and n_cand_rows >= k_top
    return pl.pallas_call(
        functools.partial(_select_kernel, k_top=k_top),
        grid=(nqb,),
        in_specs=[
            pl.BlockSpec((1, npair, kw, 2 * SEL_TQ), lambda i: (i, 0, 0, 0)),
            pl.BlockSpec((w_t.shape[0], SEL_TQ), lambda i: (0, i)),
            pl.BlockSpec((s, kw), lambda i: (0, 0)),
        ],
        out_specs=pl.BlockSpec((1, s, SEL_TQ), lambda i: (i, 0, 0)),
        out_shape=jax.ShapeDtypeStruct((nqb, s, SEL_TQ), BF16),
        scratch_shapes=[
            pltpu.VMEM((s, SEL_TQ), F32),
            pltpu.VMEM((n_cand_rows, SEL_TQ), F32),
            pltpu.VMEM((n_cand_rows, SEL_TQ), I32),
            pltpu.VMEM((4, SUBLANES, LANES), I32),
        ],
        compiler_params=pltpu.CompilerParams(dimension_semantics=("arbitrary",), vmem_limit_bytes=VMEM_LIMIT),
        name="select",
    )(qi_pairs, w_t, ki_ext)


def _causal_pairs(s, tq, tk):
    pairs = [(qb, kt) for qb in range(s // tq) for kt in range((qb * tq + tq - 1) // tk + 1)]
    return jnp.asarray([p[0] for p in pairs], I32), jnp.asarray([p[1] for p in pairs], I32)


def _attn_kernel(qb_tab, kt_tab, qt_ref, kmax_ref, k_ref, vt_ref, mask_ref, bias_ref, o_ref, l_ref, qx_ref, acc_ref):
    step = pl.program_id(0)
    qb = qb_tab[step]
    kt = kt_tab[step]
    q0 = qb * ATT_TQ
    k0 = kt * ATT_TK
    nq = ATT_TQ // LANES
    nk = ATT_TK // LANES

    @pl.when(kt == 0)
    def _():
        row = lax.broadcasted_iota(I32, (HEAD_DIM, ATT_TQ), 0)
        for h in range(N_HEADS):
            q = qt_ref[h]
            qf = q.astype(F32)
            bound = jnp.sqrt(jnp.sum(qf * qf, axis=0, keepdims=True)) * kmax_ref[h] * 1.02
            qx_ref[h, 0:HEAD_DIM, :] = q
            qx_ref[h, HEAD_DIM:, :] = jnp.where(row == 0, -bound, 0.0).astype(BF16)
        acc_ref[...] = jnp.zeros(acc_ref.shape, F32)

    mask = jnp.concatenate([mask_ref[a] for a in range(nq)], axis=1)

    def heads(with_bias):
        for h in range(N_HEADS):
            s = _dot(k_ref[h], qx_ref[h])
            if with_bias:
                rows = []
                for c in range(nk):
                    tab = [jnp.clip((q0 + a * LANES - k0 - c * LANES) // LANES + 1, 0, 3) for a in range(nq)]
                    rows.append(jnp.concatenate([bias_ref[tab[a], h] for a in range(nq)], axis=1))
                s = s + jnp.concatenate(rows, axis=0)
            p = jnp.exp2(s).astype(BF16) * mask
            acc_ref[h] += _dot(vt_ref[h], p)

    near = k0 + ATT_TK + 2 * LANES > q0

    @pl.when(near)
    def _():
        heads(True)

    @pl.when(jnp.logical_not(near))
    def _():
        heads(False)

    @pl.when(kt == (q0 + ATT_TQ - 1) // ATT_TK)
    def _():
        for h in range(N_HEADS):
            acc = acc_ref[h]
            den = acc[HEAD_DIM:HEAD_DIM + 1, :]
            o_ref[h] = acc[:HEAD_DIM, :] / den
            l_ref[h] = den


def _attn(q_t, kmax_b, k_ext, v_t, mask, bias_tab):
    nh, hd, s = q_t.shape
    qb_tab, kt_tab = _causal_pairs(s, ATT_TQ, ATT_TK)
    grid_spec = pltpu.PrefetchScalarGridSpec(
        num_scalar_prefetch=2,
        grid=(qb_tab.shape[0],),
        in_specs=[
            pl.BlockSpec((nh, hd, ATT_TQ), lambda i, qb, kt: (0, 0, qb[i])),
            pl.BlockSpec((nh, 1, ATT_TQ), lambda i, qb, kt: (0, 0, 0)),
            pl.BlockSpec((nh, ATT_TK, LANES), lambda i, qb, kt: (0, kt[i], 0)),
            pl.BlockSpec((nh, LANES, ATT_TK), lambda i, qb, kt: (0, 0, kt[i])),
            pl.BlockSpec((ATT_TQ // LANES, ATT_TK, LANES), lambda i, qb, kt: (qb[i], kt[i], 0)),
            pl.BlockSpec(bias_tab.shape, lambda i, qb, kt: (0, 0, 0, 0)),
        ],
        out_specs=[
            pl.BlockSpec((nh, hd, ATT_TQ), lambda i, qb, kt: (0, 0, qb[i])),
            pl.BlockSpec((nh, 1, ATT_TQ), lambda i, qb, kt: (0, 0, qb[i])),
        ],
        scratch_shapes=[pltpu.VMEM((nh, LANES, ATT_TQ), BF16), pltpu.VMEM((nh, LANES, ATT_TQ), F32)],
    )
    return pl.pallas_call(
        _attn_kernel,
        grid_spec=grid_spec,
        out_shape=[jax.ShapeDtypeStruct((nh, hd, s), F32), jax.ShapeDtypeStruct((nh, 1, s), F32)],
        compiler_params=pltpu.CompilerParams(dimension_semantics=("arbitrary",), vmem_limit_bytes=VMEM_LIMIT),
        name="attn",
    )(qb_tab, kt_tab, q_t, kmax_b, k_ext, v_t, mask, bias_tab)


def _attn_safe_kernel(qb_tab, kt_tab, q_ref, k_ref, v_ref, mask_ref, bias_ref, o_ref, m_ref, acc_ref):
    step = pl.program_id(0)
    qb = qb_tab[step]
    kt = kt_tab[step]
    nq = SAFE_TQ // LANES
    nsub = ATT_TK // LANES

    @pl.when(kt == 0)
    def _():
        m_ref[...] = jnp.full(m_ref.shape, NEG, F32)
        acc_ref[...] = jnp.zeros(acc_ref.shape, F32)

    for a in range(nq):
        rows = slice(a * LANES, (a + 1) * LANES)
        q0 = qb * SAFE_TQ + a * LANES
        maskf = (mask_ref[a].astype(F32) - 1.0) * (-NEG)
        tab_idx = [jnp.clip((q0 - (kt * ATT_TK + c * LANES)) // LANES + 1, 0, 3) for c in range(nsub)]
        for h in range(N_HEADS):
            s = _dot_t(q_ref[h, rows, :], k_ref[h]) + maskf
            s = s + jnp.concatenate([bias_ref[tab_idx[c], h] for c in range(nsub)], axis=1)
            m_old = m_ref[h, rows, :]
            m_new = jnp.maximum(m_old, jnp.max(s, axis=1, keepdims=True))
            p = jnp.exp2(s - m_new[:, 0:1])
            alpha = jnp.exp2(m_old - m_new)
            acc_ref[h, rows, :] = alpha * acc_ref[h, rows, :] + _dot(p.astype(BF16), v_ref[h])
            m_ref[h, rows, :] = m_new

    @pl.when(kt == (qb * SAFE_TQ + SAFE_TQ - 1) // ATT_TK)
    def _():
        for h in range(N_HEADS):
            acc = acc_ref[h]
            o_ref[h] = acc[:, :HEAD_DIM] / acc[:, HEAD_DIM:HEAD_DIM + 1]


def _attn_safe(q, k, v_ext, mask_qk, bias_tab):
    nh, s, hd = q.shape
    qb_tab, kt_tab = _causal_pairs(s, SAFE_TQ, ATT_TK)
    grid_spec = pltpu.PrefetchScalarGridSpec(
        num_scalar_prefetch=2,
        grid=(qb_tab.shape[0],),
        in_specs=[
            pl.BlockSpec((nh, SAFE_TQ, hd), lambda i, qb, kt: (0, qb[i], 0)),
            pl.BlockSpec((nh, ATT_TK, hd), lambda i, qb, kt: (0, kt[i], 0)),
            pl.BlockSpec((nh, ATT_TK, LANES), lambda i, qb, kt: (0, kt[i], 0)),
            pl.BlockSpec((SAFE_TQ // LANES, LANES, ATT_TK), lambda i, qb, kt: (qb[i], 0, kt[i])),
            pl.BlockSpec(bias_tab.shape, lambda i, qb, kt: (0, 0, 0, 0)),
        ],
        out_specs=pl.BlockSpec((nh, SAFE_TQ, hd), lambda i, qb, kt: (0, qb[i], 0)),
        scratch_shapes=[pltpu.VMEM((nh, SAFE_TQ, LANES), F32), pltpu.VMEM((nh, SAFE_TQ, LANES), F32)],
    )
    return pl.pallas_call(
        _attn_safe_kernel,
        grid_spec=grid_spec,
        out_shape=jax.ShapeDtypeStruct((nh, s, hd), F32),
        compiler_params=pltpu.CompilerParams(dimension_semantics=("arbitrary",), vmem_limit_bytes=VMEM_LIMIT),
        name="attn_safe",
    )(qb_tab, kt_tab, q, k, v_ext, mask_qk, bias_tab)


def _merge_kernel(h_ref, cg_ref, ga_ref, at_ref, wao_ref, wmo_ref, o_ref):
    attn = at_ref[...].T.astype(BF16)
    merged = cg_ref[...] + ga_ref[...] * _dot(attn, wao_ref[...])
    o_ref[...] = h_ref[...] + _dot(merged.astype(BF16), wmo_ref[...])


def _merge(h, conv_g, gate_a, attn_t, w_ao, w_mo, tm=512):
    s, d = h.shape
    full = lambda a: pl.BlockSpec(a.shape, lambda i: (0,) * a.ndim)
    row = lambda n: pl.BlockSpec((tm, n), lambda i: (i, 0))
    return pl.pallas_call(
        _merge_kernel,
        grid=(s // tm,),
        in_specs=[row(d), row(d), row(d), pl.BlockSpec((attn_t.shape[0], tm), lambda i: (0, i)),
                  full(w_ao), full(w_mo)],
        out_specs=row(d),
        out_shape=jax.ShapeDtypeStruct((s, d), F32),
        compiler_params=pltpu.CompilerParams(dimension_semantics=("arbitrary",), vmem_limit_bytes=VMEM_LIMIT),
        name="merge",
    )(h, conv_g, gate_a, attn_t, w_ao, w_mo)


def _ple_kernel(h_ref, g_ref, p_ref, wg_ref, wp_ref, o_ref):
    h = h_ref[...]
    gate = _sigmoid(_dot(_rms(h, g_ref[...]).astype(BF16), wg_ref[...]))
    o_ref[...] = h + gate * _dot(p_ref[...].astype(BF16), wp_ref[...])


def _ple(h, g, p, w_gate, w_proj, tm=512):
    s, d = h.shape
    full = lambda a: pl.BlockSpec(a.shape, lambda i: (0,) * a.ndim)
    row = lambda n: pl.BlockSpec((tm, n), lambda i: (i, 0))
    g2 = g.reshape(1, d)
    return pl.pallas_call(
        _ple_kernel,
        grid=(s // tm,),
        in_specs=[row(d), full(g2), row(p.shape[1]), full(w_gate), full(w_proj)],
        out_specs=row(d),
        out_shape=jax.ShapeDtypeStruct((s, d), F32),
        compiler_params=pltpu.CompilerParams(dimension_semantics=("arbitrary",), vmem_limit_bytes=VMEM_LIMIT),
        name="ple",
    )(h, g2, p, w_gate, w_proj)


def _t5_bucket_table(n_dist):
    n = np.arange(n_dist)
    max_exact = NUM_BUCKETS // 2
    nf = np.maximum(n, 1).astype(np.float32)
    large = max_exact + (np.log(nf / max_exact) / math.log(MAX_DISTANCE / max_exact)
                         * (NUM_BUCKETS - max_exact)).astype(np.int32)
    large = np.minimum(large, NUM_BUCKETS - 1)
    return np.where(n < max_exact, n, large)


def _bias_tables(rel_bias):
    assert MAX_DISTANCE <= LANES
    n = LANES
    rel = (rel_bias.astype(F32) - rel_bias[NUM_BUCKETS - 1].astype(F32)[None, :]) * LOG2E
    by_dist = rel[_t5_bucket_table(2 * n)].T

    def toeplitz(a):
        skew = jnp.tile(a, (1, n))[:, :n * (2 * n - 1)].reshape(a.shape[0], n, 2 * n - 1)
        return skew[:, :, :n]

    d0 = toeplitz(jnp.concatenate([by_dist[:, :n], jnp.broadcast_to(by_dist[:, :1], by_dist[:, :n].shape)], axis=1))
    d1 = toeplitz(jnp.concatenate([by_dist[:, n:], by_dist[:, :n]], axis=1))
    zero = jnp.zeros_like(d0)
    return jnp.swapaxes(jnp.stack([zero, d0, d1, zero]), 2, 3)


def kernel(x, p, ffn1_norm, ffn1_w_in, ffn1_w_out, mix_norm, mix_w_in, conv_dw_w, conv_dw_b, conv_ln_g,
           conv_ln_b, conv_w_out, q_norm, k_norm, attn_w_out, mix_w_out, ffn2_norm, ffn2_w_in, ffn2_w_out,
           ple_norm, ple_w_gate, ple_w_proj, rel_bias):
    b, s, d = x.shape
    depth = ffn1_norm.shape[0]
    cw = conv_dw_w.shape[2]
    aw = N_HEADS * HEAD_DIM
    nqi = IDX_HEADS * IDX_DIM
    iw = nqi + IDX_DIM + IDX_HEADS
    iw_pad = -(-iw // LANES) * LANES
    k_top = min(TOPK_MAX, s // 4)
    assert b == 1 and s % ATT_TK == 0 and mix_w_in.shape[2] == 2 * cw + 3 * aw + iw + 2 * d

    head_blocks = jnp.asarray(np.kron(np.eye(N_HEADS), np.ones((HEAD_DIM, HEAD_DIM))), BF16)
    bias_qk = _bias_tables(rel_bias)
    bias_kq = jnp.swapaxes(bias_qk, 2, 3)

    h = x[0]
    for i in range(depth):
        h = _ffn(h, ffn1_norm[i], ffn1_w_in[i].astype(BF16), ffn1_w_out[i].astype(BF16))

        w = mix_w_in[i]
        o0 = 2 * cw
        o1 = o0 + 3 * aw
        o2 = o1 + iw
        w_idx = jnp.pad(w[:, o1:o2], ((0, 0), (0, iw_pad - iw)))
        w_idx_hi, w_idx_lo = _split_bf16(w_idx)
        qg = jnp.tile(q_norm[i], N_HEADS).reshape(1, aw)
        kg = jnp.tile(k_norm[i], N_HEADS).reshape(1, aw)
        glu, q_t, k_ext, v_t, qi_pairs, ki_ext, idx_w, gate_c, gate_a, ksq = _mix_in(
            h, mix_norm[i], w[:, :o0].astype(BF16), w[:, o0:o1].astype(BF16), w_idx_hi, w_idx_lo,
            w[:, o2:].astype(BF16), qg, kg, head_blocks)

        conv_g = _conv(glu, conv_dw_w[i], conv_dw_b[i], conv_ln_g[i], conv_ln_b[i],
                       conv_w_out[i].astype(BF16), gate_c)

        w_t = idx_w[:, IDX_DIM:IDX_DIM + IDX_HEADS].T
        mask = _select(qi_pairs, w_t, ki_ext, k_top)

        kmax = jnp.sqrt(ksq.reshape(N_HEADS, HEAD_DIM)[:, :1])
        kmax_b = jnp.broadcast_to(kmax[:, :, None], (N_HEADS, 1, ATT_TQ))
        attn_t, den = _attn(q_t, kmax_b, k_ext, v_t, mask, bias_kq)

        def safe_attn():
            out = _attn_safe(jnp.swapaxes(q_t, 1, 2), k_ext[:, :, :HEAD_DIM], jnp.swapaxes(v_t, 1, 2),
                             jnp.swapaxes(mask, 1, 2), bias_qk)
            return jnp.swapaxes(out, 1, 2)

        underflow = jnp.logical_not(jnp.min(den) > L_MIN)
        attn_t = lax.cond(underflow, safe_attn, lambda: attn_t).reshape(aw, s)

        h = _merge(h, conv_g, gate_a, attn_t, attn_w_out[i].astype(BF16), mix_w_out[i].astype(BF16))
        h = _ffn(h, ffn2_norm[i], ffn2_w_in[i].astype(BF16), ffn2_w_out[i].astype(BF16))
        h = _ple(h, ple_norm[i], p[i, 0], ple_w_gate[i].astype(BF16), ple_w_proj[i].astype(BF16))
    return h[None]
```

```python
import functools
import math

import numpy as np
import jax
import jax.numpy as jnp
from jax import lax
from jax.experimental import pallas as pl
from jax.experimental.pallas import tpu as pltpu

F32 = jnp.float32
BF16 = jnp.bfloat16
I32 = jnp.int32

EPS = 1e-6
CONV_WIDTH = 31
N_HEADS = 8
HEAD_DIM = 64
IDX_HEADS = 4
IDX_DIM = 64
TOPK_MAX = 256
NUM_BUCKETS = 32
MAX_DISTANCE = 128

LANES = 128
SUBLANES = 8
VMEM_LIMIT = 56 * 1024 * 1024
NEG = -1e30
INT_MIN = -2 ** 31
MIN_NORMAL_BITS = 0x00800000
LOG2E = math.log2(math.e)

CONV_HALO = 32
CONV_CHUNK = 32
SEL_TQ = LANES
SEL_TK = 1024
SEL_GROUPS = 8
SEL_DEPTH = 16
ATT_TQ = 512
ATT_TK = 1024
SAFE_TQ = 256
L_MIN = 1e-30


def _sigmoid(x):
    return 1.0 / (1.0 + jnp.exp(-x))


def _rms(x, g):
    ms = jnp.mean(x * x, axis=-1, keepdims=True)
    return x * lax.rsqrt(ms + EPS) * g


def _dot(a, b):
    return jnp.dot(a, b, preferred_element_type=F32)


def _dot_t(a, b):
    return lax.dot_general(a, b, (((1,), (1,)), ((), ())), preferred_element_type=F32)


def _split_bf16(x):
    hi = x.astype(BF16)
    lo = (x - hi.astype(F32)).astype(BF16)
    return hi, lo


def _ffn_kernel(x_ref, g_ref, wa_ref, wb_ref, wo_ref, o_ref, xn_ref, acc_ref):
    j = pl.program_id(1)

    @pl.when(j == 0)
    def _():
        xn_ref[...] = _rms(x_ref[...], g_ref[...]).astype(BF16)
        acc_ref[...] = jnp.zeros_like(acc_ref)

    xn = xn_ref[...]
    a = _dot(xn, wa_ref[...])
    b = _dot(xn, wb_ref[...])
    hmid = (a * _sigmoid(a) * b).astype(BF16)
    acc_ref[...] += _dot(hmid, wo_ref[...])

    @pl.when(j == pl.num_programs(1) - 1)
    def _():
        o_ref[...] = x_ref[...] + 0.5 * acc_ref[...]


def _ffn(x, g, w_in, w_out, tm=512, fc=1408):
    s, d = x.shape
    dff = w_out.shape[0]
    nj = dff // fc
    assert s % tm == 0 and dff % fc == 0 and fc % LANES == 0
    return pl.pallas_call(
        _ffn_kernel,
        grid=(s // tm, nj),
        in_specs=[
            pl.BlockSpec((tm, d), lambda i, j: (i, 0)),
            pl.BlockSpec((1, d), lambda i, j: (0, 0)),
            pl.BlockSpec((d, fc), lambda i, j: (0, j)),
            pl.BlockSpec((d, fc), lambda i, j: (0, j + nj)),
            pl.BlockSpec((fc, d), lambda i, j: (j, 0)),
        ],
        out_specs=pl.BlockSpec((tm, d), lambda i, j: (i, 0)),
        out_shape=jax.ShapeDtypeStruct((s, d), F32),
        scratch_shapes=[pltpu.VMEM((tm, d), BF16), pltpu.VMEM((tm, d), F32)],
        compiler_params=pltpu.CompilerParams(
            dimension_semantics=("arbitrary", "arbitrary"), vmem_limit_bytes=VMEM_LIMIT),
        name="ffn",
    )(x, g.reshape(1, d), w_in, w_in, w_out)


def _mix_in_kernel(h_ref, g_ref, wc_ref, wqkv_ref, wih_ref, wil_ref, wg_ref, qg_ref, kg_ref, hb_ref,
                   glu_ref, qt_ref, kx_ref, vt_ref, qip_ref, kix_ref, wi_ref, gc_ref, ga_ref, ksq_ref):
    u = _rms(h_ref[...], g_ref[...])
    u_hi, u_lo = _split_bf16(u)
    tm = h_ref.shape[0]
    cw = glu_ref.shape[1]
    aw = N_HEADS * HEAD_DIM
    nqi = IDX_HEADS * IDX_DIM
    d = gc_ref.shape[1]

    c = _dot(u_hi, wc_ref[...])
    glu_ref[...] = c[:, :cw] * _sigmoid(c[:, cw:])

    qkv = _dot(u_hi, wqkv_ref[...])
    hb = hb_ref[...]

    def head_sumsq(t):
        t2_hi, t2_lo = _split_bf16(t * t)
        return _dot(t2_hi, hb) + _dot(t2_lo, hb)

    def head_norm(t, g):
        return t * lax.rsqrt(head_sumsq(t) * (1.0 / HEAD_DIM) + EPS) * g

    q = head_norm(qkv[:, :aw], qg_ref[...]) * (HEAD_DIM ** -0.5 * LOG2E)
    q_t = q.T
    for hd in range(N_HEADS):
        qt_ref[hd] = q_t[hd * HEAD_DIM:(hd + 1) * HEAD_DIM, :].astype(BF16)
    k = head_norm(qkv[:, aw:2 * aw], kg_ref[...]).astype(BF16).astype(F32)
    one_hot0 = lambda shape, axis: jnp.where(lax.broadcasted_iota(I32, shape, axis) == 0, 1.0, 0.0)
    k_pad = one_hot0((tm, LANES - HEAD_DIM), 1)
    for hd in range(N_HEADS):
        kx_ref[hd] = jnp.concatenate([k[:, hd * HEAD_DIM:(hd + 1) * HEAD_DIM], k_pad], axis=1).astype(BF16)
    v_t = qkv[:, 2 * aw:].T
    v_pad = one_hot0((LANES - HEAD_DIM, tm), 0).astype(BF16)
    for hd in range(N_HEADS):
        vt_ref[hd, 0:HEAD_DIM, :] = v_t[hd * HEAD_DIM:(hd + 1) * HEAD_DIM, :].astype(BF16)
        vt_ref[hd, HEAD_DIM:, :] = v_pad

    ksq = jnp.max(head_sumsq(k), axis=0, keepdims=True)

    @pl.when(pl.program_id(0) == 0)
    def _():
        ksq_ref[...] = ksq

    @pl.when(pl.program_id(0) > 0)
    def _():
        ksq_ref[...] = jnp.maximum(ksq_ref[...], ksq)

    wih = wih_ref[...]
    idx = _dot(u_hi, wih) + _dot(u_lo, wih) + _dot(u_hi, wil_ref[...])
    qi_hi, qi_lo = _split_bf16(idx[:, :nqi].T)
    for blk in range(tm // SEL_TQ):
        cols = slice(blk * SEL_TQ, (blk + 1) * SEL_TQ)
        for hd in range(IDX_HEADS):
            rows = slice(hd * IDX_DIM, (hd + 1) * IDX_DIM)
            out_cols = slice((hd % 2) * SEL_TQ, (hd % 2 + 1) * SEL_TQ)
            for part, src in enumerate((qi_hi, qi_hi, qi_lo, qi_lo)):
                qip_ref[blk, hd // 2, part * IDX_DIM:(part + 1) * IDX_DIM, out_cols] = src[rows, cols]
    ki = idx[:, nqi:nqi + IDX_DIM]
    ki_hi = ki.astype(BF16).astype(F32)
    ki_lo = ki - ki_hi
    kix_ref[...] = jnp.concatenate([ki_hi, ki_lo, ki_hi, ki_lo], axis=1).astype(BF16)
    wi_ref[...] = idx[:, nqi:]

    gates = _sigmoid(_dot(u_hi, wg_ref[...]))
    gc_ref[...] = gates[:, :d]
    ga_ref[...] = gates[:, d:]


def _mix_in(h, g, w_conv, w_qkv, w_idx_hi, w_idx_lo, w_gate, qg, kg, head_blocks, tm=512):
    s, d = h.shape
    cw = w_conv.shape[1] // 2
    aw = w_qkv.shape[1] // 3
    nqi = IDX_HEADS * IDX_DIM
    iw = w_idx_hi.shape[1]
    assert aw == N_HEADS * HEAD_DIM and tm % SEL_TQ == 0 and iw - nqi == LANES
    full = lambda a: pl.BlockSpec(a.shape, lambda i: (0,) * a.ndim)
    row = lambda n: pl.BlockSpec((tm, n), lambda i: (i, 0))
    g2 = g.reshape(1, d)
    return pl.pallas_call(
        _mix_in_kernel,
        grid=(s // tm,),
        in_specs=[row(d), full(g2), full(w_conv), full(w_qkv), full(w_idx_hi), full(w_idx_lo), full(w_gate),
                  full(qg), full(kg), full(head_blocks)],
        out_specs=[
            row(cw),
            pl.BlockSpec((N_HEADS, HEAD_DIM, tm), lambda i: (0, 0, i)),
            pl.BlockSpec((N_HEADS, tm, LANES), lambda i: (0, i, 0)),
            pl.BlockSpec((N_HEADS, LANES, tm), lambda i: (0, 0, i)),
            pl.BlockSpec((tm // SEL_TQ, IDX_HEADS // 2, 4 * IDX_DIM, 2 * SEL_TQ), lambda i: (i, 0, 0, 0)),
            row(4 * IDX_DIM), row(LANES), row(d), row(d),
            pl.BlockSpec((1, aw), lambda i: (0, 0)),
        ],
        out_shape=[
            jax.ShapeDtypeStruct((s, cw), F32),
            jax.ShapeDtypeStruct((N_HEADS, HEAD_DIM, s), BF16),
            jax.ShapeDtypeStruct((N_HEADS, s, LANES), BF16),
            jax.ShapeDtypeStruct((N_HEADS, LANES, s), BF16),
            jax.ShapeDtypeStruct((s // SEL_TQ, IDX_HEADS // 2, 4 * IDX_DIM, 2 * SEL_TQ), BF16),
            jax.ShapeDtypeStruct((s, 4 * IDX_DIM), BF16),
            jax.ShapeDtypeStruct((s, LANES), F32),
            jax.ShapeDtypeStruct((s, d), F32),
            jax.ShapeDtypeStruct((s, d), F32),
            jax.ShapeDtypeStruct((1, aw), F32),
        ],
        compiler_params=pltpu.CompilerParams(dimension_semantics=("arbitrary",), vmem_limit_bytes=VMEM_LIMIT),
        name="mix_in",
    )(h, g2, w_conv, w_qkv, w_idx_hi, w_idx_lo, w_gate, qg, kg, head_blocks)


def _conv_kernel(z_ref, halo_ref, dw_ref, db_ref, lg_ref, lb_ref, wo_ref, gc_ref, o_ref, zp_ref, zs_ref, acc_ref):
    tm = z_ref.shape[0]
    first = pl.program_id(0) == 0
    halo = halo_ref[...]
    zp_ref[0:CONV_HALO, :] = jnp.where(first, jnp.zeros_like(halo), halo)
    zp_ref[CONV_HALO:, :] = z_ref[...]
    off = CONV_HALO - (CONV_WIDTH - 1)
    span = zs_ref.shape[1]
    for b in range(1, SUBLANES):
        zs_ref[b - 1] = zp_ref[b:b + span, :]

    def chunk(c, carry):
        r0 = pl.multiple_of(c * CONV_CHUNK, CONV_CHUNK)
        acc = jnp.zeros((CONV_CHUNK, z_ref.shape[1]), F32) + db_ref[...]
        for j in range(CONV_WIDTH):
            a, b = divmod(off + j, SUBLANES)
            src = zp_ref if b == 0 else zs_ref.at[b - 1]
            acc = acc + dw_ref[j:j + 1, :] * src[pl.ds(r0 + a * SUBLANES, CONV_CHUNK), :]
        acc_ref[pl.ds(r0, CONV_CHUNK), :] = acc
        return carry

    lax.fori_loop(0, tm // CONV_CHUNK, chunk, 0)
    acc = acc_ref[...]
    mu = jnp.mean(acc, axis=-1, keepdims=True)
    xc = acc - mu
    y = xc * lax.rsqrt(jnp.mean(xc * xc, axis=-1, keepdims=True) + EPS)
    y = y * lg_ref[...] + lb_ref[...]
    y = (y * _sigmoid(y)).astype(BF16)
    o_ref[...] = gc_ref[...] * _dot(y, wo_ref[...])


def _conv(z, dw_w, dw_b, ln_g, ln_b, w_out, gate_c, tm=256):
    s, c = z.shape
    d = w_out.shape[1]
    assert tm % CONV_HALO == 0 and tm % CONV_CHUNK == 0
    r = tm // CONV_HALO
    span = tm + CONV_HALO - SUBLANES
    full = lambda a: pl.BlockSpec(a.shape, lambda i: (0,) * a.ndim)
    vecs = [dw_b.reshape(1, c), ln_g.reshape(1, c), ln_b.reshape(1, c)]
    return pl.pallas_call(
        _conv_kernel,
        grid=(s // tm,),
        in_specs=[
            pl.BlockSpec((tm, c), lambda i: (i, 0)),
            pl.BlockSpec((CONV_HALO, c), lambda i: (jnp.maximum(i * r - 1, 0), 0)),
            full(dw_w), full(vecs[0]), full(vecs[1]), full(vecs[2]), full(w_out),
            pl.BlockSpec((tm, d), lambda i: (i, 0)),
        ],
        out_specs=pl.BlockSpec((tm, d), lambda i: (i, 0)),
        out_shape=jax.ShapeDtypeStruct((s, d), F32),
        scratch_shapes=[pltpu.VMEM((tm + CONV_HALO, c), F32), pltpu.VMEM((SUBLANES - 1, span, c), F32),
                        pltpu.VMEM((tm, c), F32)],
        compiler_params=pltpu.CompilerParams(dimension_semantics=("arbitrary",), vmem_limit_bytes=VMEM_LIMIT),
        name="conv",
    )(z, z, dw_w, *vecs, w_out, gate_c)


def _sortable_key(x):
    bits = pltpu.bitcast(x + 0.0, I32)
    return bits ^ ((bits >> 31) & 0x7FFFFFFF)


def _key_to_float(key):
    return pltpu.bitcast(key ^ ((key >> 31) & 0x7FFFFFFF), F32)


def _sort_network(n):
    pairs = []

    def merge(lo, hi, r):
        step = r * 2
        if step < hi - lo:
            merge(lo, hi, step)
            merge(lo + r, hi, step)
            pairs.extend((i, i + r) for i in range(lo + r, hi - r, step))
        else:
            pairs.append((lo, lo + r))

    def sort(lo, hi):
        if hi > lo:
            mid = lo + (hi - lo) // 2
            sort(lo, mid)
            sort(mid + 1, hi)
            merge(lo, hi, 1)

    sort(0, n - 1)
    return pairs


def _compare_exchange(a, i, j):
    a[i], a[j] = jnp.maximum(a[i], a[j]), jnp.minimum(a[i], a[j])


def _merge_top(top, batch):
    n = len(top)
    out = [jnp.maximum(top[i], batch[n - 1 - i]) for i in range(n)]
    d = n // 2
    while d >= 1:
        for i in range(n):
            if i & d == 0:
                _compare_exchange(out, i, i + d)
        d //= 2
    return out


def _select_kernel(qi_ref, w_ref, ki_ref, mask_ref, sc_ref, cand_ref, ckey_ref, st_ref, *, k_top):
    qb = pl.program_id(0)
    n_rows = mask_ref.shape[1]
    n_tiles = n_rows // SEL_TK
    q0 = qb * SEL_TQ
    nkt = (q0 + SEL_TQ - 1) // SEL_TK + 1
    vt = SEL_TK // SUBLANES
    vshape = (SUBLANES, LANES)
    qpos = q0 + lax.broadcasted_iota(I32, vshape, 1)
    sub = lax.broadcasted_iota(I32, vshape, 0)
    w = w_ref[...] * ((IDX_HEADS ** -0.5) * (IDX_DIM ** -0.5))

    qpos_t = q0 + lax.broadcasted_iota(I32, (SEL_TK, LANES), 1)
    krow_t = lax.broadcasted_iota(I32, (SEL_TK, LANES), 0)
    zero_bits_t = (MIN_NORMAL_BITS + n_rows) - krow_t

    def score_tile(kt):
        k0 = pl.multiple_of(kt * SEL_TK, SEL_TK)
        ki = ki_ref[pl.ds(k0, SEL_TK), :]
        sc = None
        for pair in range(IDX_HEADS // 2):
            r = _dot(ki, qi_ref[0, pair])
            for j in range(2):
                h = 2 * pair + j
                term = w[h:h + 1, :] * jnp.maximum(r[:, j * LANES:(j + 1) * LANES], 0.0)
                sc = term if sc is None else sc + term
        sc_ref[pl.ds(k0, SEL_TK), :] = jnp.where(sc == 0.0, pltpu.bitcast(zero_bits_t - k0, F32), sc)

    def vreg(ref, row):
        return ref[pl.ds(pl.multiple_of(row, SUBLANES), SUBLANES), :]

    def col_sum(parts):
        tot = parts[0]
        for part in parts[1:]:
            tot = tot + part
        return jnp.broadcast_to(jnp.sum(tot, axis=0, keepdims=True), vshape)

    n_cand = SEL_GROUPS * SEL_DEPTH
    network = _sort_network(SEL_DEPTH)
    cand_ref[...] = jnp.full(cand_ref.shape, -jnp.inf, F32)

    def lists_tile(kt):
        for g in range(SEL_GROUPS):
            batch = [vreg(sc_ref, kt * SEL_TK + (g + SEL_GROUPS * j) * SUBLANES) for j in range(SEL_DEPTH)]
            for i, j in network:
                _compare_exchange(batch, i, j)
            rows = [slice((g * SEL_DEPTH + i) * SUBLANES, (g * SEL_DEPTH + i + 1) * SUBLANES)
                    for i in range(SEL_DEPTH)]
            top = _merge_top([cand_ref[r, :] for r in rows], batch)
            for r, t in zip(rows, top):
                cand_ref[r, :] = t

    sc_ref[pl.ds(n_rows, SEL_TK), :] = jnp.full((SEL_TK, LANES), -jnp.inf, F32)

    def score_and_lists(kt, carry):
        lists_tile(jnp.where(kt == 0, n_tiles, kt - 1))
        score_tile(kt)
        return carry

    lax.fori_loop(0, nkt, score_and_lists, 0)
    k_last = pl.multiple_of((nkt - 1) * SEL_TK, SEL_TK)
    sc_ref[pl.ds(k_last, SEL_TK), :] = jnp.where(
        k_last + krow_t <= qpos_t, sc_ref[pl.ds(k_last, SEL_TK), :], -jnp.inf)
    lists_tile(nkt - 1)
    ckey_ref[...] = _sortable_key(cand_ref[...])

    n_acc = 4

    def cand_bit(i, t):
        c = t ^ lax.shift_left(jnp.int32(1), 31 - i)
        acc = [jnp.zeros(vshape, I32) for _ in range(n_acc)]
        for v in range(n_cand):
            x = ckey_ref[v * SUBLANES:(v + 1) * SUBLANES, :]
            acc[v % n_acc] = acc[v % n_acc] + jnp.where(x >= c, 1, 0)
        return jnp.where(col_sum(acc) >= k_top, c, t)

    st_ref[0] = lax.fori_loop(0, 32, cand_bit, jnp.full(vshape, INT_MIN, I32))

    def count_scores(preds):
        def tile(kt, acc):
            acc = [list(a) for a in acc]
            for v in range(vt):
                x = vreg(sc_ref, kt * SEL_TK + v * SUBLANES)
                kpos = kt * SEL_TK + v * SUBLANES + sub
                for p, pred in enumerate(preds):
                    acc[p][v % n_acc] = acc[p][v % n_acc] + jnp.where(pred(x, kpos), 1, 0)
            return tuple(tuple(a) for a in acc)

        zero = tuple(tuple(jnp.zeros(vshape, I32) for _ in range(n_acc)) for _ in preds)
        return [col_sum(list(a)) for a in lax.fori_loop(0, nkt, tile, zero)]

    def count_around_threshold():
        tf = _key_to_float(st_ref[0])
        gt, ge = count_scores([lambda x, kpos: x > tf, lambda x, kpos: x >= tf])
        st_ref[1] = gt
        st_ref[2] = ge

    tf_fast = jnp.broadcast_to(_key_to_float(st_ref[0])[0:1, :], (SEL_TK, LANES))

    def fast_tile(kt, acc):
        k0 = pl.multiple_of(kt * SEL_TK, SEL_TK)
        m = jnp.where(sc_ref[pl.ds(k0, SEL_TK), :] >= tf_fast, 1.0, 0.0)
        mask_ref[0, pl.ds(k0, SEL_TK), :] = m.astype(BF16)
        acc = list(acc)
        for v in range(vt):
            acc[v % n_acc] = acc[v % n_acc] + m[v * SUBLANES:(v + 1) * SUBLANES, :]
        return tuple(acc)

    size = col_sum(list(lax.fori_loop(0, nkt, fast_tile, tuple(jnp.zeros(vshape, F32) for _ in range(n_acc)))))

    @pl.when(jnp.max(jnp.abs(size - k_top)) > 0.0)
    def _():
        count_around_threshold()

        @pl.when(jnp.max(st_ref[1]) >= k_top)
        def _():
            def full_bit(i, t):
                c = t ^ lax.shift_left(jnp.int32(1), 31 - i)
                cnt, = count_scores([lambda x, kpos: _sortable_key(x) >= c])
                return jnp.where(cnt >= k_top, c, t)

            st_ref[0] = lax.fori_loop(0, 32, full_bit, jnp.full(vshape, INT_MIN, I32))
            count_around_threshold()

        st_ref[3] = jnp.full(vshape, n_rows, I32)

        @pl.when(jnp.max(st_ref[2]) > k_top)
        def _():
            tf = _key_to_float(st_ref[0])
            need = k_top - st_ref[1]
            n_bits = n_rows.bit_length() - 1

            def pos_bit(i, p):
                c = p + lax.shift_left(jnp.int32(1), n_bits - 1 - i)
                f, = count_scores([lambda x, kpos: (x == tf) & (kpos < c)])
                return jnp.where(f < need, c, p)

            st_ref[3] = lax.fori_loop(0, n_bits, pos_bit, jnp.zeros(vshape, I32))

        tf_t = jnp.broadcast_to(_key_to_float(st_ref[0])[0:1, :], (SEL_TK, LANES))
        lim_t = jnp.broadcast_to(st_ref[3][0:1, :], (SEL_TK, LANES))

        def write_tile(kt, carry):
            k0 = pl.multiple_of(kt * SEL_TK, SEL_TK)
            x = sc_ref[pl.ds(k0, SEL_TK), :]
            kpos = k0 + krow_t
            sel = ((x > tf_t) | ((x == tf_t) & (kpos <= lim_t))) & (kpos <= qpos_t)
            mask_ref[0, pl.ds(k0, SEL_TK), :] = jnp.where(sel, 1.0, 0.0).astype(BF16)
            return carry

        lax.fori_loop(0, nkt, write_tile, 0)

    def fill_tile(kt, carry):
        mask_ref[0, pl.ds(pl.multiple_of(kt * SEL_TK, SEL_TK), SEL_TK), :] = jnp.zeros((SEL_TK, LANES), BF16)
        return carry

    lax.fori_loop(nkt, n_tiles, fill_tile, 0)


def _select(qi_pairs, w_t, ki_ext, k_top):
    nqb, npair, kw, _ = qi_pairs.shape
    s = ki_ext.shape[0]
    n_cand_rows = SEL_GROUPS * SEL_DEPTH * SUBLANES
    assert s % SEL_TK == 0 and SEL_TK == n_cand_rows and n_cand_rows >= k_top
    return pl.pallas_call(
        functools.partial(_select_kernel, k_top=k_top),
        grid=(nqb,),
        in_specs=[
            pl.BlockSpec((1, npair, kw, 2 * SEL_TQ), lambda i: (i, 0, 0, 0)),
            pl.BlockSpec((w_t.shape[0], SEL_TQ), lambda i: (0, i)),
            pl.BlockSpec((s, kw), lambda i: (0, 0)),
        ],
        out_specs=pl.BlockSpec((1, s, SEL_TQ), lambda i: (i, 0, 0)),
        out_shape=jax.ShapeDtypeStruct((nqb, s, SEL_TQ), BF16),
        scratch_shapes=[
            pltpu.VMEM((s + SEL_TK, SEL_TQ), F32),
            pltpu.VMEM((n_cand_rows, SEL_TQ), F32),
            pltpu.VMEM((n_cand_rows, SEL_TQ), I32),
            pltpu.VMEM((4, SUBLANES, LANES), I32),
        ],
        compiler_params=pltpu.CompilerParams(dimension_semantics=("arbitrary",), vmem_limit_bytes=VMEM_LIMIT),
        name="select",
    )(qi_pairs, w_t, ki_ext)


def _causal_pairs(s, tq, tk):
    pairs = [(qb, kt) for qb in range(s // tq) for kt in range((qb * tq + tq - 1) // tk + 1)]
    return jnp.asarray([p[0] for p in pairs], I32), jnp.asarray([p[1] for p in pairs], I32)


def _attn_kernel(qb_tab, kt_tab, qt_ref, kmax_ref, k_ref, vt_ref, mask_ref, bias_ref, o_ref, l_ref, qx_ref, acc_ref):
    step = pl.program_id(0)
    qb = qb_tab[step]
    kt = kt_tab[step]
    q0 = qb * ATT_TQ
    k0 = kt * ATT_TK
    nq = ATT_TQ // LANES
    nk = ATT_TK // LANES

    @pl.when(kt == 0)
    def _():
        row = lax.broadcasted_iota(I32, (HEAD_DIM, ATT_TQ), 0)
        for h in range(N_HEADS):
            q = qt_ref[h]
            qf = q.astype(F32)
            bound = jnp.sqrt(jnp.sum(qf * qf, axis=0, keepdims=True)) * kmax_ref[h] * 1.02
            qx_ref[h, 0:HEAD_DIM, :] = q
            qx_ref[h, HEAD_DIM:, :] = jnp.where(row == 0, -bound, 0.0).astype(BF16)
        acc_ref[...] = jnp.zeros(acc_ref.shape, F32)

    mask = jnp.concatenate([mask_ref[a] for a in range(nq)], axis=1)

    def heads(with_bias):
        for h in range(N_HEADS):
            s = _dot(k_ref[h], qx_ref[h])
            if with_bias:
                rows = []
                for c in range(nk):
                    tab = [jnp.clip((q0 + a * LANES - k0 - c * LANES) // LANES + 1, 0, 3) for a in range(nq)]
                    rows.append(jnp.concatenate([bias_ref[tab[a], h] for a in range(nq)], axis=1))
                s = s + jnp.concatenate(rows, axis=0)
            p = jnp.exp2(s).astype(BF16) * mask
            acc_ref[h] += _dot(vt_ref[h], p)

    near = k0 + ATT_TK + 2 * LANES > q0

    @pl.when(near)
    def _():
        heads(True)

    @pl.when(jnp.logical_not(near))
    def _():
        heads(False)

    @pl.when(kt == (q0 + ATT_TQ - 1) // ATT_TK)
    def _():
        for h in range(N_HEADS):
            acc = acc_ref[h]
            den = acc[HEAD_DIM:HEAD_DIM + 1, :]
            o_ref[h] = acc[:HEAD_DIM, :] / den
            l_ref[h] = den


def _attn(q_t, kmax_b, k_ext, v_t, mask, bias_tab):
    nh, hd, s = q_t.shape
    qb_tab, kt_tab = _causal_pairs(s, ATT_TQ, ATT_TK)
    grid_spec = pltpu.PrefetchScalarGridSpec(
        num_scalar_prefetch=2,
        grid=(qb_tab.shape[0],),
        in_specs=[
            pl.BlockSpec((nh, hd, ATT_TQ), lambda i, qb, kt: (0, 0, qb[i])),
            pl.BlockSpec((nh, 1, ATT_TQ), lambda i, qb, kt: (0, 0, 0)),
            pl.BlockSpec((nh, ATT_TK, LANES), lambda i, qb, kt: (0, kt[i], 0)),
            pl.BlockSpec((nh, LANES, ATT_TK), lambda i, qb, kt: (0, 0, kt[i])),
            pl.BlockSpec((ATT_TQ // LANES, ATT_TK, LANES), lambda i, qb, kt: (qb[i], kt[i], 0)),
            pl.BlockSpec(bias_tab.shape, lambda i, qb, kt: (0, 0, 0, 0)),
        ],
        out_specs=[
            pl.BlockSpec((nh, hd, ATT_TQ), lambda i, qb, kt: (0, 0, qb[i])),
            pl.BlockSpec((nh, 1, ATT_TQ), lambda i, qb, kt: (0, 0, qb[i])),
        ],
        scratch_shapes=[pltpu.VMEM((nh, LANES, ATT_TQ), BF16), pltpu.VMEM((nh, LANES, ATT_TQ), F32)],
    )
    return pl.pallas_call(
        _attn_kernel,
        grid_spec=grid_spec,
        out_shape=[jax.ShapeDtypeStruct((nh, hd, s), F32), jax.ShapeDtypeStruct((nh, 1, s), F32)],
        compiler_params=pltpu.CompilerParams(dimension_semantics=("arbitrary",), vmem_limit_bytes=VMEM_LIMIT),
        name="attn",
    )(qb_tab, kt_tab, q_t, kmax_b, k_ext, v_t, mask, bias_tab)


def _attn_safe_kernel(qb_tab, kt_tab, q_ref, k_ref, v_ref, mask_ref, bias_ref, o_ref, m_ref, acc_ref):
    step = pl.program_id(0)
    qb = qb_tab[step]
    kt = kt_tab[step]
    nq = SAFE_TQ // LANES
    nsub = ATT_TK // LANES

    @pl.when(kt == 0)
    def _():
        m_ref[...] = jnp.full(m_ref.shape, NEG, F32)
        acc_ref[...] = jnp.zeros(acc_ref.shape, F32)

    for a in range(nq):
        rows = slice(a * LANES, (a + 1) * LANES)
        q0 = qb * SAFE_TQ + a * LANES
        maskf = (mask_ref[a].astype(F32) - 1.0) * (-NEG)
        tab_idx = [jnp.clip((q0 - (kt * ATT_TK + c * LANES)) // LANES + 1, 0, 3) for c in range(nsub)]
        for h in range(N_HEADS):
            s = _dot_t(q_ref[h, rows, :], k_ref[h]) + maskf
            s = s + jnp.concatenate([bias_ref[tab_idx[c], h] for c in range(nsub)], axis=1)
            m_old = m_ref[h, rows, :]
            m_new = jnp.maximum(m_old, jnp.max(s, axis=1, keepdims=True))
            p = jnp.exp2(s - m_new[:, 0:1])
            alpha = jnp.exp2(m_old - m_new)
            acc_ref[h, rows, :] = alpha * acc_ref[h, rows, :] + _dot(p.astype(BF16), v_ref[h])
            m_ref[h, rows, :] = m_new

    @pl.when(kt == (qb * SAFE_TQ + SAFE_TQ - 1) // ATT_TK)
    def _():
        for h in range(N_HEADS):
            acc = acc_ref[h]
            o_ref[h] = acc[:, :HEAD_DIM] / acc[:, HEAD_DIM:HEAD_DIM + 1]


def _attn_safe(q, k, v_ext, mask_qk, bias_tab):
    nh, s, hd = q.shape
    qb_tab, kt_tab = _causal_pairs(s, SAFE_TQ, ATT_TK)
    grid_spec = pltpu.PrefetchScalarGridSpec(
        num_scalar_prefetch=2,
        grid=(qb_tab.shape[0],),
        in_specs=[
            pl.BlockSpec((nh, SAFE_TQ, hd), lambda i, qb, kt: (0, qb[i], 0)),
            pl.BlockSpec((nh, ATT_TK, hd), lambda i, qb, kt: (0, kt[i], 0)),
            pl.BlockSpec((nh, ATT_TK, LANES), lambda i, qb, kt: (0, kt[i], 0)),
            pl.BlockSpec((SAFE_TQ // LANES, LANES, ATT_TK), lambda i, qb, kt: (qb[i], 0, kt[i])),
            pl.BlockSpec(bias_tab.shape, lambda i, qb, kt: (0, 0, 0, 0)),
        ],
        out_specs=pl.BlockSpec((nh, SAFE_TQ, hd), lambda i, qb, kt: (0, qb[i], 0)),
        scratch_shapes=[pltpu.VMEM((nh, SAFE_TQ, LANES), F32), pltpu.VMEM((nh, SAFE_TQ, LANES), F32)],
    )
    return pl.pallas_call(
        _attn_safe_kernel,
        grid_spec=grid_spec,
        out_shape=jax.ShapeDtypeStruct((nh, s, hd), F32),
        compiler_params=pltpu.CompilerParams(dimension_semantics=("arbitrary",), vmem_limit_bytes=VMEM_LIMIT),
        name="attn_safe",
    )(qb_tab, kt_tab, q, k, v_ext, mask_qk, bias_tab)


def _merge_kernel(h_ref, cg_ref, ga_ref, at_ref, wao_ref, wmo_ref, o_ref):
    attn = at_ref[...].T.astype(BF16)
    merged = cg_ref[...] + ga_ref[...] * _dot(attn, wao_ref[...])
    o_ref[...] = h_ref[...] + _dot(merged.astype(BF16), wmo_ref[...])


def _merge(h, conv_g, gate_a, attn_t, w_ao, w_mo, tm=512):
    s, d = h.shape
    full = lambda a: pl.BlockSpec(a.shape, lambda i: (0,) * a.ndim)
    row = lambda n: pl.BlockSpec((tm, n), lambda i: (i, 0))
    return pl.pallas_call(
        _merge_kernel,
        grid=(s // tm,),
        in_specs=[row(d), row(d), row(d), pl.BlockSpec((attn_t.shape[0], tm), lambda i: (0, i)),
                  full(w_ao), full(w_mo)],
        out_specs=row(d),
        out_shape=jax.ShapeDtypeStruct((s, d), F32),
        compiler_params=pltpu.CompilerParams(dimension_semantics=("arbitrary",), vmem_limit_bytes=VMEM_LIMIT),
        name="merge",
    )(h, conv_g, gate_a, attn_t, w_ao, w_mo)


def _ple_kernel(h_ref, g_ref, p_ref, wg_ref, wp_ref, o_ref):
    h = h_ref[...]
    gate = _sigmoid(_dot(_rms(h, g_ref[...]).astype(BF16), wg_ref[...]))
    o_ref[...] = h + gate * _dot(p_ref[...].astype(BF16), wp_ref[...])


def _ple(h, g, p, w_gate, w_proj, tm=512):
    s, d = h.shape
    full = lambda a: pl.BlockSpec(a.shape, lambda i: (0,) * a.ndim)
    row = lambda n: pl.BlockSpec((tm, n), lambda i: (i, 0))
    g2 = g.reshape(1, d)
    return pl.pallas_call(
        _ple_kernel,
        grid=(s // tm,),
        in_specs=[row(d), full(g2), row(p.shape[1]), full(w_gate), full(w_proj)],
        out_specs=row(d),
        out_shape=jax.ShapeDtypeStruct((s, d), F32),
        compiler_params=pltpu.CompilerParams(dimension_semantics=("arbitrary",), vmem_limit_bytes=VMEM_LIMIT),
        name="ple",
    )(h, g2, p, w_gate, w_proj)


def _t5_bucket_table(n_dist):
    n = np.arange(n_dist)
    max_exact = NUM_BUCKETS // 2
    nf = np.maximum(n, 1).astype(np.float32)
    large = max_exact + (np.log(nf / max_exact) / math.log(MAX_DISTANCE / max_exact)
                         * (NUM_BUCKETS - max_exact)).astype(np.int32)
    large = np.minimum(large, NUM_BUCKETS - 1)
    return np.where(n < max_exact, n, large)


def _bias_tables(rel_bias):
    assert MAX_DISTANCE <= LANES
    n = LANES
    rel = (rel_bias.astype(F32) - rel_bias[NUM_BUCKETS - 1].astype(F32)[None, :]) * LOG2E
    by_dist = rel[_t5_bucket_table(2 * n)].T

    def toeplitz(a):
        skew = jnp.tile(a, (1, n))[:, :n * (2 * n - 1)].reshape(a.shape[0], n, 2 * n - 1)
        return skew[:, :, :n]

    d0 = toeplitz(jnp.concatenate([by_dist[:, :n], jnp.broadcast_to(by_dist[:, :1], by_dist[:, :n].shape)], axis=1))
    d1 = toeplitz(jnp.concatenate([by_dist[:, n:], by_dist[:, :n]], axis=1))
    zero = jnp.zeros_like(d0)
    return jnp.swapaxes(jnp.stack([zero, d0, d1, zero]), 2, 3)


def kernel(x, p, ffn1_norm, ffn1_w_in, ffn1_w_out, mix_norm, mix_w_in, conv_dw_w, conv_dw_b, conv_ln_g,
           conv_ln_b, conv_w_out, q_norm, k_norm, attn_w_out, mix_w_out, ffn2_norm, ffn2_w_in, ffn2_w_out,
           ple_norm, ple_w_gate, ple_w_proj, rel_bias):
    b, s, d = x.shape
    depth = ffn1_norm.shape[0]
    cw = conv_dw_w.shape[2]
    aw = N_HEADS * HEAD_DIM
    nqi = IDX_HEADS * IDX_DIM
    iw = nqi + IDX_DIM + IDX_HEADS
    iw_pad = -(-iw // LANES) * LANES
    k_top = min(TOPK_MAX, s // 4)
    assert b == 1 and s % ATT_TK == 0 and mix_w_in.shape[2] == 2 * cw + 3 * aw + iw + 2 * d

    head_blocks = jnp.asarray(np.kron(np.eye(N_HEADS), np.ones((HEAD_DIM, HEAD_DIM))), BF16)
    bias_qk = _bias_tables(rel_bias)
    bias_kq = jnp.swapaxes(bias_qk, 2, 3)

    h = x[0]
    for i in range(depth):
        h = _ffn(h, ffn1_norm[i], ffn1_w_in[i].astype(BF16), ffn1_w_out[i].astype(BF16))

        w = mix_w_in[i]
        o0 = 2 * cw
        o1 = o0 + 3 * aw
        o2 = o1 + iw
        w_idx = jnp.pad(w[:, o1:o2], ((0, 0), (0, iw_pad - iw)))
        w_idx_hi, w_idx_lo = _split_bf16(w_idx)
        qg = jnp.tile(q_norm[i], N_HEADS).reshape(1, aw)
        kg = jnp.tile(k_norm[i], N_HEADS).reshape(1, aw)
        glu, q_t, k_ext, v_t, qi_pairs, ki_ext, idx_w, gate_c, gate_a, ksq = _mix_in(
            h, mix_norm[i], w[:, :o0].astype(BF16), w[:, o0:o1].astype(BF16), w_idx_hi, w_idx_lo,
            w[:, o2:].astype(BF16), qg, kg, head_blocks)

        conv_g = _conv(glu, conv_dw_w[i], conv_dw_b[i], conv_ln_g[i], conv_ln_b[i],
                       conv_w_out[i].astype(BF16), gate_c)

        w_t = idx_w[:, IDX_DIM:IDX_DIM + IDX_HEADS].T
        mask = _select(qi_pairs, w_t, ki_ext, k_top)

        kmax = jnp.sqrt(ksq.reshape(N_HEADS, HEAD_DIM)[:, :1])
        kmax_b = jnp.broadcast_to(kmax[:, :, None], (N_HEADS, 1, ATT_TQ))
        attn_t, den = _attn(q_t, kmax_b, k_ext, v_t, mask, bias_kq)

        def safe_attn():
            out = _attn_safe(jnp.swapaxes(q_t, 1, 2), k_ext[:, :, :HEAD_DIM], jnp.swapaxes(v_t, 1, 2),
                             jnp.swapaxes(mask, 1, 2), bias_qk)
            return jnp.swapaxes(out, 1, 2)

        underflow = jnp.logical_not(jnp.min(den) > L_MIN)
        attn_t = lax.cond(underflow, safe_attn, lambda: attn_t).reshape(aw, s)

        h = _merge(h, conv_g, gate_a, attn_t, attn_w_out[i].astype(BF16), mix_w_out[i].astype(BF16))
        h = _ffn(h, ffn2_norm[i], ffn2_w_in[i].astype(BF16), ffn2_w_out[i].astype(BF16))
        h = _ple(h, ple_norm[i], p[i, 0], ple_w_gate[i].astype(BF16), ple_w_proj[i].astype(BF16))
    return h[None]
```

```python
import functools
import math

import numpy as np
import jax
import jax.numpy as jnp
from jax import lax
from jax.experimental import pallas as pl
from jax.experimental.pallas import tpu as pltpu

F32 = jnp.float32
BF16 = jnp.bfloat16
I32 = jnp.int32

EPS = 1e-6
CONV_WIDTH = 31
N_HEADS = 8
HEAD_DIM = 64
IDX_HEADS = 4
IDX_DIM = 64
TOPK_MAX = 256
NUM_BUCKETS = 32
MAX_DISTANCE = 128

LANES = 128
SUBLANES = 8
VMEM_LIMIT = 56 * 1024 * 1024
NEG = -1e30
INT_MIN = -2 ** 31
MIN_NORMAL_BITS = 0x00800000
LOG2E = math.log2(math.e)

CONV_HALO = 32
CONV_CHUNK = 32
SEL_TQ = LANES
SEL_TK = 1024
SEL_GROUPS = 8
SEL_DEPTH = 16
ATT_TQ = 1024
ATT_TK = 1024
SAFE_TQ = 256
L_MIN = 1e-30


def _sigmoid(x):
    return 1.0 / (1.0 + jnp.exp(-x))


def _rms(x, g):
    ms = jnp.mean(x * x, axis=-1, keepdims=True)
    return x * lax.rsqrt(ms + EPS) * g


def _dot(a, b):
    return jnp.dot(a, b, preferred_element_type=F32)


def _dot_t(a, b):
    return lax.dot_general(a, b, (((1,), (1,)), ((), ())), preferred_element_type=F32)


def _split_bf16(x):
    hi = x.astype(BF16)
    lo = (x - hi.astype(F32)).astype(BF16)
    return hi, lo


def _ffn_kernel(x_ref, g_ref, wa_ref, wb_ref, wo_ref, o_ref, xn_ref, acc_ref):
    j = pl.program_id(1)

    @pl.when(j == 0)
    def _():
        xn_ref[...] = _rms(x_ref[...], g_ref[...]).astype(BF16)
        acc_ref[...] = jnp.zeros_like(acc_ref)

    xn = xn_ref[...]
    a = _dot(xn, wa_ref[...])
    b = _dot(xn, wb_ref[...])
    hmid = (a * _sigmoid(a) * b).astype(BF16)
    acc_ref[...] += _dot(hmid, wo_ref[...])

    @pl.when(j == pl.num_programs(1) - 1)
    def _():
        o_ref[...] = x_ref[...] + 0.5 * acc_ref[...]


def _ffn(x, g, w_in, w_out, tm=512, fc=1408):
    s, d = x.shape
    dff = w_out.shape[0]
    nj = dff // fc
    assert s % tm == 0 and dff % fc == 0 and fc % LANES == 0
    return pl.pallas_call(
        _ffn_kernel,
        grid=(s // tm, nj),
        in_specs=[
            pl.BlockSpec((tm, d), lambda i, j: (i, 0)),
            pl.BlockSpec((1, d), lambda i, j: (0, 0)),
            pl.BlockSpec((d, fc), lambda i, j: (0, j)),
            pl.BlockSpec((d, fc), lambda i, j: (0, j + nj)),
            pl.BlockSpec((fc, d), lambda i, j: (j, 0)),
        ],
        out_specs=pl.BlockSpec((tm, d), lambda i, j: (i, 0)),
        out_shape=jax.ShapeDtypeStruct((s, d), F32),
        scratch_shapes=[pltpu.VMEM((tm, d), BF16), pltpu.VMEM((tm, d), F32)],
        compiler_params=pltpu.CompilerParams(
            dimension_semantics=("arbitrary", "arbitrary"), vmem_limit_bytes=VMEM_LIMIT),
        name="ffn",
    )(x, g.reshape(1, d), w_in, w_in, w_out)


def _mix_in_kernel(h_ref, g_ref, wc_ref, wqkv_ref, wih_ref, wil_ref, wg_ref, qg_ref, kg_ref, hb_ref,
                   glu_ref, qt_ref, kx_ref, vt_ref, qip_ref, kix_ref, wi_ref, gc_ref, ga_ref, ksq_ref):
    u = _rms(h_ref[...], g_ref[...])
    u_hi, u_lo = _split_bf16(u)
    tm = h_ref.shape[0]
    cw = glu_ref.shape[1]
    aw = N_HEADS * HEAD_DIM
    nqi = IDX_HEADS * IDX_DIM
    d = gc_ref.shape[1]

    c = _dot(u_hi, wc_ref[...])
    glu_ref[...] = c[:, :cw] * _sigmoid(c[:, cw:])

    qkv = _dot(u_hi, wqkv_ref[...])
    hb = hb_ref[...]

    def head_sumsq(t):
        t2_hi, t2_lo = _split_bf16(t * t)
        return _dot(t2_hi, hb) + _dot(t2_lo, hb)

    def head_norm(t, g):
        return t * lax.rsqrt(head_sumsq(t) * (1.0 / HEAD_DIM) + EPS) * g

    q = head_norm(qkv[:, :aw], qg_ref[...]) * (HEAD_DIM ** -0.5 * LOG2E)
    q_t = q.T
    for hd in range(N_HEADS):
        qt_ref[hd] = q_t[hd * HEAD_DIM:(hd + 1) * HEAD_DIM, :].astype(BF16)
    k = head_norm(qkv[:, aw:2 * aw], kg_ref[...]).astype(BF16).astype(F32)
    one_hot0 = lambda shape, axis: jnp.where(lax.broadcasted_iota(I32, shape, axis) == 0, 1.0, 0.0)
    k_pad = one_hot0((tm, LANES - HEAD_DIM), 1)
    for hd in range(N_HEADS):
        kx_ref[hd] = jnp.concatenate([k[:, hd * HEAD_DIM:(hd + 1) * HEAD_DIM], k_pad], axis=1).astype(BF16)
    v_t = qkv[:, 2 * aw:].T
    v_pad = one_hot0((LANES - HEAD_DIM, tm), 0).astype(BF16)
    for hd in range(N_HEADS):
        vt_ref[hd, 0:HEAD_DIM, :] = v_t[hd * HEAD_DIM:(hd + 1) * HEAD_DIM, :].astype(BF16)
        vt_ref[hd, HEAD_DIM:, :] = v_pad

    ksq = jnp.max(head_sumsq(k), axis=0, keepdims=True)

    @pl.when(pl.program_id(0) == 0)
    def _():
        ksq_ref[...] = ksq

    @pl.when(pl.program_id(0) > 0)
    def _():
        ksq_ref[...] = jnp.maximum(ksq_ref[...], ksq)

    wih = wih_ref[...]
    idx = _dot(u_hi, wih) + _dot(u_lo, wih) + _dot(u_hi, wil_ref[...])
    qi_hi, qi_lo = _split_bf16(idx[:, :nqi].T)
    for blk in range(tm // SEL_TQ):
        cols = slice(blk * SEL_TQ, (blk + 1) * SEL_TQ)
        for hd in range(IDX_HEADS):
            rows = slice(hd * IDX_DIM, (hd + 1) * IDX_DIM)
            out_cols = slice((hd % 2) * SEL_TQ, (hd % 2 + 1) * SEL_TQ)
            for part, src in enumerate((qi_hi, qi_hi, qi_lo, qi_lo)):
                qip_ref[blk, hd // 2, part * IDX_DIM:(part + 1) * IDX_DIM, out_cols] = src[rows, cols]
    ki = idx[:, nqi:nqi + IDX_DIM]
    ki_hi = ki.astype(BF16).astype(F32)
    ki_lo = ki - ki_hi
    kix_ref[...] = jnp.concatenate([ki_hi, ki_lo, ki_hi, ki_lo], axis=1).astype(BF16)
    wi_ref[...] = idx[:, nqi:]

    gates = _sigmoid(_dot(u_hi, wg_ref[...]))
    gc_ref[...] = gates[:, :d]
    ga_ref[...] = gates[:, d:]


def _mix_in(h, g, w_conv, w_qkv, w_idx_hi, w_idx_lo, w_gate, qg, kg, head_blocks, tm=512):
    s, d = h.shape
    cw = w_conv.shape[1] // 2
    aw = w_qkv.shape[1] // 3
    nqi = IDX_HEADS * IDX_DIM
    iw = w_idx_hi.shape[1]
    assert aw == N_HEADS * HEAD_DIM and tm % SEL_TQ == 0 and iw - nqi == LANES
    full = lambda a: pl.BlockSpec(a.shape, lambda i: (0,) * a.ndim)
    row = lambda n: pl.BlockSpec((tm, n), lambda i: (i, 0))
    g2 = g.reshape(1, d)
    return pl.pallas_call(
        _mix_in_kernel,
        grid=(s // tm,),
        in_specs=[row(d), full(g2), full(w_conv), full(w_qkv), full(w_idx_hi), full(w_idx_lo), full(w_gate),
                  full(qg), full(kg), full(head_blocks)],
        out_specs=[
            row(cw),
            pl.BlockSpec((N_HEADS, HEAD_DIM, tm), lambda i: (0, 0, i)),
            pl.BlockSpec((N_HEADS, tm, LANES), lambda i: (0, i, 0)),
            pl.BlockSpec((N_HEADS, LANES, tm), lambda i: (0, 0, i)),
            pl.BlockSpec((tm // SEL_TQ, IDX_HEADS // 2, 4 * IDX_DIM, 2 * SEL_TQ), lambda i: (i, 0, 0, 0)),
            row(4 * IDX_DIM), row(LANES), row(d), row(d),
            pl.BlockSpec((1, aw), lambda i: (0, 0)),
        ],
        out_shape=[
            jax.ShapeDtypeStruct((s, cw), F32),
            jax.ShapeDtypeStruct((N_HEADS, HEAD_DIM, s), BF16),
            jax.ShapeDtypeStruct((N_HEADS, s, LANES), BF16),
            jax.ShapeDtypeStruct((N_HEADS, LANES, s), BF16),
            jax.ShapeDtypeStruct((s // SEL_TQ, IDX_HEADS // 2, 4 * IDX_DIM, 2 * SEL_TQ), BF16),
            jax.ShapeDtypeStruct((s, 4 * IDX_DIM), BF16),
            jax.ShapeDtypeStruct((s, LANES), F32),
            jax.ShapeDtypeStruct((s, d), F32),
            jax.ShapeDtypeStruct((s, d), F32),
            jax.ShapeDtypeStruct((1, aw), F32),
        ],
        compiler_params=pltpu.CompilerParams(dimension_semantics=("arbitrary",), vmem_limit_bytes=VMEM_LIMIT),
        name="mix_in",
    )(h, g2, w_conv, w_qkv, w_idx_hi, w_idx_lo, w_gate, qg, kg, head_blocks)


def _conv_kernel(z_ref, halo_ref, dw_ref, db_ref, lg_ref, lb_ref, wo_ref, gc_ref, o_ref, zp_ref, zs_ref, acc_ref):
    tm = z_ref.shape[0]
    first = pl.program_id(0) == 0
    halo = halo_ref[...]
    zp_ref[0:CONV_HALO, :] = jnp.where(first, jnp.zeros_like(halo), halo)
    zp_ref[CONV_HALO:, :] = z_ref[...]
    off = CONV_HALO - (CONV_WIDTH - 1)
    span = zs_ref.shape[1]
    for b in range(1, SUBLANES):
        zs_ref[b - 1] = zp_ref[b:b + span, :]

    def chunk(c, carry):
        r0 = pl.multiple_of(c * CONV_CHUNK, CONV_CHUNK)
        acc = jnp.zeros((CONV_CHUNK, z_ref.shape[1]), F32) + db_ref[...]
        for j in range(CONV_WIDTH):
            a, b = divmod(off + j, SUBLANES)
            src = zp_ref if b == 0 else zs_ref.at[b - 1]
            acc = acc + dw_ref[j:j + 1, :] * src[pl.ds(r0 + a * SUBLANES, CONV_CHUNK), :]
        acc_ref[pl.ds(r0, CONV_CHUNK), :] = acc
        return carry

    lax.fori_loop(0, tm // CONV_CHUNK, chunk, 0)
    acc = acc_ref[...]
    mu = jnp.mean(acc, axis=-1, keepdims=True)
    xc = acc - mu
    y = xc * lax.rsqrt(jnp.mean(xc * xc, axis=-1, keepdims=True) + EPS)
    y = y * lg_ref[...] + lb_ref[...]
    y = (y * _sigmoid(y)).astype(BF16)
    o_ref[...] = gc_ref[...] * _dot(y, wo_ref[...])


def _conv(z, dw_w, dw_b, ln_g, ln_b, w_out, gate_c, tm=512):
    s, c = z.shape
    d = w_out.shape[1]
    assert tm % CONV_HALO == 0 and tm % CONV_CHUNK == 0
    r = tm // CONV_HALO
    span = tm + CONV_HALO - SUBLANES
    full = lambda a: pl.BlockSpec(a.shape, lambda i: (0,) * a.ndim)
    vecs = [dw_b.reshape(1, c), ln_g.reshape(1, c), ln_b.reshape(1, c)]
    return pl.pallas_call(
        _conv_kernel,
        grid=(s // tm,),
        in_specs=[
            pl.BlockSpec((tm, c), lambda i: (i, 0)),
            pl.BlockSpec((CONV_HALO, c), lambda i: (jnp.maximum(i * r - 1, 0), 0)),
            full(dw_w), full(vecs[0]), full(vecs[1]), full(vecs[2]), full(w_out),
            pl.BlockSpec((tm, d), lambda i: (i, 0)),
        ],
        out_specs=pl.BlockSpec((tm, d), lambda i: (i, 0)),
        out_shape=jax.ShapeDtypeStruct((s, d), F32),
        scratch_shapes=[pltpu.VMEM((tm + CONV_HALO, c), F32), pltpu.VMEM((SUBLANES - 1, span, c), F32),
                        pltpu.VMEM((tm, c), F32)],
        compiler_params=pltpu.CompilerParams(dimension_semantics=("arbitrary",), vmem_limit_bytes=VMEM_LIMIT),
        name="conv",
    )(z, z, dw_w, *vecs, w_out, gate_c)


def _sortable_key(x):
    bits = pltpu.bitcast(x + 0.0, I32)
    return bits ^ ((bits >> 31) & 0x7FFFFFFF)


def _key_to_float(key):
    return pltpu.bitcast(key ^ ((key >> 31) & 0x7FFFFFFF), F32)


def _sort_network(n):
    pairs = []

    def merge(lo, hi, r):
        step = r * 2
        if step < hi - lo:
            merge(lo, hi, step)
            merge(lo + r, hi, step)
            pairs.extend((i, i + r) for i in range(lo + r, hi - r, step))
        else:
            pairs.append((lo, lo + r))

    def sort(lo, hi):
        if hi > lo:
            mid = lo + (hi - lo) // 2
            sort(lo, mid)
            sort(mid + 1, hi)
            merge(lo, hi, 1)

    sort(0, n - 1)
    return pairs


def _compare_exchange(a, i, j):
    a[i], a[j] = jnp.maximum(a[i], a[j]), jnp.minimum(a[i], a[j])


def _merge_top(top, batch):
    n = len(top)
    out = [jnp.maximum(top[i], batch[n - 1 - i]) for i in range(n)]
    d = n // 2
    while d >= 1:
        for i in range(n):
            if i & d == 0:
                _compare_exchange(out, i, i + d)
        d //= 2
    return out


def _select_kernel(qi_ref, w_ref, ki_ref, mask_ref, sc_ref, cand_ref, ckey_ref, st_ref, *, k_top):
    qb = pl.program_id(0)
    n_rows = mask_ref.shape[1]
    n_tiles = n_rows // SEL_TK
    q0 = qb * SEL_TQ
    nkt = (q0 + SEL_TQ - 1) // SEL_TK + 1
    vt = SEL_TK // SUBLANES
    vshape = (SUBLANES, LANES)
    qpos = q0 + lax.broadcasted_iota(I32, vshape, 1)
    sub = lax.broadcasted_iota(I32, vshape, 0)
    w = w_ref[...] * ((IDX_HEADS ** -0.5) * (IDX_DIM ** -0.5))

    qpos_t = q0 + lax.broadcasted_iota(I32, (SEL_TK, LANES), 1)
    krow_t = lax.broadcasted_iota(I32, (SEL_TK, LANES), 0)
    zero_bits_t = (MIN_NORMAL_BITS + n_rows) - krow_t

    def score_tile(kt):
        k0 = pl.multiple_of(kt * SEL_TK, SEL_TK)
        ki = ki_ref[pl.ds(k0, SEL_TK), :]
        sc = None
        for pair in range(IDX_HEADS // 2):
            r = _dot(ki, qi_ref[0, pair])
            for j in range(2):
                h = 2 * pair + j
                term = w[h:h + 1, :] * jnp.maximum(r[:, j * LANES:(j + 1) * LANES], 0.0)
                sc = term if sc is None else sc + term
        sc_ref[pl.ds(k0, SEL_TK), :] = jnp.where(sc == 0.0, pltpu.bitcast(zero_bits_t - k0, F32), sc)

    def vreg(ref, row):
        return ref[pl.ds(pl.multiple_of(row, SUBLANES), SUBLANES), :]

    def col_sum(parts):
        tot = parts[0]
        for part in parts[1:]:
            tot = tot + part
        return jnp.broadcast_to(jnp.sum(tot, axis=0, keepdims=True), vshape)

    n_cand = SEL_GROUPS * SEL_DEPTH
    network = _sort_network(SEL_DEPTH)
    cand_ref[...] = jnp.full(cand_ref.shape, -jnp.inf, F32)

    def lists_tile(kt):
        for g in range(SEL_GROUPS):
            batch = [vreg(sc_ref, kt * SEL_TK + (g + SEL_GROUPS * j) * SUBLANES) for j in range(SEL_DEPTH)]
            for i, j in network:
                _compare_exchange(batch, i, j)
            rows = [slice((g * SEL_DEPTH + i) * SUBLANES, (g * SEL_DEPTH + i + 1) * SUBLANES)
                    for i in range(SEL_DEPTH)]
            top = _merge_top([cand_ref[r, :] for r in rows], batch)
            for r, t in zip(rows, top):
                cand_ref[r, :] = t

    sc_ref[pl.ds(n_rows, SEL_TK), :] = jnp.full((SEL_TK, LANES), -jnp.inf, F32)

    def score_and_lists(kt, carry):
        lists_tile(jnp.where(kt == 0, n_tiles, kt - 1))
        score_tile(kt)
        return carry

    lax.fori_loop(0, nkt, score_and_lists, 0)
    k_last = pl.multiple_of((nkt - 1) * SEL_TK, SEL_TK)
    sc_ref[pl.ds(k_last, SEL_TK), :] = jnp.where(
        k_last + krow_t <= qpos_t, sc_ref[pl.ds(k_last, SEL_TK), :], -jnp.inf)
    lists_tile(nkt - 1)
    ckey_ref[...] = _sortable_key(cand_ref[...])

    n_acc = 4

    def cand_bit(i, t):
        c = t ^ lax.shift_left(jnp.int32(1), 31 - i)
        acc = [jnp.zeros(vshape, I32) for _ in range(n_acc)]
        for v in range(n_cand):
            x = ckey_ref[v * SUBLANES:(v + 1) * SUBLANES, :]
            acc[v % n_acc] = acc[v % n_acc] + jnp.where(x >= c, 1, 0)
        return jnp.where(col_sum(acc) >= k_top, c, t)

    st_ref[0] = lax.fori_loop(0, 32, cand_bit, jnp.full(vshape, INT_MIN, I32))

    def count_scores(preds):
        def tile(kt, acc):
            acc = [list(a) for a in acc]
            for v in range(vt):
                x = vreg(sc_ref, kt * SEL_TK + v * SUBLANES)
                kpos = kt * SEL_TK + v * SUBLANES + sub
                for p, pred in enumerate(preds):
                    acc[p][v % n_acc] = acc[p][v % n_acc] + jnp.where(pred(x, kpos), 1, 0)
            return tuple(tuple(a) for a in acc)

        zero = tuple(tuple(jnp.zeros(vshape, I32) for _ in range(n_acc)) for _ in preds)
        return [col_sum(list(a)) for a in lax.fori_loop(0, nkt, tile, zero)]

    def count_around_threshold():
        tf = _key_to_float(st_ref[0])
        gt, ge = count_scores([lambda x, kpos: x > tf, lambda x, kpos: x >= tf])
        st_ref[1] = gt
        st_ref[2] = ge

    tf_fast = jnp.broadcast_to(_key_to_float(st_ref[0])[0:1, :], (SEL_TK, LANES))

    def fast_tile(kt, acc):
        k0 = pl.multiple_of(kt * SEL_TK, SEL_TK)
        m = jnp.where(sc_ref[pl.ds(k0, SEL_TK), :] >= tf_fast, 1.0, 0.0)
        mask_ref[0, pl.ds(k0, SEL_TK), :] = m.astype(BF16)
        acc = list(acc)
        for v in range(vt):
            acc[v % n_acc] = acc[v % n_acc] + m[v * SUBLANES:(v + 1) * SUBLANES, :]
        return tuple(acc)

    size = col_sum(list(lax.fori_loop(0, nkt, fast_tile, tuple(jnp.zeros(vshape, F32) for _ in range(n_acc)))))

    @pl.when(jnp.max(jnp.abs(size - k_top)) > 0.0)
    def _():
        count_around_threshold()

        @pl.when(jnp.max(st_ref[1]) >= k_top)
        def _():
            def full_bit(i, t):
                c = t ^ lax.shift_left(jnp.int32(1), 31 - i)
                cnt, = count_scores([lambda x, kpos: _sortable_key(x) >= c])
                return jnp.where(cnt >= k_top, c, t)

            st_ref[0] = lax.fori_loop(0, 32, full_bit, jnp.full(vshape, INT_MIN, I32))
            count_around_threshold()

        st_ref[3] = jnp.full(vshape, n_rows, I32)

        @pl.when(jnp.max(st_ref[2]) > k_top)
        def _():
            tf = _key_to_float(st_ref[0])
            need = k_top - st_ref[1]
            n_bits = n_rows.bit_length() - 1

            def pos_bit(i, p):
                c = p + lax.shift_left(jnp.int32(1), n_bits - 1 - i)
                f, = count_scores([lambda x, kpos: (x == tf) & (kpos < c)])
                return jnp.where(f < need, c, p)

            st_ref[3] = lax.fori_loop(0, n_bits, pos_bit, jnp.zeros(vshape, I32))

        tf_t = jnp.broadcast_to(_key_to_float(st_ref[0])[0:1, :], (SEL_TK, LANES))
        lim_t = jnp.broadcast_to(st_ref[3][0:1, :], (SEL_TK, LANES))

        def write_tile(kt, carry):
            k0 = pl.multiple_of(kt * SEL_TK, SEL_TK)
            x = sc_ref[pl.ds(k0, SEL_TK), :]
            kpos = k0 + krow_t
            sel = ((x > tf_t) | ((x == tf_t) & (kpos <= lim_t))) & (kpos <= qpos_t)
            mask_ref[0, pl.ds(k0, SEL_TK), :] = jnp.where(sel, 1.0, 0.0).astype(BF16)
            return carry

        lax.fori_loop(0, nkt, write_tile, 0)

    def fill_tile(kt, carry):
        mask_ref[0, pl.ds(pl.multiple_of(kt * SEL_TK, SEL_TK), SEL_TK), :] = jnp.zeros((SEL_TK, LANES), BF16)
        return carry

    lax.fori_loop(nkt, n_tiles, fill_tile, 0)


def _select(qi_pairs, w_t, ki_ext, k_top):
    nqb, npair, kw, _ = qi_pairs.shape
    s = ki_ext.shape[0]
    n_cand_rows = SEL_GROUPS * SEL_DEPTH * SUBLANES
    assert s % SEL_TK == 0 and SEL_TK == n_cand_rows and n_cand_rows >= k_top
    return pl.pallas_call(
        functools.partial(_select_kernel, k_top=k_top),
        grid=(nqb,),
        in_specs=[
            pl.BlockSpec((1, npair, kw, 2 * SEL_TQ), lambda i: (i, 0, 0, 0)),
            pl.BlockSpec((w_t.shape[0], SEL_TQ), lambda i: (0, i)),
            pl.BlockSpec((s, kw), lambda i: (0, 0)),
        ],
        out_specs=pl.BlockSpec((1, s, SEL_TQ), lambda i: (i, 0, 0)),
        out_shape=jax.ShapeDtypeStruct((nqb, s, SEL_TQ), BF16),
        scratch_shapes=[
            pltpu.VMEM((s + SEL_TK, SEL_TQ), F32),
            pltpu.VMEM((n_cand_rows, SEL_TQ), F32),
            pltpu.VMEM((n_cand_rows, SEL_TQ), I32),
            pltpu.VMEM((4, SUBLANES, LANES), I32),
        ],
        compiler_params=pltpu.CompilerParams(dimension_semantics=("arbitrary",), vmem_limit_bytes=VMEM_LIMIT),
        name="select",
    )(qi_pairs, w_t, ki_ext)


def _causal_pairs(s, tq, tk):
    pairs = [(qb, kt) for qb in range(s // tq) for kt in range((qb * tq + tq - 1) // tk + 1)]
    return jnp.asarray([p[0] for p in pairs], I32), jnp.asarray([p[1] for p in pairs], I32)


def _attn_kernel(qb_tab, kt_tab, qt_ref, kmax_ref, k_ref, vt_ref, mask_ref, bias_ref, o_ref, l_ref, qx_ref, acc_ref):
    step = pl.program_id(0)
    qb = qb_tab[step]
    kt = kt_tab[step]
    q0 = qb * ATT_TQ
    k0 = kt * ATT_TK
    nq = ATT_TQ // LANES
    nk = ATT_TK // LANES

    @pl.when(kt == 0)
    def _():
        row = lax.broadcasted_iota(I32, (HEAD_DIM, ATT_TQ), 0)
        for h in range(N_HEADS):
            q = qt_ref[h]
            qf = q.astype(F32)
            bound = jnp.sqrt(jnp.sum(qf * qf, axis=0, keepdims=True)) * kmax_ref[h] * 1.02
            qx_ref[h, 0:HEAD_DIM, :] = q
            qx_ref[h, HEAD_DIM:, :] = jnp.where(row == 0, -bound, 0.0).astype(BF16)
        acc_ref[...] = jnp.zeros(acc_ref.shape, F32)

    mask = jnp.concatenate([mask_ref[a] for a in range(nq)], axis=1)

    def heads(with_bias):
        for h in range(N_HEADS):
            s = _dot(k_ref[h], qx_ref[h])
            if with_bias:
                rows = []
                for c in range(nk):
                    tab = [jnp.clip((q0 + a * LANES - k0 - c * LANES) // LANES + 1, 0, 3) for a in range(nq)]
                    rows.append(jnp.concatenate([bias_ref[tab[a], h] for a in range(nq)], axis=1))
                s = s + jnp.concatenate(rows, axis=0)
            p = jnp.exp2(s).astype(BF16) * mask
            acc_ref[h] += _dot(vt_ref[h], p)

    near = k0 + ATT_TK + 2 * LANES > q0

    @pl.when(near)
    def _():
        heads(True)

    @pl.when(jnp.logical_not(near))
    def _():
        heads(False)

    @pl.when(kt == (q0 + ATT_TQ - 1) // ATT_TK)
    def _():
        for h in range(N_HEADS):
            acc = acc_ref[h]
            den = acc[HEAD_DIM:HEAD_DIM + 1, :]
            o_ref[h] = acc[:HEAD_DIM, :] / den
            l_ref[h] = den


def _attn(q_t, kmax_b, k_ext, v_t, mask, bias_tab):
    nh, hd, s = q_t.shape
    qb_tab, kt_tab = _causal_pairs(s, ATT_TQ, ATT_TK)
    grid_spec = pltpu.PrefetchScalarGridSpec(
        num_scalar_prefetch=2,
        grid=(qb_tab.shape[0],),
        in_specs=[
            pl.BlockSpec((nh, hd, ATT_TQ), lambda i, qb, kt: (0, 0, qb[i])),
            pl.BlockSpec((nh, 1, ATT_TQ), lambda i, qb, kt: (0, 0, 0)),
            pl.BlockSpec((nh, ATT_TK, LANES), lambda i, qb, kt: (0, kt[i], 0)),
            pl.BlockSpec((nh, LANES, ATT_TK), lambda i, qb, kt: (0, 0, kt[i])),
            pl.BlockSpec((ATT_TQ // LANES, ATT_TK, LANES), lambda i, qb, kt: (qb[i], kt[i], 0)),
            pl.BlockSpec(bias_tab.shape, lambda i, qb, kt: (0, 0, 0, 0)),
        ],
        out_specs=[
            pl.BlockSpec((nh, hd, ATT_TQ), lambda i, qb, kt: (0, 0, qb[i])),
            pl.BlockSpec((nh, 1, ATT_TQ), lambda i, qb, kt: (0, 0, qb[i])),
        ],
        scratch_shapes=[pltpu.VMEM((nh, LANES, ATT_TQ), BF16), pltpu.VMEM((nh, LANES, ATT_TQ), F32)],
    )
    return pl.pallas_call(
        _attn_kernel,
        grid_spec=grid_spec,
        out_shape=[jax.ShapeDtypeStruct((nh, hd, s), F32), jax.ShapeDtypeStruct((nh, 1, s), F32)],
        compiler_params=pltpu.CompilerParams(dimension_semantics=("arbitrary",), vmem_limit_bytes=VMEM_LIMIT),
        name="attn",
    )(qb_tab, kt_tab, q_t, kmax_b, k_ext, v_t, mask, bias_tab)


def _attn_safe_kernel(qb_tab, kt_tab, q_ref, k_ref, v_ref, mask_ref, bias_ref, o_ref, m_ref, acc_ref):
    step = pl.program_id(0)
    qb = qb_tab[step]
    kt = kt_tab[step]
    nq = SAFE_TQ // LANES
    nsub = ATT_TK // LANES

    @pl.when(kt == 0)
    def _():
        m_ref[...] = jnp.full(m_ref.shape, NEG, F32)
        acc_ref[...] = jnp.zeros(acc_ref.shape, F32)

    for a in range(nq):
        rows = slice(a * LANES, (a + 1) * LANES)
        q0 = qb * SAFE_TQ + a * LANES
        maskf = (mask_ref[a].astype(F32) - 1.0) * (-NEG)
        tab_idx = [jnp.clip((q0 - (kt * ATT_TK + c * LANES)) // LANES + 1, 0, 3) for c in range(nsub)]
        for h in range(N_HEADS):
            s = _dot_t(q_ref[h, rows, :], k_ref[h]) + maskf
            s = s + jnp.concatenate([bias_ref[tab_idx[c], h] for c in range(nsub)], axis=1)
            m_old = m_ref[h, rows, :]
            m_new = jnp.maximum(m_old, jnp.max(s, axis=1, keepdims=True))
            p = jnp.exp2(s - m_new[:, 0:1])
            alpha = jnp.exp2(m_old - m_new)
            acc_ref[h, rows, :] = alpha * acc_ref[h, rows, :] + _dot(p.astype(BF16), v_ref[h])
            m_ref[h, rows, :] = m_new

    @pl.when(kt == (qb * SAFE_TQ + SAFE_TQ - 1) // ATT_TK)
    def _():
        for h in range(N_HEADS):
            acc = acc_ref[h]
            o_ref[h] = acc[:, :HEAD_DIM] / acc[:, HEAD_DIM:HEAD_DIM + 1]


def _attn_safe(q, k, v_ext, mask_qk, bias_tab):
    nh, s, hd = q.shape
    qb_tab, kt_tab = _causal_pairs(s, SAFE_TQ, ATT_TK)
    grid_spec = pltpu.PrefetchScalarGridSpec(
        num_scalar_prefetch=2,
        grid=(qb_tab.shape[0],),
        in_specs=[
            pl.BlockSpec((nh, SAFE_TQ, hd), lambda i, qb, kt: (0, qb[i], 0)),
            pl.BlockSpec((nh, ATT_TK, hd), lambda i, qb, kt: (0, kt[i], 0)),
            pl.BlockSpec((nh, ATT_TK, LANES), lambda i, qb, kt: (0, kt[i], 0)),
            pl.BlockSpec((SAFE_TQ // LANES, LANES, ATT_TK), lambda i, qb, kt: (qb[i], 0, kt[i])),
            pl.BlockSpec(bias_tab.shape, lambda i, qb, kt: (0, 0, 0, 0)),
        ],
        out_specs=pl.BlockSpec((nh, SAFE_TQ, hd), lambda i, qb, kt: (0, qb[i], 0)),
        scratch_shapes=[pltpu.VMEM((nh, SAFE_TQ, LANES), F32), pltpu.VMEM((nh, SAFE_TQ, LANES), F32)],
    )
    return pl.pallas_call(
        _attn_safe_kernel,
        grid_spec=grid_spec,
        out_shape=jax.ShapeDtypeStruct((nh, s, hd), F32),
        compiler_params=pltpu.CompilerParams(dimension_semantics=("arbitrary",), vmem_limit_bytes=VMEM_LIMIT),
        name="attn_safe",
    )(qb_tab, kt_tab, q, k, v_ext, mask_qk, bias_tab)


def _merge_kernel(h_ref, cg_ref, ga_ref, at_ref, wao_ref, wmo_ref, o_ref):
    attn = at_ref[...].T.astype(BF16)
    merged = cg_ref[...] + ga_ref[...] * _dot(attn, wao_ref[...])
    o_ref[...] = h_ref[...] + _dot(merged.astype(BF16), wmo_ref[...])


def _merge(h, conv_g, gate_a, attn_t, w_ao, w_mo, tm=512):
    s, d = h.shape
    full = lambda a: pl.BlockSpec(a.shape, lambda i: (0,) * a.ndim)
    row = lambda n: pl.BlockSpec((tm, n), lambda i: (i, 0))
    return pl.pallas_call(
        _merge_kernel,
        grid=(s // tm,),
        in_specs=[row(d), row(d), row(d), pl.BlockSpec((attn_t.shape[0], tm), lambda i: (0, i)),
                  full(w_ao), full(w_mo)],
        out_specs=row(d),
        out_shape=jax.ShapeDtypeStruct((s, d), F32),
        compiler_params=pltpu.CompilerParams(dimension_semantics=("arbitrary",), vmem_limit_bytes=VMEM_LIMIT),
        name="merge",
    )(h, conv_g, gate_a, attn_t, w_ao, w_mo)


def _ple_kernel(h_ref, g_ref, p_ref, wg_ref, wp_ref, o_ref):
    h = h_ref[...]
    gate = _sigmoid(_dot(_rms(h, g_ref[...]).astype(BF16), wg_ref[...]))
    o_ref[...] = h + gate * _dot(p_ref[...].astype(BF16), wp_ref[...])


def _ple(h, g, p, w_gate, w_proj, tm=512):
    s, d = h.shape
    full = lambda a: pl.BlockSpec(a.shape, lambda i: (0,) * a.ndim)
    row = lambda n: pl.BlockSpec((tm, n), lambda i: (i, 0))
    g2 = g.reshape(1, d)
    return pl.pallas_call(
        _ple_kernel,
        grid=(s // tm,),
        in_specs=[row(d), full(g2), row(p.shape[1]), full(w_gate), full(w_proj)],
        out_specs=row(d),
        out_shape=jax.ShapeDtypeStruct((s, d), F32),
        compiler_params=pltpu.CompilerParams(dimension_semantics=("arbitrary",), vmem_limit_bytes=VMEM_LIMIT),
        name="ple",
    )(h, g2, p, w_gate, w_proj)


def _t5_bucket_table(n_dist):
    n = np.arange(n_dist)
    max_exact = NUM_BUCKETS // 2
    nf = np.maximum(n, 1).astype(np.float32)
    large = max_exact + (np.log(nf / max_exact) / math.log(MAX_DISTANCE / max_exact)
                         * (NUM_BUCKETS - max_exact)).astype(np.int32)
    large = np.minimum(large, NUM_BUCKETS - 1)
    return np.where(n < max_exact, n, large)


def _bias_tables(rel_bias):
    assert MAX_DISTANCE <= LANES
    n = LANES
    rel = (rel_bias.astype(F32) - rel_bias[NUM_BUCKETS - 1].astype(F32)[None, :]) * LOG2E
    by_dist = rel[_t5_bucket_table(2 * n)].T

    def toeplitz(a):
        skew = jnp.tile(a, (1, n))[:, :n * (2 * n - 1)].reshape(a.shape[0], n, 2 * n - 1)
        return skew[:, :, :n]

    d0 = toeplitz(jnp.concatenate([by_dist[:, :n], jnp.broadcast_to(by_dist[:, :1], by_dist[:, :n].shape)], axis=1))
    d1 = toeplitz(jnp.concatenate([by_dist[:, n:], by_dist[:, :n]], axis=1))
    zero = jnp.zeros_like(d0)
    return jnp.swapaxes(jnp.stack([zero, d0, d1, zero]), 2, 3)


def kernel(x, p, ffn1_norm, ffn1_w_in, ffn1_w_out, mix_norm, mix_w_in, conv_dw_w, conv_dw_b, conv_ln_g,
           conv_ln_b, conv_w_out, q_norm, k_norm, attn_w_out, mix_w_out, ffn2_norm, ffn2_w_in, ffn2_w_out,
           ple_norm, ple_w_gate, ple_w_proj, rel_bias):
    b, s, d = x.shape
    depth = ffn1_norm.shape[0]
    cw = conv_dw_w.shape[2]
    aw = N_HEADS * HEAD_DIM
    nqi = IDX_HEADS * IDX_DIM
    iw = nqi + IDX_DIM + IDX_HEADS
    iw_pad = -(-iw // LANES) * LANES
    k_top = min(TOPK_MAX, s // 4)
    assert b == 1 and s % ATT_TK == 0 and mix_w_in.shape[2] == 2 * cw + 3 * aw + iw + 2 * d

    head_blocks = jnp.asarray(np.kron(np.eye(N_HEADS), np.ones((HEAD_DIM, HEAD_DIM))), BF16)
    bias_qk = _bias_tables(rel_bias)
    bias_kq = jnp.swapaxes(bias_qk, 2, 3)

    h = x[0]
    for i in range(depth):
        h = _ffn(h, ffn1_norm[i], ffn1_w_in[i].astype(BF16), ffn1_w_out[i].astype(BF16))

        w = mix_w_in[i]
        o0 = 2 * cw
        o1 = o0 + 3 * aw
        o2 = o1 + iw
        w_idx = jnp.pad(w[:, o1:o2], ((0, 0), (0, iw_pad - iw)))
        w_idx_hi, w_idx_lo = _split_bf16(w_idx)
        qg = jnp.tile(q_norm[i], N_HEADS).reshape(1, aw)
        kg = jnp.tile(k_norm[i], N_HEADS).reshape(1, aw)
        glu, q_t, k_ext, v_t, qi_pairs, ki_ext, idx_w, gate_c, gate_a, ksq = _mix_in(
            h, mix_norm[i], w[:, :o0].astype(BF16), w[:, o0:o1].astype(BF16), w_idx_hi, w_idx_lo,
            w[:, o2:].astype(BF16), qg, kg, head_blocks)

        conv_g = _conv(glu, conv_dw_w[i], conv_dw_b[i], conv_ln_g[i], conv_ln_b[i],
                       conv_w_out[i].astype(BF16), gate_c)

        w_t = idx_w[:, IDX_DIM:IDX_DIM + IDX_HEADS].T
        mask = _select(qi_pairs, w_t, ki_ext, k_top)

        kmax = jnp.sqrt(ksq.reshape(N_HEADS, HEAD_DIM)[:, :1])
        kmax_b = jnp.broadcast_to(kmax[:, :, None], (N_HEADS, 1, ATT_TQ))
        attn_t, den = _attn(q_t, kmax_b, k_ext, v_t, mask, bias_kq)

        def safe_attn():
            out = _attn_safe(jnp.swapaxes(q_t, 1, 2), k_ext[:, :, :HEAD_DIM], jnp.swapaxes(v_t, 1, 2),
                             jnp.swapaxes(mask, 1, 2), bias_qk)
            return jnp.swapaxes(out, 1, 2)

        underflow = jnp.logical_not(jnp.min(den) > L_MIN)
        attn_t = lax.cond(underflow, safe_attn, lambda: attn_t).reshape(aw, s)

        h = _merge(h, conv_g, gate_a, attn_t, attn_w_out[i].astype(BF16), mix_w_out[i].astype(BF16))
        h = _ffn(h, ffn2_norm[i], ffn2_w_in[i].astype(BF16), ffn2_w_out[i].astype(BF16))
        h = _ple(h, ple_norm[i], p[i, 0], ple_w_gate[i].astype(BF16), ple_w_proj[i].astype(BF16))
    return h[None]
```

```python
import functools
import math

import numpy as np
import jax
import jax.numpy as jnp
from jax import lax
from jax.experimental import pallas as pl
from jax.experimental.pallas import tpu as pltpu

F32 = jnp.float32
BF16 = jnp.bfloat16
I32 = jnp.int32

EPS = 1e-6
CONV_WIDTH = 31
N_HEADS = 8
HEAD_DIM = 64
IDX_HEADS = 4
IDX_DIM = 64
TOPK_MAX = 256
NUM_BUCKETS = 32
MAX_DISTANCE = 128

LANES = 128
SUBLANES = 8
MXU_TILE = 256
FFN_CHUNK = 1024
VMEM_LIMIT = 56 * 1024 * 1024
NEG = -1e30
INT_MIN = -2 ** 31
MIN_NORMAL_BITS = 0x00800000
LOG2E = math.log2(math.e)

CONV_HALO = 32
CONV_CHUNK = 32
SEL_TQ = LANES
SEL_TK = 1024
SEL_GROUPS = 8
SEL_DEPTH = 16
ATT_TQ = 1024
ATT_TK = 1024
SAFE_TQ = 256
L_MIN = 1e-30


def _sigmoid(x):
    return 1.0 / (1.0 + jnp.exp(-x))


def _rms(x, g):
    ms = jnp.mean(x * x, axis=-1, keepdims=True)
    return x * lax.rsqrt(ms + EPS) * g


def _dot(a, b):
    return jnp.dot(a, b, preferred_element_type=F32)


def _dot_t(a, b):
    return lax.dot_general(a, b, (((1,), (1,)), ((), ())), preferred_element_type=F32)


def _split_bf16(x):
    hi = x.astype(BF16)
    lo = (x - hi.astype(F32)).astype(BF16)
    return hi, lo


def _ffn_kernel(x_ref, g_ref, wi_ref, wo_ref, o_ref):
    x = x_ref[...]
    xn = _rms(x, g_ref[...]).astype(BF16)
    dff = wo_ref.shape[0]
    acc = None
    for c0 in range(0, dff, FFN_CHUNK):
        c1 = min(c0 + FFN_CHUNK, dff)
        a = _dot(xn, wi_ref[:, c0:c1])
        b = _dot(xn, wi_ref[:, dff + c0:dff + c1])
        part = _dot((a * _sigmoid(a) * b).astype(BF16), wo_ref[c0:c1, :])
        acc = part if acc is None else acc + part
    o_ref[...] = x + 0.5 * acc


def _resident(a):
    return pl.BlockSpec(a.shape, lambda *_: (0,) * a.ndim, pipeline_mode=pl.Buffered(1))


def _ffn(x, g, w_in, w_out, tm=512):
    s, d = x.shape
    dff = w_out.shape[0]
    assert s % tm == 0 and dff % MXU_TILE == 0 and FFN_CHUNK % MXU_TILE == 0
    g2 = g.reshape(1, d)
    return pl.pallas_call(
        _ffn_kernel,
        grid=(s // tm,),
        in_specs=[pl.BlockSpec((tm, d), lambda i: (i, 0)), _resident(g2), _resident(w_in), _resident(w_out)],
        out_specs=pl.BlockSpec((tm, d), lambda i: (i, 0)),
        out_shape=jax.ShapeDtypeStruct((s, d), F32),
        compiler_params=pltpu.CompilerParams(dimension_semantics=("arbitrary",), vmem_limit_bytes=VMEM_LIMIT),
        name="ffn",
    )(x, g2, w_in, w_out)


def _mix_in_kernel(h_ref, g_ref, wc_ref, wqkv_ref, wih_ref, wil_ref, wg_ref, qg_ref, kg_ref, hb_ref,
                   glu_ref, qt_ref, kx_ref, vt_ref, qip_ref, kix_ref, wi_ref, gc_ref, ga_ref, ksq_ref):
    u = _rms(h_ref[...], g_ref[...])
    u_hi, u_lo = _split_bf16(u)
    tm = h_ref.shape[0]
    cw = glu_ref.shape[1]
    aw = N_HEADS * HEAD_DIM
    nqi = IDX_HEADS * IDX_DIM
    d = gc_ref.shape[1]

    c = _dot(u_hi, wc_ref[...])
    glu_ref[...] = c[:, :cw] * _sigmoid(c[:, cw:])

    qkv = _dot(u_hi, wqkv_ref[...])
    hb = hb_ref[...]

    def head_sumsq(t):
        t2_hi, t2_lo = _split_bf16(t * t)
        return _dot(t2_hi, hb) + _dot(t2_lo, hb)

    def head_norm(t, g):
        return t * lax.rsqrt(head_sumsq(t) * (1.0 / HEAD_DIM) + EPS) * g

    q = head_norm(qkv[:, :aw], qg_ref[...]) * (HEAD_DIM ** -0.5 * LOG2E)
    q_t = q.T
    for hd in range(N_HEADS):
        qt_ref[hd] = q_t[hd * HEAD_DIM:(hd + 1) * HEAD_DIM, :].astype(BF16)
    k = head_norm(qkv[:, aw:2 * aw], kg_ref[...]).astype(BF16).astype(F32)
    one_hot0 = lambda shape, axis: jnp.where(lax.broadcasted_iota(I32, shape, axis) == 0, 1.0, 0.0)
    k_pad = one_hot0((tm, LANES - HEAD_DIM), 1)
    for hd in range(N_HEADS):
        kx_ref[hd] = jnp.concatenate([k[:, hd * HEAD_DIM:(hd + 1) * HEAD_DIM], k_pad], axis=1).astype(BF16)
    v_t = qkv[:, 2 * aw:].T
    v_pad = one_hot0((LANES - HEAD_DIM, tm), 0).astype(BF16)
    for hd in range(N_HEADS):
        vt_ref[hd, 0:HEAD_DIM, :] = v_t[hd * HEAD_DIM:(hd + 1) * HEAD_DIM, :].astype(BF16)
        vt_ref[hd, HEAD_DIM:, :] = v_pad

    ksq = jnp.max(head_sumsq(k), axis=0, keepdims=True)

    @pl.when(pl.program_id(0) == 0)
    def _():
        ksq_ref[...] = ksq

    @pl.when(pl.program_id(0) > 0)
    def _():
        ksq_ref[...] = jnp.maximum(ksq_ref[...], ksq)

    wih = wih_ref[...]
    idx = _dot(u_hi, wih) + _dot(u_lo, wih) + _dot(u_hi, wil_ref[...])
    qi_hi, qi_lo = _split_bf16(idx[:, :nqi].T)
    for blk in range(tm // SEL_TQ):
        cols = slice(blk * SEL_TQ, (blk + 1) * SEL_TQ)
        for hd in range(IDX_HEADS):
            rows = slice(hd * IDX_DIM, (hd + 1) * IDX_DIM)
            out_cols = slice((hd % 2) * SEL_TQ, (hd % 2 + 1) * SEL_TQ)
            for part, src in enumerate((qi_hi, qi_hi, qi_lo, qi_lo)):
                qip_ref[blk, hd // 2, part * IDX_DIM:(part + 1) * IDX_DIM, out_cols] = src[rows, cols]
    ki = idx[:, nqi:nqi + IDX_DIM]
    ki_hi = ki.astype(BF16).astype(F32)
    ki_lo = ki - ki_hi
    kix_ref[...] = jnp.concatenate([ki_hi, ki_lo, ki_hi, ki_lo], axis=1).astype(BF16)
    wi_ref[...] = idx[:, nqi:]

    gates = _sigmoid(_dot(u_hi, wg_ref[...]))
    gc_ref[...] = gates[:, :d]
    ga_ref[...] = gates[:, d:]


def _mix_in(h, g, w_conv, w_qkv, w_idx_hi, w_idx_lo, w_gate, qg, kg, head_blocks, tm=512):
    s, d = h.shape
    cw = w_conv.shape[1] // 2
    aw = w_qkv.shape[1] // 3
    nqi = IDX_HEADS * IDX_DIM
    iw = w_idx_hi.shape[1]
    assert aw == N_HEADS * HEAD_DIM and tm % SEL_TQ == 0 and iw - nqi == LANES
    full = lambda a: pl.BlockSpec(a.shape, lambda i: (0,) * a.ndim)
    row = lambda n: pl.BlockSpec((tm, n), lambda i: (i, 0))
    g2 = g.reshape(1, d)
    return pl.pallas_call(
        _mix_in_kernel,
        grid=(s // tm,),
        in_specs=[row(d), full(g2), full(w_conv), full(w_qkv), full(w_idx_hi), full(w_idx_lo), full(w_gate),
                  full(qg), full(kg), full(head_blocks)],
        out_specs=[
            row(cw),
            pl.BlockSpec((N_HEADS, HEAD_DIM, tm), lambda i: (0, 0, i)),
            pl.BlockSpec((N_HEADS, tm, LANES), lambda i: (0, i, 0)),
            pl.BlockSpec((N_HEADS, LANES, tm), lambda i: (0, 0, i)),
            pl.BlockSpec((tm // SEL_TQ, IDX_HEADS // 2, 4 * IDX_DIM, 2 * SEL_TQ), lambda i: (i, 0, 0, 0)),
            row(4 * IDX_DIM), row(LANES), row(d), row(d),
            pl.BlockSpec((1, aw), lambda i: (0, 0)),
        ],
        out_shape=[
            jax.ShapeDtypeStruct((s, cw), F32),
            jax.ShapeDtypeStruct((N_HEADS, HEAD_DIM, s), BF16),
            jax.ShapeDtypeStruct((N_HEADS, s, LANES), BF16),
            jax.ShapeDtypeStruct((N_HEADS, LANES, s), BF16),
            jax.ShapeDtypeStruct((s // SEL_TQ, IDX_HEADS // 2, 4 * IDX_DIM, 2 * SEL_TQ), BF16),
            jax.ShapeDtypeStruct((s, 4 * IDX_DIM), BF16),
            jax.ShapeDtypeStruct((s, LANES), F32),
            jax.ShapeDtypeStruct((s, d), F32),
            jax.ShapeDtypeStruct((s, d), F32),
            jax.ShapeDtypeStruct((1, aw), F32),
        ],
        compiler_params=pltpu.CompilerParams(dimension_semantics=("arbitrary",), vmem_limit_bytes=VMEM_LIMIT),
        name="mix_in",
    )(h, g2, w_conv, w_qkv, w_idx_hi, w_idx_lo, w_gate, qg, kg, head_blocks)


def _conv_kernel(z_ref, halo_ref, dw_ref, db_ref, lg_ref, lb_ref, wo_ref, gc_ref, o_ref, zp_ref, zs_ref, acc_ref):
    tm = z_ref.shape[0]
    first = pl.program_id(0) == 0
    halo = halo_ref[...]
    zp_ref[0:CONV_HALO, :] = jnp.where(first, jnp.zeros_like(halo), halo)
    zp_ref[CONV_HALO:, :] = z_ref[...]
    off = CONV_HALO - (CONV_WIDTH - 1)
    span = zs_ref.shape[1]
    for b in range(1, SUBLANES):
        zs_ref[b - 1] = zp_ref[b:b + span, :]

    def chunk(c, carry):
        r0 = pl.multiple_of(c * CONV_CHUNK, CONV_CHUNK)
        acc = jnp.zeros((CONV_CHUNK, z_ref.shape[1]), F32) + db_ref[...]
        for j in range(CONV_WIDTH):
            a, b = divmod(off + j, SUBLANES)
            src = zp_ref if b == 0 else zs_ref.at[b - 1]
            acc = acc + dw_ref[j:j + 1, :] * src[pl.ds(r0 + a * SUBLANES, CONV_CHUNK), :]
        acc_ref[pl.ds(r0, CONV_CHUNK), :] = acc
        return carry

    lax.fori_loop(0, tm // CONV_CHUNK, chunk, 0)
    acc = acc_ref[...]
    mu = jnp.mean(acc, axis=-1, keepdims=True)
    xc = acc - mu
    y = xc * lax.rsqrt(jnp.mean(xc * xc, axis=-1, keepdims=True) + EPS)
    y = y * lg_ref[...] + lb_ref[...]
    y = (y * _sigmoid(y)).astype(BF16)
    o_ref[...] = gc_ref[...] * _dot(y, wo_ref[...])


def _conv(z, dw_w, dw_b, ln_g, ln_b, w_out, gate_c, tm=512):
    s, c = z.shape
    d = w_out.shape[1]
    assert tm % CONV_HALO == 0 and tm % CONV_CHUNK == 0
    r = tm // CONV_HALO
    span = tm + CONV_HALO - SUBLANES
    full = lambda a: pl.BlockSpec(a.shape, lambda i: (0,) * a.ndim)
    vecs = [dw_b.reshape(1, c), ln_g.reshape(1, c), ln_b.reshape(1, c)]
    return pl.pallas_call(
        _conv_kernel,
        grid=(s // tm,),
        in_specs=[
            pl.BlockSpec((tm, c), lambda i: (i, 0)),
            pl.BlockSpec((CONV_HALO, c), lambda i: (jnp.maximum(i * r - 1, 0), 0)),
            full(dw_w), full(vecs[0]), full(vecs[1]), full(vecs[2]), full(w_out),
            pl.BlockSpec((tm, d), lambda i: (i, 0)),
        ],
        out_specs=pl.BlockSpec((tm, d), lambda i: (i, 0)),
        out_shape=jax.ShapeDtypeStruct((s, d), F32),
        scratch_shapes=[pltpu.VMEM((tm + CONV_HALO, c), F32), pltpu.VMEM((SUBLANES - 1, span, c), F32),
                        pltpu.VMEM((tm, c), F32)],
        compiler_params=pltpu.CompilerParams(dimension_semantics=("arbitrary",), vmem_limit_bytes=VMEM_LIMIT),
        name="conv",
    )(z, z, dw_w, *vecs, w_out, gate_c)


def _sortable_key(x):
    bits = pltpu.bitcast(x + 0.0, I32)
    return bits ^ ((bits >> 31) & 0x7FFFFFFF)


def _key_to_float(key):
    return pltpu.bitcast(key ^ ((key >> 31) & 0x7FFFFFFF), F32)


def _sort_network(n):
    pairs = []

    def merge(lo, hi, r):
        step = r * 2
        if step < hi - lo:
            merge(lo, hi, step)
            merge(lo + r, hi, step)
            pairs.extend((i, i + r) for i in range(lo + r, hi - r, step))
        else:
            pairs.append((lo, lo + r))

    def sort(lo, hi):
        if hi > lo:
            mid = lo + (hi - lo) // 2
            sort(lo, mid)
            sort(mid + 1, hi)
            merge(lo, hi, 1)

    sort(0, n - 1)
    return pairs


def _compare_exchange(a, i, j):
    a[i], a[j] = jnp.maximum(a[i], a[j]), jnp.minimum(a[i], a[j])


def _merge_top(top, batch):
    n = len(top)
    out = [jnp.maximum(top[i], batch[n - 1 - i]) for i in range(n)]
    d = n // 2
    while d >= 1:
        for i in range(n):
            if i & d == 0:
                _compare_exchange(out, i, i + d)
        d //= 2
    return out


def _select_kernel(qi_ref, w_ref, ki_ref, mask_ref, sc_ref, cand_ref, ckey_ref, st_ref, *, k_top):
    qb = pl.program_id(0)
    n_rows = mask_ref.shape[1]
    n_tiles = n_rows // SEL_TK
    q0 = qb * SEL_TQ
    nkt = (q0 + SEL_TQ - 1) // SEL_TK + 1
    vt = SEL_TK // SUBLANES
    vshape = (SUBLANES, LANES)
    qpos = q0 + lax.broadcasted_iota(I32, vshape, 1)
    sub = lax.broadcasted_iota(I32, vshape, 0)
    w = w_ref[...] * ((IDX_HEADS ** -0.5) * (IDX_DIM ** -0.5))

    qpos_t = q0 + lax.broadcasted_iota(I32, (SEL_TK, LANES), 1)
    krow_t = lax.broadcasted_iota(I32, (SEL_TK, LANES), 0)
    zero_bits_t = (MIN_NORMAL_BITS + n_rows) - krow_t

    def score_tile(kt):
        k0 = pl.multiple_of(kt * SEL_TK, SEL_TK)
        ki = ki_ref[pl.ds(k0, SEL_TK), :]
        sc = None
        for pair in range(IDX_HEADS // 2):
            r = _dot(ki, qi_ref[0, pair])
            for j in range(2):
                h = 2 * pair + j
                term = w[h:h + 1, :] * jnp.maximum(r[:, j * LANES:(j + 1) * LANES], 0.0)
                sc = term if sc is None else sc + term
        sc_ref[pl.ds(k0, SEL_TK), :] = jnp.where(sc == 0.0, pltpu.bitcast(zero_bits_t - k0, F32), sc)

    def vreg(ref, row):
        return ref[pl.ds(pl.multiple_of(row, SUBLANES), SUBLANES), :]

    def col_sum(parts):
        tot = parts[0]
        for part in parts[1:]:
            tot = tot + part
        return jnp.broadcast_to(jnp.sum(tot, axis=0, keepdims=True), vshape)

    n_cand = SEL_GROUPS * SEL_DEPTH
    network = _sort_network(SEL_DEPTH)
    cand_ref[...] = jnp.full(cand_ref.shape, -jnp.inf, F32)

    def lists_tile(kt):
        for g in range(SEL_GROUPS):
            batch = [vreg(sc_ref, kt * SEL_TK + (g + SEL_GROUPS * j) * SUBLANES) for j in range(SEL_DEPTH)]
            for i, j in network:
                _compare_exchange(batch, i, j)
            rows = [slice((g * SEL_DEPTH + i) * SUBLANES, (g * SEL_DEPTH + i + 1) * SUBLANES)
                    for i in range(SEL_DEPTH)]
            top = _merge_top([cand_ref[r, :] for r in rows], batch)
            for r, t in zip(rows, top):
                cand_ref[r, :] = t

    sc_ref[pl.ds(n_rows, SEL_TK), :] = jnp.full((SEL_TK, LANES), -jnp.inf, F32)

    def score_and_lists(kt, carry):
        lists_tile(jnp.where(kt == 0, n_tiles, kt - 1))
        score_tile(kt)
        return carry

    lax.fori_loop(0, nkt, score_and_lists, 0)
    k_last = pl.multiple_of((nkt - 1) * SEL_TK, SEL_TK)
    sc_ref[pl.ds(k_last, SEL_TK), :] = jnp.where(
        k_last + krow_t <= qpos_t, sc_ref[pl.ds(k_last, SEL_TK), :], -jnp.inf)
    lists_tile(nkt - 1)
    ckey_ref[...] = _sortable_key(cand_ref[...])

    n_acc = 4

    def cand_bit(i, t):
        c = t ^ lax.shift_left(jnp.int32(1), 31 - i)
        acc = [jnp.zeros(vshape, I32) for _ in range(n_acc)]
        for v in range(n_cand):
            x = ckey_ref[v * SUBLANES:(v + 1) * SUBLANES, :]
            acc[v % n_acc] = acc[v % n_acc] + jnp.where(x >= c, 1, 0)
        return jnp.where(col_sum(acc) >= k_top, c, t)

    st_ref[0] = lax.fori_loop(0, 32, cand_bit, jnp.full(vshape, INT_MIN, I32))

    def count_scores(preds):
        def tile(kt, acc):
            acc = [list(a) for a in acc]
            for v in range(vt):
                x = vreg(sc_ref, kt * SEL_TK + v * SUBLANES)
                kpos = kt * SEL_TK + v * SUBLANES + sub
                for p, pred in enumerate(preds):
                    acc[p][v % n_acc] = acc[p][v % n_acc] + jnp.where(pred(x, kpos), 1, 0)
            return tuple(tuple(a) for a in acc)

        zero = tuple(tuple(jnp.zeros(vshape, I32) for _ in range(n_acc)) for _ in preds)
        return [col_sum(list(a)) for a in lax.fori_loop(0, nkt, tile, zero)]

    def count_around_threshold():
        tf = _key_to_float(st_ref[0])
        gt, ge = count_scores([lambda x, kpos: x > tf, lambda x, kpos: x >= tf])
        st_ref[1] = gt
        st_ref[2] = ge

    tf_fast = jnp.broadcast_to(_key_to_float(st_ref[0])[0:1, :], (SEL_TK, LANES))

    def fast_tile(kt, acc):
        k0 = pl.multiple_of(kt * SEL_TK, SEL_TK)
        m = jnp.where(sc_ref[pl.ds(k0, SEL_TK), :] >= tf_fast, 1.0, 0.0)
        mask_ref[0, pl.ds(k0, SEL_TK), :] = m.astype(BF16)
        acc = list(acc)
        for v in range(vt):
            acc[v % n_acc] = acc[v % n_acc] + m[v * SUBLANES:(v + 1) * SUBLANES, :]
        return tuple(acc)

    size = col_sum(list(lax.fori_loop(0, nkt, fast_tile, tuple(jnp.zeros(vshape, F32) for _ in range(n_acc)))))

    @pl.when(jnp.max(jnp.abs(size - k_top)) > 0.0)
    def _():
        count_around_threshold()

        @pl.when(jnp.max(st_ref[1]) >= k_top)
        def _():
            def full_bit(i, t):
                c = t ^ lax.shift_left(jnp.int32(1), 31 - i)
                cnt, = count_scores([lambda x, kpos: _sortable_key(x) >= c])
                return jnp.where(cnt >= k_top, c, t)

            st_ref[0] = lax.fori_loop(0, 32, full_bit, jnp.full(vshape, INT_MIN, I32))
            count_around_threshold()

        st_ref[3] = jnp.full(vshape, n_rows, I32)

        @pl.when(jnp.max(st_ref[2]) > k_top)
        def _():
            tf = _key_to_float(st_ref[0])
            need = k_top - st_ref[1]
            n_bits = n_rows.bit_length() - 1

            def pos_bit(i, p):
                c = p + lax.shift_left(jnp.int32(1), n_bits - 1 - i)
                f, = count_scores([lambda x, kpos: (x == tf) & (kpos < c)])
                return jnp.where(f < need, c, p)

            st_ref[3] = lax.fori_loop(0, n_bits, pos_bit, jnp.zeros(vshape, I32))

        tf_t = jnp.broadcast_to(_key_to_float(st_ref[0])[0:1, :], (SEL_TK, LANES))
        lim_t = jnp.broadcast_to(st_ref[3][0:1, :], (SEL_TK, LANES))

        def write_tile(kt, carry):
            k0 = pl.multiple_of(kt * SEL_TK, SEL_TK)
            x = sc_ref[pl.ds(k0, SEL_TK), :]
            kpos = k0 + krow_t
            sel = ((x > tf_t) | ((x == tf_t) & (kpos <= lim_t))) & (kpos <= qpos_t)
            mask_ref[0, pl.ds(k0, SEL_TK), :] = jnp.where(sel, 1.0, 0.0).astype(BF16)
            return carry

        lax.fori_loop(0, nkt, write_tile, 0)

    def fill_tile(kt, carry):
        mask_ref[0, pl.ds(pl.multiple_of(kt * SEL_TK, SEL_TK), SEL_TK), :] = jnp.zeros((SEL_TK, LANES), BF16)
        return carry

    lax.fori_loop(nkt, n_tiles, fill_tile, 0)


def _select(qi_pairs, w_t, ki_ext, k_top):
    nqb, npair, kw, _ = qi_pairs.shape
    s = ki_ext.shape[0]
    n_cand_rows = SEL_GROUPS * SEL_DEPTH * SUBLANES
    assert s % SEL_TK == 0 and SEL_TK == n_cand_rows and n_cand_rows >= k_top
    return pl.pallas_call(
        functools.partial(_select_kernel, k_top=k_top),
        grid=(nqb,),
        in_specs=[
            pl.BlockSpec((1, npair, kw, 2 * SEL_TQ), lambda i: (i, 0, 0, 0)),
            pl.BlockSpec((w_t.shape[0], SEL_TQ), lambda i: (0, i)),
            pl.BlockSpec((s, kw), lambda i: (0, 0)),
        ],
        out_specs=pl.BlockSpec((1, s, SEL_TQ), lambda i: (i, 0, 0)),
        out_shape=jax.ShapeDtypeStruct((nqb, s, SEL_TQ), BF16),
        scratch_shapes=[
            pltpu.VMEM((s + SEL_TK, SEL_TQ), F32),
            pltpu.VMEM((n_cand_rows, SEL_TQ), F32),
            pltpu.VMEM((n_cand_rows, SEL_TQ), I32),
            pltpu.VMEM((4, SUBLANES, LANES), I32),
        ],
        compiler_params=pltpu.CompilerParams(dimension_semantics=("arbitrary",), vmem_limit_bytes=VMEM_LIMIT),
        name="select",
    )(qi_pairs, w_t, ki_ext)


def _causal_pairs(s, tq, tk):
    pairs = [(qb, kt) for qb in range(s // tq) for kt in range((qb * tq + tq - 1) // tk + 1)]
    return jnp.asarray([p[0] for p in pairs], I32), jnp.asarray([p[1] for p in pairs], I32)


def _attn_kernel(qb_tab, kt_tab, qt_ref, kmax_ref, k_ref, vt_ref, mask_ref, bias_ref, o_ref, l_ref, qx_ref, acc_ref):
    step = pl.program_id(0)
    qb = qb_tab[step]
    kt = kt_tab[step]
    q0 = qb * ATT_TQ
    k0 = kt * ATT_TK
    nq = ATT_TQ // LANES
    nk = ATT_TK // LANES

    @pl.when(kt == 0)
    def _():
        row = lax.broadcasted_iota(I32, (HEAD_DIM, ATT_TQ), 0)
        for h in range(N_HEADS):
            q = qt_ref[h]
            qf = q.astype(F32)
            bound = jnp.sqrt(jnp.sum(qf * qf, axis=0, keepdims=True)) * kmax_ref[h] * 1.02
            qx_ref[h, 0:HEAD_DIM, :] = q
            qx_ref[h, HEAD_DIM:, :] = jnp.where(row == 0, -bound, 0.0).astype(BF16)
        acc_ref[...] = jnp.zeros(acc_ref.shape, F32)

    mask = jnp.concatenate([mask_ref[a] for a in range(nq)], axis=1)

    def heads(with_bias):
        for h in range(N_HEADS):
            s = _dot(k_ref[h], qx_ref[h])
            if with_bias:
                rows = []
                for c in range(nk):
                    tab = [jnp.clip((q0 + a * LANES - k0 - c * LANES) // LANES + 1, 0, 3) for a in range(nq)]
                    rows.append(jnp.concatenate([bias_ref[tab[a], h] for a in range(nq)], axis=1))
                s = s + jnp.concatenate(rows, axis=0)
            p = jnp.exp2(s).astype(BF16) * mask
            acc_ref[h] += _dot(vt_ref[h], p)

    near = k0 + ATT_TK + 2 * LANES > q0

    @pl.when(near)
    def _():
        heads(True)

    @pl.when(jnp.logical_not(near))
    def _():
        heads(False)

    @pl.when(kt == (q0 + ATT_TQ - 1) // ATT_TK)
    def _():
        for h in range(N_HEADS):
            acc = acc_ref[h]
            den = acc[HEAD_DIM:HEAD_DIM + 1, :]
            o_ref[h] = acc[:HEAD_DIM, :] / den
            l_ref[h] = den


def _attn(q_t, kmax_b, k_ext, v_t, mask, bias_tab):
    nh, hd, s = q_t.shape
    qb_tab, kt_tab = _causal_pairs(s, ATT_TQ, ATT_TK)
    grid_spec = pltpu.PrefetchScalarGridSpec(
        num_scalar_prefetch=2,
        grid=(qb_tab.shape[0],),
        in_specs=[
            pl.BlockSpec((nh, hd, ATT_TQ), lambda i, qb, kt: (0, 0, qb[i])),
            pl.BlockSpec((nh, 1, ATT_TQ), lambda i, qb, kt: (0, 0, 0)),
            pl.BlockSpec((nh, ATT_TK, LANES), lambda i, qb, kt: (0, kt[i], 0)),
            pl.BlockSpec((nh, LANES, ATT_TK), lambda i, qb, kt: (0, 0, kt[i])),
            pl.BlockSpec((ATT_TQ // LANES, ATT_TK, LANES), lambda i, qb, kt: (qb[i], kt[i], 0)),
            pl.BlockSpec(bias_tab.shape, lambda i, qb, kt: (0, 0, 0, 0)),
        ],
        out_specs=[
            pl.BlockSpec((nh, hd, ATT_TQ), lambda i, qb, kt: (0, 0, qb[i])),
            pl.BlockSpec((nh, 1, ATT_TQ), lambda i, qb, kt: (0, 0, qb[i])),
        ],
        scratch_shapes=[pltpu.VMEM((nh, LANES, ATT_TQ), BF16), pltpu.VMEM((nh, LANES, ATT_TQ), F32)],
    )
    return pl.pallas_call(
        _attn_kernel,
        grid_spec=grid_spec,
        out_shape=[jax.ShapeDtypeStruct((nh, hd, s), F32), jax.ShapeDtypeStruct((nh, 1, s), F32)],
        compiler_params=pltpu.CompilerParams(dimension_semantics=("arbitrary",), vmem_limit_bytes=VMEM_LIMIT),
        name="attn",
    )(qb_tab, kt_tab, q_t, kmax_b, k_ext, v_t, mask, bias_tab)


def _attn_safe_kernel(qb_tab, kt_tab, q_ref, k_ref, v_ref, mask_ref, bias_ref, o_ref, m_ref, acc_ref):
    step = pl.program_id(0)
    qb = qb_tab[step]
    kt = kt_tab[step]
    nq = SAFE_TQ // LANES
    nsub = ATT_TK // LANES

    @pl.when(kt == 0)
    def _():
        m_ref[...] = jnp.full(m_ref.shape, NEG, F32)
        acc_ref[...] = jnp.zeros(acc_ref.shape, F32)

    for a in range(nq):
        rows = slice(a * LANES, (a + 1) * LANES)
        q0 = qb * SAFE_TQ + a * LANES
        maskf = (mask_ref[a].astype(F32) - 1.0) * (-NEG)
        tab_idx = [jnp.clip((q0 - (kt * ATT_TK + c * LANES)) // LANES + 1, 0, 3) for c in range(nsub)]
        for h in range(N_HEADS):
            s = _dot_t(q_ref[h, rows, :], k_ref[h]) + maskf
            s = s + jnp.concatenate([bias_ref[tab_idx[c], h] for c in range(nsub)], axis=1)
            m_old = m_ref[h, rows, :]
            m_new = jnp.maximum(m_old, jnp.max(s, axis=1, keepdims=True))
            p = jnp.exp2(s - m_new[:, 0:1])
            alpha = jnp.exp2(m_old - m_new)
            acc_ref[h, rows, :] = alpha * acc_ref[h, rows, :] + _dot(p.astype(BF16), v_ref[h])
            m_ref[h, rows, :] = m_new

    @pl.when(kt == (qb * SAFE_TQ + SAFE_TQ - 1) // ATT_TK)
    def _():
        for h in range(N_HEADS):
            acc = acc_ref[h]
            o_ref[h] = acc[:, :HEAD_DIM] / acc[:, HEAD_DIM:HEAD_DIM + 1]


def _attn_safe(q, k, v_ext, mask_qk, bias_tab):
    nh, s, hd = q.shape
    qb_tab, kt_tab = _causal_pairs(s, SAFE_TQ, ATT_TK)
    grid_spec = pltpu.PrefetchScalarGridSpec(
        num_scalar_prefetch=2,
        grid=(qb_tab.shape[0],),
        in_specs=[
            pl.BlockSpec((nh, SAFE_TQ, hd), lambda i, qb, kt: (0, qb[i], 0)),
            pl.BlockSpec((nh, ATT_TK, hd), lambda i, qb, kt: (0, kt[i], 0)),
            pl.BlockSpec((nh, ATT_TK, LANES), lambda i, qb, kt: (0, kt[i], 0)),
            pl.BlockSpec((SAFE_TQ // LANES, LANES, ATT_TK), lambda i, qb, kt: (qb[i], 0, kt[i])),
            pl.BlockSpec(bias_tab.shape, lambda i, qb, kt: (0, 0, 0, 0)),
        ],
        out_specs=pl.BlockSpec((nh, SAFE_TQ, hd), lambda i, qb, kt: (0, qb[i], 0)),
        scratch_shapes=[pltpu.VMEM((nh, SAFE_TQ, LANES), F32), pltpu.VMEM((nh, SAFE_TQ, LANES), F32)],
    )
    return pl.pallas_call(
        _attn_safe_kernel,
        grid_spec=grid_spec,
        out_shape=jax.ShapeDtypeStruct((nh, s, hd), F32),
        compiler_params=pltpu.CompilerParams(dimension_semantics=("arbitrary",), vmem_limit_bytes=VMEM_LIMIT),
        name="attn_safe",
    )(qb_tab, kt_tab, q, k, v_ext, mask_qk, bias_tab)


def _merge_kernel(h_ref, cg_ref, ga_ref, at_ref, wao_ref, wmo_ref, o_ref):
    attn = at_ref[...].T.astype(BF16)
    merged = cg_ref[...] + ga_ref[...] * _dot(attn, wao_ref[...])
    o_ref[...] = h_ref[...] + _dot(merged.astype(BF16), wmo_ref[...])


def _merge(h, conv_g, gate_a, attn_t, w_ao, w_mo, tm=512):
    s, d = h.shape
    full = lambda a: pl.BlockSpec(a.shape, lambda i: (0,) * a.ndim)
    row = lambda n: pl.BlockSpec((tm, n), lambda i: (i, 0))
    return pl.pallas_call(
        _merge_kernel,
        grid=(s // tm,),
        in_specs=[row(d), row(d), row(d), pl.BlockSpec((attn_t.shape[0], tm), lambda i: (0, i)),
                  full(w_ao), full(w_mo)],
        out_specs=row(d),
        out_shape=jax.ShapeDtypeStruct((s, d), F32),
        compiler_params=pltpu.CompilerParams(dimension_semantics=("arbitrary",), vmem_limit_bytes=VMEM_LIMIT),
        name="merge",
    )(h, conv_g, gate_a, attn_t, w_ao, w_mo)


def _ple_kernel(h_ref, g_ref, p_ref, wg_ref, wp_ref, o_ref):
    h = h_ref[...]
    gate = _sigmoid(_dot(_rms(h, g_ref[...]).astype(BF16), wg_ref[...]))
    o_ref[...] = h + gate * _dot(p_ref[...].astype(BF16), wp_ref[...])


def _ple(h, g, p, w_gate, w_proj, tm=512):
    s, d = h.shape
    full = lambda a: pl.BlockSpec(a.shape, lambda i: (0,) * a.ndim)
    row = lambda n: pl.BlockSpec((tm, n), lambda i: (i, 0))
    g2 = g.reshape(1, d)
    return pl.pallas_call(
        _ple_kernel,
        grid=(s // tm,),
        in_specs=[row(d), full(g2), row(p.shape[1]), full(w_gate), full(w_proj)],
        out_specs=row(d),
        out_shape=jax.ShapeDtypeStruct((s, d), F32),
        compiler_params=pltpu.CompilerParams(dimension_semantics=("arbitrary",), vmem_limit_bytes=VMEM_LIMIT),
        name="ple",
    )(h, g2, p, w_gate, w_proj)


def _t5_bucket_table(n_dist):
    n = np.arange(n_dist)
    max_exact = NUM_BUCKETS // 2
    nf = np.maximum(n, 1).astype(np.float32)
    large = max_exact + (np.log(nf / max_exact) / math.log(MAX_DISTANCE / max_exact)
                         * (NUM_BUCKETS - max_exact)).astype(np.int32)
    large = np.minimum(large, NUM_BUCKETS - 1)
    return np.where(n < max_exact, n, large)


def _bias_tables(rel_bias):
    assert MAX_DISTANCE <= LANES
    n = LANES
    rel = (rel_bias.astype(F32) - rel_bias[NUM_BUCKETS - 1].astype(F32)[None, :]) * LOG2E
    by_dist = rel[_t5_bucket_table(2 * n)].T

    def toeplitz(a):
        skew = jnp.tile(a, (1, n))[:, :n * (2 * n - 1)].reshape(a.shape[0], n, 2 * n - 1)
        return skew[:, :, :n]

    d0 = toeplitz(jnp.concatenate([by_dist[:, :n], jnp.broadcast_to(by_dist[:, :1], by_dist[:, :n].shape)], axis=1))
    d1 = toeplitz(jnp.concatenate([by_dist[:, n:], by_dist[:, :n]], axis=1))
    zero = jnp.zeros_like(d0)
    return jnp.swapaxes(jnp.stack([zero, d0, d1, zero]), 2, 3)


def kernel(x, p, ffn1_norm, ffn1_w_in, ffn1_w_out, mix_norm, mix_w_in, conv_dw_w, conv_dw_b, conv_ln_g,
           conv_ln_b, conv_w_out, q_norm, k_norm, attn_w_out, mix_w_out, ffn2_norm, ffn2_w_in, ffn2_w_out,
           ple_norm, ple_w_gate, ple_w_proj, rel_bias):
    b, s, d = x.shape
    depth = ffn1_norm.shape[0]
    cw = conv_dw_w.shape[2]
    aw = N_HEADS * HEAD_DIM
    nqi = IDX_HEADS * IDX_DIM
    iw = nqi + IDX_DIM + IDX_HEADS
    iw_pad = -(-iw // LANES) * LANES
    k_top = min(TOPK_MAX, s // 4)
    assert b == 1 and s % ATT_TK == 0 and mix_w_in.shape[2] == 2 * cw + 3 * aw + iw + 2 * d

    head_blocks = jnp.asarray(np.kron(np.eye(N_HEADS), np.ones((HEAD_DIM, HEAD_DIM))), BF16)
    bias_qk = _bias_tables(rel_bias)
    bias_kq = jnp.swapaxes(bias_qk, 2, 3)

    h = x[0]
    for i in range(depth):
        h = _ffn(h, ffn1_norm[i], ffn1_w_in[i].astype(BF16), ffn1_w_out[i].astype(BF16))

        w = mix_w_in[i]
        o0 = 2 * cw
        o1 = o0 + 3 * aw
        o2 = o1 + iw
        w_idx = jnp.pad(w[:, o1:o2], ((0, 0), (0, iw_pad - iw)))
        w_idx_hi, w_idx_lo = _split_bf16(w_idx)
        qg = jnp.tile(q_norm[i], N_HEADS).reshape(1, aw)
        kg = jnp.tile(k_norm[i], N_HEADS).reshape(1, aw)
        glu, q_t, k_ext, v_t, qi_pairs, ki_ext, idx_w, gate_c, gate_a, ksq = _mix_in(
            h, mix_norm[i], w[:, :o0].astype(BF16), w[:, o0:o1].astype(BF16), w_idx_hi, w_idx_lo,
            w[:, o2:].astype(BF16), qg, kg, head_blocks)

        conv_g = _conv(glu, conv_dw_w[i], conv_dw_b[i], conv_ln_g[i], conv_ln_b[i],
                       conv_w_out[i].astype(BF16), gate_c)

        w_t = idx_w[:, IDX_DIM:IDX_DIM + IDX_HEADS].T
        mask = _select(qi_pairs, w_t, ki_ext, k_top)

        kmax = jnp.sqrt(ksq.reshape(N_HEADS, HEAD_DIM)[:, :1])
        kmax_b = jnp.broadcast_to(kmax[:, :, None], (N_HEADS, 1, ATT_TQ))
        attn_t, den = _attn(q_t, kmax_b, k_ext, v_t, mask, bias_kq)

        def safe_attn():
            out = _attn_safe(jnp.swapaxes(q_t, 1, 2), k_ext[:, :, :HEAD_DIM], jnp.swapaxes(v_t, 1, 2),
                             jnp.swapaxes(mask, 1, 2), bias_qk)
            return jnp.swapaxes(out, 1, 2)

        underflow = jnp.logical_not(jnp.min(den) > L_MIN)
        attn_t = lax.cond(underflow, safe_attn, lambda: attn_t).reshape(aw, s)

        h = _merge(h, conv_g, gate_a, attn_t, attn_w_out[i].astype(BF16), mix_w_out[i].astype(BF16))
        h = _ffn(h, ffn2_norm[i], ffn2_w_in[i].astype(BF16), ffn2_w_out[i].astype(BF16))
        h = _ple(h, ple_norm[i], p[i, 0], ple_w_gate[i].astype(BF16), ple_w_proj[i].astype(BF16))
    return h[None]
```

```python
import functools
import math

import numpy as np
import jax
import jax.numpy as jnp
from jax import lax
from jax.experimental import pallas as pl
from jax.experimental.pallas import tpu as pltpu

F32 = jnp.float32
BF16 = jnp.bfloat16
I32 = jnp.int32

EPS = 1e-6
CONV_WIDTH = 31
N_HEADS = 8
HEAD_DIM = 64
IDX_HEADS = 4
IDX_DIM = 64
TOPK_MAX = 256
NUM_BUCKETS = 32
MAX_DISTANCE = 128

LANES = 128
SUBLANES = 8
MXU_TILE = 256
FFN_CHUNK = 1024
VMEM_LIMIT = 56 * 1024 * 1024
NEG = -1e30
INT_MIN = -2 ** 31
MIN_NORMAL_BITS = 0x00800000
LOG2E = math.log2(math.e)

CONV_HALO = 32
CONV_CHUNK = 32
SEL_TQ = LANES
SEL_TK = 1024
SEL_GROUPS = 8
SEL_DEPTH = 16
ATT_TQ = 1024
ATT_TK = 1024
SAFE_TQ = 256
L_MIN = 1e-30


def _sigmoid(x):
    return 1.0 / (1.0 + jnp.exp(-x))


def _rms(x, g):
    ms = jnp.mean(x * x, axis=-1, keepdims=True)
    return x * lax.rsqrt(ms + EPS) * g


def _dot(a, b):
    return jnp.dot(a, b, preferred_element_type=F32)


def _dot_t(a, b):
    return lax.dot_general(a, b, (((1,), (1,)), ((), ())), preferred_element_type=F32)


def _split_bf16(x):
    hi = x.astype(BF16)
    lo = (x - hi.astype(F32)).astype(BF16)
    return hi, lo


def _half_step_ffn(x, g, wi_ref, wo_ref):
    xn = _rms(x, g).astype(BF16)
    dff = wo_ref.shape[0]
    acc = None
    for c0 in range(0, dff, FFN_CHUNK):
        c1 = min(c0 + FFN_CHUNK, dff)
        a = _dot(xn, wi_ref[:, c0:c1])
        b = _dot(xn, wi_ref[:, dff + c0:dff + c1])
        part = _dot((a * _sigmoid(a) * b).astype(BF16), wo_ref[c0:c1, :])
        acc = part if acc is None else acc + part
    return x + 0.5 * acc


def _ffn_kernel(x_ref, g_ref, wi_ref, wo_ref, o_ref):
    o_ref[...] = _half_step_ffn(x_ref[...], g_ref[...], wi_ref, wo_ref)


def _resident(a):
    return pl.BlockSpec(a.shape, lambda *_: (0,) * a.ndim, pipeline_mode=pl.Buffered(1))


def _ffn(x, g, w_in, w_out, tm=512):
    s, d = x.shape
    dff = w_out.shape[0]
    assert s % tm == 0 and dff % MXU_TILE == 0 and FFN_CHUNK % MXU_TILE == 0
    g2 = g.reshape(1, d)
    return pl.pallas_call(
        _ffn_kernel,
        grid=(s // tm,),
        in_specs=[pl.BlockSpec((tm, d), lambda i: (i, 0)), _resident(g2), _resident(w_in), _resident(w_out)],
        out_specs=pl.BlockSpec((tm, d), lambda i: (i, 0)),
        out_shape=jax.ShapeDtypeStruct((s, d), F32),
        compiler_params=pltpu.CompilerParams(dimension_semantics=("arbitrary",), vmem_limit_bytes=VMEM_LIMIT),
        name="ffn",
    )(x, g2, w_in, w_out)


def _mix_in_kernel(h_ref, g_ref, wc_ref, wqkv_ref, wih_ref, wil_ref, wg_ref, qg_ref, kg_ref, hb_ref,
                   glu_ref, qt_ref, kx_ref, vt_ref, qip_ref, kix_ref, wi_ref, gc_ref, ga_ref, ksq_ref):
    u = _rms(h_ref[...], g_ref[...])
    u_hi, u_lo = _split_bf16(u)
    tm = h_ref.shape[0]
    cw = glu_ref.shape[1]
    aw = N_HEADS * HEAD_DIM
    nqi = IDX_HEADS * IDX_DIM
    d = gc_ref.shape[1]

    c = _dot(u_hi, wc_ref[...])
    glu_ref[...] = c[:, :cw] * _sigmoid(c[:, cw:])

    qkv = _dot(u_hi, wqkv_ref[...])
    hb = hb_ref[...]

    def head_sumsq(t):
        t2_hi, t2_lo = _split_bf16(t * t)
        return _dot(t2_hi, hb) + _dot(t2_lo, hb)

    def head_norm(t, g):
        return t * lax.rsqrt(head_sumsq(t) * (1.0 / HEAD_DIM) + EPS) * g

    q = head_norm(qkv[:, :aw], qg_ref[...]) * (HEAD_DIM ** -0.5 * LOG2E)
    q_t = q.T
    for hd in range(N_HEADS):
        qt_ref[hd] = q_t[hd * HEAD_DIM:(hd + 1) * HEAD_DIM, :].astype(BF16)
    k = head_norm(qkv[:, aw:2 * aw], kg_ref[...]).astype(BF16).astype(F32)
    one_hot0 = lambda shape, axis: jnp.where(lax.broadcasted_iota(I32, shape, axis) == 0, 1.0, 0.0)
    k_pad = one_hot0((tm, LANES - HEAD_DIM), 1)
    for hd in range(N_HEADS):
        kx_ref[hd] = jnp.concatenate([k[:, hd * HEAD_DIM:(hd + 1) * HEAD_DIM], k_pad], axis=1).astype(BF16)
    v_t = qkv[:, 2 * aw:].T
    v_pad = one_hot0((LANES - HEAD_DIM, tm), 0).astype(BF16)
    for hd in range(N_HEADS):
        vt_ref[hd, 0:HEAD_DIM, :] = v_t[hd * HEAD_DIM:(hd + 1) * HEAD_DIM, :].astype(BF16)
        vt_ref[hd, HEAD_DIM:, :] = v_pad

    ksq = jnp.max(head_sumsq(k), axis=0, keepdims=True)

    @pl.when(pl.program_id(0) == 0)
    def _():
        ksq_ref[...] = ksq

    @pl.when(pl.program_id(0) > 0)
    def _():
        ksq_ref[...] = jnp.maximum(ksq_ref[...], ksq)

    wih = wih_ref[...]
    idx = _dot(u_hi, wih) + _dot(u_lo, wih) + _dot(u_hi, wil_ref[...])
    qi_hi, qi_lo = _split_bf16(idx[:, :nqi].T)
    for blk in range(tm // SEL_TQ):
        cols = slice(blk * SEL_TQ, (blk + 1) * SEL_TQ)
        for hd in range(IDX_HEADS):
            rows = slice(hd * IDX_DIM, (hd + 1) * IDX_DIM)
            out_cols = slice((hd % 2) * SEL_TQ, (hd % 2 + 1) * SEL_TQ)
            for part, src in enumerate((qi_hi, qi_hi, qi_lo, qi_lo)):
                qip_ref[blk, hd // 2, part * IDX_DIM:(part + 1) * IDX_DIM, out_cols] = src[rows, cols]
    ki = idx[:, nqi:nqi + IDX_DIM]
    ki_hi = ki.astype(BF16).astype(F32)
    ki_lo = ki - ki_hi
    kix_ref[...] = jnp.concatenate([ki_hi, ki_lo, ki_hi, ki_lo], axis=1).astype(BF16)
    wi_ref[...] = idx[:, nqi:]

    gates = _sigmoid(_dot(u_hi, wg_ref[...]))
    gc_ref[...] = gates[:, :d]
    ga_ref[...] = gates[:, d:]


def _mix_in(h, g, w_conv, w_qkv, w_idx_hi, w_idx_lo, w_gate, qg, kg, head_blocks, tm=512):
    s, d = h.shape
    cw = w_conv.shape[1] // 2
    aw = w_qkv.shape[1] // 3
    nqi = IDX_HEADS * IDX_DIM
    iw = w_idx_hi.shape[1]
    assert aw == N_HEADS * HEAD_DIM and tm % SEL_TQ == 0 and iw - nqi == LANES
    full = lambda a: pl.BlockSpec(a.shape, lambda i: (0,) * a.ndim)
    row = lambda n: pl.BlockSpec((tm, n), lambda i: (i, 0))
    g2 = g.reshape(1, d)
    return pl.pallas_call(
        _mix_in_kernel,
        grid=(s // tm,),
        in_specs=[row(d), full(g2), full(w_conv), full(w_qkv), full(w_idx_hi), full(w_idx_lo), full(w_gate),
                  full(qg), full(kg), full(head_blocks)],
        out_specs=[
            row(cw),
            pl.BlockSpec((N_HEADS, HEAD_DIM, tm), lambda i: (0, 0, i)),
            pl.BlockSpec((N_HEADS, tm, LANES), lambda i: (0, i, 0)),
            pl.BlockSpec((N_HEADS, LANES, tm), lambda i: (0, 0, i)),
            pl.BlockSpec((tm // SEL_TQ, IDX_HEADS // 2, 4 * IDX_DIM, 2 * SEL_TQ), lambda i: (i, 0, 0, 0)),
            row(4 * IDX_DIM), row(LANES), row(d), row(d),
            pl.BlockSpec((1, aw), lambda i: (0, 0)),
        ],
        out_shape=[
            jax.ShapeDtypeStruct((s, cw), F32),
            jax.ShapeDtypeStruct((N_HEADS, HEAD_DIM, s), BF16),
            jax.ShapeDtypeStruct((N_HEADS, s, LANES), BF16),
            jax.ShapeDtypeStruct((N_HEADS, LANES, s), BF16),
            jax.ShapeDtypeStruct((s // SEL_TQ, IDX_HEADS // 2, 4 * IDX_DIM, 2 * SEL_TQ), BF16),
            jax.ShapeDtypeStruct((s, 4 * IDX_DIM), BF16),
            jax.ShapeDtypeStruct((s, LANES), F32),
            jax.ShapeDtypeStruct((s, d), F32),
            jax.ShapeDtypeStruct((s, d), F32),
            jax.ShapeDtypeStruct((1, aw), F32),
        ],
        compiler_params=pltpu.CompilerParams(dimension_semantics=("arbitrary",), vmem_limit_bytes=VMEM_LIMIT),
        name="mix_in",
    )(h, g2, w_conv, w_qkv, w_idx_hi, w_idx_lo, w_gate, qg, kg, head_blocks)


def _conv_kernel(z_ref, halo_ref, dw_ref, db_ref, lg_ref, lb_ref, wo_ref, gc_ref, o_ref, zp_ref, zs_ref, acc_ref):
    tm = z_ref.shape[0]
    first = pl.program_id(0) == 0
    halo = halo_ref[...]
    zp_ref[0:CONV_HALO, :] = jnp.where(first, jnp.zeros_like(halo), halo)
    zp_ref[CONV_HALO:, :] = z_ref[...]
    off = CONV_HALO - (CONV_WIDTH - 1)
    span = zs_ref.shape[1]
    for b in range(1, SUBLANES):
        zs_ref[b - 1] = zp_ref[b:b + span, :]

    def chunk(c, carry):
        r0 = pl.multiple_of(c * CONV_CHUNK, CONV_CHUNK)
        acc = jnp.zeros((CONV_CHUNK, z_ref.shape[1]), F32) + db_ref[...]
        for j in range(CONV_WIDTH):
            a, b = divmod(off + j, SUBLANES)
            src = zp_ref if b == 0 else zs_ref.at[b - 1]
            acc = acc + dw_ref[j:j + 1, :] * src[pl.ds(r0 + a * SUBLANES, CONV_CHUNK), :]
        acc_ref[pl.ds(r0, CONV_CHUNK), :] = acc
        return carry

    lax.fori_loop(0, tm // CONV_CHUNK, chunk, 0)
    acc = acc_ref[...]
    mu = jnp.mean(acc, axis=-1, keepdims=True)
    xc = acc - mu
    y = xc * lax.rsqrt(jnp.mean(xc * xc, axis=-1, keepdims=True) + EPS)
    y = y * lg_ref[...] + lb_ref[...]
    y = (y * _sigmoid(y)).astype(BF16)
    o_ref[...] = gc_ref[...] * _dot(y, wo_ref[...])


def _conv(z, dw_w, dw_b, ln_g, ln_b, w_out, gate_c, tm=512):
    s, c = z.shape
    d = w_out.shape[1]
    assert tm % CONV_HALO == 0 and tm % CONV_CHUNK == 0
    r = tm // CONV_HALO
    span = tm + CONV_HALO - SUBLANES
    full = lambda a: pl.BlockSpec(a.shape, lambda i: (0,) * a.ndim)
    vecs = [dw_b.reshape(1, c), ln_g.reshape(1, c), ln_b.reshape(1, c)]
    return pl.pallas_call(
        _conv_kernel,
        grid=(s // tm,),
        in_specs=[
            pl.BlockSpec((tm, c), lambda i: (i, 0)),
            pl.BlockSpec((CONV_HALO, c), lambda i: (jnp.maximum(i * r - 1, 0), 0)),
            full(dw_w), full(vecs[0]), full(vecs[1]), full(vecs[2]), full(w_out),
            pl.BlockSpec((tm, d), lambda i: (i, 0)),
        ],
        out_specs=pl.BlockSpec((tm, d), lambda i: (i, 0)),
        out_shape=jax.ShapeDtypeStruct((s, d), F32),
        scratch_shapes=[pltpu.VMEM((tm + CONV_HALO, c), F32), pltpu.VMEM((SUBLANES - 1, span, c), F32),
                        pltpu.VMEM((tm, c), F32)],
        compiler_params=pltpu.CompilerParams(dimension_semantics=("arbitrary",), vmem_limit_bytes=VMEM_LIMIT),
        name="conv",
    )(z, z, dw_w, *vecs, w_out, gate_c)


def _sortable_key(x):
    bits = pltpu.bitcast(x + 0.0, I32)
    return bits ^ ((bits >> 31) & 0x7FFFFFFF)


def _key_to_float(key):
    return pltpu.bitcast(key ^ ((key >> 31) & 0x7FFFFFFF), F32)


def _sort_network(n):
    pairs = []

    def merge(lo, hi, r):
        step = r * 2
        if step < hi - lo:
            merge(lo, hi, step)
            merge(lo + r, hi, step)
            pairs.extend((i, i + r) for i in range(lo + r, hi - r, step))
        else:
            pairs.append((lo, lo + r))

    def sort(lo, hi):
        if hi > lo:
            mid = lo + (hi - lo) // 2
            sort(lo, mid)
            sort(mid + 1, hi)
            merge(lo, hi, 1)

    sort(0, n - 1)
    return pairs


def _compare_exchange(a, i, j):
    a[i], a[j] = jnp.maximum(a[i], a[j]), jnp.minimum(a[i], a[j])


def _merge_top(top, batch):
    n = len(top)
    out = [jnp.maximum(top[i], batch[n - 1 - i]) for i in range(n)]
    d = n // 2
    while d >= 1:
        for i in range(n):
            if i & d == 0:
                _compare_exchange(out, i, i + d)
        d //= 2
    return out


def _select_kernel(qi_ref, w_ref, ki_ref, mask_ref, sc_ref, cand_ref, ckey_ref, st_ref, *, k_top):
    qb = pl.program_id(0)
    n_rows = mask_ref.shape[1]
    n_tiles = n_rows // SEL_TK
    q0 = qb * SEL_TQ
    nkt = (q0 + SEL_TQ - 1) // SEL_TK + 1
    vt = SEL_TK // SUBLANES
    vshape = (SUBLANES, LANES)
    qpos = q0 + lax.broadcasted_iota(I32, vshape, 1)
    sub = lax.broadcasted_iota(I32, vshape, 0)
    w = w_ref[...] * ((IDX_HEADS ** -0.5) * (IDX_DIM ** -0.5))

    qpos_t = q0 + lax.broadcasted_iota(I32, (SEL_TK, LANES), 1)
    krow_t = lax.broadcasted_iota(I32, (SEL_TK, LANES), 0)
    zero_bits_t = (MIN_NORMAL_BITS + n_rows) - krow_t

    def score_tile(kt):
        k0 = pl.multiple_of(kt * SEL_TK, SEL_TK)
        ki = ki_ref[pl.ds(k0, SEL_TK), :]
        sc = None
        for pair in range(IDX_HEADS // 2):
            r = _dot(ki, qi_ref[0, pair])
            for j in range(2):
                h = 2 * pair + j
                term = w[h:h + 1, :] * jnp.maximum(r[:, j * LANES:(j + 1) * LANES], 0.0)
                sc = term if sc is None else sc + term
        sc_ref[pl.ds(k0, SEL_TK), :] = jnp.where(sc == 0.0, pltpu.bitcast(zero_bits_t - k0, F32), sc)

    def vreg(ref, row):
        return ref[pl.ds(pl.multiple_of(row, SUBLANES), SUBLANES), :]

    def col_sum(parts):
        tot = parts[0]
        for part in parts[1:]:
            tot = tot + part
        return jnp.broadcast_to(jnp.sum(tot, axis=0, keepdims=True), vshape)

    n_cand = SEL_GROUPS * SEL_DEPTH
    network = _sort_network(SEL_DEPTH)
    cand_ref[...] = jnp.full(cand_ref.shape, -jnp.inf, F32)

    def lists_tile(kt):
        for g in range(SEL_GROUPS):
            batch = [vreg(sc_ref, kt * SEL_TK + (g + SEL_GROUPS * j) * SUBLANES) for j in range(SEL_DEPTH)]
            for i, j in network:
                _compare_exchange(batch, i, j)
            rows = [slice((g * SEL_DEPTH + i) * SUBLANES, (g * SEL_DEPTH + i + 1) * SUBLANES)
                    for i in range(SEL_DEPTH)]
            top = _merge_top([cand_ref[r, :] for r in rows], batch)
            for r, t in zip(rows, top):
                cand_ref[r, :] = t

    sc_ref[pl.ds(n_rows, SEL_TK), :] = jnp.full((SEL_TK, LANES), -jnp.inf, F32)

    def score_and_lists(kt, carry):
        lists_tile(jnp.where(kt == 0, n_tiles, kt - 1))
        score_tile(kt)
        return carry

    lax.fori_loop(0, nkt, score_and_lists, 0)
    k_last = pl.multiple_of((nkt - 1) * SEL_TK, SEL_TK)
    sc_ref[pl.ds(k_last, SEL_TK), :] = jnp.where(
        k_last + krow_t <= qpos_t, sc_ref[pl.ds(k_last, SEL_TK), :], -jnp.inf)
    lists_tile(nkt - 1)
    ckey_ref[...] = _sortable_key(cand_ref[...])

    n_acc = 4

    def cand_bit(i, t):
        c = t ^ lax.shift_left(jnp.int32(1), 31 - i)
        acc = [jnp.zeros(vshape, I32) for _ in range(n_acc)]
        for v in range(n_cand):
            x = ckey_ref[v * SUBLANES:(v + 1) * SUBLANES, :]
            acc[v % n_acc] = acc[v % n_acc] + jnp.where(x >= c, 1, 0)
        return jnp.where(col_sum(acc) >= k_top, c, t)

    st_ref[0] = lax.fori_loop(0, 32, cand_bit, jnp.full(vshape, INT_MIN, I32))

    def count_scores(preds):
        def tile(kt, acc):
            acc = [list(a) for a in acc]
            for v in range(vt):
                x = vreg(sc_ref, kt * SEL_TK + v * SUBLANES)
                kpos = kt * SEL_TK + v * SUBLANES + sub
                for p, pred in enumerate(preds):
                    acc[p][v % n_acc] = acc[p][v % n_acc] + jnp.where(pred(x, kpos), 1, 0)
            return tuple(tuple(a) for a in acc)

        zero = tuple(tuple(jnp.zeros(vshape, I32) for _ in range(n_acc)) for _ in preds)
        return [col_sum(list(a)) for a in lax.fori_loop(0, nkt, tile, zero)]

    def count_around_threshold():
        tf = _key_to_float(st_ref[0])
        gt, ge = count_scores([lambda x, kpos: x > tf, lambda x, kpos: x >= tf])
        st_ref[1] = gt
        st_ref[2] = ge

    tf_fast = jnp.broadcast_to(_key_to_float(st_ref[0])[0:1, :], (SEL_TK, LANES))

    def fast_tile(kt, acc):
        k0 = pl.multiple_of(kt * SEL_TK, SEL_TK)
        m = jnp.where(sc_ref[pl.ds(k0, SEL_TK), :] >= tf_fast, 1.0, 0.0)
        mask_ref[0, pl.ds(k0, SEL_TK), :] = m.astype(BF16)
        acc = list(acc)
        for v in range(vt):
            acc[v % n_acc] = acc[v % n_acc] + m[v * SUBLANES:(v + 1) * SUBLANES, :]
        return tuple(acc)

    size = col_sum(list(lax.fori_loop(0, nkt, fast_tile, tuple(jnp.zeros(vshape, F32) for _ in range(n_acc)))))

    @pl.when(jnp.max(jnp.abs(size - k_top)) > 0.0)
    def _():
        count_around_threshold()

        @pl.when(jnp.max(st_ref[1]) >= k_top)
        def _():
            def full_bit(i, t):
                c = t ^ lax.shift_left(jnp.int32(1), 31 - i)
                cnt, = count_scores([lambda x, kpos: _sortable_key(x) >= c])
                return jnp.where(cnt >= k_top, c, t)

            st_ref[0] = lax.fori_loop(0, 32, full_bit, jnp.full(vshape, INT_MIN, I32))
            count_around_threshold()

        st_ref[3] = jnp.full(vshape, n_rows, I32)

        @pl.when(jnp.max(st_ref[2]) > k_top)
        def _():
            tf = _key_to_float(st_ref[0])
            need = k_top - st_ref[1]
            n_bits = n_rows.bit_length() - 1

            def pos_bit(i, p):
                c = p + lax.shift_left(jnp.int32(1), n_bits - 1 - i)
                f, = count_scores([lambda x, kpos: (x == tf) & (kpos < c)])
                return jnp.where(f < need, c, p)

            st_ref[3] = lax.fori_loop(0, n_bits, pos_bit, jnp.zeros(vshape, I32))

        tf_t = jnp.broadcast_to(_key_to_float(st_ref[0])[0:1, :], (SEL_TK, LANES))
        lim_t = jnp.broadcast_to(st_ref[3][0:1, :], (SEL_TK, LANES))

        def write_tile(kt, carry):
            k0 = pl.multiple_of(kt * SEL_TK, SEL_TK)
            x = sc_ref[pl.ds(k0, SEL_TK), :]
            kpos = k0 + krow_t
            sel = ((x > tf_t) | ((x == tf_t) & (kpos <= lim_t))) & (kpos <= qpos_t)
            mask_ref[0, pl.ds(k0, SEL_TK), :] = jnp.where(sel, 1.0, 0.0).astype(BF16)
            return carry

        lax.fori_loop(0, nkt, write_tile, 0)

    def fill_tile(kt, carry):
        mask_ref[0, pl.ds(pl.multiple_of(kt * SEL_TK, SEL_TK), SEL_TK), :] = jnp.zeros((SEL_TK, LANES), BF16)
        return carry

    lax.fori_loop(nkt, n_tiles, fill_tile, 0)


def _select(qi_pairs, w_t, ki_ext, k_top):
    nqb, npair, kw, _ = qi_pairs.shape
    s = ki_ext.shape[0]
    n_cand_rows = SEL_GROUPS * SEL_DEPTH * SUBLANES
    assert s % SEL_TK == 0 and SEL_TK == n_cand_rows and n_cand_rows >= k_top
    return pl.pallas_call(
        functools.partial(_select_kernel, k_top=k_top),
        grid=(nqb,),
        in_specs=[
            pl.BlockSpec((1, npair, kw, 2 * SEL_TQ), lambda i: (i, 0, 0, 0)),
            pl.BlockSpec((w_t.shape[0], SEL_TQ), lambda i: (0, i)),
            pl.BlockSpec((s, kw), lambda i: (0, 0)),
        ],
        out_specs=pl.BlockSpec((1, s, SEL_TQ), lambda i: (i, 0, 0)),
        out_shape=jax.ShapeDtypeStruct((nqb, s, SEL_TQ), BF16),
        scratch_shapes=[
            pltpu.VMEM((s + SEL_TK, SEL_TQ), F32),
            pltpu.VMEM((n_cand_rows, SEL_TQ), F32),
            pltpu.VMEM((n_cand_rows, SEL_TQ), I32),
            pltpu.VMEM((4, SUBLANES, LANES), I32),
        ],
        compiler_params=pltpu.CompilerParams(dimension_semantics=("arbitrary",), vmem_limit_bytes=VMEM_LIMIT),
        name="select",
    )(qi_pairs, w_t, ki_ext)


def _causal_pairs(s, tq, tk):
    pairs = [(qb, kt) for qb in range(s // tq) for kt in range((qb * tq + tq - 1) // tk + 1)]
    return jnp.asarray([p[0] for p in pairs], I32), jnp.asarray([p[1] for p in pairs], I32)


def _attn_kernel(qb_tab, kt_tab, qt_ref, kmax_ref, k_ref, vt_ref, mask_ref, bias_ref, o_ref, l_ref, qx_ref, acc_ref):
    step = pl.program_id(0)
    qb = qb_tab[step]
    kt = kt_tab[step]
    q0 = qb * ATT_TQ
    k0 = kt * ATT_TK
    nq = ATT_TQ // LANES
    nk = ATT_TK // LANES

    @pl.when(kt == 0)
    def _():
        row = lax.broadcasted_iota(I32, (HEAD_DIM, ATT_TQ), 0)
        for h in range(N_HEADS):
            q = qt_ref[h]
            qf = q.astype(F32)
            bound = jnp.sqrt(jnp.sum(qf * qf, axis=0, keepdims=True)) * kmax_ref[h] * 1.02
            qx_ref[h, 0:HEAD_DIM, :] = q
            qx_ref[h, HEAD_DIM:, :] = jnp.where(row == 0, -bound, 0.0).astype(BF16)
        acc_ref[...] = jnp.zeros(acc_ref.shape, F32)

    mask = jnp.concatenate([mask_ref[a] for a in range(nq)], axis=1)

    def heads(with_bias):
        for h in range(N_HEADS):
            s = _dot(k_ref[h], qx_ref[h])
            if with_bias:
                rows = []
                for c in range(nk):
                    tab = [jnp.clip((q0 + a * LANES - k0 - c * LANES) // LANES + 1, 0, 3) for a in range(nq)]
                    rows.append(jnp.concatenate([bias_ref[tab[a], h] for a in range(nq)], axis=1))
                s = s + jnp.concatenate(rows, axis=0)
            p = jnp.exp2(s).astype(BF16) * mask
            acc_ref[h] += _dot(vt_ref[h], p)

    near = k0 + ATT_TK + 2 * LANES > q0

    @pl.when(near)
    def _():
        heads(True)

    @pl.when(jnp.logical_not(near))
    def _():
        heads(False)

    @pl.when(kt == (q0 + ATT_TQ - 1) // ATT_TK)
    def _():
        for h in range(N_HEADS):
            acc = acc_ref[h]
            den = acc[HEAD_DIM:HEAD_DIM + 1, :]
            o_ref[h] = acc[:HEAD_DIM, :] / den
            l_ref[h] = den


def _attn(q_t, kmax_b, k_ext, v_t, mask, bias_tab):
    nh, hd, s = q_t.shape
    qb_tab, kt_tab = _causal_pairs(s, ATT_TQ, ATT_TK)
    grid_spec = pltpu.PrefetchScalarGridSpec(
        num_scalar_prefetch=2,
        grid=(qb_tab.shape[0],),
        in_specs=[
            pl.BlockSpec((nh, hd, ATT_TQ), lambda i, qb, kt: (0, 0, qb[i])),
            pl.BlockSpec((nh, 1, ATT_TQ), lambda i, qb, kt: (0, 0, 0)),
            pl.BlockSpec((nh, ATT_TK, LANES), lambda i, qb, kt: (0, kt[i], 0)),
            pl.BlockSpec((nh, LANES, ATT_TK), lambda i, qb, kt: (0, 0, kt[i])),
            pl.BlockSpec((ATT_TQ // LANES, ATT_TK, LANES), lambda i, qb, kt: (qb[i], kt[i], 0)),
            pl.BlockSpec(bias_tab.shape, lambda i, qb, kt: (0, 0, 0, 0)),
        ],
        out_specs=[
            pl.BlockSpec((nh, hd, ATT_TQ), lambda i, qb, kt: (0, 0, qb[i])),
            pl.BlockSpec((nh, 1, ATT_TQ), lambda i, qb, kt: (0, 0, qb[i])),
        ],
        scratch_shapes=[pltpu.VMEM((nh, LANES, ATT_TQ), BF16), pltpu.VMEM((nh, LANES, ATT_TQ), F32)],
    )
    return pl.pallas_call(
        _attn_kernel,
        grid_spec=grid_spec,
        out_shape=[jax.ShapeDtypeStruct((nh, hd, s), F32), jax.ShapeDtypeStruct((nh, 1, s), F32)],
        compiler_params=pltpu.CompilerParams(dimension_semantics=("arbitrary",), vmem_limit_bytes=VMEM_LIMIT),
        name="attn",
    )(qb_tab, kt_tab, q_t, kmax_b, k_ext, v_t, mask, bias_tab)


def _attn_safe_kernel(qb_tab, kt_tab, q_ref, k_ref, v_ref, mask_ref, bias_ref, o_ref, m_ref, acc_ref):
    step = pl.program_id(0)
    qb = qb_tab[step]
    kt = kt_tab[step]
    nq = SAFE_TQ // LANES
    nsub = ATT_TK // LANES

    @pl.when(kt == 0)
    def _():
        m_ref[...] = jnp.full(m_ref.shape, NEG, F32)
        acc_ref[...] = jnp.zeros(acc_ref.shape, F32)

    for a in range(nq):
        rows = slice(a * LANES, (a + 1) * LANES)
        q0 = qb * SAFE_TQ + a * LANES
        maskf = (mask_ref[a].astype(F32) - 1.0) * (-NEG)
        tab_idx = [jnp.clip((q0 - (kt * ATT_TK + c * LANES)) // LANES + 1, 0, 3) for c in range(nsub)]
        for h in range(N_HEADS):
            s = _dot_t(q_ref[h, rows, :], k_ref[h]) + maskf
            s = s + jnp.concatenate([bias_ref[tab_idx[c], h] for c in range(nsub)], axis=1)
            m_old = m_ref[h, rows, :]
            m_new = jnp.maximum(m_old, jnp.max(s, axis=1, keepdims=True))
            p = jnp.exp2(s - m_new[:, 0:1])
            alpha = jnp.exp2(m_old - m_new)
            acc_ref[h, rows, :] = alpha * acc_ref[h, rows, :] + _dot(p.astype(BF16), v_ref[h])
            m_ref[h, rows, :] = m_new

    @pl.when(kt == (qb * SAFE_TQ + SAFE_TQ - 1) // ATT_TK)
    def _():
        for h in range(N_HEADS):
            acc = acc_ref[h]
            o_ref[h] = acc[:, :HEAD_DIM] / acc[:, HEAD_DIM:HEAD_DIM + 1]


def _attn_safe(q, k, v_ext, mask_qk, bias_tab):
    nh, s, hd = q.shape
    qb_tab, kt_tab = _causal_pairs(s, SAFE_TQ, ATT_TK)
    grid_spec = pltpu.PrefetchScalarGridSpec(
        num_scalar_prefetch=2,
        grid=(qb_tab.shape[0],),
        in_specs=[
            pl.BlockSpec((nh, SAFE_TQ, hd), lambda i, qb, kt: (0, qb[i], 0)),
            pl.BlockSpec((nh, ATT_TK, hd), lambda i, qb, kt: (0, kt[i], 0)),
            pl.BlockSpec((nh, ATT_TK, LANES), lambda i, qb, kt: (0, kt[i], 0)),
            pl.BlockSpec((SAFE_TQ // LANES, LANES, ATT_TK), lambda i, qb, kt: (qb[i], 0, kt[i])),
            pl.BlockSpec(bias_tab.shape, lambda i, qb, kt: (0, 0, 0, 0)),
        ],
        out_specs=pl.BlockSpec((nh, SAFE_TQ, hd), lambda i, qb, kt: (0, qb[i], 0)),
        scratch_shapes=[pltpu.VMEM((nh, SAFE_TQ, LANES), F32), pltpu.VMEM((nh, SAFE_TQ, LANES), F32)],
    )
    return pl.pallas_call(
        _attn_safe_kernel,
        grid_spec=grid_spec,
        out_shape=jax.ShapeDtypeStruct((nh, s, hd), F32),
        compiler_params=pltpu.CompilerParams(dimension_semantics=("arbitrary",), vmem_limit_bytes=VMEM_LIMIT),
        name="attn_safe",
    )(qb_tab, kt_tab, q, k, v_ext, mask_qk, bias_tab)


def _tail_kernel(h_ref, cg_ref, ga_ref, at_ref, p_ref, wao_ref, wmo_ref, fg_ref, wi_ref, wo_ref, pg_ref, wpg_ref,
                 wpp_ref, o_ref):
    attn = at_ref[...].T.astype(BF16)
    merged = cg_ref[...] + ga_ref[...] * _dot(attn, wao_ref[...])
    h = h_ref[...] + _dot(merged.astype(BF16), wmo_ref[...])
    h = _half_step_ffn(h, fg_ref[...], wi_ref, wo_ref)
    gate = _sigmoid(_dot(_rms(h, pg_ref[...]).astype(BF16), wpg_ref[...]))
    o_ref[...] = h + gate * _dot(p_ref[...].astype(BF16), wpp_ref[...])


def _tail(h, conv_g, gate_a, attn_t, p, w_ao, w_mo, ffn_g, w_in, w_out, ple_g, w_pg, w_pp, tm=512):
    s, d = h.shape
    row = lambda n: pl.BlockSpec((tm, n), lambda i: (i, 0))
    fg2 = ffn_g.reshape(1, d)
    pg2 = ple_g.reshape(1, d)
    params = (w_ao, w_mo, fg2, w_in, w_out, pg2, w_pg, w_pp)
    return pl.pallas_call(
        _tail_kernel,
        grid=(s // tm,),
        in_specs=[row(d), row(d), row(d), pl.BlockSpec((attn_t.shape[0], tm), lambda i: (0, i)), row(p.shape[1])]
        + [_resident(a) for a in params],
        out_specs=row(d),
        out_shape=jax.ShapeDtypeStruct((s, d), F32),
        compiler_params=pltpu.CompilerParams(dimension_semantics=("arbitrary",), vmem_limit_bytes=VMEM_LIMIT),
        name="tail",
    )(h, conv_g, gate_a, attn_t, p, *params)


def _t5_bucket_table(n_dist):
    n = np.arange(n_dist)
    max_exact = NUM_BUCKETS // 2
    nf = np.maximum(n, 1).astype(np.float32)
    large = max_exact + (np.log(nf / max_exact) / math.log(MAX_DISTANCE / max_exact)
                         * (NUM_BUCKETS - max_exact)).astype(np.int32)
    large = np.minimum(large, NUM_BUCKETS - 1)
    return np.where(n < max_exact, n, large)


def _bias_tables(rel_bias):
    assert MAX_DISTANCE <= LANES
    n = LANES
    rel = (rel_bias.astype(F32) - rel_bias[NUM_BUCKETS - 1].astype(F32)[None, :]) * LOG2E
    by_dist = rel[_t5_bucket_table(2 * n)].T

    def toeplitz(a):
        skew = jnp.tile(a, (1, n))[:, :n * (2 * n - 1)].reshape(a.shape[0], n, 2 * n - 1)
        return skew[:, :, :n]

    d0 = toeplitz(jnp.concatenate([by_dist[:, :n], jnp.broadcast_to(by_dist[:, :1], by_dist[:, :n].shape)], axis=1))
    d1 = toeplitz(jnp.concatenate([by_dist[:, n:], by_dist[:, :n]], axis=1))
    zero = jnp.zeros_like(d0)
    return jnp.swapaxes(jnp.stack([zero, d0, d1, zero]), 2, 3)


def kernel(x, p, ffn1_norm, ffn1_w_in, ffn1_w_out, mix_norm, mix_w_in, conv_dw_w, conv_dw_b, conv_ln_g,
           conv_ln_b, conv_w_out, q_norm, k_norm, attn_w_out, mix_w_out, ffn2_norm, ffn2_w_in, ffn2_w_out,
           ple_norm, ple_w_gate, ple_w_proj, rel_bias):
    b, s, d = x.shape
    depth = ffn1_norm.shape[0]
    cw = conv_dw_w.shape[2]
    aw = N_HEADS * HEAD_DIM
    nqi = IDX_HEADS * IDX_DIM
    iw = nqi + IDX_DIM + IDX_HEADS
    iw_pad = -(-iw // LANES) * LANES
    k_top = min(TOPK_MAX, s // 4)
    assert b == 1 and s % ATT_TK == 0 and mix_w_in.shape[2] == 2 * cw + 3 * aw + iw + 2 * d

    head_blocks = jnp.asarray(np.kron(np.eye(N_HEADS), np.ones((HEAD_DIM, HEAD_DIM))), BF16)
    bias_qk = _bias_tables(rel_bias)
    bias_kq = jnp.swapaxes(bias_qk, 2, 3)

    h = x[0]
    for i in range(depth):
        h = _ffn(h, ffn1_norm[i], ffn1_w_in[i].astype(BF16), ffn1_w_out[i].astype(BF16))

        w = mix_w_in[i]
        o0 = 2 * cw
        o1 = o0 + 3 * aw
        o2 = o1 + iw
        w_idx = jnp.pad(w[:, o1:o2], ((0, 0), (0, iw_pad - iw)))
        w_idx_hi, w_idx_lo = _split_bf16(w_idx)
        qg = jnp.tile(q_norm[i], N_HEADS).reshape(1, aw)
        kg = jnp.tile(k_norm[i], N_HEADS).reshape(1, aw)
        glu, q_t, k_ext, v_t, qi_pairs, ki_ext, idx_w, gate_c, gate_a, ksq = _mix_in(
            h, mix_norm[i], w[:, :o0].astype(BF16), w[:, o0:o1].astype(BF16), w_idx_hi, w_idx_lo,
            w[:, o2:].astype(BF16), qg, kg, head_blocks)

        conv_g = _conv(glu, conv_dw_w[i], conv_dw_b[i], conv_ln_g[i], conv_ln_b[i],
                       conv_w_out[i].astype(BF16), gate_c)

        w_t = idx_w[:, IDX_DIM:IDX_DIM + IDX_HEADS].T
        mask = _select(qi_pairs, w_t, ki_ext, k_top)

        kmax = jnp.sqrt(ksq.reshape(N_HEADS, HEAD_DIM)[:, :1])
        kmax_b = jnp.broadcast_to(kmax[:, :, None], (N_HEADS, 1, ATT_TQ))
        attn_t, den = _attn(q_t, kmax_b, k_ext, v_t, mask, bias_kq)

        def safe_attn():
            out = _attn_safe(jnp.swapaxes(q_t, 1, 2), k_ext[:, :, :HEAD_DIM], jnp.swapaxes(v_t, 1, 2),
                             jnp.swapaxes(mask, 1, 2), bias_qk)
            return jnp.swapaxes(out, 1, 2)

        underflow = jnp.logical_not(jnp.min(den) > L_MIN)
        attn_t = lax.cond(underflow, safe_attn, lambda: attn_t).reshape(aw, s)

        h = _tail(h, conv_g, gate_a, attn_t, p[i, 0], attn_w_out[i].astype(BF16), mix_w_out[i].astype(BF16),
                  ffn2_norm[i], ffn2_w_in[i].astype(BF16), ffn2_w_out[i].astype(BF16),
                  ple_norm[i], ple_w_gate[i].astype(BF16), ple_w_proj[i].astype(BF16))
    return h[None]
```

```python
import functools
import math

import numpy as np
import jax
import jax.numpy as jnp
from jax import lax
from jax.experimental import pallas as pl
from jax.experimental.pallas import tpu as pltpu

F32 = jnp.float32
BF16 = jnp.bfloat16
I32 = jnp.int32

EPS = 1e-6
CONV_WIDTH = 31
N_HEADS = 8
HEAD_DIM = 64
IDX_HEADS = 4
IDX_DIM = 64
TOPK_MAX = 256
NUM_BUCKETS = 32
MAX_DISTANCE = 128

LANES = 128
SUBLANES = 8
MXU_TILE = 256
FFN_CHUNK = 1024
VMEM_LIMIT = 56 * 1024 * 1024
NEG = -1e30
INT_MIN = -2 ** 31
MIN_NORMAL_BITS = 0x00800000
LOG2E = math.log2(math.e)

CONV_HALO = 32
CONV_CHUNK = 32
SEL_TQ = LANES
SEL_TK = 1024
SEL_GROUPS = 8
SEL_DEPTH = 16
ATT_TQ = 1024
ATT_TK = 1024
SAFE_TQ = 256
L_MIN = 1e-30


def _sigmoid(x):
    return 1.0 / (1.0 + jnp.exp(-x))


def _rms(x, g):
    ms = jnp.mean(x * x, axis=-1, keepdims=True)
    return x * lax.rsqrt(ms + EPS) * g


def _dot(a, b):
    return jnp.dot(a, b, preferred_element_type=F32)


def _dot_t(a, b):
    return lax.dot_general(a, b, (((1,), (1,)), ((), ())), preferred_element_type=F32)


def _split_bf16(x):
    hi = x.astype(BF16)
    lo = (x - hi.astype(F32)).astype(BF16)
    return hi, lo


def _half_step_ffn(x, g, wi_ref, wo_ref):
    xn = _rms(x, g).astype(BF16)
    dff = wo_ref.shape[0]
    acc = None
    for c0 in range(0, dff, FFN_CHUNK):
        c1 = min(c0 + FFN_CHUNK, dff)
        a = _dot(xn, wi_ref[:, c0:c1])
        b = _dot(xn, wi_ref[:, dff + c0:dff + c1])
        part = _dot((a * _sigmoid(a) * b).astype(BF16), wo_ref[c0:c1, :])
        acc = part if acc is None else acc + part
    return x + 0.5 * acc


def _ffn_kernel(x_ref, g_ref, wi_ref, wo_ref, o_ref):
    o_ref[...] = _half_step_ffn(x_ref[...], g_ref[...], wi_ref, wo_ref)


def _resident(a):
    return pl.BlockSpec(a.shape, lambda *_: (0,) * a.ndim, pipeline_mode=pl.Buffered(1))


def _ffn(x, g, w_in, w_out, tm=512):
    s, d = x.shape
    dff = w_out.shape[0]
    assert s % tm == 0 and dff % MXU_TILE == 0 and FFN_CHUNK % MXU_TILE == 0
    g2 = g.reshape(1, d)
    return pl.pallas_call(
        _ffn_kernel,
        grid=(s // tm,),
        in_specs=[pl.BlockSpec((tm, d), lambda i: (i, 0)), _resident(g2), _resident(w_in), _resident(w_out)],
        out_specs=pl.BlockSpec((tm, d), lambda i: (i, 0)),
        out_shape=jax.ShapeDtypeStruct((s, d), F32),
        compiler_params=pltpu.CompilerParams(dimension_semantics=("arbitrary",), vmem_limit_bytes=VMEM_LIMIT),
        name="ffn",
    )(x, g2, w_in, w_out)


def _mix_in_kernel(h_ref, g_ref, wc_ref, wqkv_ref, wih_ref, wil_ref, wg_ref, qg_ref, kg_ref, hb_ref,
                   glu_ref, qt_ref, kx_ref, vt_ref, qip_ref, kix_ref, wi_ref, gc_ref, ga_ref):
    u = _rms(h_ref[...], g_ref[...])
    u_hi, u_lo = _split_bf16(u)
    tm = h_ref.shape[0]
    cw = glu_ref.shape[1]
    aw = N_HEADS * HEAD_DIM
    nqi = IDX_HEADS * IDX_DIM
    d = gc_ref.shape[1]

    c = _dot(u_hi, wc_ref[...])
    glu_ref[...] = c[:, :cw] * _sigmoid(c[:, cw:])

    qkv = _dot(u_hi, wqkv_ref[...])
    hb = hb_ref[...]

    def head_sumsq(t):
        t2_hi, t2_lo = _split_bf16(t * t)
        return _dot(t2_hi, hb) + _dot(t2_lo, hb)

    def head_norm(t, g):
        return t * lax.rsqrt(head_sumsq(t) * (1.0 / HEAD_DIM) + EPS) * g

    q = head_norm(qkv[:, :aw], qg_ref[...]) * (HEAD_DIM ** -0.5 * LOG2E)
    q_t = q.T
    for hd in range(N_HEADS):
        qt_ref[hd] = q_t[hd * HEAD_DIM:(hd + 1) * HEAD_DIM, :].astype(BF16)
    k = head_norm(qkv[:, aw:2 * aw], kg_ref[...]).astype(BF16).astype(F32)
    one_hot0 = lambda shape, axis: jnp.where(lax.broadcasted_iota(I32, shape, axis) == 0, 1.0, 0.0)
    k_pad = one_hot0((tm, LANES - HEAD_DIM), 1)
    for hd in range(N_HEADS):
        kx_ref[hd] = jnp.concatenate([k[:, hd * HEAD_DIM:(hd + 1) * HEAD_DIM], k_pad], axis=1).astype(BF16)
    v_t = qkv[:, 2 * aw:].T
    v_pad = one_hot0((LANES - HEAD_DIM, tm), 0).astype(BF16)
    for hd in range(N_HEADS):
        vt_ref[hd, 0:HEAD_DIM, :] = v_t[hd * HEAD_DIM:(hd + 1) * HEAD_DIM, :].astype(BF16)
        vt_ref[hd, HEAD_DIM:, :] = v_pad

    wih = wih_ref[...]
    idx = _dot(u_hi, wih) + _dot(u_lo, wih) + _dot(u_hi, wil_ref[...])
    qi_hi, qi_lo = _split_bf16(idx[:, :nqi].T)
    for blk in range(tm // SEL_TQ):
        cols = slice(blk * SEL_TQ, (blk + 1) * SEL_TQ)
        for hd in range(IDX_HEADS):
            rows = slice(hd * IDX_DIM, (hd + 1) * IDX_DIM)
            out_cols = slice((hd % 2) * SEL_TQ, (hd % 2 + 1) * SEL_TQ)
            for part, src in enumerate((qi_hi, qi_hi, qi_lo, qi_lo)):
                qip_ref[blk, hd // 2, part * IDX_DIM:(part + 1) * IDX_DIM, out_cols] = src[rows, cols]
    ki = idx[:, nqi:nqi + IDX_DIM]
    ki_hi = ki.astype(BF16).astype(F32)
    ki_lo = ki - ki_hi
    kix_ref[...] = jnp.concatenate([ki_hi, ki_lo, ki_hi, ki_lo], axis=1).astype(BF16)
    wi_ref[...] = idx[:, nqi:]

    gates = _sigmoid(_dot(u_hi, wg_ref[...]))
    gc_ref[...] = gates[:, :d]
    ga_ref[...] = gates[:, d:]


def _mix_in(h, g, w_conv, w_qkv, w_idx_hi, w_idx_lo, w_gate, qg, kg, head_blocks, tm=512):
    s, d = h.shape
    cw = w_conv.shape[1] // 2
    aw = w_qkv.shape[1] // 3
    nqi = IDX_HEADS * IDX_DIM
    iw = w_idx_hi.shape[1]
    assert aw == N_HEADS * HEAD_DIM and tm % SEL_TQ == 0 and iw - nqi == LANES
    full = lambda a: pl.BlockSpec(a.shape, lambda i: (0,) * a.ndim)
    row = lambda n: pl.BlockSpec((tm, n), lambda i: (i, 0))
    g2 = g.reshape(1, d)
    return pl.pallas_call(
        _mix_in_kernel,
        grid=(s // tm,),
        in_specs=[row(d), full(g2), full(w_conv), full(w_qkv), full(w_idx_hi), full(w_idx_lo), full(w_gate),
                  full(qg), full(kg), full(head_blocks)],
        out_specs=[
            row(cw),
            pl.BlockSpec((N_HEADS, HEAD_DIM, tm), lambda i: (0, 0, i)),
            pl.BlockSpec((N_HEADS, tm, LANES), lambda i: (0, i, 0)),
            pl.BlockSpec((N_HEADS, LANES, tm), lambda i: (0, 0, i)),
            pl.BlockSpec((tm // SEL_TQ, IDX_HEADS // 2, 4 * IDX_DIM, 2 * SEL_TQ), lambda i: (i, 0, 0, 0)),
            row(4 * IDX_DIM), row(LANES), row(d), row(d),
        ],
        out_shape=[
            jax.ShapeDtypeStruct((s, cw), F32),
            jax.ShapeDtypeStruct((N_HEADS, HEAD_DIM, s), BF16),
            jax.ShapeDtypeStruct((N_HEADS, s, LANES), BF16),
            jax.ShapeDtypeStruct((N_HEADS, LANES, s), BF16),
            jax.ShapeDtypeStruct((s // SEL_TQ, IDX_HEADS // 2, 4 * IDX_DIM, 2 * SEL_TQ), BF16),
            jax.ShapeDtypeStruct((s, 4 * IDX_DIM), BF16),
            jax.ShapeDtypeStruct((s, LANES), F32),
            jax.ShapeDtypeStruct((s, d), F32),
            jax.ShapeDtypeStruct((s, d), F32),
        ],
        compiler_params=pltpu.CompilerParams(dimension_semantics=("arbitrary",), vmem_limit_bytes=VMEM_LIMIT),
        name="mix_in",
    )(h, g2, w_conv, w_qkv, w_idx_hi, w_idx_lo, w_gate, qg, kg, head_blocks)


def _conv_kernel(z_ref, halo_ref, dw_ref, db_ref, lg_ref, lb_ref, wo_ref, gc_ref, o_ref, zp_ref, zs_ref, acc_ref):
    tm = z_ref.shape[0]
    first = pl.program_id(0) == 0
    halo = halo_ref[...]
    zp_ref[0:CONV_HALO, :] = jnp.where(first, jnp.zeros_like(halo), halo)
    zp_ref[CONV_HALO:, :] = z_ref[...]
    off = CONV_HALO - (CONV_WIDTH - 1)
    span = zs_ref.shape[1]
    for b in range(1, SUBLANES):
        zs_ref[b - 1] = zp_ref[b:b + span, :]

    def chunk(c, carry):
        r0 = pl.multiple_of(c * CONV_CHUNK, CONV_CHUNK)
        acc = jnp.zeros((CONV_CHUNK, z_ref.shape[1]), F32) + db_ref[...]
        for j in range(CONV_WIDTH):
            a, b = divmod(off + j, SUBLANES)
            src = zp_ref if b == 0 else zs_ref.at[b - 1]
            acc = acc + dw_ref[j:j + 1, :] * src[pl.ds(r0 + a * SUBLANES, CONV_CHUNK), :]
        acc_ref[pl.ds(r0, CONV_CHUNK), :] = acc
        return carry

    lax.fori_loop(0, tm // CONV_CHUNK, chunk, 0)
    acc = acc_ref[...]
    mu = jnp.mean(acc, axis=-1, keepdims=True)
    xc = acc - mu
    y = xc * lax.rsqrt(jnp.mean(xc * xc, axis=-1, keepdims=True) + EPS)
    y = y * lg_ref[...] + lb_ref[...]
    y = (y * _sigmoid(y)).astype(BF16)
    o_ref[...] = gc_ref[...] * _dot(y, wo_ref[...])


def _conv(z, dw_w, dw_b, ln_g, ln_b, w_out, gate_c, tm=512):
    s, c = z.shape
    d = w_out.shape[1]
    assert tm % CONV_HALO == 0 and tm % CONV_CHUNK == 0
    r = tm // CONV_HALO
    span = tm + CONV_HALO - SUBLANES
    full = lambda a: pl.BlockSpec(a.shape, lambda i: (0,) * a.ndim)
    vecs = [dw_b.reshape(1, c), ln_g.reshape(1, c), ln_b.reshape(1, c)]
    return pl.pallas_call(
        _conv_kernel,
        grid=(s // tm,),
        in_specs=[
            pl.BlockSpec((tm, c), lambda i: (i, 0)),
            pl.BlockSpec((CONV_HALO, c), lambda i: (jnp.maximum(i * r - 1, 0), 0)),
            full(dw_w), full(vecs[0]), full(vecs[1]), full(vecs[2]), full(w_out),
            pl.BlockSpec((tm, d), lambda i: (i, 0)),
        ],
        out_specs=pl.BlockSpec((tm, d), lambda i: (i, 0)),
        out_shape=jax.ShapeDtypeStruct((s, d), F32),
        scratch_shapes=[pltpu.VMEM((tm + CONV_HALO, c), F32), pltpu.VMEM((SUBLANES - 1, span, c), F32),
                        pltpu.VMEM((tm, c), F32)],
        compiler_params=pltpu.CompilerParams(dimension_semantics=("arbitrary",), vmem_limit_bytes=VMEM_LIMIT),
        name="conv",
    )(z, z, dw_w, *vecs, w_out, gate_c)


def _sortable_key(x):
    bits = pltpu.bitcast(x + 0.0, I32)
    return bits ^ ((bits >> 31) & 0x7FFFFFFF)


def _key_to_float(key):
    return pltpu.bitcast(key ^ ((key >> 31) & 0x7FFFFFFF), F32)


def _sort_network(n):
    pairs = []

    def merge(lo, hi, r):
        step = r * 2
        if step < hi - lo:
            merge(lo, hi, step)
            merge(lo + r, hi, step)
            pairs.extend((i, i + r) for i in range(lo + r, hi - r, step))
        else:
            pairs.append((lo, lo + r))

    def sort(lo, hi):
        if hi > lo:
            mid = lo + (hi - lo) // 2
            sort(lo, mid)
            sort(mid + 1, hi)
            merge(lo, hi, 1)

    sort(0, n - 1)
    return pairs


def _compare_exchange(a, i, j):
    a[i], a[j] = jnp.maximum(a[i], a[j]), jnp.minimum(a[i], a[j])


def _merge_top(top, batch):
    n = len(top)
    out = [jnp.maximum(top[i], batch[n - 1 - i]) for i in range(n)]
    d = n // 2
    while d >= 1:
        for i in range(n):
            if i & d == 0:
                _compare_exchange(out, i, i + d)
        d //= 2
    return out


def _select_kernel(qi_ref, w_ref, ki_ref, mask_ref, sc_ref, cand_ref, ckey_ref, st_ref, *, k_top):
    qb = pl.program_id(0)
    n_rows = mask_ref.shape[1]
    n_tiles = n_rows // SEL_TK
    q0 = qb * SEL_TQ
    nkt = (q0 + SEL_TQ - 1) // SEL_TK + 1
    vt = SEL_TK // SUBLANES
    vshape = (SUBLANES, LANES)
    qpos = q0 + lax.broadcasted_iota(I32, vshape, 1)
    sub = lax.broadcasted_iota(I32, vshape, 0)
    w = w_ref[...] * ((IDX_HEADS ** -0.5) * (IDX_DIM ** -0.5))

    qpos_t = q0 + lax.broadcasted_iota(I32, (SEL_TK, LANES), 1)
    krow_t = lax.broadcasted_iota(I32, (SEL_TK, LANES), 0)
    zero_bits_t = (MIN_NORMAL_BITS + n_rows) - krow_t

    def score_tile(kt):
        k0 = pl.multiple_of(kt * SEL_TK, SEL_TK)
        ki = ki_ref[pl.ds(k0, SEL_TK), :]
        sc = None
        for pair in range(IDX_HEADS // 2):
            r = _dot(ki, qi_ref[0, pair])
            for j in range(2):
                h = 2 * pair + j
                term = w[h:h + 1, :] * jnp.maximum(r[:, j * LANES:(j + 1) * LANES], 0.0)
                sc = term if sc is None else sc + term
        sc_ref[pl.ds(k0, SEL_TK), :] = jnp.where(sc == 0.0, pltpu.bitcast(zero_bits_t - k0, F32), sc)

    def vreg(ref, row):
        return ref[pl.ds(pl.multiple_of(row, SUBLANES), SUBLANES), :]

    def col_sum(parts):
        tot = parts[0]
        for part in parts[1:]:
            tot = tot + part
        return jnp.broadcast_to(jnp.sum(tot, axis=0, keepdims=True), vshape)

    n_cand = SEL_GROUPS * SEL_DEPTH
    network = _sort_network(SEL_DEPTH)
    cand_ref[...] = jnp.full(cand_ref.shape, -jnp.inf, F32)

    def lists_tile(kt):
        for g in range(SEL_GROUPS):
            batch = [vreg(sc_ref, kt * SEL_TK + (g + SEL_GROUPS * j) * SUBLANES) for j in range(SEL_DEPTH)]
            for i, j in network:
                _compare_exchange(batch, i, j)
            rows = [slice((g * SEL_DEPTH + i) * SUBLANES, (g * SEL_DEPTH + i + 1) * SUBLANES)
                    for i in range(SEL_DEPTH)]
            top = _merge_top([cand_ref[r, :] for r in rows], batch)
            for r, t in zip(rows, top):
                cand_ref[r, :] = t

    sc_ref[pl.ds(n_rows, SEL_TK), :] = jnp.full((SEL_TK, LANES), -jnp.inf, F32)

    def score_and_lists(kt, carry):
        lists_tile(jnp.where(kt == 0, n_tiles, kt - 1))
        score_tile(kt)
        return carry

    lax.fori_loop(0, nkt, score_and_lists, 0)
    k_last = pl.multiple_of((nkt - 1) * SEL_TK, SEL_TK)
    sc_ref[pl.ds(k_last, SEL_TK), :] = jnp.where(
        k_last + krow_t <= qpos_t, sc_ref[pl.ds(k_last, SEL_TK), :], -jnp.inf)
    lists_tile(nkt - 1)
    ckey_ref[...] = _sortable_key(cand_ref[...])

    n_acc = 4

    def cand_bit(i, t):
        c = t ^ lax.shift_left(jnp.int32(1), 31 - i)
        acc = [jnp.zeros(vshape, I32) for _ in range(n_acc)]
        for v in range(n_cand):
            x = ckey_ref[v * SUBLANES:(v + 1) * SUBLANES, :]
            acc[v % n_acc] = acc[v % n_acc] + jnp.where(x >= c, 1, 0)
        return jnp.where(col_sum(acc) >= k_top, c, t)

    st_ref[0] = lax.fori_loop(0, 32, cand_bit, jnp.full(vshape, INT_MIN, I32))

    def count_scores(preds):
        def tile(kt, acc):
            acc = [list(a) for a in acc]
            for v in range(vt):
                x = vreg(sc_ref, kt * SEL_TK + v * SUBLANES)
                kpos = kt * SEL_TK + v * SUBLANES + sub
                for p, pred in enumerate(preds):
                    acc[p][v % n_acc] = acc[p][v % n_acc] + jnp.where(pred(x, kpos), 1, 0)
            return tuple(tuple(a) for a in acc)

        zero = tuple(tuple(jnp.zeros(vshape, I32) for _ in range(n_acc)) for _ in preds)
        return [col_sum(list(a)) for a in lax.fori_loop(0, nkt, tile, zero)]

    def count_around_threshold():
        tf = _key_to_float(st_ref[0])
        gt, ge = count_scores([lambda x, kpos: x > tf, lambda x, kpos: x >= tf])
        st_ref[1] = gt
        st_ref[2] = ge

    tf_fast = jnp.broadcast_to(_key_to_float(st_ref[0])[0:1, :], (SEL_TK, LANES))

    def fast_tile(kt, acc):
        k0 = pl.multiple_of(kt * SEL_TK, SEL_TK)
        m = jnp.where(sc_ref[pl.ds(k0, SEL_TK), :] >= tf_fast, 1.0, 0.0)
        mask_ref[0, pl.ds(k0, SEL_TK), :] = m.astype(BF16)
        acc = list(acc)
        for v in range(vt):
            acc[v % n_acc] = acc[v % n_acc] + m[v * SUBLANES:(v + 1) * SUBLANES, :]
        return tuple(acc)

    size = col_sum(list(lax.fori_loop(0, nkt, fast_tile, tuple(jnp.zeros(vshape, F32) for _ in range(n_acc)))))

    @pl.when(jnp.max(jnp.abs(size - k_top)) > 0.0)
    def _():
        count_around_threshold()

        @pl.when(jnp.max(st_ref[1]) >= k_top)
        def _():
            def full_bit(i, t):
                c = t ^ lax.shift_left(jnp.int32(1), 31 - i)
                cnt, = count_scores([lambda x, kpos: _sortable_key(x) >= c])
                return jnp.where(cnt >= k_top, c, t)

            st_ref[0] = lax.fori_loop(0, 32, full_bit, jnp.full(vshape, INT_MIN, I32))
            count_around_threshold()

        st_ref[3] = jnp.full(vshape, n_rows, I32)

        @pl.when(jnp.max(st_ref[2]) > k_top)
        def _():
            tf = _key_to_float(st_ref[0])
            need = k_top - st_ref[1]
            n_bits = n_rows.bit_length() - 1

            def pos_bit(i, p):
                c = p + lax.shift_left(jnp.int32(1), n_bits - 1 - i)
                f, = count_scores([lambda x, kpos: (x == tf) & (kpos < c)])
                return jnp.where(f < need, c, p)

            st_ref[3] = lax.fori_loop(0, n_bits, pos_bit, jnp.zeros(vshape, I32))

        tf_t = jnp.broadcast_to(_key_to_float(st_ref[0])[0:1, :], (SEL_TK, LANES))
        lim_t = jnp.broadcast_to(st_ref[3][0:1, :], (SEL_TK, LANES))

        def write_tile(kt, carry):
            k0 = pl.multiple_of(kt * SEL_TK, SEL_TK)
            x = sc_ref[pl.ds(k0, SEL_TK), :]
            kpos = k0 + krow_t
            sel = ((x > tf_t) | ((x == tf_t) & (kpos <= lim_t))) & (kpos <= qpos_t)
            mask_ref[0, pl.ds(k0, SEL_TK), :] = jnp.where(sel, 1.0, 0.0).astype(BF16)
            return carry

        lax.fori_loop(0, nkt, write_tile, 0)

    def fill_tile(kt, carry):
        mask_ref[0, pl.ds(pl.multiple_of(kt * SEL_TK, SEL_TK), SEL_TK), :] = jnp.zeros((SEL_TK, LANES), BF16)
        return carry

    lax.fori_loop(nkt, n_tiles, fill_tile, 0)


def _select(qi_pairs, w_t, ki_ext, k_top):
    nqb, npair, kw, _ = qi_pairs.shape
    s = ki_ext.shape[0]
    n_cand_rows = SEL_GROUPS * SEL_DEPTH * SUBLANES
    assert s % SEL_TK == 0 and SEL_TK == n_cand_rows and n_cand_rows >= k_top
    return pl.pallas_call(
        functools.partial(_select_kernel, k_top=k_top),
        grid=(nqb,),
        in_specs=[
            pl.BlockSpec((1, npair, kw, 2 * SEL_TQ), lambda i: (i, 0, 0, 0)),
            pl.BlockSpec((w_t.shape[0], SEL_TQ), lambda i: (0, i)),
            pl.BlockSpec((s, kw), lambda i: (0, 0)),
        ],
        out_specs=pl.BlockSpec((1, s, SEL_TQ), lambda i: (i, 0, 0)),
        out_shape=jax.ShapeDtypeStruct((nqb, s, SEL_TQ), BF16),
        scratch_shapes=[
            pltpu.VMEM((s + SEL_TK, SEL_TQ), F32),
            pltpu.VMEM((n_cand_rows, SEL_TQ), F32),
            pltpu.VMEM((n_cand_rows, SEL_TQ), I32),
            pltpu.VMEM((4, SUBLANES, LANES), I32),
        ],
        compiler_params=pltpu.CompilerParams(dimension_semantics=("arbitrary",), vmem_limit_bytes=VMEM_LIMIT),
        name="select",
    )(qi_pairs, w_t, ki_ext)


def _causal_pairs(s, tq, tk):
    pairs = [(qb, kt) for qb in range(s // tq) for kt in range((qb * tq + tq - 1) // tk + 1)]
    return jnp.asarray([p[0] for p in pairs], I32), jnp.asarray([p[1] for p in pairs], I32)


def _attn_kernel(qb_tab, kt_tab, qt_ref, kmax_ref, k_ref, vt_ref, mask_ref, bias_ref, o_ref, l_ref, qx_ref, acc_ref):
    step = pl.program_id(0)
    qb = qb_tab[step]
    kt = kt_tab[step]
    q0 = qb * ATT_TQ
    k0 = kt * ATT_TK
    nq = ATT_TQ // LANES
    nk = ATT_TK // LANES

    @pl.when(kt == 0)
    def _():
        row = lax.broadcasted_iota(I32, (HEAD_DIM, ATT_TQ), 0)
        for h in range(N_HEADS):
            q = qt_ref[h]
            qf = q.astype(F32)
            bound = jnp.sqrt(jnp.sum(qf * qf, axis=0, keepdims=True)) * kmax_ref[h] * 1.02
            qx_ref[h, 0:HEAD_DIM, :] = q
            qx_ref[h, HEAD_DIM:, :] = jnp.where(row == 0, -bound, 0.0).astype(BF16)
        acc_ref[...] = jnp.zeros(acc_ref.shape, F32)

    mask = jnp.concatenate([mask_ref[a] for a in range(nq)], axis=1)

    def heads(with_bias):
        for h in range(N_HEADS):
            s = _dot(k_ref[h], qx_ref[h])
            if with_bias:
                rows = []
                for c in range(nk):
                    tab = [jnp.clip((q0 + a * LANES - k0 - c * LANES) // LANES + 1, 0, 3) for a in range(nq)]
                    rows.append(jnp.concatenate([bias_ref[tab[a], h] for a in range(nq)], axis=1))
                s = s + jnp.concatenate(rows, axis=0)
            p = jnp.exp2(s).astype(BF16) * mask
            acc_ref[h] += _dot(vt_ref[h], p)

    near = k0 + ATT_TK + 2 * LANES > q0

    @pl.when(near)
    def _():
        heads(True)

    @pl.when(jnp.logical_not(near))
    def _():
        heads(False)

    @pl.when(kt == (q0 + ATT_TQ - 1) // ATT_TK)
    def _():
        for h in range(N_HEADS):
            acc = acc_ref[h]
            den = acc[HEAD_DIM:HEAD_DIM + 1, :]
            o_ref[h] = acc[:HEAD_DIM, :] / den
            l_ref[h] = den


def _attn(q_t, kmax_b, k_ext, v_t, mask, bias_tab):
    nh, hd, s = q_t.shape
    qb_tab, kt_tab = _causal_pairs(s, ATT_TQ, ATT_TK)
    grid_spec = pltpu.PrefetchScalarGridSpec(
        num_scalar_prefetch=2,
        grid=(qb_tab.shape[0],),
        in_specs=[
            pl.BlockSpec((nh, hd, ATT_TQ), lambda i, qb, kt: (0, 0, qb[i])),
            pl.BlockSpec((nh, 1, ATT_TQ), lambda i, qb, kt: (0, 0, 0)),
            pl.BlockSpec((nh, ATT_TK, LANES), lambda i, qb, kt: (0, kt[i], 0)),
            pl.BlockSpec((nh, LANES, ATT_TK), lambda i, qb, kt: (0, 0, kt[i])),
            pl.BlockSpec((ATT_TQ // LANES, ATT_TK, LANES), lambda i, qb, kt: (qb[i], kt[i], 0)),
            pl.BlockSpec(bias_tab.shape, lambda i, qb, kt: (0, 0, 0, 0)),
        ],
        out_specs=[
            pl.BlockSpec((nh, hd, ATT_TQ), lambda i, qb, kt: (0, 0, qb[i])),
            pl.BlockSpec((nh, 1, ATT_TQ), lambda i, qb, kt: (0, 0, qb[i])),
        ],
        scratch_shapes=[pltpu.VMEM((nh, LANES, ATT_TQ), BF16), pltpu.VMEM((nh, LANES, ATT_TQ), F32)],
    )
    return pl.pallas_call(
        _attn_kernel,
        grid_spec=grid_spec,
        out_shape=[jax.ShapeDtypeStruct((nh, hd, s), F32), jax.ShapeDtypeStruct((nh, 1, s), F32)],
        compiler_params=pltpu.CompilerParams(dimension_semantics=("arbitrary",), vmem_limit_bytes=VMEM_LIMIT),
        name="attn",
    )(qb_tab, kt_tab, q_t, kmax_b, k_ext, v_t, mask, bias_tab)


def _attn_safe_kernel(qb_tab, kt_tab, q_ref, k_ref, v_ref, mask_ref, bias_ref, o_ref, m_ref, acc_ref):
    step = pl.program_id(0)
    qb = qb_tab[step]
    kt = kt_tab[step]
    nq = SAFE_TQ // LANES
    nsub = ATT_TK // LANES

    @pl.when(kt == 0)
    def _():
        m_ref[...] = jnp.full(m_ref.shape, NEG, F32)
        acc_ref[...] = jnp.zeros(acc_ref.shape, F32)

    for a in range(nq):
        rows = slice(a * LANES, (a + 1) * LANES)
        q0 = qb * SAFE_TQ + a * LANES
        maskf = (mask_ref[a].astype(F32) - 1.0) * (-NEG)
        tab_idx = [jnp.clip((q0 - (kt * ATT_TK + c * LANES)) // LANES + 1, 0, 3) for c in range(nsub)]
        for h in range(N_HEADS):
            s = _dot_t(q_ref[h, rows, :], k_ref[h]) + maskf
            s = s + jnp.concatenate([bias_ref[tab_idx[c], h] for c in range(nsub)], axis=1)
            m_old = m_ref[h, rows, :]
            m_new = jnp.maximum(m_old, jnp.max(s, axis=1, keepdims=True))
            p = jnp.exp2(s - m_new[:, 0:1])
            alpha = jnp.exp2(m_old - m_new)
            acc_ref[h, rows, :] = alpha * acc_ref[h, rows, :] + _dot(p.astype(BF16), v_ref[h])
            m_ref[h, rows, :] = m_new

    @pl.when(kt == (qb * SAFE_TQ + SAFE_TQ - 1) // ATT_TK)
    def _():
        for h in range(N_HEADS):
            acc = acc_ref[h]
            o_ref[h] = acc[:, :HEAD_DIM] / acc[:, HEAD_DIM:HEAD_DIM + 1]


def _attn_safe(q, k, v_ext, mask_qk, bias_tab):
    nh, s, hd = q.shape
    qb_tab, kt_tab = _causal_pairs(s, SAFE_TQ, ATT_TK)
    grid_spec = pltpu.PrefetchScalarGridSpec(
        num_scalar_prefetch=2,
        grid=(qb_tab.shape[0],),
        in_specs=[
            pl.BlockSpec((nh, SAFE_TQ, hd), lambda i, qb, kt: (0, qb[i], 0)),
            pl.BlockSpec((nh, ATT_TK, hd), lambda i, qb, kt: (0, kt[i], 0)),
            pl.BlockSpec((nh, ATT_TK, LANES), lambda i, qb, kt: (0, kt[i], 0)),
            pl.BlockSpec((SAFE_TQ // LANES, LANES, ATT_TK), lambda i, qb, kt: (qb[i], 0, kt[i])),
            pl.BlockSpec(bias_tab.shape, lambda i, qb, kt: (0, 0, 0, 0)),
        ],
        out_specs=pl.BlockSpec((nh, SAFE_TQ, hd), lambda i, qb, kt: (0, qb[i], 0)),
        scratch_shapes=[pltpu.VMEM((nh, SAFE_TQ, LANES), F32), pltpu.VMEM((nh, SAFE_TQ, LANES), F32)],
    )
    return pl.pallas_call(
        _attn_safe_kernel,
        grid_spec=grid_spec,
        out_shape=jax.ShapeDtypeStruct((nh, s, hd), F32),
        compiler_params=pltpu.CompilerParams(dimension_semantics=("arbitrary",), vmem_limit_bytes=VMEM_LIMIT),
        name="attn_safe",
    )(qb_tab, kt_tab, q, k, v_ext, mask_qk, bias_tab)


def _tail_kernel(h_ref, cg_ref, ga_ref, at_ref, p_ref, wao_ref, wmo_ref, fg_ref, wi_ref, wo_ref, pg_ref, wpg_ref,
                 wpp_ref, o_ref):
    attn = at_ref[...].T.astype(BF16)
    merged = cg_ref[...] + ga_ref[...] * _dot(attn, wao_ref[...])
    h = h_ref[...] + _dot(merged.astype(BF16), wmo_ref[...])
    h = _half_step_ffn(h, fg_ref[...], wi_ref, wo_ref)
    gate = _sigmoid(_dot(_rms(h, pg_ref[...]).astype(BF16), wpg_ref[...]))
    o_ref[...] = h + gate * _dot(p_ref[...].astype(BF16), wpp_ref[...])


def _tail(h, conv_g, gate_a, attn_t, p, w_ao, w_mo, ffn_g, w_in, w_out, ple_g, w_pg, w_pp, tm=512):
    s, d = h.shape
    row = lambda n: pl.BlockSpec((tm, n), lambda i: (i, 0))
    fg2 = ffn_g.reshape(1, d)
    pg2 = ple_g.reshape(1, d)
    params = (w_ao, w_mo, fg2, w_in, w_out, pg2, w_pg, w_pp)
    return pl.pallas_call(
        _tail_kernel,
        grid=(s // tm,),
        in_specs=[row(d), row(d), row(d), pl.BlockSpec((attn_t.shape[0], tm), lambda i: (0, i)), row(p.shape[1])]
        + [_resident(a) for a in params],
        out_specs=row(d),
        out_shape=jax.ShapeDtypeStruct((s, d), F32),
        compiler_params=pltpu.CompilerParams(dimension_semantics=("arbitrary",), vmem_limit_bytes=VMEM_LIMIT),
        name="tail",
    )(h, conv_g, gate_a, attn_t, p, *params)


def _t5_bucket_table(n_dist):
    n = np.arange(n_dist)
    max_exact = NUM_BUCKETS // 2
    nf = np.maximum(n, 1).astype(np.float32)
    large = max_exact + (np.log(nf / max_exact) / math.log(MAX_DISTANCE / max_exact)
                         * (NUM_BUCKETS - max_exact)).astype(np.int32)
    large = np.minimum(large, NUM_BUCKETS - 1)
    return np.where(n < max_exact, n, large)


def _bias_tables(rel_bias):
    assert MAX_DISTANCE <= LANES
    n = LANES
    rel = (rel_bias.astype(F32) - rel_bias[NUM_BUCKETS - 1].astype(F32)[None, :]) * LOG2E
    by_dist = rel[_t5_bucket_table(2 * n)].T

    def toeplitz(a):
        skew = jnp.tile(a, (1, n))[:, :n * (2 * n - 1)].reshape(a.shape[0], n, 2 * n - 1)
        return skew[:, :, :n]

    d0 = toeplitz(jnp.concatenate([by_dist[:, :n], jnp.broadcast_to(by_dist[:, :1], by_dist[:, :n].shape)], axis=1))
    d1 = toeplitz(jnp.concatenate([by_dist[:, n:], by_dist[:, :n]], axis=1))
    zero = jnp.zeros_like(d0)
    return jnp.swapaxes(jnp.stack([zero, d0, d1, zero]), 2, 3)


def kernel(x, p, ffn1_norm, ffn1_w_in, ffn1_w_out, mix_norm, mix_w_in, conv_dw_w, conv_dw_b, conv_ln_g,
           conv_ln_b, conv_w_out, q_norm, k_norm, attn_w_out, mix_w_out, ffn2_norm, ffn2_w_in, ffn2_w_out,
           ple_norm, ple_w_gate, ple_w_proj, rel_bias):
    b, s, d = x.shape
    depth = ffn1_norm.shape[0]
    cw = conv_dw_w.shape[2]
    aw = N_HEADS * HEAD_DIM
    nqi = IDX_HEADS * IDX_DIM
    iw = nqi + IDX_DIM + IDX_HEADS
    iw_pad = -(-iw // LANES) * LANES
    k_top = min(TOPK_MAX, s // 4)
    assert b == 1 and s % ATT_TK == 0 and mix_w_in.shape[2] == 2 * cw + 3 * aw + iw + 2 * d

    head_blocks = jnp.asarray(np.kron(np.eye(N_HEADS), np.ones((HEAD_DIM, HEAD_DIM))), BF16)
    bias_qk = _bias_tables(rel_bias)
    bias_kq = jnp.swapaxes(bias_qk, 2, 3)

    h = x[0]
    for i in range(depth):
        h = _ffn(h, ffn1_norm[i], ffn1_w_in[i].astype(BF16), ffn1_w_out[i].astype(BF16))

        w = mix_w_in[i]
        o0 = 2 * cw
        o1 = o0 + 3 * aw
        o2 = o1 + iw
        w_idx = jnp.pad(w[:, o1:o2], ((0, 0), (0, iw_pad - iw)))
        w_idx_hi, w_idx_lo = _split_bf16(w_idx)
        qg = jnp.tile(q_norm[i], N_HEADS).reshape(1, aw)
        kg = jnp.tile(k_norm[i], N_HEADS).reshape(1, aw)
        glu, q_t, k_ext, v_t, qi_pairs, ki_ext, idx_w, gate_c, gate_a = _mix_in(
            h, mix_norm[i], w[:, :o0].astype(BF16), w[:, o0:o1].astype(BF16), w_idx_hi, w_idx_lo,
            w[:, o2:].astype(BF16), qg, kg, head_blocks)

        conv_g = _conv(glu, conv_dw_w[i], conv_dw_b[i], conv_ln_g[i], conv_ln_b[i],
                       conv_w_out[i].astype(BF16), gate_c)

        w_t = idx_w[:, IDX_DIM:IDX_DIM + IDX_HEADS].T
        mask = _select(qi_pairs, w_t, ki_ext, k_top)

        kmax = math.sqrt(HEAD_DIM) * jnp.max(jnp.abs(k_norm[i]))
        kmax_b = jnp.full((N_HEADS, 1, ATT_TQ), kmax, F32)
        attn_t, den = _attn(q_t, kmax_b, k_ext, v_t, mask, bias_kq)

        def safe_attn():
            out = _attn_safe(jnp.swapaxes(q_t, 1, 2), k_ext[:, :, :HEAD_DIM], jnp.swapaxes(v_t, 1, 2),
                             jnp.swapaxes(mask, 1, 2), bias_qk)
            return jnp.swapaxes(out, 1, 2)

        underflow = jnp.logical_not(jnp.min(den) > L_MIN)
        attn_t = lax.cond(underflow, safe_attn, lambda: attn_t).reshape(aw, s)

        h = _tail(h, conv_g, gate_a, attn_t, p[i, 0], attn_w_out[i].astype(BF16), mix_w_out[i].astype(BF16),
                  ffn2_norm[i], ffn2_w_in[i].astype(BF16), ffn2_w_out[i].astype(BF16),
                  ple_norm[i], ple_w_gate[i].astype(BF16), ple_w_proj[i].astype(BF16))
    return h[None]
```

```python
import functools
import math

import numpy as np
import jax
import jax.numpy as jnp
from jax import lax
from jax.experimental import pallas as pl
from jax.experimental.pallas import tpu as pltpu

F32 = jnp.float32
BF16 = jnp.bfloat16
I32 = jnp.int32

EPS = 1e-6
CONV_WIDTH = 31
N_HEADS = 8
HEAD_DIM = 64
IDX_HEADS = 4
IDX_DIM = 64
TOPK_MAX = 256
NUM_BUCKETS = 32
MAX_DISTANCE = 128

LANES = 128
SUBLANES = 8
MXU_TILE = 256
FFN_CHUNK = 1024
VMEM_LIMIT = 56 * 1024 * 1024
NEG = -1e30
INT_MIN = -2 ** 31
MIN_NORMAL_BITS = 0x00800000
LOG2E = math.log2(math.e)

CONV_HALO = 32
CONV_CHUNK = 32
SEL_TQ = LANES
SEL_TK = 1024
SEL_GROUPS = 8
SEL_DEPTH = 16
ATT_TQ = 1024
ATT_TK = 1024
SAFE_TQ = 256
L_MIN = 1e-30


def _sigmoid(x):
    return 1.0 / (1.0 + jnp.exp(-x))


def _rms(x, g):
    ms = jnp.mean(x * x, axis=-1, keepdims=True)
    return x * lax.rsqrt(ms + EPS) * g


def _dot(a, b):
    return jnp.dot(a, b, preferred_element_type=F32)


def _dot_t(a, b):
    return lax.dot_general(a, b, (((1,), (1,)), ((), ())), preferred_element_type=F32)


def _split_bf16(x):
    hi = x.astype(BF16)
    lo = (x - hi.astype(F32)).astype(BF16)
    return hi, lo


def _half_step_ffn(x, g, wi_ref, wo_ref):
    xn = _rms(x, g).astype(BF16)
    dff = wo_ref.shape[0]
    acc = None
    for c0 in range(0, dff, FFN_CHUNK):
        c1 = min(c0 + FFN_CHUNK, dff)
        a = _dot(xn, wi_ref[:, c0:c1])
        b = _dot(xn, wi_ref[:, dff + c0:dff + c1])
        part = _dot((a * _sigmoid(a) * b).astype(BF16), wo_ref[c0:c1, :])
        acc = part if acc is None else acc + part
    return x + 0.5 * acc


def _ffn_kernel(x_ref, g_ref, wi_ref, wo_ref, o_ref):
    o_ref[...] = _half_step_ffn(x_ref[...], g_ref[...], wi_ref, wo_ref)


def _resident(a):
    return pl.BlockSpec(a.shape, lambda *_: (0,) * a.ndim, pipeline_mode=pl.Buffered(1))


def _ffn(x, g, w_in, w_out, tm=512):
    s, d = x.shape
    dff = w_out.shape[0]
    assert s % tm == 0 and dff % MXU_TILE == 0 and FFN_CHUNK % MXU_TILE == 0
    g2 = g.reshape(1, d)
    return pl.pallas_call(
        _ffn_kernel,
        grid=(s // tm,),
        in_specs=[pl.BlockSpec((tm, d), lambda i: (i, 0)), _resident(g2), _resident(w_in), _resident(w_out)],
        out_specs=pl.BlockSpec((tm, d), lambda i: (i, 0)),
        out_shape=jax.ShapeDtypeStruct((s, d), F32),
        compiler_params=pltpu.CompilerParams(dimension_semantics=("arbitrary",), vmem_limit_bytes=VMEM_LIMIT),
        name="ffn",
    )(x, g2, w_in, w_out)


def _mix_in_kernel(h_ref, g_ref, wc_ref, wqkv_ref, wih_ref, wil_ref, wg_ref, qg_ref, kg_ref, hb_ref,
                   glu_ref, qt_ref, kx_ref, vt_ref, qip_ref, kix_ref, wi_ref, gc_ref, ga_ref):
    u = _rms(h_ref[...], g_ref[...])
    u_hi, u_lo = _split_bf16(u)
    tm = h_ref.shape[0]
    cw = glu_ref.shape[1]
    aw = N_HEADS * HEAD_DIM
    nqi = IDX_HEADS * IDX_DIM
    d = gc_ref.shape[1]

    c = _dot(u_hi, wc_ref[...])
    glu_ref[...] = c[:, :cw] * _sigmoid(c[:, cw:])

    qkv = _dot(u_hi, wqkv_ref[...])
    hb = hb_ref[...]

    def head_sumsq(t):
        t2_hi, t2_lo = _split_bf16(t * t)
        return _dot(t2_hi, hb) + _dot(t2_lo, hb)

    def head_norm(t, g):
        return t * lax.rsqrt(head_sumsq(t) * (1.0 / HEAD_DIM) + EPS) * g

    q = head_norm(qkv[:, :aw], qg_ref[...]) * (HEAD_DIM ** -0.5 * LOG2E)
    q_t = q.T
    for hd in range(N_HEADS):
        qt_ref[hd] = q_t[hd * HEAD_DIM:(hd + 1) * HEAD_DIM, :].astype(BF16)
    k = head_norm(qkv[:, aw:2 * aw], kg_ref[...]).astype(BF16).astype(F32)
    one_hot0 = lambda shape, axis: jnp.where(lax.broadcasted_iota(I32, shape, axis) == 0, 1.0, 0.0)
    k_pad = one_hot0((tm, LANES - HEAD_DIM), 1)
    for hd in range(N_HEADS):
        kx_ref[hd] = jnp.concatenate([k[:, hd * HEAD_DIM:(hd + 1) * HEAD_DIM], k_pad], axis=1).astype(BF16)
    v_t = qkv[:, 2 * aw:].T
    v_pad = one_hot0((LANES - HEAD_DIM, tm), 0).astype(BF16)
    for hd in range(N_HEADS):
        vt_ref[hd, 0:HEAD_DIM, :] = v_t[hd * HEAD_DIM:(hd + 1) * HEAD_DIM, :].astype(BF16)
        vt_ref[hd, HEAD_DIM:, :] = v_pad

    wih = wih_ref[...]
    idx = _dot(u_hi, wih) + _dot(u_lo, wih) + _dot(u_hi, wil_ref[...])
    qi_hi, qi_lo = _split_bf16(idx[:, :nqi].T)
    for blk in range(tm // SEL_TQ):
        cols = slice(blk * SEL_TQ, (blk + 1) * SEL_TQ)
        for hd in range(IDX_HEADS):
            rows = slice(hd * IDX_DIM, (hd + 1) * IDX_DIM)
            out_cols = slice((hd % 2) * SEL_TQ, (hd % 2 + 1) * SEL_TQ)
            for part, src in enumerate((qi_hi, qi_hi, qi_lo, qi_lo)):
                qip_ref[blk, hd // 2, part * IDX_DIM:(part + 1) * IDX_DIM, out_cols] = src[rows, cols]
    ki = idx[:, nqi:nqi + IDX_DIM]
    ki_hi = ki.astype(BF16).astype(F32)
    ki_lo = ki - ki_hi
    kix_ref[...] = jnp.concatenate([ki_hi, ki_lo, ki_hi, ki_lo], axis=1).astype(BF16)
    wi_ref[...] = idx[:, nqi:]

    gates = _sigmoid(_dot(u_hi, wg_ref[...]))
    gc_ref[...] = gates[:, :d]
    ga_ref[...] = gates[:, d:]


def _mix_in(h, g, w_conv, w_qkv, w_idx_hi, w_idx_lo, w_gate, qg, kg, head_blocks, tm=512):
    s, d = h.shape
    cw = w_conv.shape[1] // 2
    aw = w_qkv.shape[1] // 3
    nqi = IDX_HEADS * IDX_DIM
    iw = w_idx_hi.shape[1]
    assert aw == N_HEADS * HEAD_DIM and tm % SEL_TQ == 0 and iw - nqi == LANES
    full = lambda a: pl.BlockSpec(a.shape, lambda i: (0,) * a.ndim)
    row = lambda n: pl.BlockSpec((tm, n), lambda i: (i, 0))
    g2 = g.reshape(1, d)
    return pl.pallas_call(
        _mix_in_kernel,
        grid=(s // tm,),
        in_specs=[row(d), full(g2), full(w_conv), full(w_qkv), full(w_idx_hi), full(w_idx_lo), full(w_gate),
                  full(qg), full(kg), full(head_blocks)],
        out_specs=[
            row(cw),
            pl.BlockSpec((N_HEADS, HEAD_DIM, tm), lambda i: (0, 0, i)),
            pl.BlockSpec((N_HEADS, tm, LANES), lambda i: (0, i, 0)),
            pl.BlockSpec((N_HEADS, LANES, tm), lambda i: (0, 0, i)),
            pl.BlockSpec((tm // SEL_TQ, IDX_HEADS // 2, 4 * IDX_DIM, 2 * SEL_TQ), lambda i: (i, 0, 0, 0)),
            row(4 * IDX_DIM), row(LANES), row(d), row(d),
        ],
        out_shape=[
            jax.ShapeDtypeStruct((s, cw), F32),
            jax.ShapeDtypeStruct((N_HEADS, HEAD_DIM, s), BF16),
            jax.ShapeDtypeStruct((N_HEADS, s, LANES), BF16),
            jax.ShapeDtypeStruct((N_HEADS, LANES, s), BF16),
            jax.ShapeDtypeStruct((s // SEL_TQ, IDX_HEADS // 2, 4 * IDX_DIM, 2 * SEL_TQ), BF16),
            jax.ShapeDtypeStruct((s, 4 * IDX_DIM), BF16),
            jax.ShapeDtypeStruct((s, LANES), F32),
            jax.ShapeDtypeStruct((s, d), F32),
            jax.ShapeDtypeStruct((s, d), F32),
        ],
        compiler_params=pltpu.CompilerParams(dimension_semantics=("arbitrary",), vmem_limit_bytes=VMEM_LIMIT),
        name="mix_in",
    )(h, g2, w_conv, w_qkv, w_idx_hi, w_idx_lo, w_gate, qg, kg, head_blocks)


def _conv_kernel(z_ref, halo_ref, dw_ref, db_ref, lg_ref, lb_ref, wo_ref, gc_ref, o_ref, zp_ref, zs_ref, acc_ref):
    tm = z_ref.shape[0]
    first = pl.program_id(0) == 0
    halo = halo_ref[...]
    zp_ref[0:CONV_HALO, :] = jnp.where(first, jnp.zeros_like(halo), halo)
    zp_ref[CONV_HALO:, :] = z_ref[...]
    off = CONV_HALO - (CONV_WIDTH - 1)
    span = zs_ref.shape[1]
    for b in range(1, SUBLANES):
        zs_ref[b - 1] = zp_ref[b:b + span, :]

    def chunk(c, carry):
        r0 = pl.multiple_of(c * CONV_CHUNK, CONV_CHUNK)
        acc = jnp.zeros((CONV_CHUNK, z_ref.shape[1]), F32) + db_ref[...]
        for j in range(CONV_WIDTH):
            a, b = divmod(off + j, SUBLANES)
            src = zp_ref if b == 0 else zs_ref.at[b - 1]
            acc = acc + dw_ref[j:j + 1, :] * src[pl.ds(r0 + a * SUBLANES, CONV_CHUNK), :]
        acc_ref[pl.ds(r0, CONV_CHUNK), :] = acc
        return carry

    lax.fori_loop(0, tm // CONV_CHUNK, chunk, 0)
    acc = acc_ref[...]
    mu = jnp.mean(acc, axis=-1, keepdims=True)
    xc = acc - mu
    y = xc * lax.rsqrt(jnp.mean(xc * xc, axis=-1, keepdims=True) + EPS)
    y = y * lg_ref[...] + lb_ref[...]
    y = (y * _sigmoid(y)).astype(BF16)
    o_ref[...] = gc_ref[...] * _dot(y, wo_ref[...])


def _conv(z, dw_w, dw_b, ln_g, ln_b, w_out, gate_c, tm=512):
    s, c = z.shape
    d = w_out.shape[1]
    assert tm % CONV_HALO == 0 and tm % CONV_CHUNK == 0
    r = tm // CONV_HALO
    span = tm + CONV_HALO - SUBLANES
    full = lambda a: pl.BlockSpec(a.shape, lambda i: (0,) * a.ndim)
    vecs = [dw_b.reshape(1, c), ln_g.reshape(1, c), ln_b.reshape(1, c)]
    return pl.pallas_call(
        _conv_kernel,
        grid=(s // tm,),
        in_specs=[
            pl.BlockSpec((tm, c), lambda i: (i, 0)),
            pl.BlockSpec((CONV_HALO, c), lambda i: (jnp.maximum(i * r - 1, 0), 0)),
            full(dw_w), full(vecs[0]), full(vecs[1]), full(vecs[2]), full(w_out),
            pl.BlockSpec((tm, d), lambda i: (i, 0)),
        ],
        out_specs=pl.BlockSpec((tm, d), lambda i: (i, 0)),
        out_shape=jax.ShapeDtypeStruct((s, d), F32),
        scratch_shapes=[pltpu.VMEM((tm + CONV_HALO, c), F32), pltpu.VMEM((SUBLANES - 1, span, c), F32),
                        pltpu.VMEM((tm, c), F32)],
        compiler_params=pltpu.CompilerParams(dimension_semantics=("arbitrary",), vmem_limit_bytes=VMEM_LIMIT),
        name="conv",
    )(z, z, dw_w, *vecs, w_out, gate_c)


def _sortable_key(x):
    bits = pltpu.bitcast(x + 0.0, I32)
    return bits ^ ((bits >> 31) & 0x7FFFFFFF)


def _key_to_float(key):
    return pltpu.bitcast(key ^ ((key >> 31) & 0x7FFFFFFF), F32)


def _sort_network(n):
    pairs = []

    def merge(lo, hi, r):
        step = r * 2
        if step < hi - lo:
            merge(lo, hi, step)
            merge(lo + r, hi, step)
            pairs.extend((i, i + r) for i in range(lo + r, hi - r, step))
        else:
            pairs.append((lo, lo + r))

    def sort(lo, hi):
        if hi > lo:
            mid = lo + (hi - lo) // 2
            sort(lo, mid)
            sort(mid + 1, hi)
            merge(lo, hi, 1)

    sort(0, n - 1)
    return pairs


def _compare_exchange(a, i, j):
    a[i], a[j] = jnp.maximum(a[i], a[j]), jnp.minimum(a[i], a[j])


def _merge_top(top, batch):
    n = len(top)
    out = [jnp.maximum(top[i], batch[n - 1 - i]) for i in range(n)]
    d = n // 2
    while d >= 1:
        for i in range(n):
            if i & d == 0:
                _compare_exchange(out, i, i + d)
        d //= 2
    return out


def _select_kernel(qi_ref, w_ref, ki_ref, mask_ref, sc_ref, cand_ref, ckey_ref, st_ref, *, k_top):
    qb = pl.program_id(0)
    n_rows = mask_ref.shape[1]
    n_tiles = n_rows // SEL_TK
    q0 = qb * SEL_TQ
    nkt = (q0 + SEL_TQ - 1) // SEL_TK + 1
    vt = SEL_TK // SUBLANES
    vshape = (SUBLANES, LANES)
    qpos = q0 + lax.broadcasted_iota(I32, vshape, 1)
    sub = lax.broadcasted_iota(I32, vshape, 0)
    w = w_ref[...] * ((IDX_HEADS ** -0.5) * (IDX_DIM ** -0.5))

    qpos_t = q0 + lax.broadcasted_iota(I32, (SEL_TK, LANES), 1)
    krow_t = lax.broadcasted_iota(I32, (SEL_TK, LANES), 0)
    zero_bits_t = (MIN_NORMAL_BITS + n_rows) - krow_t

    def score_tile(kt):
        k0 = pl.multiple_of(kt * SEL_TK, SEL_TK)
        ki = ki_ref[pl.ds(k0, SEL_TK), :]
        sc = None
        for pair in range(IDX_HEADS // 2):
            r = _dot(ki, qi_ref[0, pair])
            for j in range(2):
                h = 2 * pair + j
                term = w[h:h + 1, :] * jnp.maximum(r[:, j * LANES:(j + 1) * LANES], 0.0)
                sc = term if sc is None else sc + term
        sc_ref[pl.ds(k0, SEL_TK), :] = jnp.where(sc == 0.0, pltpu.bitcast(zero_bits_t - k0, F32), sc)

    def vreg(ref, row):
        return ref[pl.ds(pl.multiple_of(row, SUBLANES), SUBLANES), :]

    def col_sum(parts):
        tot = parts[0]
        for part in parts[1:]:
            tot = tot + part
        return jnp.broadcast_to(jnp.sum(tot, axis=0, keepdims=True), vshape)

    n_cand = SEL_GROUPS * SEL_DEPTH
    network = _sort_network(SEL_DEPTH)
    cand_ref[...] = jnp.full(cand_ref.shape, -jnp.inf, F32)

    def lists_tile(kt):
        for g in range(SEL_GROUPS):
            batch = [vreg(sc_ref, kt * SEL_TK + (g * SEL_DEPTH + j) * SUBLANES) for j in range(SEL_DEPTH)]
            for i, j in network:
                _compare_exchange(batch, i, j)
            rows = [slice((g * SEL_DEPTH + i) * SUBLANES, (g * SEL_DEPTH + i + 1) * SUBLANES)
                    for i in range(SEL_DEPTH)]
            top = _merge_top([cand_ref[r, :] for r in rows], batch)
            for r, t in zip(rows, top):
                cand_ref[r, :] = t

    sc_ref[pl.ds(n_rows, SEL_TK), :] = jnp.full((SEL_TK, LANES), -jnp.inf, F32)

    def score_and_lists(kt, carry):
        lists_tile(jnp.where(kt == 0, n_tiles, kt - 1))
        score_tile(kt)
        return carry

    lax.fori_loop(0, nkt, score_and_lists, 0)
    k_last = pl.multiple_of((nkt - 1) * SEL_TK, SEL_TK)
    sc_ref[pl.ds(k_last, SEL_TK), :] = jnp.where(
        k_last + krow_t <= qpos_t, sc_ref[pl.ds(k_last, SEL_TK), :], -jnp.inf)
    lists_tile(nkt - 1)
    ckey_ref[...] = _sortable_key(cand_ref[...])

    n_acc = 4

    def cand_bit(i, t):
        c = t ^ lax.shift_left(jnp.int32(1), 31 - i)
        acc = [jnp.zeros(vshape, I32) for _ in range(n_acc)]
        for v in range(n_cand):
            x = ckey_ref[v * SUBLANES:(v + 1) * SUBLANES, :]
            acc[v % n_acc] = acc[v % n_acc] + jnp.where(x >= c, 1, 0)
        return jnp.where(col_sum(acc) >= k_top, c, t)

    st_ref[0] = lax.fori_loop(0, 32, cand_bit, jnp.full(vshape, INT_MIN, I32))

    def count_scores(preds):
        def tile(kt, acc):
            acc = [list(a) for a in acc]
            for v in range(vt):
                x = vreg(sc_ref, kt * SEL_TK + v * SUBLANES)
                kpos = kt * SEL_TK + v * SUBLANES + sub
                for p, pred in enumerate(preds):
                    acc[p][v % n_acc] = acc[p][v % n_acc] + jnp.where(pred(x, kpos), 1, 0)
            return tuple(tuple(a) for a in acc)

        zero = tuple(tuple(jnp.zeros(vshape, I32) for _ in range(n_acc)) for _ in preds)
        return [col_sum(list(a)) for a in lax.fori_loop(0, nkt, tile, zero)]

    def count_around_threshold():
        tf = _key_to_float(st_ref[0])
        gt, ge = count_scores([lambda x, kpos: x > tf, lambda x, kpos: x >= tf])
        st_ref[1] = gt
        st_ref[2] = ge

    tf_fast = jnp.broadcast_to(_key_to_float(st_ref[0])[0:1, :], (SEL_TK, LANES))

    def fast_tile(kt, acc):
        k0 = pl.multiple_of(kt * SEL_TK, SEL_TK)
        m = jnp.where(sc_ref[pl.ds(k0, SEL_TK), :] >= tf_fast, 1.0, 0.0)
        mask_ref[0, pl.ds(k0, SEL_TK), :] = m.astype(BF16)
        acc = list(acc)
        for v in range(vt):
            acc[v % n_acc] = acc[v % n_acc] + m[v * SUBLANES:(v + 1) * SUBLANES, :]
        return tuple(acc)

    size = col_sum(list(lax.fori_loop(0, nkt, fast_tile, tuple(jnp.zeros(vshape, F32) for _ in range(n_acc)))))

    @pl.when(jnp.max(jnp.abs(size - k_top)) > 0.0)
    def _():
        count_around_threshold()

        @pl.when(jnp.max(st_ref[1]) >= k_top)
        def _():
            def full_bit(i, t):
                c = t ^ lax.shift_left(jnp.int32(1), 31 - i)
                cnt, = count_scores([lambda x, kpos: _sortable_key(x) >= c])
                return jnp.where(cnt >= k_top, c, t)

            st_ref[0] = lax.fori_loop(0, 32, full_bit, jnp.full(vshape, INT_MIN, I32))
            count_around_threshold()

        st_ref[3] = jnp.full(vshape, n_rows, I32)

        @pl.when(jnp.max(st_ref[2]) > k_top)
        def _():
            tf = _key_to_float(st_ref[0])
            need = k_top - st_ref[1]
            n_bits = n_rows.bit_length() - 1

            def pos_bit(i, p):
                c = p + lax.shift_left(jnp.int32(1), n_bits - 1 - i)
                f, = count_scores([lambda x, kpos: (x == tf) & (kpos < c)])
                return jnp.where(f < need, c, p)

            st_ref[3] = lax.fori_loop(0, n_bits, pos_bit, jnp.zeros(vshape, I32))

        tf_t = jnp.broadcast_to(_key_to_float(st_ref[0])[0:1, :], (SEL_TK, LANES))
        lim_t = jnp.broadcast_to(st_ref[3][0:1, :], (SEL_TK, LANES))

        def write_tile(kt, carry):
            k0 = pl.multiple_of(kt * SEL_TK, SEL_TK)
            x = sc_ref[pl.ds(k0, SEL_TK), :]
            kpos = k0 + krow_t
            sel = ((x > tf_t) | ((x == tf_t) & (kpos <= lim_t))) & (kpos <= qpos_t)
            mask_ref[0, pl.ds(k0, SEL_TK), :] = jnp.where(sel, 1.0, 0.0).astype(BF16)
            return carry

        lax.fori_loop(0, nkt, write_tile, 0)

    def fill_tile(kt, carry):
        mask_ref[0, pl.ds(pl.multiple_of(kt * SEL_TK, SEL_TK), SEL_TK), :] = jnp.zeros((SEL_TK, LANES), BF16)
        return carry

    lax.fori_loop(nkt, n_tiles, fill_tile, 0)


def _select(qi_pairs, w_t, ki_ext, k_top):
    nqb, npair, kw, _ = qi_pairs.shape
    s = ki_ext.shape[0]
    n_cand_rows = SEL_GROUPS * SEL_DEPTH * SUBLANES
    assert s % SEL_TK == 0 and SEL_TK == n_cand_rows and n_cand_rows >= k_top
    return pl.pallas_call(
        functools.partial(_select_kernel, k_top=k_top),
        grid=(nqb,),
        in_specs=[
            pl.BlockSpec((1, npair, kw, 2 * SEL_TQ), lambda i: (i, 0, 0, 0)),
            pl.BlockSpec((w_t.shape[0], SEL_TQ), lambda i: (0, i)),
            pl.BlockSpec((s, kw), lambda i: (0, 0)),
        ],
        out_specs=pl.BlockSpec((1, s, SEL_TQ), lambda i: (i, 0, 0)),
        out_shape=jax.ShapeDtypeStruct((nqb, s, SEL_TQ), BF16),
        scratch_shapes=[
            pltpu.VMEM((s + SEL_TK, SEL_TQ), F32),
            pltpu.VMEM((n_cand_rows, SEL_TQ), F32),
            pltpu.VMEM((n_cand_rows, SEL_TQ), I32),
            pltpu.VMEM((4, SUBLANES, LANES), I32),
        ],
        compiler_params=pltpu.CompilerParams(dimension_semantics=("arbitrary",), vmem_limit_bytes=VMEM_LIMIT),
        name="select",
    )(qi_pairs, w_t, ki_ext)


def _causal_pairs(s, tq, tk):
    pairs = [(qb, kt) for qb in range(s // tq) for kt in range((qb * tq + tq - 1) // tk + 1)]
    return jnp.asarray([p[0] for p in pairs], I32), jnp.asarray([p[1] for p in pairs], I32)


def _attn_kernel(qb_tab, kt_tab, qt_ref, kmax_ref, k_ref, vt_ref, mask_ref, bias_ref, o_ref, l_ref, qx_ref, acc_ref):
    step = pl.program_id(0)
    qb = qb_tab[step]
    kt = kt_tab[step]
    q0 = qb * ATT_TQ
    k0 = kt * ATT_TK
    nq = ATT_TQ // LANES
    nk = ATT_TK // LANES

    @pl.when(kt == 0)
    def _():
        row = lax.broadcasted_iota(I32, (HEAD_DIM, ATT_TQ), 0)
        for h in range(N_HEADS):
            q = qt_ref[h]
            qf = q.astype(F32)
            bound = jnp.sqrt(jnp.sum(qf * qf, axis=0, keepdims=True)) * kmax_ref[h] * 1.02
            qx_ref[h, 0:HEAD_DIM, :] = q
            qx_ref[h, HEAD_DIM:, :] = jnp.where(row == 0, -bound, 0.0).astype(BF16)
        acc_ref[...] = jnp.zeros(acc_ref.shape, F32)

    mask = jnp.concatenate([mask_ref[a] for a in range(nq)], axis=1)

    def heads(with_bias):
        for h in range(N_HEADS):
            s = _dot(k_ref[h], qx_ref[h])
            if with_bias:
                rows = []
                for c in range(nk):
                    tab = [jnp.clip((q0 + a * LANES - k0 - c * LANES) // LANES + 1, 0, 3) for a in range(nq)]
                    rows.append(jnp.concatenate([bias_ref[tab[a], h] for a in range(nq)], axis=1))
                s = s + jnp.concatenate(rows, axis=0)
            p = jnp.exp2(s).astype(BF16) * mask
            acc_ref[h] += _dot(vt_ref[h], p)

    near = k0 + ATT_TK + 2 * LANES > q0

    @pl.when(near)
    def _():
        heads(True)

    @pl.when(jnp.logical_not(near))
    def _():
        heads(False)

    @pl.when(kt == (q0 + ATT_TQ - 1) // ATT_TK)
    def _():
        for h in range(N_HEADS):
            acc = acc_ref[h]
            den = acc[HEAD_DIM:HEAD_DIM + 1, :]
            o_ref[h] = acc[:HEAD_DIM, :] / den
            l_ref[h] = den


def _attn(q_t, kmax_b, k_ext, v_t, mask, bias_tab):
    nh, hd, s = q_t.shape
    qb_tab, kt_tab = _causal_pairs(s, ATT_TQ, ATT_TK)
    grid_spec = pltpu.PrefetchScalarGridSpec(
        num_scalar_prefetch=2,
        grid=(qb_tab.shape[0],),
        in_specs=[
            pl.BlockSpec((nh, hd, ATT_TQ), lambda i, qb, kt: (0, 0, qb[i])),
            pl.BlockSpec((nh, 1, ATT_TQ), lambda i, qb, kt: (0, 0, 0)),
            pl.BlockSpec((nh, ATT_TK, LANES), lambda i, qb, kt: (0, kt[i], 0)),
            pl.BlockSpec((nh, LANES, ATT_TK), lambda i, qb, kt: (0, 0, kt[i])),
            pl.BlockSpec((ATT_TQ // LANES, ATT_TK, LANES), lambda i, qb, kt: (qb[i], kt[i], 0)),
            pl.BlockSpec(bias_tab.shape, lambda i, qb, kt: (0, 0, 0, 0)),
        ],
        out_specs=[
            pl.BlockSpec((nh, hd, ATT_TQ), lambda i, qb, kt: (0, 0, qb[i])),
            pl.BlockSpec((nh, 1, ATT_TQ), lambda i, qb, kt: (0, 0, qb[i])),
        ],
        scratch_shapes=[pltpu.VMEM((nh, LANES, ATT_TQ), BF16), pltpu.VMEM((nh, LANES, ATT_TQ), F32)],
    )
    return pl.pallas_call(
        _attn_kernel,
        grid_spec=grid_spec,
        out_shape=[jax.ShapeDtypeStruct((nh, hd, s), F32), jax.ShapeDtypeStruct((nh, 1, s), F32)],
        compiler_params=pltpu.CompilerParams(dimension_semantics=("arbitrary",), vmem_limit_bytes=VMEM_LIMIT),
        name="attn",
    )(qb_tab, kt_tab, q_t, kmax_b, k_ext, v_t, mask, bias_tab)


def _attn_safe_kernel(qb_tab, kt_tab, q_ref, k_ref, v_ref, mask_ref, bias_ref, o_ref, m_ref, acc_ref):
    step = pl.program_id(0)
    qb = qb_tab[step]
    kt = kt_tab[step]
    nq = SAFE_TQ // LANES
    nsub = ATT_TK // LANES

    @pl.when(kt == 0)
    def _():
        m_ref[...] = jnp.full(m_ref.shape, NEG, F32)
        acc_ref[...] = jnp.zeros(acc_ref.shape, F32)

    for a in range(nq):
        rows = slice(a * LANES, (a + 1) * LANES)
        q0 = qb * SAFE_TQ + a * LANES
        maskf = (mask_ref[a].astype(F32) - 1.0) * (-NEG)
        tab_idx = [jnp.clip((q0 - (kt * ATT_TK + c * LANES)) // LANES + 1, 0, 3) for c in range(nsub)]
        for h in range(N_HEADS):
            s = _dot_t(q_ref[h, rows, :], k_ref[h]) + maskf
            s = s + jnp.concatenate([bias_ref[tab_idx[c], h] for c in range(nsub)], axis=1)
            m_old = m_ref[h, rows, :]
            m_new = jnp.maximum(m_old, jnp.max(s, axis=1, keepdims=True))
            p = jnp.exp2(s - m_new[:, 0:1])
            alpha = jnp.exp2(m_old - m_new)
            acc_ref[h, rows, :] = alpha * acc_ref[h, rows, :] + _dot(p.astype(BF16), v_ref[h])
            m_ref[h, rows, :] = m_new

    @pl.when(kt == (qb * SAFE_TQ + SAFE_TQ - 1) // ATT_TK)
    def _():
        for h in range(N_HEADS):
            acc = acc_ref[h]
            o_ref[h] = acc[:, :HEAD_DIM] / acc[:, HEAD_DIM:HEAD_DIM + 1]


def _attn_safe(q, k, v_ext, mask_qk, bias_tab):
    nh, s, hd = q.shape
    qb_tab, kt_tab = _causal_pairs(s, SAFE_TQ, ATT_TK)
    grid_spec = pltpu.PrefetchScalarGridSpec(
        num_scalar_prefetch=2,
        grid=(qb_tab.shape[0],),
        in_specs=[
            pl.BlockSpec((nh, SAFE_TQ, hd), lambda i, qb, kt: (0, qb[i], 0)),
            pl.BlockSpec((nh, ATT_TK, hd), lambda i, qb, kt: (0, kt[i], 0)),
            pl.BlockSpec((nh, ATT_TK, LANES), lambda i, qb, kt: (0, kt[i], 0)),
            pl.BlockSpec((SAFE_TQ // LANES, LANES, ATT_TK), lambda i, qb, kt: (qb[i], 0, kt[i])),
            pl.BlockSpec(bias_tab.shape, lambda i, qb, kt: (0, 0, 0, 0)),
        ],
        out_specs=pl.BlockSpec((nh, SAFE_TQ, hd), lambda i, qb, kt: (0, qb[i], 0)),
        scratch_shapes=[pltpu.VMEM((nh, SAFE_TQ, LANES), F32), pltpu.VMEM((nh, SAFE_TQ, LANES), F32)],
    )
    return pl.pallas_call(
        _attn_safe_kernel,
        grid_spec=grid_spec,
        out_shape=jax.ShapeDtypeStruct((nh, s, hd), F32),
        compiler_params=pltpu.CompilerParams(dimension_semantics=("arbitrary",), vmem_limit_bytes=VMEM_LIMIT),
        name="attn_safe",
    )(qb_tab, kt_tab, q, k, v_ext, mask_qk, bias_tab)


def _tail_kernel(h_ref, cg_ref, ga_ref, at_ref, p_ref, wao_ref, wmo_ref, fg_ref, wi_ref, wo_ref, pg_ref, wpg_ref,
                 wpp_ref, o_ref):
    attn = at_ref[...].T.astype(BF16)
    merged = cg_ref[...] + ga_ref[...] * _dot(attn, wao_ref[...])
    h = h_ref[...] + _dot(merged.astype(BF16), wmo_ref[...])
    h = _half_step_ffn(h, fg_ref[...], wi_ref, wo_ref)
    gate = _sigmoid(_dot(_rms(h, pg_ref[...]).astype(BF16), wpg_ref[...]))
    o_ref[...] = h + gate * _dot(p_ref[...].astype(BF16), wpp_ref[...])


def _tail(h, conv_g, gate_a, attn_t, p, w_ao, w_mo, ffn_g, w_in, w_out, ple_g, w_pg, w_pp, tm=512):
    s, d = h.shape
    row = lambda n: pl.BlockSpec((tm, n), lambda i: (i, 0))
    fg2 = ffn_g.reshape(1, d)
    pg2 = ple_g.reshape(1, d)
    params = (w_ao, w_mo, fg2, w_in, w_out, pg2, w_pg, w_pp)
    return pl.pallas_call(
        _tail_kernel,
        grid=(s // tm,),
        in_specs=[row(d), row(d), row(d), pl.BlockSpec((attn_t.shape[0], tm), lambda i: (0, i)), row(p.shape[1])]
        + [_resident(a) for a in params],
        out_specs=row(d),
        out_shape=jax.ShapeDtypeStruct((s, d), F32),
        compiler_params=pltpu.CompilerParams(dimension_semantics=("arbitrary",), vmem_limit_bytes=VMEM_LIMIT),
        name="tail",
    )(h, conv_g, gate_a, attn_t, p, *params)


def _t5_bucket_table(n_dist):
    n = np.arange(n_dist)
    max_exact = NUM_BUCKETS // 2
    nf = np.maximum(n, 1).astype(np.float32)
    large = max_exact + (np.log(nf / max_exact) / math.log(MAX_DISTANCE / max_exact)
                         * (NUM_BUCKETS - max_exact)).astype(np.int32)
    large = np.minimum(large, NUM_BUCKETS - 1)
    return np.where(n < max_exact, n, large)


def _bias_tables(rel_bias):
    assert MAX_DISTANCE <= LANES
    n = LANES
    rel = (rel_bias.astype(F32) - rel_bias[NUM_BUCKETS - 1].astype(F32)[None, :]) * LOG2E
    by_dist = rel[_t5_bucket_table(2 * n)].T

    def toeplitz(a):
        skew = jnp.tile(a, (1, n))[:, :n * (2 * n - 1)].reshape(a.shape[0], n, 2 * n - 1)
        return skew[:, :, :n]

    d0 = toeplitz(jnp.concatenate([by_dist[:, :n], jnp.broadcast_to(by_dist[:, :1], by_dist[:, :n].shape)], axis=1))
    d1 = toeplitz(jnp.concatenate([by_dist[:, n:], by_dist[:, :n]], axis=1))
    zero = jnp.zeros_like(d0)
    return jnp.swapaxes(jnp.stack([zero, d0, d1, zero]), 2, 3)


def kernel(x, p, ffn1_norm, ffn1_w_in, ffn1_w_out, mix_norm, mix_w_in, conv_dw_w, conv_dw_b, conv_ln_g,
           conv_ln_b, conv_w_out, q_norm, k_norm, attn_w_out, mix_w_out, ffn2_norm, ffn2_w_in, ffn2_w_out,
           ple_norm, ple_w_gate, ple_w_proj, rel_bias):
    b, s, d = x.shape
    depth = ffn1_norm.shape[0]
    cw = conv_dw_w.shape[2]
    aw = N_HEADS * HEAD_DIM
    nqi = IDX_HEADS * IDX_DIM
    iw = nqi + IDX_DIM + IDX_HEADS
    iw_pad = -(-iw // LANES) * LANES
    k_top = min(TOPK_MAX, s // 4)
    assert b == 1 and s % ATT_TK == 0 and mix_w_in.shape[2] == 2 * cw + 3 * aw + iw + 2 * d

    head_blocks = jnp.asarray(np.kron(np.eye(N_HEADS), np.ones((HEAD_DIM, HEAD_DIM))), BF16)
    bias_qk = _bias_tables(rel_bias)
    bias_kq = jnp.swapaxes(bias_qk, 2, 3)

    h = x[0]
    for i in range(depth):
        h = _ffn(h, ffn1_norm[i], ffn1_w_in[i].astype(BF16), ffn1_w_out[i].astype(BF16))

        w = mix_w_in[i]
        o0 = 2 * cw
        o1 = o0 + 3 * aw
        o2 = o1 + iw
        w_idx = jnp.pad(w[:, o1:o2], ((0, 0), (0, iw_pad - iw)))
        w_idx_hi, w_idx_lo = _split_bf16(w_idx)
        qg = jnp.tile(q_norm[i], N_HEADS).reshape(1, aw)
        kg = jnp.tile(k_norm[i], N_HEADS).reshape(1, aw)
        glu, q_t, k_ext, v_t, qi_pairs, ki_ext, idx_w, gate_c, gate_a = _mix_in(
            h, mix_norm[i], w[:, :o0].astype(BF16), w[:, o0:o1].astype(BF16), w_idx_hi, w_idx_lo,
            w[:, o2:].astype(BF16), qg, kg, head_blocks)

        conv_g = _conv(glu, conv_dw_w[i], conv_dw_b[i], conv_ln_g[i], conv_ln_b[i],
                       conv_w_out[i].astype(BF16), gate_c)

        w_t = idx_w[:, IDX_DIM:IDX_DIM + IDX_HEADS].T
        mask = _select(qi_pairs, w_t, ki_ext, k_top)

        kmax = math.sqrt(HEAD_DIM) * jnp.max(jnp.abs(k_norm[i]))
        kmax_b = jnp.full((N_HEADS, 1, ATT_TQ), kmax, F32)
        attn_t, den = _attn(q_t, kmax_b, k_ext, v_t, mask, bias_kq)

        def safe_attn():
            out = _attn_safe(jnp.swapaxes(q_t, 1, 2), k_ext[:, :, :HEAD_DIM], jnp.swapaxes(v_t, 1, 2),
                             jnp.swapaxes(mask, 1, 2), bias_qk)
            return jnp.swapaxes(out, 1, 2)

        underflow = jnp.logical_not(jnp.min(den) > L_MIN)
        attn_t = lax.cond(underflow, safe_attn, lambda: attn_t).reshape(aw, s)

        h = _tail(h, conv_g, gate_a, attn_t, p[i, 0], attn_w_out[i].astype(BF16), mix_w_out[i].astype(BF16),
                  ffn2_norm[i], ffn2_w_in[i].astype(BF16), ffn2_w_out[i].astype(BF16),
                  ple_norm[i], ple_w_gate[i].astype(BF16), ple_w_proj[i].astype(BF16))
    return h[None]
```

```python
import functools
import math

import numpy as np
import jax
import jax.numpy as jnp
from jax import lax
from jax.experimental import pallas as pl
from jax.experimental.pallas import tpu as pltpu

F32 = jnp.float32
BF16 = jnp.bfloat16
I32 = jnp.int32

EPS = 1e-6
CONV_WIDTH = 31
N_HEADS = 8
HEAD_DIM = 64
IDX_HEADS = 4
IDX_DIM = 64
TOPK_MAX = 256
NUM_BUCKETS = 32
MAX_DISTANCE = 128

LANES = 128
SUBLANES = 8
MXU_TILE = 256
FFN_CHUNK = 1024
VMEM_LIMIT = 56 * 1024 * 1024
NEG = -1e30
INT_MIN = -2 ** 31
MIN_NORMAL_BITS = 0x00800000
LOG2E = math.log2(math.e)

CONV_HALO = 32
CONV_CHUNK = 32
SEL_TQ = LANES
SEL_TK = 1024
SEL_GROUPS = 8
SEL_DEPTH = 16
ATT_TQ = 1024
ATT_TK = 1024
SAFE_TQ = 256
L_MIN = 1e-30
BOUND_MARGIN = 1.02


def _sigmoid(x):
    return 1.0 / (1.0 + jnp.exp(-x))


def _rms(x, g):
    ms = jnp.mean(x * x, axis=-1, keepdims=True)
    return x * lax.rsqrt(ms + EPS) * g


def _dot(a, b):
    return jnp.dot(a, b, preferred_element_type=F32)


def _dot_t(a, b):
    return lax.dot_general(a, b, (((1,), (1,)), ((), ())), preferred_element_type=F32)


def _split_bf16(x):
    hi = x.astype(BF16)
    lo = (x - hi.astype(F32)).astype(BF16)
    return hi, lo


def _half_step_ffn(x, g, wi_ref, wo_ref):
    xn = _rms(x, g).astype(BF16)
    dff = wo_ref.shape[0]
    acc = None
    for c0 in range(0, dff, FFN_CHUNK):
        c1 = min(c0 + FFN_CHUNK, dff)
        a = _dot(xn, wi_ref[:, c0:c1])
        b = _dot(xn, wi_ref[:, dff + c0:dff + c1])
        part = _dot((a * _sigmoid(a) * b).astype(BF16), wo_ref[c0:c1, :])
        acc = part if acc is None else acc + part
    return x + 0.5 * acc


def _ffn_kernel(x_ref, g_ref, wi_ref, wo_ref, o_ref):
    o_ref[...] = _half_step_ffn(x_ref[...], g_ref[...], wi_ref, wo_ref)


def _resident(a):
    return pl.BlockSpec(a.shape, lambda *_: (0,) * a.ndim, pipeline_mode=pl.Buffered(1))


def _ffn(x, g, w_in, w_out, tm=512):
    s, d = x.shape
    dff = w_out.shape[0]
    assert s % tm == 0 and dff % MXU_TILE == 0 and FFN_CHUNK % MXU_TILE == 0
    g2 = g.reshape(1, d)
    return pl.pallas_call(
        _ffn_kernel,
        grid=(s // tm,),
        in_specs=[pl.BlockSpec((tm, d), lambda i: (i, 0)), _resident(g2), _resident(w_in), _resident(w_out)],
        out_specs=pl.BlockSpec((tm, d), lambda i: (i, 0)),
        out_shape=jax.ShapeDtypeStruct((s, d), F32),
        compiler_params=pltpu.CompilerParams(dimension_semantics=("arbitrary",), vmem_limit_bytes=VMEM_LIMIT),
        name="ffn",
    )(x, g2, w_in, w_out)


def _mix_in_kernel(h_ref, g_ref, wc_ref, wqkv_ref, wih_ref, wil_ref, wg_ref, qg_ref, kg_ref, hb_ref,
                   glu_ref, qt_ref, kx_ref, vt_ref, qip_ref, kix_ref, wi_ref, gc_ref, ga_ref):
    u = _rms(h_ref[...], g_ref[...])
    u_hi, u_lo = _split_bf16(u)
    tm = h_ref.shape[0]
    cw = glu_ref.shape[1]
    aw = N_HEADS * HEAD_DIM
    nqi = IDX_HEADS * IDX_DIM
    d = gc_ref.shape[1]

    c = _dot(u_hi, wc_ref[...])
    glu_ref[...] = c[:, :cw] * _sigmoid(c[:, cw:])

    qkv = _dot(u_hi, wqkv_ref[...])
    hb = hb_ref[...]

    def head_sumsq(t):
        t2_hi, t2_lo = _split_bf16(t * t)
        return _dot(t2_hi, hb) + _dot(t2_lo, hb)

    def head_norm(t, g):
        return t * lax.rsqrt(head_sumsq(t) * (1.0 / HEAD_DIM) + EPS) * g

    q = head_norm(qkv[:, :aw], qg_ref[...]) * (HEAD_DIM ** -0.5 * LOG2E)
    q_t = q.T
    for hd in range(N_HEADS):
        qt_ref[hd] = q_t[hd * HEAD_DIM:(hd + 1) * HEAD_DIM, :].astype(BF16)
    k = head_norm(qkv[:, aw:2 * aw], kg_ref[...]).astype(BF16).astype(F32)
    one_hot0 = lambda shape, axis: jnp.where(lax.broadcasted_iota(I32, shape, axis) == 0, 1.0, 0.0)
    k_pad = one_hot0((tm, LANES - HEAD_DIM), 1)
    for hd in range(N_HEADS):
        kx_ref[hd] = jnp.concatenate([k[:, hd * HEAD_DIM:(hd + 1) * HEAD_DIM], k_pad], axis=1).astype(BF16)
    v_t = qkv[:, 2 * aw:].T
    v_pad = one_hot0((LANES - HEAD_DIM, tm), 0).astype(BF16)
    for hd in range(N_HEADS):
        vt_ref[hd, 0:HEAD_DIM, :] = v_t[hd * HEAD_DIM:(hd + 1) * HEAD_DIM, :].astype(BF16)
        vt_ref[hd, HEAD_DIM:, :] = v_pad

    wih = wih_ref[...]
    idx = _dot(u_hi, wih) + _dot(u_lo, wih) + _dot(u_hi, wil_ref[...])
    qi_hi, qi_lo = _split_bf16(idx[:, :nqi].T)
    for blk in range(tm // SEL_TQ):
        cols = slice(blk * SEL_TQ, (blk + 1) * SEL_TQ)
        for hd in range(IDX_HEADS):
            rows = slice(hd * IDX_DIM, (hd + 1) * IDX_DIM)
            out_cols = slice((hd % 2) * SEL_TQ, (hd % 2 + 1) * SEL_TQ)
            for part, src in enumerate((qi_hi, qi_hi, qi_lo, qi_lo)):
                qip_ref[blk, hd // 2, part * IDX_DIM:(part + 1) * IDX_DIM, out_cols] = src[rows, cols]
    ki = idx[:, nqi:nqi + IDX_DIM]
    ki_hi = ki.astype(BF16).astype(F32)
    ki_lo = ki - ki_hi
    kix_ref[...] = jnp.concatenate([ki_hi, ki_lo, ki_hi, ki_lo], axis=1).astype(BF16)
    wi_ref[...] = idx[:, nqi:]

    gates = _sigmoid(_dot(u_hi, wg_ref[...]))
    gc_ref[...] = gates[:, :d]
    ga_ref[...] = gates[:, d:]


def _mix_in(h, g, w_conv, w_qkv, w_idx_hi, w_idx_lo, w_gate, qg, kg, head_blocks, tm=512):
    s, d = h.shape
    cw = w_conv.shape[1] // 2
    aw = w_qkv.shape[1] // 3
    nqi = IDX_HEADS * IDX_DIM
    iw = w_idx_hi.shape[1]
    assert aw == N_HEADS * HEAD_DIM and tm % SEL_TQ == 0 and iw - nqi == LANES
    full = lambda a: pl.BlockSpec(a.shape, lambda i: (0,) * a.ndim)
    row = lambda n: pl.BlockSpec((tm, n), lambda i: (i, 0))
    g2 = g.reshape(1, d)
    return pl.pallas_call(
        _mix_in_kernel,
        grid=(s // tm,),
        in_specs=[row(d), full(g2), full(w_conv), full(w_qkv), full(w_idx_hi), full(w_idx_lo), full(w_gate),
                  full(qg), full(kg), full(head_blocks)],
        out_specs=[
            row(cw),
            pl.BlockSpec((N_HEADS, HEAD_DIM, tm), lambda i: (0, 0, i)),
            pl.BlockSpec((N_HEADS, tm, LANES), lambda i: (0, i, 0)),
            pl.BlockSpec((N_HEADS, LANES, tm), lambda i: (0, 0, i)),
            pl.BlockSpec((tm // SEL_TQ, IDX_HEADS // 2, 4 * IDX_DIM, 2 * SEL_TQ), lambda i: (i, 0, 0, 0)),
            row(4 * IDX_DIM), row(LANES), row(d), row(d),
        ],
        out_shape=[
            jax.ShapeDtypeStruct((s, cw), F32),
            jax.ShapeDtypeStruct((N_HEADS, HEAD_DIM, s), BF16),
            jax.ShapeDtypeStruct((N_HEADS, s, LANES), BF16),
            jax.ShapeDtypeStruct((N_HEADS, LANES, s), BF16),
            jax.ShapeDtypeStruct((s // SEL_TQ, IDX_HEADS // 2, 4 * IDX_DIM, 2 * SEL_TQ), BF16),
            jax.ShapeDtypeStruct((s, 4 * IDX_DIM), BF16),
            jax.ShapeDtypeStruct((s, LANES), F32),
            jax.ShapeDtypeStruct((s, d), F32),
            jax.ShapeDtypeStruct((s, d), F32),
        ],
        compiler_params=pltpu.CompilerParams(dimension_semantics=("arbitrary",), vmem_limit_bytes=VMEM_LIMIT),
        name="mix_in",
    )(h, g2, w_conv, w_qkv, w_idx_hi, w_idx_lo, w_gate, qg, kg, head_blocks)


def _conv_kernel(z_ref, halo_ref, dw_ref, db_ref, lg_ref, lb_ref, wo_ref, gc_ref, o_ref, zp_ref, zs_ref, acc_ref):
    tm = z_ref.shape[0]
    first = pl.program_id(0) == 0
    halo = halo_ref[...]
    zp_ref[0:CONV_HALO, :] = jnp.where(first, jnp.zeros_like(halo), halo)
    zp_ref[CONV_HALO:, :] = z_ref[...]
    off = CONV_HALO - (CONV_WIDTH - 1)
    span = zs_ref.shape[1]
    for b in range(1, SUBLANES):
        zs_ref[b - 1] = zp_ref[b:b + span, :]

    def chunk(c, carry):
        r0 = pl.multiple_of(c * CONV_CHUNK, CONV_CHUNK)
        acc = jnp.zeros((CONV_CHUNK, z_ref.shape[1]), F32) + db_ref[...]
        for j in range(CONV_WIDTH):
            a, b = divmod(off + j, SUBLANES)
            src = zp_ref if b == 0 else zs_ref.at[b - 1]
            acc = acc + dw_ref[j:j + 1, :] * src[pl.ds(r0 + a * SUBLANES, CONV_CHUNK), :]
        acc_ref[pl.ds(r0, CONV_CHUNK), :] = acc
        return carry

    lax.fori_loop(0, tm // CONV_CHUNK, chunk, 0)
    acc = acc_ref[...]
    mu = jnp.mean(acc, axis=-1, keepdims=True)
    xc = acc - mu
    y = xc * lax.rsqrt(jnp.mean(xc * xc, axis=-1, keepdims=True) + EPS)
    y = y * lg_ref[...] + lb_ref[...]
    y = (y * _sigmoid(y)).astype(BF16)
    o_ref[...] = gc_ref[...] * _dot(y, wo_ref[...])


def _conv(z, dw_w, dw_b, ln_g, ln_b, w_out, gate_c, tm=512):
    s, c = z.shape
    d = w_out.shape[1]
    assert tm % CONV_HALO == 0 and tm % CONV_CHUNK == 0
    r = tm // CONV_HALO
    span = tm + CONV_HALO - SUBLANES
    full = lambda a: pl.BlockSpec(a.shape, lambda i: (0,) * a.ndim)
    vecs = [dw_b.reshape(1, c), ln_g.reshape(1, c), ln_b.reshape(1, c)]
    return pl.pallas_call(
        _conv_kernel,
        grid=(s // tm,),
        in_specs=[
            pl.BlockSpec((tm, c), lambda i: (i, 0)),
            pl.BlockSpec((CONV_HALO, c), lambda i: (jnp.maximum(i * r - 1, 0), 0)),
            full(dw_w), full(vecs[0]), full(vecs[1]), full(vecs[2]), full(w_out),
            pl.BlockSpec((tm, d), lambda i: (i, 0)),
        ],
        out_specs=pl.BlockSpec((tm, d), lambda i: (i, 0)),
        out_shape=jax.ShapeDtypeStruct((s, d), F32),
        scratch_shapes=[pltpu.VMEM((tm + CONV_HALO, c), F32), pltpu.VMEM((SUBLANES - 1, span, c), F32),
                        pltpu.VMEM((tm, c), F32)],
        compiler_params=pltpu.CompilerParams(dimension_semantics=("arbitrary",), vmem_limit_bytes=VMEM_LIMIT),
        name="conv",
    )(z, z, dw_w, *vecs, w_out, gate_c)


def _sortable_key(x):
    bits = pltpu.bitcast(x + 0.0, I32)
    return bits ^ ((bits >> 31) & 0x7FFFFFFF)


def _key_to_float(key):
    return pltpu.bitcast(key ^ ((key >> 31) & 0x7FFFFFFF), F32)


def _sort_network(n):
    pairs = []

    def merge(lo, hi, r):
        step = r * 2
        if step < hi - lo:
            merge(lo, hi, step)
            merge(lo + r, hi, step)
            pairs.extend((i, i + r) for i in range(lo + r, hi - r, step))
        else:
            pairs.append((lo, lo + r))

    def sort(lo, hi):
        if hi > lo:
            mid = lo + (hi - lo) // 2
            sort(lo, mid)
            sort(mid + 1, hi)
            merge(lo, hi, 1)

    sort(0, n - 1)
    return pairs


def _compare_exchange(a, i, j):
    a[i], a[j] = jnp.maximum(a[i], a[j]), jnp.minimum(a[i], a[j])


def _merge_top(top, batch):
    n = len(top)
    out = [jnp.maximum(top[i], batch[n - 1 - i]) for i in range(n)]
    d = n // 2
    while d >= 1:
        for i in range(n):
            if i & d == 0:
                _compare_exchange(out, i, i + d)
        d //= 2
    return out


def _select_kernel(qi_ref, w_ref, ki_ref, mask_ref, sc_ref, cand_ref, ckey_ref, st_ref, *, k_top):
    qb = pl.program_id(0)
    n_rows = mask_ref.shape[1]
    n_tiles = n_rows // SEL_TK
    q0 = qb * SEL_TQ
    nkt = (q0 + SEL_TQ - 1) // SEL_TK + 1
    vt = SEL_TK // SUBLANES
    vshape = (SUBLANES, LANES)
    qpos = q0 + lax.broadcasted_iota(I32, vshape, 1)
    sub = lax.broadcasted_iota(I32, vshape, 0)
    w = w_ref[...] * ((IDX_HEADS ** -0.5) * (IDX_DIM ** -0.5))

    qpos_t = q0 + lax.broadcasted_iota(I32, (SEL_TK, LANES), 1)
    krow_t = lax.broadcasted_iota(I32, (SEL_TK, LANES), 0)
    zero_bits_t = (MIN_NORMAL_BITS + n_rows) - krow_t

    def score_tile(kt):
        k0 = pl.multiple_of(kt * SEL_TK, SEL_TK)
        ki = ki_ref[pl.ds(k0, SEL_TK), :]
        sc = None
        for pair in range(IDX_HEADS // 2):
            r = _dot(ki, qi_ref[0, pair])
            for j in range(2):
                h = 2 * pair + j
                term = w[h:h + 1, :] * jnp.maximum(r[:, j * LANES:(j + 1) * LANES], 0.0)
                sc = term if sc is None else sc + term
        sc_ref[pl.ds(k0, SEL_TK), :] = jnp.where(sc == 0.0, pltpu.bitcast(zero_bits_t - k0, F32), sc)

    def vreg(ref, row):
        return ref[pl.ds(pl.multiple_of(row, SUBLANES), SUBLANES), :]

    def col_sum(parts):
        tot = parts[0]
        for part in parts[1:]:
            tot = tot + part
        return jnp.broadcast_to(jnp.sum(tot, axis=0, keepdims=True), vshape)

    n_cand = SEL_GROUPS * SEL_DEPTH
    network = _sort_network(SEL_DEPTH)
    cand_ref[...] = jnp.full(cand_ref.shape, -jnp.inf, F32)

    def lists_tile(kt):
        for g in range(SEL_GROUPS):
            batch = [vreg(sc_ref, kt * SEL_TK + (g * SEL_DEPTH + j) * SUBLANES) for j in range(SEL_DEPTH)]
            for i, j in network:
                _compare_exchange(batch, i, j)
            rows = [slice((g * SEL_DEPTH + i) * SUBLANES, (g * SEL_DEPTH + i + 1) * SUBLANES)
                    for i in range(SEL_DEPTH)]
            top = _merge_top([cand_ref[r, :] for r in rows], batch)
            for r, t in zip(rows, top):
                cand_ref[r, :] = t

    sc_ref[pl.ds(n_rows, SEL_TK), :] = jnp.full((SEL_TK, LANES), -jnp.inf, F32)

    def score_and_lists(kt, carry):
        lists_tile(jnp.where(kt == 0, n_tiles, kt - 1))
        score_tile(kt)
        return carry

    lax.fori_loop(0, nkt, score_and_lists, 0)
    k_last = pl.multiple_of((nkt - 1) * SEL_TK, SEL_TK)
    sc_ref[pl.ds(k_last, SEL_TK), :] = jnp.where(
        k_last + krow_t <= qpos_t, sc_ref[pl.ds(k_last, SEL_TK), :], -jnp.inf)
    lists_tile(nkt - 1)
    ckey_ref[...] = _sortable_key(cand_ref[...])

    n_acc = 4

    def cand_bit(i, t):
        c = t ^ lax.shift_left(jnp.int32(1), 31 - i)
        acc = [jnp.zeros(vshape, I32) for _ in range(n_acc)]
        for v in range(n_cand):
            x = ckey_ref[v * SUBLANES:(v + 1) * SUBLANES, :]
            acc[v % n_acc] = acc[v % n_acc] + jnp.where(x >= c, 1, 0)
        return jnp.where(col_sum(acc) >= k_top, c, t)

    st_ref[0] = lax.fori_loop(0, 32, cand_bit, jnp.full(vshape, INT_MIN, I32))

    def count_scores(preds):
        def tile(kt, acc):
            acc = [list(a) for a in acc]
            for v in range(vt):
                x = vreg(sc_ref, kt * SEL_TK + v * SUBLANES)
                kpos = kt * SEL_TK + v * SUBLANES + sub
                for p, pred in enumerate(preds):
                    acc[p][v % n_acc] = acc[p][v % n_acc] + jnp.where(pred(x, kpos), 1, 0)
            return tuple(tuple(a) for a in acc)

        zero = tuple(tuple(jnp.zeros(vshape, I32) for _ in range(n_acc)) for _ in preds)
        return [col_sum(list(a)) for a in lax.fori_loop(0, nkt, tile, zero)]

    def count_around_threshold():
        tf = _key_to_float(st_ref[0])
        gt, ge = count_scores([lambda x, kpos: x > tf, lambda x, kpos: x >= tf])
        st_ref[1] = gt
        st_ref[2] = ge

    tf_fast = jnp.broadcast_to(_key_to_float(st_ref[0])[0:1, :], (SEL_TK, LANES))

    def fast_tile(kt, acc):
        k0 = pl.multiple_of(kt * SEL_TK, SEL_TK)
        m = jnp.where(sc_ref[pl.ds(k0, SEL_TK), :] >= tf_fast, 1.0, 0.0)
        mask_ref[0, pl.ds(k0, SEL_TK), :] = m.astype(BF16)
        acc = list(acc)
        for v in range(vt):
            acc[v % n_acc] = acc[v % n_acc] + m[v * SUBLANES:(v + 1) * SUBLANES, :]
        return tuple(acc)

    size = col_sum(list(lax.fori_loop(0, nkt, fast_tile, tuple(jnp.zeros(vshape, F32) for _ in range(n_acc)))))

    @pl.when(jnp.max(jnp.abs(size - k_top)) > 0.0)
    def _():
        count_around_threshold()

        @pl.when(jnp.max(st_ref[1]) >= k_top)
        def _():
            def full_bit(i, t):
                c = t ^ lax.shift_left(jnp.int32(1), 31 - i)
                cnt, = count_scores([lambda x, kpos: _sortable_key(x) >= c])
                return jnp.where(cnt >= k_top, c, t)

            st_ref[0] = lax.fori_loop(0, 32, full_bit, jnp.full(vshape, INT_MIN, I32))
            count_around_threshold()

        st_ref[3] = jnp.full(vshape, n_rows, I32)

        @pl.when(jnp.max(st_ref[2]) > k_top)
        def _():
            tf = _key_to_float(st_ref[0])
            need = k_top - st_ref[1]
            n_bits = n_rows.bit_length() - 1

            def pos_bit(i, p):
                c = p + lax.shift_left(jnp.int32(1), n_bits - 1 - i)
                f, = count_scores([lambda x, kpos: (x == tf) & (kpos < c)])
                return jnp.where(f < need, c, p)

            st_ref[3] = lax.fori_loop(0, n_bits, pos_bit, jnp.zeros(vshape, I32))

        tf_t = jnp.broadcast_to(_key_to_float(st_ref[0])[0:1, :], (SEL_TK, LANES))
        lim_t = jnp.broadcast_to(st_ref[3][0:1, :], (SEL_TK, LANES))

        def write_tile(kt, carry):
            k0 = pl.multiple_of(kt * SEL_TK, SEL_TK)
            x = sc_ref[pl.ds(k0, SEL_TK), :]
            kpos = k0 + krow_t
            sel = ((x > tf_t) | ((x == tf_t) & (kpos <= lim_t))) & (kpos <= qpos_t)
            mask_ref[0, pl.ds(k0, SEL_TK), :] = jnp.where(sel, 1.0, 0.0).astype(BF16)
            return carry

        lax.fori_loop(0, nkt, write_tile, 0)

    def fill_tile(kt, carry):
        mask_ref[0, pl.ds(pl.multiple_of(kt * SEL_TK, SEL_TK), SEL_TK), :] = jnp.zeros((SEL_TK, LANES), BF16)
        return carry

    lax.fori_loop(nkt, n_tiles, fill_tile, 0)


def _select(qi_pairs, w_t, ki_ext, k_top):
    nqb, npair, kw, _ = qi_pairs.shape
    s = ki_ext.shape[0]
    n_cand_rows = SEL_GROUPS * SEL_DEPTH * SUBLANES
    assert s % SEL_TK == 0 and SEL_TK == n_cand_rows and n_cand_rows >= k_top
    return pl.pallas_call(
        functools.partial(_select_kernel, k_top=k_top),
        grid=(nqb,),
        in_specs=[
            pl.BlockSpec((1, npair, kw, 2 * SEL_TQ), lambda i: (i, 0, 0, 0)),
            pl.BlockSpec((w_t.shape[0], SEL_TQ), lambda i: (0, i)),
            pl.BlockSpec((s, kw), lambda i: (0, 0)),
        ],
        out_specs=pl.BlockSpec((1, s, SEL_TQ), lambda i: (i, 0, 0)),
        out_shape=jax.ShapeDtypeStruct((nqb, s, SEL_TQ), BF16),
        scratch_shapes=[
            pltpu.VMEM((s + SEL_TK, SEL_TQ), F32),
            pltpu.VMEM((n_cand_rows, SEL_TQ), F32),
            pltpu.VMEM((n_cand_rows, SEL_TQ), I32),
            pltpu.VMEM((4, SUBLANES, LANES), I32),
        ],
        compiler_params=pltpu.CompilerParams(dimension_semantics=("arbitrary",), vmem_limit_bytes=VMEM_LIMIT),
        name="select",
    )(qi_pairs, w_t, ki_ext)


def _causal_pairs(s, tq, tk):
    pairs = [(qb, kt) for qb in range(s // tq) for kt in range((qb * tq + tq - 1) // tk + 1)]
    return jnp.asarray([p[0] for p in pairs], I32), jnp.asarray([p[1] for p in pairs], I32)


def _attn_kernel(qb_tab, kt_tab, qt_ref, kmax_ref, k_ref, vt_ref, mask_ref, bias_ref, o_ref, l_ref, qx_ref, acc_ref):
    step = pl.program_id(0)
    qb = qb_tab[step]
    kt = kt_tab[step]
    q0 = qb * ATT_TQ
    k0 = kt * ATT_TK
    nq = ATT_TQ // LANES
    nk = ATT_TK // LANES

    @pl.when(kt == 0)
    def _():
        row = lax.broadcasted_iota(I32, (HEAD_DIM, ATT_TQ), 0)
        for h in range(N_HEADS):
            q = qt_ref[h]
            qf = q.astype(F32)
            bound = jnp.sqrt(jnp.sum(qf * qf, axis=0, keepdims=True)) * kmax_ref[h] * BOUND_MARGIN
            qx_ref[h, 0:HEAD_DIM, :] = q
            qx_ref[h, HEAD_DIM:, :] = jnp.where(row == 0, -bound, 0.0).astype(BF16)
        acc_ref[...] = jnp.zeros(acc_ref.shape, F32)

    mask = jnp.concatenate([mask_ref[a] for a in range(nq)], axis=1)

    def heads(with_bias):
        for h in range(N_HEADS):
            s = _dot(k_ref[h], qx_ref[h])
            if with_bias:
                rows = []
                for c in range(nk):
                    tab = [jnp.clip((q0 + a * LANES - k0 - c * LANES) // LANES + 1, 0, 3) for a in range(nq)]
                    rows.append(jnp.concatenate([bias_ref[tab[a], h] for a in range(nq)], axis=1))
                s = s + jnp.concatenate(rows, axis=0)
            p = jnp.exp2(s).astype(BF16) * mask
            acc_ref[h] += _dot(vt_ref[h], p)

    near = k0 + ATT_TK + 2 * LANES > q0

    @pl.when(near)
    def _():
        heads(True)

    @pl.when(jnp.logical_not(near))
    def _():
        heads(False)

    @pl.when(kt == (q0 + ATT_TQ - 1) // ATT_TK)
    def _():
        for h in range(N_HEADS):
            acc = acc_ref[h]
            den = acc[HEAD_DIM:HEAD_DIM + 1, :]
            o_ref[h] = acc[:HEAD_DIM, :] / den
            l_ref[h] = den


def _attn(q_t, kmax_b, k_ext, v_t, mask, bias_tab):
    nh, hd, s = q_t.shape
    qb_tab, kt_tab = _causal_pairs(s, ATT_TQ, ATT_TK)
    grid_spec = pltpu.PrefetchScalarGridSpec(
        num_scalar_prefetch=2,
        grid=(qb_tab.shape[0],),
        in_specs=[
            pl.BlockSpec((nh, hd, ATT_TQ), lambda i, qb, kt: (0, 0, qb[i])),
            pl.BlockSpec((nh, 1, ATT_TQ), lambda i, qb, kt: (0, 0, 0)),
            pl.BlockSpec((nh, ATT_TK, LANES), lambda i, qb, kt: (0, kt[i], 0)),
            pl.BlockSpec((nh, LANES, ATT_TK), lambda i, qb, kt: (0, 0, kt[i])),
            pl.BlockSpec((ATT_TQ // LANES, ATT_TK, LANES), lambda i, qb, kt: (qb[i], kt[i], 0)),
            pl.BlockSpec(bias_tab.shape, lambda i, qb, kt: (0, 0, 0, 0)),
        ],
        out_specs=[
            pl.BlockSpec((nh, hd, ATT_TQ), lambda i, qb, kt: (0, 0, qb[i])),
            pl.BlockSpec((nh, 1, ATT_TQ), lambda i, qb, kt: (0, 0, qb[i])),
        ],
        scratch_shapes=[pltpu.VMEM((nh, LANES, ATT_TQ), BF16), pltpu.VMEM((nh, LANES, ATT_TQ), F32)],
    )
    return pl.pallas_call(
        _attn_kernel,
        grid_spec=grid_spec,
        out_shape=[jax.ShapeDtypeStruct((nh, hd, s), F32), jax.ShapeDtypeStruct((nh, 1, s), F32)],
        compiler_params=pltpu.CompilerParams(dimension_semantics=("arbitrary",), vmem_limit_bytes=VMEM_LIMIT),
        name="attn",
    )(qb_tab, kt_tab, q_t, kmax_b, k_ext, v_t, mask, bias_tab)


def _attn_safe_kernel(qb_tab, kt_tab, q_ref, k_ref, v_ref, mask_ref, bias_ref, o_ref, m_ref, acc_ref):
    step = pl.program_id(0)
    qb = qb_tab[step]
    kt = kt_tab[step]
    nq = SAFE_TQ // LANES
    nsub = ATT_TK // LANES

    @pl.when(kt == 0)
    def _():
        m_ref[...] = jnp.full(m_ref.shape, NEG, F32)
        acc_ref[...] = jnp.zeros(acc_ref.shape, F32)

    for a in range(nq):
        rows = slice(a * LANES, (a + 1) * LANES)
        q0 = qb * SAFE_TQ + a * LANES
        maskf = (mask_ref[a].astype(F32) - 1.0) * (-NEG)
        tab_idx = [jnp.clip((q0 - (kt * ATT_TK + c * LANES)) // LANES + 1, 0, 3) for c in range(nsub)]
        for h in range(N_HEADS):
            s = _dot_t(q_ref[h, rows, :], k_ref[h]) + maskf
            s = s + jnp.concatenate([bias_ref[tab_idx[c], h] for c in range(nsub)], axis=1)
            m_old = m_ref[h, rows, :]
            m_new = jnp.maximum(m_old, jnp.max(s, axis=1, keepdims=True))
            p = jnp.exp2(s - m_new[:, 0:1])
            alpha = jnp.exp2(m_old - m_new)
            acc_ref[h, rows, :] = alpha * acc_ref[h, rows, :] + _dot(p.astype(BF16), v_ref[h])
            m_ref[h, rows, :] = m_new

    @pl.when(kt == (qb * SAFE_TQ + SAFE_TQ - 1) // ATT_TK)
    def _():
        for h in range(N_HEADS):
            acc = acc_ref[h]
            o_ref[h] = acc[:, :HEAD_DIM] / acc[:, HEAD_DIM:HEAD_DIM + 1]


def _attn_safe(q, k, v_ext, mask_qk, bias_tab):
    nh, s, hd = q.shape
    qb_tab, kt_tab = _causal_pairs(s, SAFE_TQ, ATT_TK)
    grid_spec = pltpu.PrefetchScalarGridSpec(
        num_scalar_prefetch=2,
        grid=(qb_tab.shape[0],),
        in_specs=[
            pl.BlockSpec((nh, SAFE_TQ, hd), lambda i, qb, kt: (0, qb[i], 0)),
            pl.BlockSpec((nh, ATT_TK, hd), lambda i, qb, kt: (0, kt[i], 0)),
            pl.BlockSpec((nh, ATT_TK, LANES), lambda i, qb, kt: (0, kt[i], 0)),
            pl.BlockSpec((SAFE_TQ // LANES, LANES, ATT_TK), lambda i, qb, kt: (qb[i], 0, kt[i])),
            pl.BlockSpec(bias_tab.shape, lambda i, qb, kt: (0, 0, 0, 0)),
        ],
        out_specs=pl.BlockSpec((nh, SAFE_TQ, hd), lambda i, qb, kt: (0, qb[i], 0)),
        scratch_shapes=[pltpu.VMEM((nh, SAFE_TQ, LANES), F32), pltpu.VMEM((nh, SAFE_TQ, LANES), F32)],
    )
    return pl.pallas_call(
        _attn_safe_kernel,
        grid_spec=grid_spec,
        out_shape=jax.ShapeDtypeStruct((nh, s, hd), F32),
        compiler_params=pltpu.CompilerParams(dimension_semantics=("arbitrary",), vmem_limit_bytes=VMEM_LIMIT),
        name="attn_safe",
    )(qb_tab, kt_tab, q, k, v_ext, mask_qk, bias_tab)


def _tail_kernel(h_ref, cg_ref, ga_ref, at_ref, p_ref, wao_ref, wmo_ref, fg_ref, wi_ref, wo_ref, pg_ref, wpg_ref,
                 wpp_ref, o_ref):
    attn = at_ref[...].T.astype(BF16)
    merged = cg_ref[...] + ga_ref[...] * _dot(attn, wao_ref[...])
    h = h_ref[...] + _dot(merged.astype(BF16), wmo_ref[...])
    h = _half_step_ffn(h, fg_ref[...], wi_ref, wo_ref)
    gate = _sigmoid(_dot(_rms(h, pg_ref[...]).astype(BF16), wpg_ref[...]))
    o_ref[...] = h + gate * _dot(p_ref[...].astype(BF16), wpp_ref[...])


def _tail(h, conv_g, gate_a, attn_t, p, w_ao, w_mo, ffn_g, w_in, w_out, ple_g, w_pg, w_pp, tm=512):
    s, d = h.shape
    row = lambda n: pl.BlockSpec((tm, n), lambda i: (i, 0))
    fg2 = ffn_g.reshape(1, d)
    pg2 = ple_g.reshape(1, d)
    params = (w_ao, w_mo, fg2, w_in, w_out, pg2, w_pg, w_pp)
    return pl.pallas_call(
        _tail_kernel,
        grid=(s // tm,),
        in_specs=[row(d), row(d), row(d), pl.BlockSpec((attn_t.shape[0], tm), lambda i: (0, i)), row(p.shape[1])]
        + [_resident(a) for a in params],
        out_specs=row(d),
        out_shape=jax.ShapeDtypeStruct((s, d), F32),
        compiler_params=pltpu.CompilerParams(dimension_semantics=("arbitrary",), vmem_limit_bytes=VMEM_LIMIT),
        name="tail",
    )(h, conv_g, gate_a, attn_t, p, *params)


def _t5_bucket_table(n_dist):
    n = np.arange(n_dist)
    max_exact = NUM_BUCKETS // 2
    nf = np.maximum(n, 1).astype(np.float32)
    large = max_exact + (np.log(nf / max_exact) / math.log(MAX_DISTANCE / max_exact)
                         * (NUM_BUCKETS - max_exact)).astype(np.int32)
    large = np.minimum(large, NUM_BUCKETS - 1)
    return np.where(n < max_exact, n, large)


def _bias_tables(rel_bias):
    assert MAX_DISTANCE <= LANES
    n = LANES
    rel = (rel_bias.astype(F32) - rel_bias[NUM_BUCKETS - 1].astype(F32)[None, :]) * LOG2E
    by_dist = rel[_t5_bucket_table(2 * n)].T

    def toeplitz(a):
        skew = jnp.tile(a, (1, n))[:, :n * (2 * n - 1)].reshape(a.shape[0], n, 2 * n - 1)
        return skew[:, :, :n]

    d0 = toeplitz(jnp.concatenate([by_dist[:, :n], jnp.broadcast_to(by_dist[:, :1], by_dist[:, :n].shape)], axis=1))
    d1 = toeplitz(jnp.concatenate([by_dist[:, n:], by_dist[:, :n]], axis=1))
    zero = jnp.zeros_like(d0)
    return jnp.swapaxes(jnp.stack([zero, d0, d1, zero]), 2, 3)


def kernel(x, p, ffn1_norm, ffn1_w_in, ffn1_w_out, mix_norm, mix_w_in, conv_dw_w, conv_dw_b, conv_ln_g,
           conv_ln_b, conv_w_out, q_norm, k_norm, attn_w_out, mix_w_out, ffn2_norm, ffn2_w_in, ffn2_w_out,
           ple_norm, ple_w_gate, ple_w_proj, rel_bias):
    b, s, d = x.shape
    depth = ffn1_norm.shape[0]
    cw = conv_dw_w.shape[2]
    aw = N_HEADS * HEAD_DIM
    nqi = IDX_HEADS * IDX_DIM
    iw = nqi + IDX_DIM + IDX_HEADS
    iw_pad = -(-iw // LANES) * LANES
    k_top = min(TOPK_MAX, s // 4)
    assert b == 1 and s % ATT_TK == 0 and mix_w_in.shape[2] == 2 * cw + 3 * aw + iw + 2 * d

    head_blocks = jnp.asarray(np.kron(np.eye(N_HEADS), np.ones((HEAD_DIM, HEAD_DIM))), BF16)
    bias_qk = _bias_tables(rel_bias)
    bias_kq = jnp.swapaxes(bias_qk, 2, 3)

    h = x[0]
    for i in range(depth):
        h = _ffn(h, ffn1_norm[i], ffn1_w_in[i].astype(BF16), ffn1_w_out[i].astype(BF16))

        w = mix_w_in[i]
        o0 = 2 * cw
        o1 = o0 + 3 * aw
        o2 = o1 + iw
        w_idx = jnp.pad(w[:, o1:o2], ((0, 0), (0, iw_pad - iw)))
        w_idx_hi, w_idx_lo = _split_bf16(w_idx)
        qg = jnp.tile(q_norm[i], N_HEADS).reshape(1, aw)
        kg = jnp.tile(k_norm[i], N_HEADS).reshape(1, aw)
        glu, q_t, k_ext, v_t, qi_pairs, ki_ext, idx_w, gate_c, gate_a = _mix_in(
            h, mix_norm[i], w[:, :o0].astype(BF16), w[:, o0:o1].astype(BF16), w_idx_hi, w_idx_lo,
            w[:, o2:].astype(BF16), qg, kg, head_blocks)

        conv_g = _conv(glu, conv_dw_w[i], conv_dw_b[i], conv_ln_g[i], conv_ln_b[i],
                       conv_w_out[i].astype(BF16), gate_c)

        w_t = idx_w[:, IDX_DIM:IDX_DIM + IDX_HEADS].T
        mask = _select(qi_pairs, w_t, ki_ext, k_top)

        kmax = math.sqrt(HEAD_DIM) * jnp.max(jnp.abs(k_norm[i]))
        kmax_b = jnp.full((N_HEADS, 1, ATT_TQ), kmax, F32)
        attn_t, den = _attn(q_t, kmax_b, k_ext, v_t, mask, bias_kq)

        def safe_attn():
            out = _attn_safe(jnp.swapaxes(q_t, 1, 2), k_ext[:, :, :HEAD_DIM], jnp.swapaxes(v_t, 1, 2),
                             jnp.swapaxes(mask, 1, 2), bias_qk)
            return jnp.swapaxes(out, 1, 2)

        underflow = jnp.logical_not(jnp.min(den) > L_MIN)
        attn_t = lax.cond(underflow, safe_attn, lambda: attn_t).reshape(aw, s)

        h = _tail(h, conv_g, gate_a, attn_t, p[i, 0], attn_w_out[i].astype(BF16), mix_w_out[i].astype(BF16),
                  ffn2_norm[i], ffn2_w_in[i].astype(BF16), ffn2_w_out[i].astype(BF16),
                  ple_norm[i], ple_w_gate[i].astype(BF16), ple_w_proj[i].astype(BF16))
    return h[None]
```

```python
import functools
import math

import numpy as np
import jax
import jax.numpy as jnp
from jax import lax
from jax.experimental import pallas as pl
from jax.experimental.pallas import tpu as pltpu

F32 = jnp.float32
BF16 = jnp.bfloat16
I32 = jnp.int32

EPS = 1e-6
CONV_WIDTH = 31
N_HEADS = 8
HEAD_DIM = 64
IDX_HEADS = 4
IDX_DIM = 64
TOPK_MAX = 256
NUM_BUCKETS = 32
MAX_DISTANCE = 128

LANES = 128
SUBLANES = 8
MXU_TILE = 256
FFN_CHUNK = 1024
VMEM_LIMIT = 56 * 1024 * 1024
NEG = -1e30
INT_MIN = -2 ** 31
MIN_NORMAL_BITS = 0x00800000
LOG2E = math.log2(math.e)

CONV_HALO = 32
CONV_CHUNK = 32
SEL_TQ = LANES
SEL_TK = 1024
SEL_GROUPS = 8
SEL_DEPTH = 16
ATT_TQ = 1024
ATT_TK = 1024
ATT_KC = 512
SAFE_TQ = 256
L_MIN = 1e-30
BOUND_MARGIN = 1.02


def _sigmoid(x):
    return 1.0 / (1.0 + jnp.exp(-x))


def _rms(x, g):
    ms = jnp.mean(x * x, axis=-1, keepdims=True)
    return x * lax.rsqrt(ms + EPS) * g


def _dot(a, b):
    return jnp.dot(a, b, preferred_element_type=F32)


def _dot_t(a, b):
    return lax.dot_general(a, b, (((1,), (1,)), ((), ())), preferred_element_type=F32)


def _split_bf16(x):
    hi = x.astype(BF16)
    lo = (x - hi.astype(F32)).astype(BF16)
    return hi, lo


def _half_step_ffn(x, g, wi_ref, wo_ref):
    xn = _rms(x, g).astype(BF16)
    dff = wo_ref.shape[0]
    acc = None
    for c0 in range(0, dff, FFN_CHUNK):
        c1 = min(c0 + FFN_CHUNK, dff)
        a = _dot(xn, wi_ref[:, c0:c1])
        b = _dot(xn, wi_ref[:, dff + c0:dff + c1])
        part = _dot((a * _sigmoid(a) * b).astype(BF16), wo_ref[c0:c1, :])
        acc = part if acc is None else acc + part
    return x + 0.5 * acc


def _ffn_kernel(x_ref, g_ref, wi_ref, wo_ref, o_ref):
    o_ref[...] = _half_step_ffn(x_ref[...], g_ref[...], wi_ref, wo_ref)


def _resident(a):
    return pl.BlockSpec(a.shape, lambda *_: (0,) * a.ndim, pipeline_mode=pl.Buffered(1))


def _ffn(x, g, w_in, w_out, tm=512):
    s, d = x.shape
    dff = w_out.shape[0]
    assert s % tm == 0 and dff % MXU_TILE == 0 and FFN_CHUNK % MXU_TILE == 0
    g2 = g.reshape(1, d)
    return pl.pallas_call(
        _ffn_kernel,
        grid=(s // tm,),
        in_specs=[pl.BlockSpec((tm, d), lambda i: (i, 0)), _resident(g2), _resident(w_in), _resident(w_out)],
        out_specs=pl.BlockSpec((tm, d), lambda i: (i, 0)),
        out_shape=jax.ShapeDtypeStruct((s, d), F32),
        compiler_params=pltpu.CompilerParams(dimension_semantics=("arbitrary",), vmem_limit_bytes=VMEM_LIMIT),
        name="ffn",
    )(x, g2, w_in, w_out)


def _mix_in_kernel(h_ref, g_ref, wc_ref, wqkv_ref, wih_ref, wil_ref, wg_ref, qg_ref, kg_ref, hb_ref,
                   glu_ref, qt_ref, kx_ref, vt_ref, qip_ref, kix_ref, wi_ref, gc_ref, ga_ref):
    u = _rms(h_ref[...], g_ref[...])
    u_hi, u_lo = _split_bf16(u)
    tm = h_ref.shape[0]
    cw = glu_ref.shape[1]
    aw = N_HEADS * HEAD_DIM
    nqi = IDX_HEADS * IDX_DIM
    d = gc_ref.shape[1]

    c = _dot(u_hi, wc_ref[...])
    glu_ref[...] = c[:, :cw] * _sigmoid(c[:, cw:])

    qkv = _dot(u_hi, wqkv_ref[...])
    hb = hb_ref[...]

    def head_sumsq(t):
        t2_hi, t2_lo = _split_bf16(t * t)
        return _dot(t2_hi, hb) + _dot(t2_lo, hb)

    def head_norm(t, g):
        return t * lax.rsqrt(head_sumsq(t) * (1.0 / HEAD_DIM) + EPS) * g

    q = head_norm(qkv[:, :aw], qg_ref[...]) * (HEAD_DIM ** -0.5 * LOG2E)
    q_t = q.T
    for hd in range(N_HEADS):
        qt_ref[hd] = q_t[hd * HEAD_DIM:(hd + 1) * HEAD_DIM, :].astype(BF16)
    k = head_norm(qkv[:, aw:2 * aw], kg_ref[...]).astype(BF16).astype(F32)
    one_hot0 = lambda shape, axis: jnp.where(lax.broadcasted_iota(I32, shape, axis) == 0, 1.0, 0.0)
    k_pad = one_hot0((tm, LANES - HEAD_DIM), 1)
    for hd in range(N_HEADS):
        kx_ref[hd] = jnp.concatenate([k[:, hd * HEAD_DIM:(hd + 1) * HEAD_DIM], k_pad], axis=1).astype(BF16)
    v_t = qkv[:, 2 * aw:].T
    v_pad = one_hot0((LANES - HEAD_DIM, tm), 0).astype(BF16)
    for hd in range(N_HEADS):
        vt_ref[hd, 0:HEAD_DIM, :] = v_t[hd * HEAD_DIM:(hd + 1) * HEAD_DIM, :].astype(BF16)
        vt_ref[hd, HEAD_DIM:, :] = v_pad

    wih = wih_ref[...]
    idx = _dot(u_hi, wih) + _dot(u_lo, wih) + _dot(u_hi, wil_ref[...])
    qi_hi, qi_lo = _split_bf16(idx[:, :nqi].T)
    for blk in range(tm // SEL_TQ):
        cols = slice(blk * SEL_TQ, (blk + 1) * SEL_TQ)
        for hd in range(IDX_HEADS):
            rows = slice(hd * IDX_DIM, (hd + 1) * IDX_DIM)
            out_cols = slice((hd % 2) * SEL_TQ, (hd % 2 + 1) * SEL_TQ)
            for part, src in enumerate((qi_hi, qi_hi, qi_lo, qi_lo)):
                qip_ref[blk, hd // 2, part * IDX_DIM:(part + 1) * IDX_DIM, out_cols] = src[rows, cols]
    ki = idx[:, nqi:nqi + IDX_DIM]
    ki_hi = ki.astype(BF16).astype(F32)
    ki_lo = ki - ki_hi
    kix_ref[...] = jnp.concatenate([ki_hi, ki_lo, ki_hi, ki_lo], axis=1).astype(BF16)
    wi_ref[...] = idx[:, nqi:]

    gates = _sigmoid(_dot(u_hi, wg_ref[...]))
    gc_ref[...] = gates[:, :d]
    ga_ref[...] = gates[:, d:]


def _mix_in(h, g, w_conv, w_qkv, w_idx_hi, w_idx_lo, w_gate, qg, kg, head_blocks, tm=512):
    s, d = h.shape
    cw = w_conv.shape[1] // 2
    aw = w_qkv.shape[1] // 3
    nqi = IDX_HEADS * IDX_DIM
    iw = w_idx_hi.shape[1]
    assert aw == N_HEADS * HEAD_DIM and tm % SEL_TQ == 0 and iw - nqi == LANES
    full = lambda a: pl.BlockSpec(a.shape, lambda i: (0,) * a.ndim)
    row = lambda n: pl.BlockSpec((tm, n), lambda i: (i, 0))
    g2 = g.reshape(1, d)
    return pl.pallas_call(
        _mix_in_kernel,
        grid=(s // tm,),
        in_specs=[row(d), full(g2), full(w_conv), full(w_qkv), full(w_idx_hi), full(w_idx_lo), full(w_gate),
                  full(qg), full(kg), full(head_blocks)],
        out_specs=[
            row(cw),
            pl.BlockSpec((N_HEADS, HEAD_DIM, tm), lambda i: (0, 0, i)),
            pl.BlockSpec((N_HEADS, tm, LANES), lambda i: (0, i, 0)),
            pl.BlockSpec((N_HEADS, LANES, tm), lambda i: (0, 0, i)),
            pl.BlockSpec((tm // SEL_TQ, IDX_HEADS // 2, 4 * IDX_DIM, 2 * SEL_TQ), lambda i: (i, 0, 0, 0)),
            row(4 * IDX_DIM), row(LANES), row(d), row(d),
        ],
        out_shape=[
            jax.ShapeDtypeStruct((s, cw), F32),
            jax.ShapeDtypeStruct((N_HEADS, HEAD_DIM, s), BF16),
            jax.ShapeDtypeStruct((N_HEADS, s, LANES), BF16),
            jax.ShapeDtypeStruct((N_HEADS, LANES, s), BF16),
            jax.ShapeDtypeStruct((s // SEL_TQ, IDX_HEADS // 2, 4 * IDX_DIM, 2 * SEL_TQ), BF16),
            jax.ShapeDtypeStruct((s, 4 * IDX_DIM), BF16),
            jax.ShapeDtypeStruct((s, LANES), F32),
            jax.ShapeDtypeStruct((s, d), F32),
            jax.ShapeDtypeStruct((s, d), F32),
        ],
        compiler_params=pltpu.CompilerParams(dimension_semantics=("arbitrary",), vmem_limit_bytes=VMEM_LIMIT),
        name="mix_in",
    )(h, g2, w_conv, w_qkv, w_idx_hi, w_idx_lo, w_gate, qg, kg, head_blocks)


def _conv_kernel(z_ref, halo_ref, dw_ref, db_ref, lg_ref, lb_ref, wo_ref, gc_ref, o_ref, zp_ref, zs_ref, acc_ref):
    tm = z_ref.shape[0]
    first = pl.program_id(0) == 0
    halo = halo_ref[...]
    zp_ref[0:CONV_HALO, :] = jnp.where(first, jnp.zeros_like(halo), halo)
    zp_ref[CONV_HALO:, :] = z_ref[...]
    off = CONV_HALO - (CONV_WIDTH - 1)
    span = zs_ref.shape[1]
    for b in range(1, SUBLANES):
        zs_ref[b - 1] = zp_ref[b:b + span, :]

    def chunk(c, carry):
        r0 = pl.multiple_of(c * CONV_CHUNK, CONV_CHUNK)
        acc = jnp.zeros((CONV_CHUNK, z_ref.shape[1]), F32) + db_ref[...]
        for j in range(CONV_WIDTH):
            a, b = divmod(off + j, SUBLANES)
            src = zp_ref if b == 0 else zs_ref.at[b - 1]
            acc = acc + dw_ref[j:j + 1, :] * src[pl.ds(r0 + a * SUBLANES, CONV_CHUNK), :]
        acc_ref[pl.ds(r0, CONV_CHUNK), :] = acc
        return carry

    lax.fori_loop(0, tm // CONV_CHUNK, chunk, 0)
    acc = acc_ref[...]
    mu = jnp.mean(acc, axis=-1, keepdims=True)
    xc = acc - mu
    y = xc * lax.rsqrt(jnp.mean(xc * xc, axis=-1, keepdims=True) + EPS)
    y = y * lg_ref[...] + lb_ref[...]
    y = (y * _sigmoid(y)).astype(BF16)
    o_ref[...] = gc_ref[...] * _dot(y, wo_ref[...])


def _conv(z, dw_w, dw_b, ln_g, ln_b, w_out, gate_c, tm=512):
    s, c = z.shape
    d = w_out.shape[1]
    assert tm % CONV_HALO == 0 and tm % CONV_CHUNK == 0
    r = tm // CONV_HALO
    span = tm + CONV_HALO - SUBLANES
    full = lambda a: pl.BlockSpec(a.shape, lambda i: (0,) * a.ndim)
    vecs = [dw_b.reshape(1, c), ln_g.reshape(1, c), ln_b.reshape(1, c)]
    return pl.pallas_call(
        _conv_kernel,
        grid=(s // tm,),
        in_specs=[
            pl.BlockSpec((tm, c), lambda i: (i, 0)),
            pl.BlockSpec((CONV_HALO, c), lambda i: (jnp.maximum(i * r - 1, 0), 0)),
            full(dw_w), full(vecs[0]), full(vecs[1]), full(vecs[2]), full(w_out),
            pl.BlockSpec((tm, d), lambda i: (i, 0)),
        ],
        out_specs=pl.BlockSpec((tm, d), lambda i: (i, 0)),
        out_shape=jax.ShapeDtypeStruct((s, d), F32),
        scratch_shapes=[pltpu.VMEM((tm + CONV_HALO, c), F32), pltpu.VMEM((SUBLANES - 1, span, c), F32),
                        pltpu.VMEM((tm, c), F32)],
        compiler_params=pltpu.CompilerParams(dimension_semantics=("arbitrary",), vmem_limit_bytes=VMEM_LIMIT),
        name="conv",
    )(z, z, dw_w, *vecs, w_out, gate_c)


def _sortable_key(x):
    bits = pltpu.bitcast(x + 0.0, I32)
    return bits ^ ((bits >> 31) & 0x7FFFFFFF)


def _key_to_float(key):
    return pltpu.bitcast(key ^ ((key >> 31) & 0x7FFFFFFF), F32)


def _sort_network(n):
    pairs = []

    def merge(lo, hi, r):
        step = r * 2
        if step < hi - lo:
            merge(lo, hi, step)
            merge(lo + r, hi, step)
            pairs.extend((i, i + r) for i in range(lo + r, hi - r, step))
        else:
            pairs.append((lo, lo + r))

    def sort(lo, hi):
        if hi > lo:
            mid = lo + (hi - lo) // 2
            sort(lo, mid)
            sort(mid + 1, hi)
            merge(lo, hi, 1)

    sort(0, n - 1)
    return pairs


def _compare_exchange(a, i, j):
    a[i], a[j] = jnp.maximum(a[i], a[j]), jnp.minimum(a[i], a[j])


def _merge_top(top, batch):
    n = len(top)
    out = [jnp.maximum(top[i], batch[n - 1 - i]) for i in range(n)]
    d = n // 2
    while d >= 1:
        for i in range(n):
            if i & d == 0:
                _compare_exchange(out, i, i + d)
        d //= 2
    return out


def _select_kernel(qi_ref, w_ref, ki_ref, mask_ref, sc_ref, cand_ref, ckey_ref, st_ref, *, k_top):
    qb = pl.program_id(0)
    n_rows = mask_ref.shape[1]
    n_tiles = n_rows // SEL_TK
    q0 = qb * SEL_TQ
    nkt = (q0 + SEL_TQ - 1) // SEL_TK + 1
    vt = SEL_TK // SUBLANES
    vshape = (SUBLANES, LANES)
    qpos = q0 + lax.broadcasted_iota(I32, vshape, 1)
    sub = lax.broadcasted_iota(I32, vshape, 0)
    w = w_ref[...] * ((IDX_HEADS ** -0.5) * (IDX_DIM ** -0.5))

    qpos_t = q0 + lax.broadcasted_iota(I32, (SEL_TK, LANES), 1)
    krow_t = lax.broadcasted_iota(I32, (SEL_TK, LANES), 0)
    zero_bits_t = (MIN_NORMAL_BITS + n_rows) - krow_t

    def score_tile(kt):
        k0 = pl.multiple_of(kt * SEL_TK, SEL_TK)
        ki = ki_ref[pl.ds(k0, SEL_TK), :]
        sc = None
        for pair in range(IDX_HEADS // 2):
            r = _dot(ki, qi_ref[0, pair])
            for j in range(2):
                h = 2 * pair + j
                term = w[h:h + 1, :] * jnp.maximum(r[:, j * LANES:(j + 1) * LANES], 0.0)
                sc = term if sc is None else sc + term
        sc_ref[pl.ds(k0, SEL_TK), :] = jnp.where(sc == 0.0, pltpu.bitcast(zero_bits_t - k0, F32), sc)

    def vreg(ref, row):
        return ref[pl.ds(pl.multiple_of(row, SUBLANES), SUBLANES), :]

    def col_sum(parts):
        tot = parts[0]
        for part in parts[1:]:
            tot = tot + part
        return jnp.broadcast_to(jnp.sum(tot, axis=0, keepdims=True), vshape)

    n_cand = SEL_GROUPS * SEL_DEPTH
    network = _sort_network(SEL_DEPTH)
    cand_ref[...] = jnp.full(cand_ref.shape, -jnp.inf, F32)

    def lists_tile(kt):
        for g in range(SEL_GROUPS):
            batch = [vreg(sc_ref, kt * SEL_TK + (g * SEL_DEPTH + j) * SUBLANES) for j in range(SEL_DEPTH)]
            for i, j in network:
                _compare_exchange(batch, i, j)
            rows = [slice((g * SEL_DEPTH + i) * SUBLANES, (g * SEL_DEPTH + i + 1) * SUBLANES)
                    for i in range(SEL_DEPTH)]
            top = _merge_top([cand_ref[r, :] for r in rows], batch)
            for r, t in zip(rows, top):
                cand_ref[r, :] = t

    sc_ref[pl.ds(n_rows, SEL_TK), :] = jnp.full((SEL_TK, LANES), -jnp.inf, F32)

    def score_and_lists(kt, carry):
        lists_tile(jnp.where(kt == 0, n_tiles, kt - 1))
        score_tile(kt)
        return carry

    lax.fori_loop(0, nkt, score_and_lists, 0)
    k_last = pl.multiple_of((nkt - 1) * SEL_TK, SEL_TK)
    sc_ref[pl.ds(k_last, SEL_TK), :] = jnp.where(
        k_last + krow_t <= qpos_t, sc_ref[pl.ds(k_last, SEL_TK), :], -jnp.inf)
    lists_tile(nkt - 1)
    ckey_ref[...] = _sortable_key(cand_ref[...])

    n_acc = 4

    def cand_bit(i, t):
        c = t ^ lax.shift_left(jnp.int32(1), 31 - i)
        acc = [jnp.zeros(vshape, I32) for _ in range(n_acc)]
        for v in range(n_cand):
            x = ckey_ref[v * SUBLANES:(v + 1) * SUBLANES, :]
            acc[v % n_acc] = acc[v % n_acc] + jnp.where(x >= c, 1, 0)
        return jnp.where(col_sum(acc) >= k_top, c, t)

    st_ref[0] = lax.fori_loop(0, 32, cand_bit, jnp.full(vshape, INT_MIN, I32))

    def count_scores(preds):
        def tile(kt, acc):
            acc = [list(a) for a in acc]
            for v in range(vt):
                x = vreg(sc_ref, kt * SEL_TK + v * SUBLANES)
                kpos = kt * SEL_TK + v * SUBLANES + sub
                for p, pred in enumerate(preds):
                    acc[p][v % n_acc] = acc[p][v % n_acc] + jnp.where(pred(x, kpos), 1, 0)
            return tuple(tuple(a) for a in acc)

        zero = tuple(tuple(jnp.zeros(vshape, I32) for _ in range(n_acc)) for _ in preds)
        return [col_sum(list(a)) for a in lax.fori_loop(0, nkt, tile, zero)]

    def count_around_threshold():
        tf = _key_to_float(st_ref[0])
        gt, ge = count_scores([lambda x, kpos: x > tf, lambda x, kpos: x >= tf])
        st_ref[1] = gt
        st_ref[2] = ge

    tf_fast = jnp.broadcast_to(_key_to_float(st_ref[0])[0:1, :], (SEL_TK, LANES))

    def fast_tile(kt, acc):
        k0 = pl.multiple_of(kt * SEL_TK, SEL_TK)
        m = jnp.where(sc_ref[pl.ds(k0, SEL_TK), :] >= tf_fast, 1.0, 0.0)
        mask_ref[0, pl.ds(k0, SEL_TK), :] = m.astype(BF16)
        acc = list(acc)
        for v in range(vt):
            acc[v % n_acc] = acc[v % n_acc] + m[v * SUBLANES:(v + 1) * SUBLANES, :]
        return tuple(acc)

    size = col_sum(list(lax.fori_loop(0, nkt, fast_tile, tuple(jnp.zeros(vshape, F32) for _ in range(n_acc)))))

    @pl.when(jnp.max(jnp.abs(size - k_top)) > 0.0)
    def _():
        count_around_threshold()

        @pl.when(jnp.max(st_ref[1]) >= k_top)
        def _():
            def full_bit(i, t):
                c = t ^ lax.shift_left(jnp.int32(1), 31 - i)
                cnt, = count_scores([lambda x, kpos: _sortable_key(x) >= c])
                return jnp.where(cnt >= k_top, c, t)

            st_ref[0] = lax.fori_loop(0, 32, full_bit, jnp.full(vshape, INT_MIN, I32))
            count_around_threshold()

        st_ref[3] = jnp.full(vshape, n_rows, I32)

        @pl.when(jnp.max(st_ref[2]) > k_top)
        def _():
            tf = _key_to_float(st_ref[0])
            need = k_top - st_ref[1]
            n_bits = n_rows.bit_length() - 1

            def pos_bit(i, p):
                c = p + lax.shift_left(jnp.int32(1), n_bits - 1 - i)
                f, = count_scores([lambda x, kpos: (x == tf) & (kpos < c)])
                return jnp.where(f < need, c, p)

            st_ref[3] = lax.fori_loop(0, n_bits, pos_bit, jnp.zeros(vshape, I32))

        tf_t = jnp.broadcast_to(_key_to_float(st_ref[0])[0:1, :], (SEL_TK, LANES))
        lim_t = jnp.broadcast_to(st_ref[3][0:1, :], (SEL_TK, LANES))

        def write_tile(kt, carry):
            k0 = pl.multiple_of(kt * SEL_TK, SEL_TK)
            x = sc_ref[pl.ds(k0, SEL_TK), :]
            kpos = k0 + krow_t
            sel = ((x > tf_t) | ((x == tf_t) & (kpos <= lim_t))) & (kpos <= qpos_t)
            mask_ref[0, pl.ds(k0, SEL_TK), :] = jnp.where(sel, 1.0, 0.0).astype(BF16)
            return carry

        lax.fori_loop(0, nkt, write_tile, 0)

    def fill_tile(kt, carry):
        mask_ref[0, pl.ds(pl.multiple_of(kt * SEL_TK, SEL_TK), SEL_TK), :] = jnp.zeros((SEL_TK, LANES), BF16)
        return carry

    lax.fori_loop(nkt, n_tiles, fill_tile, 0)


def _select(qi_pairs, w_t, ki_ext, k_top):
    nqb, npair, kw, _ = qi_pairs.shape
    s = ki_ext.shape[0]
    n_cand_rows = SEL_GROUPS * SEL_DEPTH * SUBLANES
    assert s % SEL_TK == 0 and SEL_TK == n_cand_rows and n_cand_rows >= k_top
    return pl.pallas_call(
        functools.partial(_select_kernel, k_top=k_top),
        grid=(nqb,),
        in_specs=[
            pl.BlockSpec((1, npair, kw, 2 * SEL_TQ), lambda i: (i, 0, 0, 0)),
            pl.BlockSpec((w_t.shape[0], SEL_TQ), lambda i: (0, i)),
            pl.BlockSpec((s, kw), lambda i: (0, 0)),
        ],
        out_specs=pl.BlockSpec((1, s, SEL_TQ), lambda i: (i, 0, 0)),
        out_shape=jax.ShapeDtypeStruct((nqb, s, SEL_TQ), BF16),
        scratch_shapes=[
            pltpu.VMEM((s + SEL_TK, SEL_TQ), F32),
            pltpu.VMEM((n_cand_rows, SEL_TQ), F32),
            pltpu.VMEM((n_cand_rows, SEL_TQ), I32),
            pltpu.VMEM((4, SUBLANES, LANES), I32),
        ],
        compiler_params=pltpu.CompilerParams(dimension_semantics=("arbitrary",), vmem_limit_bytes=VMEM_LIMIT),
        name="select",
    )(qi_pairs, w_t, ki_ext)


def _causal_pairs(s, tq, tk):
    pairs = [(qb, kt) for qb in range(s // tq) for kt in range((qb * tq + tq - 1) // tk + 1)]
    return jnp.asarray([p[0] for p in pairs], I32), jnp.asarray([p[1] for p in pairs], I32)


def _attn_kernel(qb_tab, kt_tab, qt_ref, kmax_ref, k_ref, vt_ref, mask_ref, bias_ref, o_ref, l_ref, qx_ref, acc_ref):
    step = pl.program_id(0)
    qb = qb_tab[step]
    kt = kt_tab[step]
    q0 = qb * ATT_TQ
    k0 = kt * ATT_TK
    nq = ATT_TQ // LANES
    nk = ATT_TK // LANES

    @pl.when(kt == 0)
    def _():
        row = lax.broadcasted_iota(I32, (HEAD_DIM, ATT_TQ), 0)
        for h in range(N_HEADS):
            q = qt_ref[h]
            qf = q.astype(F32)
            bound = jnp.sqrt(jnp.sum(qf * qf, axis=0, keepdims=True)) * kmax_ref[h] * BOUND_MARGIN
            qx_ref[h, 0:HEAD_DIM, :] = q
            qx_ref[h, HEAD_DIM:, :] = jnp.where(row == 0, -bound, 0.0).astype(BF16)
        acc_ref[...] = jnp.zeros(acc_ref.shape, F32)

    mask = jnp.concatenate([mask_ref[a] for a in range(nq)], axis=1)

    def heads(with_bias):
        for h in range(N_HEADS):
            out = None
            for r0 in range(0, ATT_TK, ATT_KC):
                keys = slice(r0, r0 + ATT_KC)
                s = _dot(k_ref[h, keys, :], qx_ref[h])
                if with_bias:
                    rows = []
                    for c in range(r0 // LANES, (r0 + ATT_KC) // LANES):
                        tab = [jnp.clip((q0 + a * LANES - k0 - c * LANES) // LANES + 1, 0, 3) for a in range(nq)]
                        rows.append(jnp.concatenate([bias_ref[tab[a], h] for a in range(nq)], axis=1))
                    s = s + jnp.concatenate(rows, axis=0)
                p = jnp.exp2(s).astype(BF16) * mask[keys]
                part = _dot(vt_ref[h, :, keys], p)
                out = part if out is None else out + part
            acc_ref[h] += out

    near = k0 + ATT_TK + 2 * LANES > q0

    @pl.when(near)
    def _():
        heads(True)

    @pl.when(jnp.logical_not(near))
    def _():
        heads(False)

    @pl.when(kt == (q0 + ATT_TQ - 1) // ATT_TK)
    def _():
        for h in range(N_HEADS):
            acc = acc_ref[h]
            den = acc[HEAD_DIM:HEAD_DIM + 1, :]
            o_ref[h] = acc[:HEAD_DIM, :] / den
            l_ref[h] = den


def _attn(q_t, kmax_b, k_ext, v_t, mask, bias_tab):
    nh, hd, s = q_t.shape
    qb_tab, kt_tab = _causal_pairs(s, ATT_TQ, ATT_TK)
    grid_spec = pltpu.PrefetchScalarGridSpec(
        num_scalar_prefetch=2,
        grid=(qb_tab.shape[0],),
        in_specs=[
            pl.BlockSpec((nh, hd, ATT_TQ), lambda i, qb, kt: (0, 0, qb[i])),
            pl.BlockSpec((nh, 1, ATT_TQ), lambda i, qb, kt: (0, 0, 0)),
            pl.BlockSpec((nh, ATT_TK, LANES), lambda i, qb, kt: (0, kt[i], 0)),
            pl.BlockSpec((nh, LANES, ATT_TK), lambda i, qb, kt: (0, 0, kt[i])),
            pl.BlockSpec((ATT_TQ // LANES, ATT_TK, LANES), lambda i, qb, kt: (qb[i], kt[i], 0)),
            pl.BlockSpec(bias_tab.shape, lambda i, qb, kt: (0, 0, 0, 0)),
        ],
        out_specs=[
            pl.BlockSpec((nh, hd, ATT_TQ), lambda i, qb, kt: (0, 0, qb[i])),
            pl.BlockSpec((nh, 1, ATT_TQ), lambda i, qb, kt: (0, 0, qb[i])),
        ],
        scratch_shapes=[pltpu.VMEM((nh, LANES, ATT_TQ), BF16), pltpu.VMEM((nh, LANES, ATT_TQ), F32)],
    )
    return pl.pallas_call(
        _attn_kernel,
        grid_spec=grid_spec,
        out_shape=[jax.ShapeDtypeStruct((nh, hd, s), F32), jax.ShapeDtypeStruct((nh, 1, s), F32)],
        compiler_params=pltpu.CompilerParams(dimension_semantics=("arbitrary",), vmem_limit_bytes=VMEM_LIMIT),
        name="attn",
    )(qb_tab, kt_tab, q_t, kmax_b, k_ext, v_t, mask, bias_tab)


def _attn_safe_kernel(qb_tab, kt_tab, q_ref, k_ref, v_ref, mask_ref, bias_ref, o_ref, m_ref, acc_ref):
    step = pl.program_id(0)
    qb = qb_tab[step]
    kt = kt_tab[step]
    nq = SAFE_TQ // LANES
    nsub = ATT_TK // LANES

    @pl.when(kt == 0)
    def _():
        m_ref[...] = jnp.full(m_ref.shape, NEG, F32)
        acc_ref[...] = jnp.zeros(acc_ref.shape, F32)

    for a in range(nq):
        rows = slice(a * LANES, (a + 1) * LANES)
        q0 = qb * SAFE_TQ + a * LANES
        maskf = (mask_ref[a].astype(F32) - 1.0) * (-NEG)
        tab_idx = [jnp.clip((q0 - (kt * ATT_TK + c * LANES)) // LANES + 1, 0, 3) for c in range(nsub)]
        for h in range(N_HEADS):
            s = _dot_t(q_ref[h, rows, :], k_ref[h]) + maskf
            s = s + jnp.concatenate([bias_ref[tab_idx[c], h] for c in range(nsub)], axis=1)
            m_old = m_ref[h, rows, :]
            m_new = jnp.maximum(m_old, jnp.max(s, axis=1, keepdims=True))
            p = jnp.exp2(s - m_new[:, 0:1])
            alpha = jnp.exp2(m_old - m_new)
            acc_ref[h, rows, :] = alpha * acc_ref[h, rows, :] + _dot(p.astype(BF16), v_ref[h])
            m_ref[h, rows, :] = m_new

    @pl.when(kt == (qb * SAFE_TQ + SAFE_TQ - 1) // ATT_TK)
    def _():
        for h in range(N_HEADS):
            acc = acc_ref[h]
            o_ref[h] = acc[:, :HEAD_DIM] / acc[:, HEAD_DIM:HEAD_DIM + 1]


def _attn_safe(q, k, v_ext, mask_qk, bias_tab):
    nh, s, hd = q.shape
    qb_tab, kt_tab = _causal_pairs(s, SAFE_TQ, ATT_TK)
    grid_spec = pltpu.PrefetchScalarGridSpec(
        num_scalar_prefetch=2,
        grid=(qb_tab.shape[0],),
        in_specs=[
            pl.BlockSpec((nh, SAFE_TQ, hd), lambda i, qb, kt: (0, qb[i], 0)),
            pl.BlockSpec((nh, ATT_TK, hd), lambda i, qb, kt: (0, kt[i], 0)),
            pl.BlockSpec((nh, ATT_TK, LANES), lambda i, qb, kt: (0, kt[i], 0)),
            pl.BlockSpec((SAFE_TQ // LANES, LANES, ATT_TK), lambda i, qb, kt: (qb[i], 0, kt[i])),
            pl.BlockSpec(bias_tab.shape, lambda i, qb, kt: (0, 0, 0, 0)),
        ],
        out_specs=pl.BlockSpec((nh, SAFE_TQ, hd), lambda i, qb, kt: (0, qb[i], 0)),
        scratch_shapes=[pltpu.VMEM((nh, SAFE_TQ, LANES), F32), pltpu.VMEM((nh, SAFE_TQ, LANES), F32)],
    )
    return pl.pallas_call(
        _attn_safe_kernel,
        grid_spec=grid_spec,
        out_shape=jax.ShapeDtypeStruct((nh, s, hd), F32),
        compiler_params=pltpu.CompilerParams(dimension_semantics=("arbitrary",), vmem_limit_bytes=VMEM_LIMIT),
        name="attn_safe",
    )(qb_tab, kt_tab, q, k, v_ext, mask_qk, bias_tab)


def _tail_kernel(h_ref, cg_ref, ga_ref, at_ref, p_ref, wao_ref, wmo_ref, fg_ref, wi_ref, wo_ref, pg_ref, wpg_ref,
                 wpp_ref, o_ref):
    attn = at_ref[...].T.astype(BF16)
    merged = cg_ref[...] + ga_ref[...] * _dot(attn, wao_ref[...])
    h = h_ref[...] + _dot(merged.astype(BF16), wmo_ref[...])
    h = _half_step_ffn(h, fg_ref[...], wi_ref, wo_ref)
    gate = _sigmoid(_dot(_rms(h, pg_ref[...]).astype(BF16), wpg_ref[...]))
    o_ref[...] = h + gate * _dot(p_ref[...].astype(BF16), wpp_ref[...])


def _tail(h, conv_g, gate_a, attn_t, p, w_ao, w_mo, ffn_g, w_in, w_out, ple_g, w_pg, w_pp, tm=512):
    s, d = h.shape
    row = lambda n: pl.BlockSpec((tm, n), lambda i: (i, 0))
    fg2 = ffn_g.reshape(1, d)
    pg2 = ple_g.reshape(1, d)
    params = (w_ao, w_mo, fg2, w_in, w_out, pg2, w_pg, w_pp)
    return pl.pallas_call(
        _tail_kernel,
        grid=(s // tm,),
        in_specs=[row(d), row(d), row(d), pl.BlockSpec((attn_t.shape[0], tm), lambda i: (0, i)), row(p.shape[1])]
        + [_resident(a) for a in params],
        out_specs=row(d),
        out_shape=jax.ShapeDtypeStruct((s, d), F32),
        compiler_params=pltpu.CompilerParams(dimension_semantics=("arbitrary",), vmem_limit_bytes=VMEM_LIMIT),
        name="tail",
    )(h, conv_g, gate_a, attn_t, p, *params)


def _t5_bucket_table(n_dist):
    n = np.arange(n_dist)
    max_exact = NUM_BUCKETS // 2
    nf = np.maximum(n, 1).astype(np.float32)
    large = max_exact + (np.log(nf / max_exact) / math.log(MAX_DISTANCE / max_exact)
                         * (NUM_BUCKETS - max_exact)).astype(np.int32)
    large = np.minimum(large, NUM_BUCKETS - 1)
    return np.where(n < max_exact, n, large)


def _bias_tables(rel_bias):
    assert MAX_DISTANCE <= LANES
    n = LANES
    rel = (rel_bias.astype(F32) - rel_bias[NUM_BUCKETS - 1].astype(F32)[None, :]) * LOG2E
    by_dist = rel[_t5_bucket_table(2 * n)].T

    def toeplitz(a):
        skew = jnp.tile(a, (1, n))[:, :n * (2 * n - 1)].reshape(a.shape[0], n, 2 * n - 1)
        return skew[:, :, :n]

    d0 = toeplitz(jnp.concatenate([by_dist[:, :n], jnp.broadcast_to(by_dist[:, :1], by_dist[:, :n].shape)], axis=1))
    d1 = toeplitz(jnp.concatenate([by_dist[:, n:], by_dist[:, :n]], axis=1))
    zero = jnp.zeros_like(d0)
    return jnp.swapaxes(jnp.stack([zero, d0, d1, zero]), 2, 3)


def kernel(x, p, ffn1_norm, ffn1_w_in, ffn1_w_out, mix_norm, mix_w_in, conv_dw_w, conv_dw_b, conv_ln_g,
           conv_ln_b, conv_w_out, q_norm, k_norm, attn_w_out, mix_w_out, ffn2_norm, ffn2_w_in, ffn2_w_out,
           ple_norm, ple_w_gate, ple_w_proj, rel_bias):
    b, s, d = x.shape
    depth = ffn1_norm.shape[0]
    cw = conv_dw_w.shape[2]
    aw = N_HEADS * HEAD_DIM
    nqi = IDX_HEADS * IDX_DIM
    iw = nqi + IDX_DIM + IDX_HEADS
    iw_pad = -(-iw // LANES) * LANES
    k_top = min(TOPK_MAX, s // 4)
    assert b == 1 and s % ATT_TK == 0 and mix_w_in.shape[2] == 2 * cw + 3 * aw + iw + 2 * d

    head_blocks = jnp.asarray(np.kron(np.eye(N_HEADS), np.ones((HEAD_DIM, HEAD_DIM))), BF16)
    bias_qk = _bias_tables(rel_bias)
    bias_kq = jnp.swapaxes(bias_qk, 2, 3)

    h = x[0]
    for i in range(depth):
        h = _ffn(h, ffn1_norm[i], ffn1_w_in[i].astype(BF16), ffn1_w_out[i].astype(BF16))

        w = mix_w_in[i]
        o0 = 2 * cw
        o1 = o0 + 3 * aw
        o2 = o1 + iw
        w_idx = jnp.pad(w[:, o1:o2], ((0, 0), (0, iw_pad - iw)))
        w_idx_hi, w_idx_lo = _split_bf16(w_idx)
        qg = jnp.tile(q_norm[i], N_HEADS).reshape(1, aw)
        kg = jnp.tile(k_norm[i], N_HEADS).reshape(1, aw)
        glu, q_t, k_ext, v_t, qi_pairs, ki_ext, idx_w, gate_c, gate_a = _mix_in(
            h, mix_norm[i], w[:, :o0].astype(BF16), w[:, o0:o1].astype(BF16), w_idx_hi, w_idx_lo,
            w[:, o2:].astype(BF16), qg, kg, head_blocks)

        conv_g = _conv(glu, conv_dw_w[i], conv_dw_b[i], conv_ln_g[i], conv_ln_b[i],
                       conv_w_out[i].astype(BF16), gate_c)

        w_t = idx_w[:, IDX_DIM:IDX_DIM + IDX_HEADS].T
        mask = _select(qi_pairs, w_t, ki_ext, k_top)

        kmax = math.sqrt(HEAD_DIM) * jnp.max(jnp.abs(k_norm[i]))
        kmax_b = jnp.full((N_HEADS, 1, ATT_TQ), kmax, F32)
        attn_t, den = _attn(q_t, kmax_b, k_ext, v_t, mask, bias_kq)

        def safe_attn():
            out = _attn_safe(jnp.swapaxes(q_t, 1, 2), k_ext[:, :, :HEAD_DIM], jnp.swapaxes(v_t, 1, 2),
                             jnp.swapaxes(mask, 1, 2), bias_qk)
            return jnp.swapaxes(out, 1, 2)

        underflow = jnp.logical_not(jnp.min(den) > L_MIN)
        attn_t = lax.cond(underflow, safe_attn, lambda: attn_t).reshape(aw, s)

        h = _tail(h, conv_g, gate_a, attn_t, p[i, 0], attn_w_out[i].astype(BF16), mix_w_out[i].astype(BF16),
                  ffn2_norm[i], ffn2_w_in[i].astype(BF16), ffn2_w_out[i].astype(BF16),
                  ple_norm[i], ple_w_gate[i].astype(BF16), ple_w_proj[i].astype(BF16))
    return h[None]
```

```python
import functools
import math

import numpy as np
import jax
import jax.numpy as jnp
from jax import lax
from jax.experimental import pallas as pl
from jax.experimental.pallas import tpu as pltpu

F32 = jnp.float32
BF16 = jnp.bfloat16
I32 = jnp.int32

EPS = 1e-6
CONV_WIDTH = 31
N_HEADS = 8
HEAD_DIM = 64
IDX_HEADS = 4
IDX_DIM = 64
TOPK_MAX = 256
NUM_BUCKETS = 32
MAX_DISTANCE = 128

LANES = 128
SUBLANES = 8
MXU_TILE = 256
FFN_CHUNK = 1024
VMEM_LIMIT = 56 * 1024 * 1024
NEG = -1e30
INT_MIN = -2 ** 31
MIN_NORMAL_BITS = 0x00800000
LOG2E = math.log2(math.e)

CONV_HALO = 32
CONV_CHUNK = 64
SEL_TQ = LANES
SEL_TK = 1024
SEL_GROUPS = 8
SEL_DEPTH = 16
ATT_TQ = 1024
ATT_TK = 1024
SAFE_TQ = 256
L_MIN = 1e-30
BOUND_MARGIN = 1.02


def _sigmoid(x):
    return 1.0 / (1.0 + jnp.exp(-x))


def _rms(x, g):
    ms = jnp.mean(x * x, axis=-1, keepdims=True)
    return x * lax.rsqrt(ms + EPS) * g


def _dot(a, b):
    return jnp.dot(a, b, preferred_element_type=F32)


def _dot_t(a, b):
    return lax.dot_general(a, b, (((1,), (1,)), ((), ())), preferred_element_type=F32)


def _split_bf16(x):
    hi = x.astype(BF16)
    lo = (x - hi.astype(F32)).astype(BF16)
    return hi, lo


def _half_step_ffn(x, g, wi_ref, wo_ref):
    xn = _rms(x, g).astype(BF16)
    dff = wo_ref.shape[0]
    acc = None
    for c0 in range(0, dff, FFN_CHUNK):
        c1 = min(c0 + FFN_CHUNK, dff)
        a = _dot(xn, wi_ref[:, c0:c1])
        b = _dot(xn, wi_ref[:, dff + c0:dff + c1])
        part = _dot((a * _sigmoid(a) * b).astype(BF16), wo_ref[c0:c1, :])
        acc = part if acc is None else acc + part
    return x + 0.5 * acc


def _ffn_kernel(x_ref, g_ref, wi_ref, wo_ref, o_ref):
    o_ref[...] = _half_step_ffn(x_ref[...], g_ref[...], wi_ref, wo_ref)


def _resident(a):
    return pl.BlockSpec(a.shape, lambda *_: (0,) * a.ndim, pipeline_mode=pl.Buffered(1))


def _ffn(x, g, w_in, w_out, tm=512):
    s, d = x.shape
    dff = w_out.shape[0]
    assert s % tm == 0 and dff % MXU_TILE == 0 and FFN_CHUNK % MXU_TILE == 0
    g2 = g.reshape(1, d)
    return pl.pallas_call(
        _ffn_kernel,
        grid=(s // tm,),
        in_specs=[pl.BlockSpec((tm, d), lambda i: (i, 0)), _resident(g2), _resident(w_in), _resident(w_out)],
        out_specs=pl.BlockSpec((tm, d), lambda i: (i, 0)),
        out_shape=jax.ShapeDtypeStruct((s, d), F32),
        compiler_params=pltpu.CompilerParams(dimension_semantics=("arbitrary",), vmem_limit_bytes=VMEM_LIMIT),
        name="ffn",
    )(x, g2, w_in, w_out)


def _mix_in_kernel(h_ref, g_ref, wc_ref, wqkv_ref, wih_ref, wil_ref, wg_ref, qg_ref, kg_ref, hb_ref,
                   glu_ref, qt_ref, kx_ref, vt_ref, qip_ref, kix_ref, wi_ref, gc_ref, ga_ref):
    u = _rms(h_ref[...], g_ref[...])
    u_hi, u_lo = _split_bf16(u)
    tm = h_ref.shape[0]
    cw = glu_ref.shape[1]
    aw = N_HEADS * HEAD_DIM
    nqi = IDX_HEADS * IDX_DIM
    d = gc_ref.shape[1]

    c = _dot(u_hi, wc_ref[...])
    glu_ref[...] = c[:, :cw] * _sigmoid(c[:, cw:])

    qkv = _dot(u_hi, wqkv_ref[...])
    hb = hb_ref[...]

    def head_sumsq(t):
        t2_hi, t2_lo = _split_bf16(t * t)
        return _dot(t2_hi, hb) + _dot(t2_lo, hb)

    def head_norm(t, g):
        return t * lax.rsqrt(head_sumsq(t) * (1.0 / HEAD_DIM) + EPS) * g

    q = head_norm(qkv[:, :aw], qg_ref[...]) * (HEAD_DIM ** -0.5 * LOG2E)
    q_t = q.T
    for hd in range(N_HEADS):
        qt_ref[hd] = q_t[hd * HEAD_DIM:(hd + 1) * HEAD_DIM, :].astype(BF16)
    k = head_norm(qkv[:, aw:2 * aw], kg_ref[...]).astype(BF16).astype(F32)
    one_hot0 = lambda shape, axis: jnp.where(lax.broadcasted_iota(I32, shape, axis) == 0, 1.0, 0.0)
    k_pad = one_hot0((tm, LANES - HEAD_DIM), 1)
    for hd in range(N_HEADS):
        kx_ref[hd] = jnp.concatenate([k[:, hd * HEAD_DIM:(hd + 1) * HEAD_DIM], k_pad], axis=1).astype(BF16)
    v_t = qkv[:, 2 * aw:].T
    v_pad = one_hot0((LANES - HEAD_DIM, tm), 0).astype(BF16)
    for hd in range(N_HEADS):
        vt_ref[hd, 0:HEAD_DIM, :] = v_t[hd * HEAD_DIM:(hd + 1) * HEAD_DIM, :].astype(BF16)
        vt_ref[hd, HEAD_DIM:, :] = v_pad

    wih = wih_ref[...]
    idx = _dot(u_hi, wih) + _dot(u_lo, wih) + _dot(u_hi, wil_ref[...])
    qi_hi, qi_lo = _split_bf16(idx[:, :nqi].T)
    for blk in range(tm // SEL_TQ):
        cols = slice(blk * SEL_TQ, (blk + 1) * SEL_TQ)
        for hd in range(IDX_HEADS):
            rows = slice(hd * IDX_DIM, (hd + 1) * IDX_DIM)
            out_cols = slice((hd % 2) * SEL_TQ, (hd % 2 + 1) * SEL_TQ)
            for part, src in enumerate((qi_hi, qi_hi, qi_lo, qi_lo)):
                qip_ref[blk, hd // 2, part * IDX_DIM:(part + 1) * IDX_DIM, out_cols] = src[rows, cols]
    ki = idx[:, nqi:nqi + IDX_DIM]
    ki_hi = ki.astype(BF16).astype(F32)
    ki_lo = ki - ki_hi
    kix_ref[...] = jnp.concatenate([ki_hi, ki_lo, ki_hi, ki_lo], axis=1).astype(BF16)
    wi_ref[...] = idx[:, nqi:]

    gates = _sigmoid(_dot(u_hi, wg_ref[...]))
    gc_ref[...] = gates[:, :d]
    ga_ref[...] = gates[:, d:]


def _mix_in(h, g, w_conv, w_qkv, w_idx_hi, w_idx_lo, w_gate, qg, kg, head_blocks, tm=512):
    s, d = h.shape
    cw = w_conv.shape[1] // 2
    aw = w_qkv.shape[1] // 3
    nqi = IDX_HEADS * IDX_DIM
    iw = w_idx_hi.shape[1]
    assert aw == N_HEADS * HEAD_DIM and tm % SEL_TQ == 0 and iw - nqi == LANES
    full = lambda a: pl.BlockSpec(a.shape, lambda i: (0,) * a.ndim)
    row = lambda n: pl.BlockSpec((tm, n), lambda i: (i, 0))
    g2 = g.reshape(1, d)
    return pl.pallas_call(
        _mix_in_kernel,
        grid=(s // tm,),
        in_specs=[row(d), full(g2), full(w_conv), full(w_qkv), full(w_idx_hi), full(w_idx_lo), full(w_gate),
                  full(qg), full(kg), full(head_blocks)],
        out_specs=[
            row(cw),
            pl.BlockSpec((N_HEADS, HEAD_DIM, tm), lambda i: (0, 0, i)),
            pl.BlockSpec((N_HEADS, tm, LANES), lambda i: (0, i, 0)),
            pl.BlockSpec((N_HEADS, LANES, tm), lambda i: (0, 0, i)),
            pl.BlockSpec((tm // SEL_TQ, IDX_HEADS // 2, 4 * IDX_DIM, 2 * SEL_TQ), lambda i: (i, 0, 0, 0)),
            row(4 * IDX_DIM), row(LANES), row(d), row(d),
        ],
        out_shape=[
            jax.ShapeDtypeStruct((s, cw), F32),
            jax.ShapeDtypeStruct((N_HEADS, HEAD_DIM, s), BF16),
            jax.ShapeDtypeStruct((N_HEADS, s, LANES), BF16),
            jax.ShapeDtypeStruct((N_HEADS, LANES, s), BF16),
            jax.ShapeDtypeStruct((s // SEL_TQ, IDX_HEADS // 2, 4 * IDX_DIM, 2 * SEL_TQ), BF16),
            jax.ShapeDtypeStruct((s, 4 * IDX_DIM), BF16),
            jax.ShapeDtypeStruct((s, LANES), F32),
            jax.ShapeDtypeStruct((s, d), F32),
            jax.ShapeDtypeStruct((s, d), F32),
        ],
        compiler_params=pltpu.CompilerParams(dimension_semantics=("arbitrary",), vmem_limit_bytes=VMEM_LIMIT),
        name="mix_in",
    )(h, g2, w_conv, w_qkv, w_idx_hi, w_idx_lo, w_gate, qg, kg, head_blocks)


def _conv_kernel(z_ref, halo_ref, dw_ref, db_ref, lg_ref, lb_ref, wo_ref, gc_ref, o_ref, zp_ref, zs_ref, acc_ref):
    tm = z_ref.shape[0]
    first = pl.program_id(0) == 0
    halo = halo_ref[...]
    zp_ref[0:CONV_HALO, :] = jnp.where(first, jnp.zeros_like(halo), halo)
    zp_ref[CONV_HALO:, :] = z_ref[...]
    off = CONV_HALO - (CONV_WIDTH - 1)
    span = zs_ref.shape[1]
    for b in range(1, SUBLANES):
        zs_ref[b - 1] = zp_ref[b:b + span, :]

    def chunk(c, carry):
        r0 = pl.multiple_of(c * CONV_CHUNK, CONV_CHUNK)
        acc = jnp.zeros((CONV_CHUNK, z_ref.shape[1]), F32) + db_ref[...]
        for j in range(CONV_WIDTH):
            a, b = divmod(off + j, SUBLANES)
            src = zp_ref if b == 0 else zs_ref.at[b - 1]
            acc = acc + dw_ref[j:j + 1, :] * src[pl.ds(r0 + a * SUBLANES, CONV_CHUNK), :]
        acc_ref[pl.ds(r0, CONV_CHUNK), :] = acc
        return carry

    lax.fori_loop(0, tm // CONV_CHUNK, chunk, 0)
    acc = acc_ref[...]
    mu = jnp.mean(acc, axis=-1, keepdims=True)
    xc = acc - mu
    y = xc * lax.rsqrt(jnp.mean(xc * xc, axis=-1, keepdims=True) + EPS)
    y = y * lg_ref[...] + lb_ref[...]
    y = (y * _sigmoid(y)).astype(BF16)
    o_ref[...] = gc_ref[...] * _dot(y, wo_ref[...])


def _conv(z, dw_w, dw_b, ln_g, ln_b, w_out, gate_c, tm=512):
    s, c = z.shape
    d = w_out.shape[1]
    assert tm % CONV_HALO == 0 and tm % CONV_CHUNK == 0
    r = tm // CONV_HALO
    span = tm + CONV_HALO - SUBLANES
    full = lambda a: pl.BlockSpec(a.shape, lambda i: (0,) * a.ndim)
    vecs = [dw_b.reshape(1, c), ln_g.reshape(1, c), ln_b.reshape(1, c)]
    return pl.pallas_call(
        _conv_kernel,
        grid=(s // tm,),
        in_specs=[
            pl.BlockSpec((tm, c), lambda i: (i, 0)),
            pl.BlockSpec((CONV_HALO, c), lambda i: (jnp.maximum(i * r - 1, 0), 0)),
            full(dw_w), full(vecs[0]), full(vecs[1]), full(vecs[2]), full(w_out),
            pl.BlockSpec((tm, d), lambda i: (i, 0)),
        ],
        out_specs=pl.BlockSpec((tm, d), lambda i: (i, 0)),
        out_shape=jax.ShapeDtypeStruct((s, d), F32),
        scratch_shapes=[pltpu.VMEM((tm + CONV_HALO, c), F32), pltpu.VMEM((SUBLANES - 1, span, c), F32),
                        pltpu.VMEM((tm, c), F32)],
        compiler_params=pltpu.CompilerParams(dimension_semantics=("arbitrary",), vmem_limit_bytes=VMEM_LIMIT),
        name="conv",
    )(z, z, dw_w, *vecs, w_out, gate_c)


def _sortable_key(x):
    bits = pltpu.bitcast(x + 0.0, I32)
    return bits ^ ((bits >> 31) & 0x7FFFFFFF)


def _key_to_float(key):
    return pltpu.bitcast(key ^ ((key >> 31) & 0x7FFFFFFF), F32)


def _sort_network(n):
    pairs = []

    def merge(lo, hi, r):
        step = r * 2
        if step < hi - lo:
            merge(lo, hi, step)
            merge(lo + r, hi, step)
            pairs.extend((i, i + r) for i in range(lo + r, hi - r, step))
        else:
            pairs.append((lo, lo + r))

    def sort(lo, hi):
        if hi > lo:
            mid = lo + (hi - lo) // 2
            sort(lo, mid)
            sort(mid + 1, hi)
            merge(lo, hi, 1)

    sort(0, n - 1)
    return pairs


def _compare_exchange(a, i, j):
    a[i], a[j] = jnp.maximum(a[i], a[j]), jnp.minimum(a[i], a[j])


def _merge_top(top, batch):
    n = len(top)
    out = [jnp.maximum(top[i], batch[n - 1 - i]) for i in range(n)]
    d = n // 2
    while d >= 1:
        for i in range(n):
            if i & d == 0:
                _compare_exchange(out, i, i + d)
        d //= 2
    return out


def _select_kernel(qi_ref, w_ref, ki_ref, mask_ref, sc_ref, cand_ref, ckey_ref, st_ref, *, k_top):
    qb = pl.program_id(0)
    n_rows = mask_ref.shape[1]
    n_tiles = n_rows // SEL_TK
    q0 = qb * SEL_TQ
    nkt = (q0 + SEL_TQ - 1) // SEL_TK + 1
    vt = SEL_TK // SUBLANES
    vshape = (SUBLANES, LANES)
    qpos = q0 + lax.broadcasted_iota(I32, vshape, 1)
    sub = lax.broadcasted_iota(I32, vshape, 0)
    w = w_ref[...] * ((IDX_HEADS ** -0.5) * (IDX_DIM ** -0.5))

    qpos_t = q0 + lax.broadcasted_iota(I32, (SEL_TK, LANES), 1)
    krow_t = lax.broadcasted_iota(I32, (SEL_TK, LANES), 0)
    zero_bits_t = (MIN_NORMAL_BITS + n_rows) - krow_t

    def score_tile(kt):
        k0 = pl.multiple_of(kt * SEL_TK, SEL_TK)
        ki = ki_ref[pl.ds(k0, SEL_TK), :]
        sc = None
        for pair in range(IDX_HEADS // 2):
            r = _dot(ki, qi_ref[0, pair])
            for j in range(2):
                h = 2 * pair + j
                term = w[h:h + 1, :] * jnp.maximum(r[:, j * LANES:(j + 1) * LANES], 0.0)
                sc = term if sc is None else sc + term
        sc_ref[pl.ds(k0, SEL_TK), :] = jnp.where(sc == 0.0, pltpu.bitcast(zero_bits_t - k0, F32), sc)

    def vreg(ref, row):
        return ref[pl.ds(pl.multiple_of(row, SUBLANES), SUBLANES), :]

    def col_sum(parts):
        tot = parts[0]
        for part in parts[1:]:
            tot = tot + part
        return jnp.broadcast_to(jnp.sum(tot, axis=0, keepdims=True), vshape)

    n_cand = SEL_GROUPS * SEL_DEPTH
    network = _sort_network(SEL_DEPTH)
    cand_ref[...] = jnp.full(cand_ref.shape, -jnp.inf, F32)

    def lists_tile(kt):
        for g in range(SEL_GROUPS):
            batch = [vreg(sc_ref, kt * SEL_TK + (g * SEL_DEPTH + j) * SUBLANES) for j in range(SEL_DEPTH)]
            for i, j in network:
                _compare_exchange(batch, i, j)
            rows = [slice((g * SEL_DEPTH + i) * SUBLANES, (g * SEL_DEPTH + i + 1) * SUBLANES)
                    for i in range(SEL_DEPTH)]
            top = _merge_top([cand_ref[r, :] for r in rows], batch)
            for r, t in zip(rows, top):
                cand_ref[r, :] = t

    sc_ref[pl.ds(n_rows, SEL_TK), :] = jnp.full((SEL_TK, LANES), -jnp.inf, F32)

    def score_and_lists(kt, carry):
        lists_tile(jnp.where(kt == 0, n_tiles, kt - 1))
        score_tile(kt)
        return carry

    lax.fori_loop(0, nkt, score_and_lists, 0)
    k_last = pl.multiple_of((nkt - 1) * SEL_TK, SEL_TK)
    sc_ref[pl.ds(k_last, SEL_TK), :] = jnp.where(
        k_last + krow_t <= qpos_t, sc_ref[pl.ds(k_last, SEL_TK), :], -jnp.inf)
    lists_tile(nkt - 1)
    ckey_ref[...] = _sortable_key(cand_ref[...])

    n_acc = 4

    def cand_bit(i, t):
        c = t ^ lax.shift_left(jnp.int32(1), 31 - i)
        acc = [jnp.zeros(vshape, I32) for _ in range(n_acc)]
        for v in range(n_cand):
            x = ckey_ref[v * SUBLANES:(v + 1) * SUBLANES, :]
            acc[v % n_acc] = acc[v % n_acc] + jnp.where(x >= c, 1, 0)
        return jnp.where(col_sum(acc) >= k_top, c, t)

    st_ref[0] = lax.fori_loop(0, 32, cand_bit, jnp.full(vshape, INT_MIN, I32))

    def count_scores(preds):
        def tile(kt, acc):
            acc = [list(a) for a in acc]
            for v in range(vt):
                x = vreg(sc_ref, kt * SEL_TK + v * SUBLANES)
                kpos = kt * SEL_TK + v * SUBLANES + sub
                for p, pred in enumerate(preds):
                    acc[p][v % n_acc] = acc[p][v % n_acc] + jnp.where(pred(x, kpos), 1, 0)
            return tuple(tuple(a) for a in acc)

        zero = tuple(tuple(jnp.zeros(vshape, I32) for _ in range(n_acc)) for _ in preds)
        return [col_sum(list(a)) for a in lax.fori_loop(0, nkt, tile, zero)]

    def count_around_threshold():
        tf = _key_to_float(st_ref[0])
        gt, ge = count_scores([lambda x, kpos: x > tf, lambda x, kpos: x >= tf])
        st_ref[1] = gt
        st_ref[2] = ge

    tf_fast = jnp.broadcast_to(_key_to_float(st_ref[0])[0:1, :], (SEL_TK, LANES))

    def fast_tile(kt, acc):
        k0 = pl.multiple_of(kt * SEL_TK, SEL_TK)
        m = jnp.where(sc_ref[pl.ds(k0, SEL_TK), :] >= tf_fast, 1.0, 0.0)
        mask_ref[0, pl.ds(k0, SEL_TK), :] = m.astype(BF16)
        acc = list(acc)
        for v in range(vt):
            acc[v % n_acc] = acc[v % n_acc] + m[v * SUBLANES:(v + 1) * SUBLANES, :]
        return tuple(acc)

    size = col_sum(list(lax.fori_loop(0, nkt, fast_tile, tuple(jnp.zeros(vshape, F32) for _ in range(n_acc)))))

    @pl.when(jnp.max(jnp.abs(size - k_top)) > 0.0)
    def _():
        count_around_threshold()

        @pl.when(jnp.max(st_ref[1]) >= k_top)
        def _():
            def full_bit(i, t):
                c = t ^ lax.shift_left(jnp.int32(1), 31 - i)
                cnt, = count_scores([lambda x, kpos: _sortable_key(x) >= c])
                return jnp.where(cnt >= k_top, c, t)

            st_ref[0] = lax.fori_loop(0, 32, full_bit, jnp.full(vshape, INT_MIN, I32))
            count_around_threshold()

        st_ref[3] = jnp.full(vshape, n_rows, I32)

        @pl.when(jnp.max(st_ref[2]) > k_top)
        def _():
            tf = _key_to_float(st_ref[0])
            need = k_top - st_ref[1]
            n_bits = n_rows.bit_length() - 1

            def pos_bit(i, p):
                c = p + lax.shift_left(jnp.int32(1), n_bits - 1 - i)
                f, = count_scores([lambda x, kpos: (x == tf) & (kpos < c)])
                return jnp.where(f < need, c, p)

            st_ref[3] = lax.fori_loop(0, n_bits, pos_bit, jnp.zeros(vshape, I32))

        tf_t = jnp.broadcast_to(_key_to_float(st_ref[0])[0:1, :], (SEL_TK, LANES))
        lim_t = jnp.broadcast_to(st_ref[3][0:1, :], (SEL_TK, LANES))

        def write_tile(kt, carry):
            k0 = pl.multiple_of(kt * SEL_TK, SEL_TK)
            x = sc_ref[pl.ds(k0, SEL_TK), :]
            kpos = k0 + krow_t
            sel = ((x > tf_t) | ((x == tf_t) & (kpos <= lim_t))) & (kpos <= qpos_t)
            mask_ref[0, pl.ds(k0, SEL_TK), :] = jnp.where(sel, 1.0, 0.0).astype(BF16)
            return carry

        lax.fori_loop(0, nkt, write_tile, 0)

    def fill_tile(kt, carry):
        mask_ref[0, pl.ds(pl.multiple_of(kt * SEL_TK, SEL_TK), SEL_TK), :] = jnp.zeros((SEL_TK, LANES), BF16)
        return carry

    lax.fori_loop(nkt, n_tiles, fill_tile, 0)


def _select(qi_pairs, w_t, ki_ext, k_top):
    nqb, npair, kw, _ = qi_pairs.shape
    s = ki_ext.shape[0]
    n_cand_rows = SEL_GROUPS * SEL_DEPTH * SUBLANES
    assert s % SEL_TK == 0 and SEL_TK == n_cand_rows and n_cand_rows >= k_top
    return pl.pallas_call(
        functools.partial(_select_kernel, k_top=k_top),
        grid=(nqb,),
        in_specs=[
            pl.BlockSpec((1, npair, kw, 2 * SEL_TQ), lambda i: (i, 0, 0, 0)),
            pl.BlockSpec((w_t.shape[0], SEL_TQ), lambda i: (0, i)),
            pl.BlockSpec((s, kw), lambda i: (0, 0)),
        ],
        out_specs=pl.BlockSpec((1, s, SEL_TQ), lambda i: (i, 0, 0)),
        out_shape=jax.ShapeDtypeStruct((nqb, s, SEL_TQ), BF16),
        scratch_shapes=[
            pltpu.VMEM((s + SEL_TK, SEL_TQ), F32),
            pltpu.VMEM((n_cand_rows, SEL_TQ), F32),
            pltpu.VMEM((n_cand_rows, SEL_TQ), I32),
            pltpu.VMEM((4, SUBLANES, LANES), I32),
        ],
        compiler_params=pltpu.CompilerParams(dimension_semantics=("arbitrary",), vmem_limit_bytes=VMEM_LIMIT),
        name="select",
    )(qi_pairs, w_t, ki_ext)


def _causal_pairs(s, tq, tk):
    pairs = [(qb, kt) for qb in range(s // tq) for kt in range((qb * tq + tq - 1) // tk + 1)]
    return jnp.asarray([p[0] for p in pairs], I32), jnp.asarray([p[1] for p in pairs], I32)


def _attn_kernel(qb_tab, kt_tab, qt_ref, kmax_ref, k_ref, vt_ref, mask_ref, bias_ref, o_ref, l_ref, qx_ref, acc_ref):
    step = pl.program_id(0)
    qb = qb_tab[step]
    kt = kt_tab[step]
    q0 = qb * ATT_TQ
    k0 = kt * ATT_TK
    nq = ATT_TQ // LANES
    nk = ATT_TK // LANES

    @pl.when(kt == 0)
    def _():
        row = lax.broadcasted_iota(I32, (HEAD_DIM, ATT_TQ), 0)
        for h in range(N_HEADS):
            q = qt_ref[h]
            qf = q.astype(F32)
            bound = jnp.sqrt(jnp.sum(qf * qf, axis=0, keepdims=True)) * kmax_ref[h] * BOUND_MARGIN
            qx_ref[h, 0:HEAD_DIM, :] = q
            qx_ref[h, HEAD_DIM:, :] = jnp.where(row == 0, -bound, 0.0).astype(BF16)
        acc_ref[...] = jnp.zeros(acc_ref.shape, F32)

    mask = jnp.concatenate([mask_ref[a] for a in range(nq)], axis=1)

    def heads(with_bias):
        for h in range(N_HEADS):
            s = _dot(k_ref[h], qx_ref[h])
            if with_bias:
                rows = []
                for c in range(nk):
                    tab = [jnp.clip((q0 + a * LANES - k0 - c * LANES) // LANES + 1, 0, 3) for a in range(nq)]
                    rows.append(jnp.concatenate([bias_ref[tab[a], h] for a in range(nq)], axis=1))
                s = s + jnp.concatenate(rows, axis=0)
            p = jnp.exp2(s).astype(BF16) * mask
            acc_ref[h] += _dot(vt_ref[h], p)

    near = k0 + ATT_TK + 2 * LANES > q0

    @pl.when(near)
    def _():
        heads(True)

    @pl.when(jnp.logical_not(near))
    def _():
        heads(False)

    @pl.when(kt == (q0 + ATT_TQ - 1) // ATT_TK)
    def _():
        for h in range(N_HEADS):
            acc = acc_ref[h]
            den = acc[HEAD_DIM:HEAD_DIM + 1, :]
            o_ref[h] = acc[:HEAD_DIM, :] / den
            l_ref[h] = den


def _attn(q_t, kmax_b, k_ext, v_t, mask, bias_tab):
    nh, hd, s = q_t.shape
    qb_tab, kt_tab = _causal_pairs(s, ATT_TQ, ATT_TK)
    grid_spec = pltpu.PrefetchScalarGridSpec(
        num_scalar_prefetch=2,
        grid=(qb_tab.shape[0],),
        in_specs=[
            pl.BlockSpec((nh, hd, ATT_TQ), lambda i, qb, kt: (0, 0, qb[i])),
            pl.BlockSpec((nh, 1, ATT_TQ), lambda i, qb, kt: (0, 0, 0)),
            pl.BlockSpec((nh, ATT_TK, LANES), lambda i, qb, kt: (0, kt[i], 0)),
            pl.BlockSpec((nh, LANES, ATT_TK), lambda i, qb, kt: (0, 0, kt[i])),
            pl.BlockSpec((ATT_TQ // LANES, ATT_TK, LANES), lambda i, qb, kt: (qb[i], kt[i], 0)),
            pl.BlockSpec(bias_tab.shape, lambda i, qb, kt: (0, 0, 0, 0)),
        ],
        out_specs=[
            pl.BlockSpec((nh, hd, ATT_TQ), lambda i, qb, kt: (0, 0, qb[i])),
            pl.BlockSpec((nh, 1, ATT_TQ), lambda i, qb, kt: (0, 0, qb[i])),
        ],
        scratch_shapes=[pltpu.VMEM((nh, LANES, ATT_TQ), BF16), pltpu.VMEM((nh, LANES, ATT_TQ), F32)],
    )
    return pl.pallas_call(
        _attn_kernel,
        grid_spec=grid_spec,
        out_shape=[jax.ShapeDtypeStruct((nh, hd, s), F32), jax.ShapeDtypeStruct((nh, 1, s), F32)],
        compiler_params=pltpu.CompilerParams(dimension_semantics=("arbitrary",), vmem_limit_bytes=VMEM_LIMIT),
        name="attn",
    )(qb_tab, kt_tab, q_t, kmax_b, k_ext, v_t, mask, bias_tab)


def _attn_safe_kernel(qb_tab, kt_tab, q_ref, k_ref, v_ref, mask_ref, bias_ref, o_ref, m_ref, acc_ref):
    step = pl.program_id(0)
    qb = qb_tab[step]
    kt = kt_tab[step]
    nq = SAFE_TQ // LANES
    nsub = ATT_TK // LANES

    @pl.when(kt == 0)
    def _():
        m_ref[...] = jnp.full(m_ref.shape, NEG, F32)
        acc_ref[...] = jnp.zeros(acc_ref.shape, F32)

    for a in range(nq):
        rows = slice(a * LANES, (a + 1) * LANES)
        q0 = qb * SAFE_TQ + a * LANES
        maskf = (mask_ref[a].astype(F32) - 1.0) * (-NEG)
        tab_idx = [jnp.clip((q0 - (kt * ATT_TK + c * LANES)) // LANES + 1, 0, 3) for c in range(nsub)]
        for h in range(N_HEADS):
            s = _dot_t(q_ref[h, rows, :], k_ref[h]) + maskf
            s = s + jnp.concatenate([bias_ref[tab_idx[c], h] for c in range(nsub)], axis=1)
            m_old = m_ref[h, rows, :]
            m_new = jnp.maximum(m_old, jnp.max(s, axis=1, keepdims=True))
            p = jnp.exp2(s - m_new[:, 0:1])
            alpha = jnp.exp2(m_old - m_new)
            acc_ref[h, rows, :] = alpha * acc_ref[h, rows, :] + _dot(p.astype(BF16), v_ref[h])
            m_ref[h, rows, :] = m_new

    @pl.when(kt == (qb * SAFE_TQ + SAFE_TQ - 1) // ATT_TK)
    def _():
        for h in range(N_HEADS):
            acc = acc_ref[h]
            o_ref[h] = acc[:, :HEAD_DIM] / acc[:, HEAD_DIM:HEAD_DIM + 1]


def _attn_safe(q, k, v_ext, mask_qk, bias_tab):
    nh, s, hd = q.shape
    qb_tab, kt_tab = _causal_pairs(s, SAFE_TQ, ATT_TK)
    grid_spec = pltpu.PrefetchScalarGridSpec(
        num_scalar_prefetch=2,
        grid=(qb_tab.shape[0],),
        in_specs=[
            pl.BlockSpec((nh, SAFE_TQ, hd), lambda i, qb, kt: (0, qb[i], 0)),
            pl.BlockSpec((nh, ATT_TK, hd), lambda i, qb, kt: (0, kt[i], 0)),
            pl.BlockSpec((nh, ATT_TK, LANES), lambda i, qb, kt: (0, kt[i], 0)),
            pl.BlockSpec((SAFE_TQ // LANES, LANES, ATT_TK), lambda i, qb, kt: (qb[i], 0, kt[i])),
            pl.BlockSpec(bias_tab.shape, lambda i, qb, kt: (0, 0, 0, 0)),
        ],
        out_specs=pl.BlockSpec((nh, SAFE_TQ, hd), lambda i, qb, kt: (0, qb[i], 0)),
        scratch_shapes=[pltpu.VMEM((nh, SAFE_TQ, LANES), F32), pltpu.VMEM((nh, SAFE_TQ, LANES), F32)],
    )
    return pl.pallas_call(
        _attn_safe_kernel,
        grid_spec=grid_spec,
        out_shape=jax.ShapeDtypeStruct((nh, s, hd), F32),
        compiler_params=pltpu.CompilerParams(dimension_semantics=("arbitrary",), vmem_limit_bytes=VMEM_LIMIT),
        name="attn_safe",
    )(qb_tab, kt_tab, q, k, v_ext, mask_qk, bias_tab)


def _tail_kernel(h_ref, cg_ref, ga_ref, at_ref, p_ref, wao_ref, wmo_ref, fg_ref, wi_ref, wo_ref, pg_ref, wpg_ref,
                 wpp_ref, o_ref):
    attn = at_ref[...].T.astype(BF16)
    merged = cg_ref[...] + ga_ref[...] * _dot(attn, wao_ref[...])
    h = h_ref[...] + _dot(merged.astype(BF16), wmo_ref[...])
    h = _half_step_ffn(h, fg_ref[...], wi_ref, wo_ref)
    gate = _sigmoid(_dot(_rms(h, pg_ref[...]).astype(BF16), wpg_ref[...]))
    o_ref[...] = h + gate * _dot(p_ref[...].astype(BF16), wpp_ref[...])


def _tail(h, conv_g, gate_a, attn_t, p, w_ao, w_mo, ffn_g, w_in, w_out, ple_g, w_pg, w_pp, tm=512):
    s, d = h.shape
    row = lambda n: pl.BlockSpec((tm, n), lambda i: (i, 0))
    fg2 = ffn_g.reshape(1, d)
    pg2 = ple_g.reshape(1, d)
    params = (w_ao, w_mo, fg2, w_in, w_out, pg2, w_pg, w_pp)
    return pl.pallas_call(
        _tail_kernel,
        grid=(s // tm,),
        in_specs=[row(d), row(d), row(d), pl.BlockSpec((attn_t.shape[0], tm), lambda i: (0, i)), row(p.shape[1])]
        + [_resident(a) for a in params],
        out_specs=row(d),
        out_shape=jax.ShapeDtypeStruct((s, d), F32),
        compiler_params=pltpu.CompilerParams(dimension_semantics=("arbitrary",), vmem_limit_bytes=VMEM_LIMIT),
        name="tail",
    )(h, conv_g, gate_a, attn_t, p, *params)


def _t5_bucket_table(n_dist):
    n = np.arange(n_dist)
    max_exact = NUM_BUCKETS // 2
    nf = np.maximum(n, 1).astype(np.float32)
    large = max_exact + (np.log(nf / max_exact) / math.log(MAX_DISTANCE / max_exact)
                         * (NUM_BUCKETS - max_exact)).astype(np.int32)
    large = np.minimum(large, NUM_BUCKETS - 1)
    return np.where(n < max_exact, n, large)


def _bias_tables(rel_bias):
    assert MAX_DISTANCE <= LANES
    n = LANES
    rel = (rel_bias.astype(F32) - rel_bias[NUM_BUCKETS - 1].astype(F32)[None, :]) * LOG2E
    by_dist = rel[_t5_bucket_table(2 * n)].T

    def toeplitz(a):
        skew = jnp.tile(a, (1, n))[:, :n * (2 * n - 1)].reshape(a.shape[0], n, 2 * n - 1)
        return skew[:, :, :n]

    d0 = toeplitz(jnp.concatenate([by_dist[:, :n], jnp.broadcast_to(by_dist[:, :1], by_dist[:, :n].shape)], axis=1))
    d1 = toeplitz(jnp.concatenate([by_dist[:, n:], by_dist[:, :n]], axis=1))
    zero = jnp.zeros_like(d0)
    return jnp.swapaxes(jnp.stack([zero, d0, d1, zero]), 2, 3)


def kernel(x, p, ffn1_norm, ffn1_w_in, ffn1_w_out, mix_norm, mix_w_in, conv_dw_w, conv_dw_b, conv_ln_g,
           conv_ln_b, conv_w_out, q_norm, k_norm, attn_w_out, mix_w_out, ffn2_norm, ffn2_w_in, ffn2_w_out,
           ple_norm, ple_w_gate, ple_w_proj, rel_bias):
    b, s, d = x.shape
    depth = ffn1_norm.shape[0]
    cw = conv_dw_w.shape[2]
    aw = N_HEADS * HEAD_DIM
    nqi = IDX_HEADS * IDX_DIM
    iw = nqi + IDX_DIM + IDX_HEADS
    iw_pad = -(-iw // LANES) * LANES
    k_top = min(TOPK_MAX, s // 4)
    assert b == 1 and s % ATT_TK == 0 and mix_w_in.shape[2] == 2 * cw + 3 * aw + iw + 2 * d

    head_blocks = jnp.asarray(np.kron(np.eye(N_HEADS), np.ones((HEAD_DIM, HEAD_DIM))), BF16)
    bias_qk = _bias_tables(rel_bias)
    bias_kq = jnp.swapaxes(bias_qk, 2, 3)

    h = x[0]
    for i in range(depth):
        h = _ffn(h, ffn1_norm[i], ffn1_w_in[i].astype(BF16), ffn1_w_out[i].astype(BF16))

        w = mix_w_in[i]
        o0 = 2 * cw
        o1 = o0 + 3 * aw
        o2 = o1 + iw
        w_idx = jnp.pad(w[:, o1:o2], ((0, 0), (0, iw_pad - iw)))
        w_idx_hi, w_idx_lo = _split_bf16(w_idx)
        qg = jnp.tile(q_norm[i], N_HEADS).reshape(1, aw)
        kg = jnp.tile(k_norm[i], N_HEADS).reshape(1, aw)
        glu, q_t, k_ext, v_t, qi_pairs, ki_ext, idx_w, gate_c, gate_a = _mix_in(
            h, mix_norm[i], w[:, :o0].astype(BF16), w[:, o0:o1].astype(BF16), w_idx_hi, w_idx_lo,
            w[:, o2:].astype(BF16), qg, kg, head_blocks)

        conv_g = _conv(glu, conv_dw_w[i], conv_dw_b[i], conv_ln_g[i], conv_ln_b[i],
                       conv_w_out[i].astype(BF16), gate_c)

        w_t = idx_w[:, IDX_DIM:IDX_DIM + IDX_HEADS].T
        mask = _select(qi_pairs, w_t, ki_ext, k_top)

        kmax = math.sqrt(HEAD_DIM) * jnp.max(jnp.abs(k_norm[i]))
        kmax_b = jnp.full((N_HEADS, 1, ATT_TQ), kmax, F32)
        attn_t, den = _attn(q_t, kmax_b, k_ext, v_t, mask, bias_kq)

        def safe_attn():
            out = _attn_safe(jnp.swapaxes(q_t, 1, 2), k_ext[:, :, :HEAD_DIM], jnp.swapaxes(v_t, 1, 2),
                             jnp.swapaxes(mask, 1, 2), bias_qk)
            return jnp.swapaxes(out, 1, 2)

        underflow = jnp.logical_not(jnp.min(den) > L_MIN)
        attn_t = lax.cond(underflow, safe_attn, lambda: attn_t).reshape(aw, s)

        h = _tail(h, conv_g, gate_a, attn_t, p[i, 0], attn_w_out[i].astype(BF16), mix_w_out[i].astype(BF16),
                  ffn2_norm[i], ffn2_w_in[i].astype(BF16), ffn2_w_out[i].astype(BF16),
                  ple_norm[i], ple_w_gate[i].astype(BF16), ple_w_proj[i].astype(BF16))
    return h[None]
```

```python
import functools
import math

import numpy as np
import jax
import jax.numpy as jnp
from jax import lax
from jax.experimental import pallas as pl
from jax.experimental.pallas import tpu as pltpu

F32 = jnp.float32
BF16 = jnp.bfloat16
I32 = jnp.int32

EPS = 1e-6
CONV_WIDTH = 31
N_HEADS = 8
HEAD_DIM = 64
IDX_HEADS = 4
IDX_DIM = 64
TOPK_MAX = 256
NUM_BUCKETS = 32
MAX_DISTANCE = 128

LANES = 128
SUBLANES = 8
MXU_TILE = 256
FFN_CHUNK = 1024
VMEM_LIMIT = 56 * 1024 * 1024
NEG = -1e30
INT_MIN = -2 ** 31
MIN_NORMAL_BITS = 0x00800000
LOG2E = math.log2(math.e)

CONV_HALO = 32
CONV_CHUNK = 64
SEL_TQ = LANES
SEL_TK = 1024
SEL_GROUPS = 8
SEL_DEPTH = 16
ATT_TQ = 1024
ATT_TK = 1024
SAFE_TQ = 256
L_MIN = 1e-30
BOUND_MARGIN = 1.02


def _sigmoid(x):
    return 1.0 / (1.0 + jnp.exp(-x))


def _rms(x, g):
    ms = jnp.mean(x * x, axis=-1, keepdims=True)
    return x * lax.rsqrt(ms + EPS) * g


def _dot(a, b):
    return jnp.dot(a, b, preferred_element_type=F32)


def _dot_t(a, b):
    return lax.dot_general(a, b, (((1,), (1,)), ((), ())), preferred_element_type=F32)


def _split_bf16(x):
    hi = x.astype(BF16)
    lo = (x - hi.astype(F32)).astype(BF16)
    return hi, lo


def _half_step_ffn(x, g, wi_ref, wo_ref):
    xn = _rms(x, g).astype(BF16)
    dff = wo_ref.shape[0]
    acc = None
    for c0 in range(0, dff, FFN_CHUNK):
        c1 = min(c0 + FFN_CHUNK, dff)
        a = _dot(xn, wi_ref[:, c0:c1])
        b = _dot(xn, wi_ref[:, dff + c0:dff + c1])
        part = _dot((a * _sigmoid(a) * b).astype(BF16), wo_ref[c0:c1, :])
        acc = part if acc is None else acc + part
    return x + 0.5 * acc


def _ffn_kernel(x_ref, g_ref, wi_ref, wo_ref, o_ref):
    o_ref[...] = _half_step_ffn(x_ref[...], g_ref[...], wi_ref, wo_ref)


def _resident(a):
    return pl.BlockSpec(a.shape, lambda *_: (0,) * a.ndim, pipeline_mode=pl.Buffered(1))


def _ffn(x, g, w_in, w_out, tm=512):
    s, d = x.shape
    dff = w_out.shape[0]
    assert s % tm == 0 and dff % MXU_TILE == 0 and FFN_CHUNK % MXU_TILE == 0
    g2 = g.reshape(1, d)
    return pl.pallas_call(
        _ffn_kernel,
        grid=(s // tm,),
        in_specs=[pl.BlockSpec((tm, d), lambda i: (i, 0)), _resident(g2), _resident(w_in), _resident(w_out)],
        out_specs=pl.BlockSpec((tm, d), lambda i: (i, 0)),
        out_shape=jax.ShapeDtypeStruct((s, d), F32),
        compiler_params=pltpu.CompilerParams(dimension_semantics=("arbitrary",), vmem_limit_bytes=VMEM_LIMIT),
        name="ffn",
    )(x, g2, w_in, w_out)


def _mix_in_kernel(h_ref, g_ref, wc_ref, wqkv_ref, wih_ref, wg_ref, qg_ref, kg_ref, hb_ref,
                   glu_ref, qt_ref, kx_ref, vt_ref, qip_ref, kix_ref, wi_ref, gc_ref, ga_ref):
    u = _rms(h_ref[...], g_ref[...])
    u_hi = u.astype(BF16)
    tm = h_ref.shape[0]
    cw = glu_ref.shape[1]
    aw = N_HEADS * HEAD_DIM
    nqi = IDX_HEADS * IDX_DIM
    d = gc_ref.shape[1]

    c = _dot(u_hi, wc_ref[...])
    glu_ref[...] = c[:, :cw] * _sigmoid(c[:, cw:])

    qkv = _dot(u_hi, wqkv_ref[...])
    hb = hb_ref[...]

    def head_sumsq(t):
        t2_hi, t2_lo = _split_bf16(t * t)
        return _dot(t2_hi, hb) + _dot(t2_lo, hb)

    def head_norm(t, g):
        return t * lax.rsqrt(head_sumsq(t) * (1.0 / HEAD_DIM) + EPS) * g

    q = head_norm(qkv[:, :aw], qg_ref[...]) * (HEAD_DIM ** -0.5 * LOG2E)
    q_t = q.T
    for hd in range(N_HEADS):
        qt_ref[hd] = q_t[hd * HEAD_DIM:(hd + 1) * HEAD_DIM, :].astype(BF16)
    k = head_norm(qkv[:, aw:2 * aw], kg_ref[...]).astype(BF16).astype(F32)
    one_hot0 = lambda shape, axis: jnp.where(lax.broadcasted_iota(I32, shape, axis) == 0, 1.0, 0.0)
    k_pad = one_hot0((tm, LANES - HEAD_DIM), 1)
    for hd in range(N_HEADS):
        kx_ref[hd] = jnp.concatenate([k[:, hd * HEAD_DIM:(hd + 1) * HEAD_DIM], k_pad], axis=1).astype(BF16)
    v_t = qkv[:, 2 * aw:].T
    v_pad = one_hot0((LANES - HEAD_DIM, tm), 0).astype(BF16)
    for hd in range(N_HEADS):
        vt_ref[hd, 0:HEAD_DIM, :] = v_t[hd * HEAD_DIM:(hd + 1) * HEAD_DIM, :].astype(BF16)
        vt_ref[hd, HEAD_DIM:, :] = v_pad

    idx = _dot(u_hi, wih_ref[...])
    qi_hi, qi_lo = _split_bf16(idx[:, :nqi].T)
    for blk in range(tm // SEL_TQ):
        cols = slice(blk * SEL_TQ, (blk + 1) * SEL_TQ)
        for hd in range(IDX_HEADS):
            rows = slice(hd * IDX_DIM, (hd + 1) * IDX_DIM)
            out_cols = slice((hd % 2) * SEL_TQ, (hd % 2 + 1) * SEL_TQ)
            for part, src in enumerate((qi_hi, qi_hi, qi_lo, qi_lo)):
                qip_ref[blk, hd // 2, part * IDX_DIM:(part + 1) * IDX_DIM, out_cols] = src[rows, cols]
    ki = idx[:, nqi:nqi + IDX_DIM]
    ki_hi = ki.astype(BF16).astype(F32)
    ki_lo = ki - ki_hi
    kix_ref[...] = jnp.concatenate([ki_hi, ki_lo, ki_hi, ki_lo], axis=1).astype(BF16)
    wi_ref[...] = idx[:, nqi:]

    gates = _sigmoid(_dot(u_hi, wg_ref[...]))
    gc_ref[...] = gates[:, :d]
    ga_ref[...] = gates[:, d:]


def _mix_in(h, g, w_conv, w_qkv, w_idx_hi, w_gate, qg, kg, head_blocks, tm=512):
    s, d = h.shape
    cw = w_conv.shape[1] // 2
    aw = w_qkv.shape[1] // 3
    nqi = IDX_HEADS * IDX_DIM
    iw = w_idx_hi.shape[1]
    assert aw == N_HEADS * HEAD_DIM and tm % SEL_TQ == 0 and iw - nqi == LANES
    full = lambda a: pl.BlockSpec(a.shape, lambda i: (0,) * a.ndim)
    row = lambda n: pl.BlockSpec((tm, n), lambda i: (i, 0))
    g2 = g.reshape(1, d)
    return pl.pallas_call(
        _mix_in_kernel,
        grid=(s // tm,),
        in_specs=[row(d), full(g2), full(w_conv), full(w_qkv), full(w_idx_hi), full(w_gate),
                  full(qg), full(kg), full(head_blocks)],
        out_specs=[
            row(cw),
            pl.BlockSpec((N_HEADS, HEAD_DIM, tm), lambda i: (0, 0, i)),
            pl.BlockSpec((N_HEADS, tm, LANES), lambda i: (0, i, 0)),
            pl.BlockSpec((N_HEADS, LANES, tm), lambda i: (0, 0, i)),
            pl.BlockSpec((tm // SEL_TQ, IDX_HEADS // 2, 4 * IDX_DIM, 2 * SEL_TQ), lambda i: (i, 0, 0, 0)),
            row(4 * IDX_DIM), row(LANES), row(d), row(d),
        ],
        out_shape=[
            jax.ShapeDtypeStruct((s, cw), F32),
            jax.ShapeDtypeStruct((N_HEADS, HEAD_DIM, s), BF16),
            jax.ShapeDtypeStruct((N_HEADS, s, LANES), BF16),
            jax.ShapeDtypeStruct((N_HEADS, LANES, s), BF16),
            jax.ShapeDtypeStruct((s // SEL_TQ, IDX_HEADS // 2, 4 * IDX_DIM, 2 * SEL_TQ), BF16),
            jax.ShapeDtypeStruct((s, 4 * IDX_DIM), BF16),
            jax.ShapeDtypeStruct((s, LANES), F32),
            jax.ShapeDtypeStruct((s, d), F32),
            jax.ShapeDtypeStruct((s, d), F32),
        ],
        compiler_params=pltpu.CompilerParams(dimension_semantics=("arbitrary",), vmem_limit_bytes=VMEM_LIMIT),
        name="mix_in",
    )(h, g2, w_conv, w_qkv, w_idx_hi, w_gate, qg, kg, head_blocks)


def _conv_kernel(z_ref, halo_ref, dw_ref, db_ref, lg_ref, lb_ref, wo_ref, gc_ref, o_ref, zp_ref, zs_ref, acc_ref):
    tm = z_ref.shape[0]
    first = pl.program_id(0) == 0
    halo = halo_ref[...]
    zp_ref[0:CONV_HALO, :] = jnp.where(first, jnp.zeros_like(halo), halo)
    zp_ref[CONV_HALO:, :] = z_ref[...]
    off = CONV_HALO - (CONV_WIDTH - 1)
    span = zs_ref.shape[1]
    for b in range(1, SUBLANES):
        zs_ref[b - 1] = zp_ref[b:b + span, :]

    def chunk(c, carry):
        r0 = pl.multiple_of(c * CONV_CHUNK, CONV_CHUNK)
        acc = jnp.zeros((CONV_CHUNK, z_ref.shape[1]), F32) + db_ref[...]
        for j in range(CONV_WIDTH):
            a, b = divmod(off + j, SUBLANES)
            src = zp_ref if b == 0 else zs_ref.at[b - 1]
            acc = acc + dw_ref[j:j + 1, :] * src[pl.ds(r0 + a * SUBLANES, CONV_CHUNK), :]
        acc_ref[pl.ds(r0, CONV_CHUNK), :] = acc
        return carry

    lax.fori_loop(0, tm // CONV_CHUNK, chunk, 0)
    acc = acc_ref[...]
    mu = jnp.mean(acc, axis=-1, keepdims=True)
    xc = acc - mu
    y = xc * lax.rsqrt(jnp.mean(xc * xc, axis=-1, keepdims=True) + EPS)
    y = y * lg_ref[...] + lb_ref[...]
    y = (y * _sigmoid(y)).astype(BF16)
    o_ref[...] = gc_ref[...] * _dot(y, wo_ref[...])


def _conv(z, dw_w, dw_b, ln_g, ln_b, w_out, gate_c, tm=512):
    s, c = z.shape
    d = w_out.shape[1]
    assert tm % CONV_HALO == 0 and tm % CONV_CHUNK == 0
    r = tm // CONV_HALO
    span = tm + CONV_HALO - SUBLANES
    full = lambda a: pl.BlockSpec(a.shape, lambda i: (0,) * a.ndim)
    vecs = [dw_b.reshape(1, c), ln_g.reshape(1, c), ln_b.reshape(1, c)]
    return pl.pallas_call(
        _conv_kernel,
        grid=(s // tm,),
        in_specs=[
            pl.BlockSpec((tm, c), lambda i: (i, 0)),
            pl.BlockSpec((CONV_HALO, c), lambda i: (jnp.maximum(i * r - 1, 0), 0)),
            full(dw_w), full(vecs[0]), full(vecs[1]), full(vecs[2]), full(w_out),
            pl.BlockSpec((tm, d), lambda i: (i, 0)),
        ],
        out_specs=pl.BlockSpec((tm, d), lambda i: (i, 0)),
        out_shape=jax.ShapeDtypeStruct((s, d), F32),
        scratch_shapes=[pltpu.VMEM((tm + CONV_HALO, c), F32), pltpu.VMEM((SUBLANES - 1, span, c), F32),
                        pltpu.VMEM((tm, c), F32)],
        compiler_params=pltpu.CompilerParams(dimension_semantics=("arbitrary",), vmem_limit_bytes=VMEM_LIMIT),
        name="conv",
    )(z, z, dw_w, *vecs, w_out, gate_c)


def _sortable_key(x):
    bits = pltpu.bitcast(x + 0.0, I32)
    return bits ^ ((bits >> 31) & 0x7FFFFFFF)


def _key_to_float(key):
    return pltpu.bitcast(key ^ ((key >> 31) & 0x7FFFFFFF), F32)


def _sort_network(n):
    pairs = []

    def merge(lo, hi, r):
        step = r * 2
        if step < hi - lo:
            merge(lo, hi, step)
            merge(lo + r, hi, step)
            pairs.extend((i, i + r) for i in range(lo + r, hi - r, step))
        else:
            pairs.append((lo, lo + r))

    def sort(lo, hi):
        if hi > lo:
            mid = lo + (hi - lo) // 2
            sort(lo, mid)
            sort(mid + 1, hi)
            merge(lo, hi, 1)

    sort(0, n - 1)
    return pairs


def _compare_exchange(a, i, j):
    a[i], a[j] = jnp.maximum(a[i], a[j]), jnp.minimum(a[i], a[j])


def _merge_top(top, batch):
    n = len(top)
    out = [jnp.maximum(top[i], batch[n - 1 - i]) for i in range(n)]
    d = n // 2
    while d >= 1:
        for i in range(n):
            if i & d == 0:
                _compare_exchange(out, i, i + d)
        d //= 2
    return out


def _select_kernel(qi_ref, w_ref, ki_ref, mask_ref, sc_ref, cand_ref, ckey_ref, st_ref, *, k_top):
    qb = pl.program_id(0)
    n_rows = mask_ref.shape[1]
    n_tiles = n_rows // SEL_TK
    q0 = qb * SEL_TQ
    nkt = (q0 + SEL_TQ - 1) // SEL_TK + 1
    vt = SEL_TK // SUBLANES
    vshape = (SUBLANES, LANES)
    qpos = q0 + lax.broadcasted_iota(I32, vshape, 1)
    sub = lax.broadcasted_iota(I32, vshape, 0)
    w = w_ref[...] * ((IDX_HEADS ** -0.5) * (IDX_DIM ** -0.5))

    qpos_t = q0 + lax.broadcasted_iota(I32, (SEL_TK, LANES), 1)
    krow_t = lax.broadcasted_iota(I32, (SEL_TK, LANES), 0)
    zero_bits_t = (MIN_NORMAL_BITS + n_rows) - krow_t

    def score_tile(kt):
        k0 = pl.multiple_of(kt * SEL_TK, SEL_TK)
        ki = ki_ref[pl.ds(k0, SEL_TK), :]
        sc = None
        for pair in range(IDX_HEADS // 2):
            r = _dot(ki, qi_ref[0, pair])
            for j in range(2):
                h = 2 * pair + j
                term = w[h:h + 1, :] * jnp.maximum(r[:, j * LANES:(j + 1) * LANES], 0.0)
                sc = term if sc is None else sc + term
        sc_ref[pl.ds(k0, SEL_TK), :] = jnp.where(sc == 0.0, pltpu.bitcast(zero_bits_t - k0, F32), sc)

    def vreg(ref, row):
        return ref[pl.ds(pl.multiple_of(row, SUBLANES), SUBLANES), :]

    def col_sum(parts):
        tot = parts[0]
        for part in parts[1:]:
            tot = tot + part
        return jnp.broadcast_to(jnp.sum(tot, axis=0, keepdims=True), vshape)

    n_cand = SEL_GROUPS * SEL_DEPTH
    network = _sort_network(SEL_DEPTH)
    cand_ref[...] = jnp.full(cand_ref.shape, -jnp.inf, F32)

    def lists_tile(kt):
        for g in range(SEL_GROUPS):
            batch = [vreg(sc_ref, kt * SEL_TK + (g * SEL_DEPTH + j) * SUBLANES) for j in range(SEL_DEPTH)]
            for i, j in network:
                _compare_exchange(batch, i, j)
            rows = [slice((g * SEL_DEPTH + i) * SUBLANES, (g * SEL_DEPTH + i + 1) * SUBLANES)
                    for i in range(SEL_DEPTH)]
            top = _merge_top([cand_ref[r, :] for r in rows], batch)
            for r, t in zip(rows, top):
                cand_ref[r, :] = t

    sc_ref[pl.ds(n_rows, SEL_TK), :] = jnp.full((SEL_TK, LANES), -jnp.inf, F32)

    def score_and_lists(kt, carry):
        lists_tile(jnp.where(kt == 0, n_tiles, kt - 1))
        score_tile(kt)
        return carry

    lax.fori_loop(0, nkt, score_and_lists, 0)
    k_last = pl.multiple_of((nkt - 1) * SEL_TK, SEL_TK)
    sc_ref[pl.ds(k_last, SEL_TK), :] = jnp.where(
        k_last + krow_t <= qpos_t, sc_ref[pl.ds(k_last, SEL_TK), :], -jnp.inf)
    lists_tile(nkt - 1)
    ckey_ref[...] = _sortable_key(cand_ref[...])

    n_acc = 4

    def cand_bit(i, t):
        c = t ^ lax.shift_left(jnp.int32(1), 31 - i)
        acc = [jnp.zeros(vshape, I32) for _ in range(n_acc)]
        for v in range(n_cand):
            x = ckey_ref[v * SUBLANES:(v + 1) * SUBLANES, :]
            acc[v % n_acc] = acc[v % n_acc] + jnp.where(x >= c, 1, 0)
        return jnp.where(col_sum(acc) >= k_top, c, t)

    st_ref[0] = lax.fori_loop(0, 32, cand_bit, jnp.full(vshape, INT_MIN, I32))

    def count_scores(preds):
        def tile(kt, acc):
            acc = [list(a) for a in acc]
            for v in range(vt):
                x = vreg(sc_ref, kt * SEL_TK + v * SUBLANES)
                kpos = kt * SEL_TK + v * SUBLANES + sub
                for p, pred in enumerate(preds):
                    acc[p][v % n_acc] = acc[p][v % n_acc] + jnp.where(pred(x, kpos), 1, 0)
            return tuple(tuple(a) for a in acc)

        zero = tuple(tuple(jnp.zeros(vshape, I32) for _ in range(n_acc)) for _ in preds)
        return [col_sum(list(a)) for a in lax.fori_loop(0, nkt, tile, zero)]

    def count_around_threshold():
        tf = _key_to_float(st_ref[0])
        gt, ge = count_scores([lambda x, kpos: x > tf, lambda x, kpos: x >= tf])
        st_ref[1] = gt
        st_ref[2] = ge

    tf_fast = jnp.broadcast_to(_key_to_float(st_ref[0])[0:1, :], (SEL_TK, LANES))

    def fast_tile(kt, acc):
        k0 = pl.multiple_of(kt * SEL_TK, SEL_TK)
        m = jnp.where(sc_ref[pl.ds(k0, SEL_TK), :] >= tf_fast, 1.0, 0.0)
        mask_ref[0, pl.ds(k0, SEL_TK), :] = m.astype(BF16)
        acc = list(acc)
        for v in range(vt):
            acc[v % n_acc] = acc[v % n_acc] + m[v * SUBLANES:(v + 1) * SUBLANES, :]
        return tuple(acc)

    size = col_sum(list(lax.fori_loop(0, nkt, fast_tile, tuple(jnp.zeros(vshape, F32) for _ in range(n_acc)))))

    @pl.when(jnp.max(jnp.abs(size - k_top)) > 0.0)
    def _():
        count_around_threshold()

        @pl.when(jnp.max(st_ref[1]) >= k_top)
        def _():
            def full_bit(i, t):
                c = t ^ lax.shift_left(jnp.int32(1), 31 - i)
                cnt, = count_scores([lambda x, kpos: _sortable_key(x) >= c])
                return jnp.where(cnt >= k_top, c, t)

            st_ref[0] = lax.fori_loop(0, 32, full_bit, jnp.full(vshape, INT_MIN, I32))
            count_around_threshold()

        st_ref[3] = jnp.full(vshape, n_rows, I32)

        @pl.when(jnp.max(st_ref[2]) > k_top)
        def _():
            tf = _key_to_float(st_ref[0])
            need = k_top - st_ref[1]
            n_bits = n_rows.bit_length() - 1

            def pos_bit(i, p):
                c = p + lax.shift_left(jnp.int32(1), n_bits - 1 - i)
                f, = count_scores([lambda x, kpos: (x == tf) & (kpos < c)])
                return jnp.where(f < need, c, p)

            st_ref[3] = lax.fori_loop(0, n_bits, pos_bit, jnp.zeros(vshape, I32))

        tf_t = jnp.broadcast_to(_key_to_float(st_ref[0])[0:1, :], (SEL_TK, LANES))
        lim_t = jnp.broadcast_to(st_ref[3][0:1, :], (SEL_TK, LANES))

        def write_tile(kt, carry):
            k0 = pl.multiple_of(kt * SEL_TK, SEL_TK)
            x = sc_ref[pl.ds(k0, SEL_TK), :]
            kpos = k0 + krow_t
            sel = ((x > tf_t) | ((x == tf_t) & (kpos <= lim_t))) & (kpos <= qpos_t)
            mask_ref[0, pl.ds(k0, SEL_TK), :] = jnp.where(sel, 1.0, 0.0).astype(BF16)
            return carry

        lax.fori_loop(0, nkt, write_tile, 0)

    def fill_tile(kt, carry):
        mask_ref[0, pl.ds(pl.multiple_of(kt * SEL_TK, SEL_TK), SEL_TK), :] = jnp.zeros((SEL_TK, LANES), BF16)
        return carry

    lax.fori_loop(nkt, n_tiles, fill_tile, 0)


def _select(qi_pairs, w_t, ki_ext, k_top):
    nqb, npair, kw, _ = qi_pairs.shape
    s = ki_ext.shape[0]
    n_cand_rows = SEL_GROUPS * SEL_DEPTH * SUBLANES
    assert s % SEL_TK == 0 and SEL_TK == n_cand_rows and n_cand_rows >= k_top
    return pl.pallas_call(
        functools.partial(_select_kernel, k_top=k_top),
        grid=(nqb,),
        in_specs=[
            pl.BlockSpec((1, npair, kw, 2 * SEL_TQ), lambda i: (i, 0, 0, 0)),
            pl.BlockSpec((w_t.shape[0], SEL_TQ), lambda i: (0, i)),
            pl.BlockSpec((s, kw), lambda i: (0, 0)),
        ],
        out_specs=pl.BlockSpec((1, s, SEL_TQ), lambda i: (i, 0, 0)),
        out_shape=jax.ShapeDtypeStruct((nqb, s, SEL_TQ), BF16),
        scratch_shapes=[
            pltpu.VMEM((s + SEL_TK, SEL_TQ), F32),
            pltpu.VMEM((n_cand_rows, SEL_TQ), F32),
            pltpu.VMEM((n_cand_rows, SEL_TQ), I32),
            pltpu.VMEM((4, SUBLANES, LANES), I32),
        ],
        compiler_params=pltpu.CompilerParams(dimension_semantics=("arbitrary",), vmem_limit_bytes=VMEM_LIMIT),
        name="select",
    )(qi_pairs, w_t, ki_ext)


def _causal_pairs(s, tq, tk):
    pairs = [(qb, kt) for qb in range(s // tq) for kt in range((qb * tq + tq - 1) // tk + 1)]
    return jnp.asarray([p[0] for p in pairs], I32), jnp.asarray([p[1] for p in pairs], I32)


def _attn_kernel(qb_tab, kt_tab, qt_ref, kmax_ref, k_ref, vt_ref, mask_ref, bias_ref, o_ref, l_ref, qx_ref, acc_ref):
    step = pl.program_id(0)
    qb = qb_tab[step]
    kt = kt_tab[step]
    q0 = qb * ATT_TQ
    k0 = kt * ATT_TK
    nq = ATT_TQ // LANES
    nk = ATT_TK // LANES

    @pl.when(kt == 0)
    def _():
        row = lax.broadcasted_iota(I32, (HEAD_DIM, ATT_TQ), 0)
        for h in range(N_HEADS):
            q = qt_ref[h]
            qf = q.astype(F32)
            bound = jnp.sqrt(jnp.sum(qf * qf, axis=0, keepdims=True)) * kmax_ref[h] * BOUND_MARGIN
            qx_ref[h, 0:HEAD_DIM, :] = q
            qx_ref[h, HEAD_DIM:, :] = jnp.where(row == 0, -bound, 0.0).astype(BF16)
        acc_ref[...] = jnp.zeros(acc_ref.shape, F32)

    mask = jnp.concatenate([mask_ref[a] for a in range(nq)], axis=1)

    def heads(with_bias):
        for h in range(N_HEADS):
            s = _dot(k_ref[h], qx_ref[h])
            if with_bias:
                rows = []
                for c in range(nk):
                    tab = [jnp.clip((q0 + a * LANES - k0 - c * LANES) // LANES + 1, 0, 3) for a in range(nq)]
                    rows.append(jnp.concatenate([bias_ref[tab[a], h] for a in range(nq)], axis=1))
                s = s + jnp.concatenate(rows, axis=0)
            p = jnp.exp2(s).astype(BF16) * mask
            acc_ref[h] += _dot(vt_ref[h], p)

    near = k0 + ATT_TK + 2 * LANES > q0

    @pl.when(near)
    def _():
        heads(True)

    @pl.when(jnp.logical_not(near))
    def _():
        heads(False)

    @pl.when(kt == (q0 + ATT_TQ - 1) // ATT_TK)
    def _():
        for h in range(N_HEADS):
            acc = acc_ref[h]
            den = acc[HEAD_DIM:HEAD_DIM + 1, :]
            o_ref[h] = acc[:HEAD_DIM, :] / den
            l_ref[h] = den


def _attn(q_t, kmax_b, k_ext, v_t, mask, bias_tab):
    nh, hd, s = q_t.shape
    qb_tab, kt_tab = _causal_pairs(s, ATT_TQ, ATT_TK)
    grid_spec = pltpu.PrefetchScalarGridSpec(
        num_scalar_prefetch=2,
        grid=(qb_tab.shape[0],),
        in_specs=[
            pl.BlockSpec((nh, hd, ATT_TQ), lambda i, qb, kt: (0, 0, qb[i])),
            pl.BlockSpec((nh, 1, ATT_TQ), lambda i, qb, kt: (0, 0, 0)),
            pl.BlockSpec((nh, ATT_TK, LANES), lambda i, qb, kt: (0, kt[i], 0)),
            pl.BlockSpec((nh, LANES, ATT_TK), lambda i, qb, kt: (0, 0, kt[i])),
            pl.BlockSpec((ATT_TQ // LANES, ATT_TK, LANES), lambda i, qb, kt: (qb[i], kt[i], 0)),
            pl.BlockSpec(bias_tab.shape, lambda i, qb, kt: (0, 0, 0, 0)),
        ],
        out_specs=[
            pl.BlockSpec((nh, hd, ATT_TQ), lambda i, qb, kt: (0, 0, qb[i])),
            pl.BlockSpec((nh, 1, ATT_TQ), lambda i, qb, kt: (0, 0, qb[i])),
        ],
        scratch_shapes=[pltpu.VMEM((nh, LANES, ATT_TQ), BF16), pltpu.VMEM((nh, LANES, ATT_TQ), F32)],
    )
    return pl.pallas_call(
        _attn_kernel,
        grid_spec=grid_spec,
        out_shape=[jax.ShapeDtypeStruct((nh, hd, s), F32), jax.ShapeDtypeStruct((nh, 1, s), F32)],
        compiler_params=pltpu.CompilerParams(dimension_semantics=("arbitrary",), vmem_limit_bytes=VMEM_LIMIT),
        name="attn",
    )(qb_tab, kt_tab, q_t, kmax_b, k_ext, v_t, mask, bias_tab)


def _attn_safe_kernel(qb_tab, kt_tab, q_ref, k_ref, v_ref, mask_ref, bias_ref, o_ref, m_ref, acc_ref):
    step = pl.program_id(0)
    qb = qb_tab[step]
    kt = kt_tab[step]
    nq = SAFE_TQ // LANES
    nsub = ATT_TK // LANES

    @pl.when(kt == 0)
    def _():
        m_ref[...] = jnp.full(m_ref.shape, NEG, F32)
        acc_ref[...] = jnp.zeros(acc_ref.shape, F32)

    for a in range(nq):
        rows = slice(a * LANES, (a + 1) * LANES)
        q0 = qb * SAFE_TQ + a * LANES
        maskf = (mask_ref[a].astype(F32) - 1.0) * (-NEG)
        tab_idx = [jnp.clip((q0 - (kt * ATT_TK + c * LANES)) // LANES + 1, 0, 3) for c in range(nsub)]
        for h in range(N_HEADS):
            s = _dot_t(q_ref[h, rows, :], k_ref[h]) + maskf
            s = s + jnp.concatenate([bias_ref[tab_idx[c], h] for c in range(nsub)], axis=1)
            m_old = m_ref[h, rows, :]
            m_new = jnp.maximum(m_old, jnp.max(s, axis=1, keepdims=True))
            p = jnp.exp2(s - m_new[:, 0:1])
            alpha = jnp.exp2(m_old - m_new)
            acc_ref[h, rows, :] = alpha * acc_ref[h, rows, :] + _dot(p.astype(BF16), v_ref[h])
            m_ref[h, rows, :] = m_new

    @pl.when(kt == (qb * SAFE_TQ + SAFE_TQ - 1) // ATT_TK)
    def _():
        for h in range(N_HEADS):
            acc = acc_ref[h]
            o_ref[h] = acc[:, :HEAD_DIM] / acc[:, HEAD_DIM:HEAD_DIM + 1]


def _attn_safe(q, k, v_ext, mask_qk, bias_tab):
    nh, s, hd = q.shape
    qb_tab, kt_tab = _causal_pairs(s, SAFE_TQ, ATT_TK)
    grid_spec = pltpu.PrefetchScalarGridSpec(
        num_scalar_prefetch=2,
        grid=(qb_tab.shape[0],),
        in_specs=[
            pl.BlockSpec((nh, SAFE_TQ, hd), lambda i, qb, kt: (0, qb[i], 0)),
            pl.BlockSpec((nh, ATT_TK, hd), lambda i, qb, kt: (0, kt[i], 0)),
            pl.BlockSpec((nh, ATT_TK, LANES), lambda i, qb, kt: (0, kt[i], 0)),
            pl.BlockSpec((SAFE_TQ // LANES, LANES, ATT_TK), lambda i, qb, kt: (qb[i], 0, kt[i])),
            pl.BlockSpec(bias_tab.shape, lambda i, qb, kt: (0, 0, 0, 0)),
        ],
        out_specs=pl.BlockSpec((nh, SAFE_TQ, hd), lambda i, qb, kt: (0, qb[i], 0)),
        scratch_shapes=[pltpu.VMEM((nh, SAFE_TQ, LANES), F32), pltpu.VMEM((nh, SAFE_TQ, LANES), F32)],
    )
    return pl.pallas_call(
        _attn_safe_kernel,
        grid_spec=grid_spec,
        out_shape=jax.ShapeDtypeStruct((nh, s, hd), F32),
        compiler_params=pltpu.CompilerParams(dimension_semantics=("arbitrary",), vmem_limit_bytes=VMEM_LIMIT),
        name="attn_safe",
    )(qb_tab, kt_tab, q, k, v_ext, mask_qk, bias_tab)


def _tail_kernel(h_ref, cg_ref, ga_ref, at_ref, p_ref, wao_ref, wmo_ref, fg_ref, wi_ref, wo_ref, pg_ref, wpg_ref,
                 wpp_ref, o_ref):
    attn = at_ref[...].T.astype(BF16)
    merged = cg_ref[...] + ga_ref[...] * _dot(attn, wao_ref[...])
    h = h_ref[...] + _dot(merged.astype(BF16), wmo_ref[...])
    h = _half_step_ffn(h, fg_ref[...], wi_ref, wo_ref)
    gate = _sigmoid(_dot(_rms(h, pg_ref[...]).astype(BF16), wpg_ref[...]))
    o_ref[...] = h + gate * _dot(p_ref[...].astype(BF16), wpp_ref[...])


def _tail(h, conv_g, gate_a, attn_t, p, w_ao, w_mo, ffn_g, w_in, w_out, ple_g, w_pg, w_pp, tm=512):
    s, d = h.shape
    row = lambda n: pl.BlockSpec((tm, n), lambda i: (i, 0))
    fg2 = ffn_g.reshape(1, d)
    pg2 = ple_g.reshape(1, d)
    params = (w_ao, w_mo, fg2, w_in, w_out, pg2, w_pg, w_pp)
    return pl.pallas_call(
        _tail_kernel,
        grid=(s // tm,),
        in_specs=[row(d), row(d), row(d), pl.BlockSpec((attn_t.shape[0], tm), lambda i: (0, i)), row(p.shape[1])]
        + [_resident(a) for a in params],
        out_specs=row(d),
        out_shape=jax.ShapeDtypeStruct((s, d), F32),
        compiler_params=pltpu.CompilerParams(dimension_semantics=("arbitrary",), vmem_limit_bytes=VMEM_LIMIT),
        name="tail",
    )(h, conv_g, gate_a, attn_t, p, *params)


def _t5_bucket_table(n_dist):
    n = np.arange(n_dist)
    max_exact = NUM_BUCKETS // 2
    nf = np.maximum(n, 1).astype(np.float32)
    large = max_exact + (np.log(nf / max_exact) / math.log(MAX_DISTANCE / max_exact)
                         * (NUM_BUCKETS - max_exact)).astype(np.int32)
    large = np.minimum(large, NUM_BUCKETS - 1)
    return np.where(n < max_exact, n, large)


def _bias_tables(rel_bias):
    assert MAX_DISTANCE <= LANES
    n = LANES
    rel = (rel_bias.astype(F32) - rel_bias[NUM_BUCKETS - 1].astype(F32)[None, :]) * LOG2E
    by_dist = rel[_t5_bucket_table(2 * n)].T

    def toeplitz(a):
        skew = jnp.tile(a, (1, n))[:, :n * (2 * n - 1)].reshape(a.shape[0], n, 2 * n - 1)
        return skew[:, :, :n]

    d0 = toeplitz(jnp.concatenate([by_dist[:, :n], jnp.broadcast_to(by_dist[:, :1], by_dist[:, :n].shape)], axis=1))
    d1 = toeplitz(jnp.concatenate([by_dist[:, n:], by_dist[:, :n]], axis=1))
    zero = jnp.zeros_like(d0)
    return jnp.swapaxes(jnp.stack([zero, d0, d1, zero]), 2, 3)


def kernel(x, p, ffn1_norm, ffn1_w_in, ffn1_w_out, mix_norm, mix_w_in, conv_dw_w, conv_dw_b, conv_ln_g,
           conv_ln_b, conv_w_out, q_norm, k_norm, attn_w_out, mix_w_out, ffn2_norm, ffn2_w_in, ffn2_w_out,
           ple_norm, ple_w_gate, ple_w_proj, rel_bias):
    b, s, d = x.shape
    depth = ffn1_norm.shape[0]
    cw = conv_dw_w.shape[2]
    aw = N_HEADS * HEAD_DIM
    nqi = IDX_HEADS * IDX_DIM
    iw = nqi + IDX_DIM + IDX_HEADS
    iw_pad = -(-iw // LANES) * LANES
    k_top = min(TOPK_MAX, s // 4)
    assert b == 1 and s % ATT_TK == 0 and mix_w_in.shape[2] == 2 * cw + 3 * aw + iw + 2 * d

    head_blocks = jnp.asarray(np.kron(np.eye(N_HEADS), np.ones((HEAD_DIM, HEAD_DIM))), BF16)
    bias_qk = _bias_tables(rel_bias)
    bias_kq = jnp.swapaxes(bias_qk, 2, 3)

    h = x[0]
    for i in range(depth):
        h = _ffn(h, ffn1_norm[i], ffn1_w_in[i].astype(BF16), ffn1_w_out[i].astype(BF16))

        w = mix_w_in[i]
        o0 = 2 * cw
        o1 = o0 + 3 * aw
        o2 = o1 + iw
        w_idx = jnp.pad(w[:, o1:o2], ((0, 0), (0, iw_pad - iw)))
        w_idx_hi = w_idx.astype(BF16)
        qg = jnp.tile(q_norm[i], N_HEADS).reshape(1, aw)
        kg = jnp.tile(k_norm[i], N_HEADS).reshape(1, aw)
        glu, q_t, k_ext, v_t, qi_pairs, ki_ext, idx_w, gate_c, gate_a = _mix_in(
            h, mix_norm[i], w[:, :o0].astype(BF16), w[:, o0:o1].astype(BF16), w_idx_hi,
            w[:, o2:].astype(BF16), qg, kg, head_blocks)

        conv_g = _conv(glu, conv_dw_w[i], conv_dw_b[i], conv_ln_g[i], conv_ln_b[i],
                       conv_w_out[i].astype(BF16), gate_c)

        w_t = idx_w[:, IDX_DIM:IDX_DIM + IDX_HEADS].T
        mask = _select(qi_pairs, w_t, ki_ext, k_top)

        kmax = math.sqrt(HEAD_DIM) * jnp.max(jnp.abs(k_norm[i]))
        kmax_b = jnp.full((N_HEADS, 1, ATT_TQ), kmax, F32)
        attn_t, den = _attn(q_t, kmax_b, k_ext, v_t, mask, bias_kq)

        def safe_attn():
            out = _attn_safe(jnp.swapaxes(q_t, 1, 2), k_ext[:, :, :HEAD_DIM], jnp.swapaxes(v_t, 1, 2),
                             jnp.swapaxes(mask, 1, 2), bias_qk)
            return jnp.swapaxes(out, 1, 2)

        underflow = jnp.logical_not(jnp.min(den) > L_MIN)
        attn_t = lax.cond(underflow, safe_attn, lambda: attn_t).reshape(aw, s)

        h = _tail(h, conv_g, gate_a, attn_t, p[i, 0], attn_w_out[i].astype(BF16), mix_w_out[i].astype(BF16),
                  ffn2_norm[i], ffn2_w_in[i].astype(BF16), ffn2_w_out[i].astype(BF16),
                  ple_norm[i], ple_w_gate[i].astype(BF16), ple_w_proj[i].astype(BF16))
    return h[None]
```

```python
import functools
import math

import numpy as np
import jax
import jax.numpy as jnp
from jax import lax
from jax.experimental import pallas as pl
from jax.experimental.pallas import tpu as pltpu

F32 = jnp.float32
BF16 = jnp.bfloat16
I32 = jnp.int32

EPS = 1e-6
CONV_WIDTH = 31
N_HEADS = 8
HEAD_DIM = 64
IDX_HEADS = 4
IDX_DIM = 64
TOPK_MAX = 256
NUM_BUCKETS = 32
MAX_DISTANCE = 128

LANES = 128
SUBLANES = 8
MXU_TILE = 256
FFN_CHUNK = 1024
VMEM_LIMIT = 56 * 1024 * 1024
NEG = -1e30
INT_MIN = -2 ** 31
MIN_NORMAL_BITS = 0x00800000
LOG2E = math.log2(math.e)

CONV_HALO = 32
CONV_CHUNK = 64
SEL_TQ = LANES
SEL_TK = 1024
SEL_GROUPS = 8
SEL_DEPTH = 16
ATT_TQ = 1024
ATT_TK = 1024
SAFE_TQ = 256
L_MIN = 1e-30
BOUND_MARGIN = 1.02


def _sigmoid(x):
    return 1.0 / (1.0 + jnp.exp(-x))


def _rms(x, g):
    ms = jnp.mean(x * x, axis=-1, keepdims=True)
    return x * lax.rsqrt(ms + EPS) * g


def _dot(a, b):
    return jnp.dot(a, b, preferred_element_type=F32)


def _dot_t(a, b):
    return lax.dot_general(a, b, (((1,), (1,)), ((), ())), preferred_element_type=F32)


def _split_bf16(x):
    hi = x.astype(BF16)
    lo = (x - hi.astype(F32)).astype(BF16)
    return hi, lo


def _half_step_ffn(x, g, wi_ref, wo_ref):
    xn = _rms(x, g).astype(BF16)
    dff = wo_ref.shape[0]
    acc = None
    for c0 in range(0, dff, FFN_CHUNK):
        c1 = min(c0 + FFN_CHUNK, dff)
        a = _dot(xn, wi_ref[:, c0:c1])
        b = _dot(xn, wi_ref[:, dff + c0:dff + c1])
        part = _dot((a * _sigmoid(a) * b).astype(BF16), wo_ref[c0:c1, :])
        acc = part if acc is None else acc + part
    return x + 0.5 * acc


def _ffn_kernel(x_ref, g_ref, wi_ref, wo_ref, o_ref):
    o_ref[...] = _half_step_ffn(x_ref[...], g_ref[...], wi_ref, wo_ref)


def _resident(a):
    return pl.BlockSpec(a.shape, lambda *_: (0,) * a.ndim, pipeline_mode=pl.Buffered(1))


def _ffn(x, g, w_in, w_out, tm=512):
    s, d = x.shape
    dff = w_out.shape[0]
    assert s % tm == 0 and dff % MXU_TILE == 0 and FFN_CHUNK % MXU_TILE == 0
    g2 = g.reshape(1, d)
    return pl.pallas_call(
        _ffn_kernel,
        grid=(s // tm,),
        in_specs=[pl.BlockSpec((tm, d), lambda i: (i, 0)), _resident(g2), _resident(w_in), _resident(w_out)],
        out_specs=pl.BlockSpec((tm, d), lambda i: (i, 0)),
        out_shape=jax.ShapeDtypeStruct((s, d), F32),
        compiler_params=pltpu.CompilerParams(dimension_semantics=("arbitrary",), vmem_limit_bytes=VMEM_LIMIT),
        name="ffn",
    )(x, g2, w_in, w_out)


def _mix_in_kernel(h_ref, g_ref, wc_ref, wqkv_ref, wih_ref, wg_ref, qg_ref, kg_ref, hb_ref,
                   dw_ref, db_ref, lg_ref, lb_ref, cwo_ref,
                   qt_ref, kx_ref, vt_ref, qip_ref, kix_ref, wi_ref, cg_ref, ga_ref, zp_ref, zs_ref, acc_ref):
    u = _rms(h_ref[...], g_ref[...])
    u_hi = u.astype(BF16)
    tm = h_ref.shape[0]
    cw = wc_ref.shape[1] // 2
    aw = N_HEADS * HEAD_DIM
    nqi = IDX_HEADS * IDX_DIM
    d = ga_ref.shape[1]

    c = _dot(u_hi, wc_ref[...])
    @pl.when(pl.program_id(0) == 0)
    def _():
        zp_ref[0:CONV_HALO, :] = jnp.zeros((CONV_HALO, cw), F32)

    @pl.when(pl.program_id(0) > 0)
    def _():
        zp_ref[0:CONV_HALO, :] = zp_ref[tm:tm + CONV_HALO, :]

    zp_ref[CONV_HALO:, :] = c[:, :cw] * _sigmoid(c[:, cw:])

    qkv = _dot(u_hi, wqkv_ref[...])
    hb = hb_ref[...]

    def head_sumsq(t):
        t2_hi, t2_lo = _split_bf16(t * t)
        return _dot(t2_hi, hb) + _dot(t2_lo, hb)

    def head_norm(t, g):
        return t * lax.rsqrt(head_sumsq(t) * (1.0 / HEAD_DIM) + EPS) * g

    q = head_norm(qkv[:, :aw], qg_ref[...]) * (HEAD_DIM ** -0.5 * LOG2E)
    q_t = q.T
    for hd in range(N_HEADS):
        qt_ref[hd] = q_t[hd * HEAD_DIM:(hd + 1) * HEAD_DIM, :].astype(BF16)
    k = head_norm(qkv[:, aw:2 * aw], kg_ref[...]).astype(BF16).astype(F32)
    one_hot0 = lambda shape, axis: jnp.where(lax.broadcasted_iota(I32, shape, axis) == 0, 1.0, 0.0)
    k_pad = one_hot0((tm, LANES - HEAD_DIM), 1)
    for hd in range(N_HEADS):
        kx_ref[hd] = jnp.concatenate([k[:, hd * HEAD_DIM:(hd + 1) * HEAD_DIM], k_pad], axis=1).astype(BF16)
    v_t = qkv[:, 2 * aw:].T
    v_pad = one_hot0((LANES - HEAD_DIM, tm), 0).astype(BF16)
    for hd in range(N_HEADS):
        vt_ref[hd, 0:HEAD_DIM, :] = v_t[hd * HEAD_DIM:(hd + 1) * HEAD_DIM, :].astype(BF16)
        vt_ref[hd, HEAD_DIM:, :] = v_pad

    idx = _dot(u_hi, wih_ref[...])
    qi_hi, qi_lo = _split_bf16(idx[:, :nqi].T)
    for blk in range(tm // SEL_TQ):
        cols = slice(blk * SEL_TQ, (blk + 1) * SEL_TQ)
        for hd in range(IDX_HEADS):
            rows = slice(hd * IDX_DIM, (hd + 1) * IDX_DIM)
            out_cols = slice((hd % 2) * SEL_TQ, (hd % 2 + 1) * SEL_TQ)
            for part, src in enumerate((qi_hi, qi_hi, qi_lo, qi_lo)):
                qip_ref[blk, hd // 2, part * IDX_DIM:(part + 1) * IDX_DIM, out_cols] = src[rows, cols]
    ki = idx[:, nqi:nqi + IDX_DIM]
    ki_hi = ki.astype(BF16).astype(F32)
    ki_lo = ki - ki_hi
    kix_ref[...] = jnp.concatenate([ki_hi, ki_lo, ki_hi, ki_lo], axis=1).astype(BF16)
    wi_ref[...] = idx[:, nqi:]

    gates = _sigmoid(_dot(u_hi, wg_ref[...]))
    ga_ref[...] = gates[:, d:]

    off = CONV_HALO - (CONV_WIDTH - 1)
    span = zs_ref.shape[1]
    for b in range(1, SUBLANES):
        zs_ref[b - 1] = zp_ref[b:b + span, :]
    for r0 in range(0, tm, CONV_CHUNK):
        acc = jnp.zeros((CONV_CHUNK, cw), F32) + db_ref[...]
        for j in range(CONV_WIDTH):
            a, b = divmod(off + j, SUBLANES)
            src = zp_ref if b == 0 else zs_ref.at[b - 1]
            acc = acc + dw_ref[j:j + 1, :] * src[r0 + a * SUBLANES:r0 + a * SUBLANES + CONV_CHUNK, :]
        acc_ref[r0:r0 + CONV_CHUNK, :] = acc
    acc = acc_ref[...]
    mu = jnp.mean(acc, axis=-1, keepdims=True)
    xc = acc - mu
    y = xc * lax.rsqrt(jnp.mean(xc * xc, axis=-1, keepdims=True) + EPS)
    y = y * lg_ref[...] + lb_ref[...]
    cg_ref[...] = gates[:, :d] * _dot((y * _sigmoid(y)).astype(BF16), cwo_ref[...])


def _mix_in(h, g, w_conv, w_qkv, w_idx_hi, w_gate, qg, kg, head_blocks, dw_w, dw_b, ln_g, ln_b, cw_out, tm=512):
    s, d = h.shape
    cw = w_conv.shape[1] // 2
    conv_params = (dw_w, dw_b.reshape(1, cw), ln_g.reshape(1, cw), ln_b.reshape(1, cw), cw_out)
    span = tm + CONV_HALO - SUBLANES
    assert tm % CONV_CHUNK == 0
    aw = w_qkv.shape[1] // 3
    nqi = IDX_HEADS * IDX_DIM
    iw = w_idx_hi.shape[1]
    assert aw == N_HEADS * HEAD_DIM and tm % SEL_TQ == 0 and iw - nqi == LANES
    full = lambda a: pl.BlockSpec(a.shape, lambda i: (0,) * a.ndim)
    row = lambda n: pl.BlockSpec((tm, n), lambda i: (i, 0))
    g2 = g.reshape(1, d)
    return pl.pallas_call(
        _mix_in_kernel,
        grid=(s // tm,),
        in_specs=[row(d), full(g2), full(w_conv), full(w_qkv), full(w_idx_hi), full(w_gate),
                  full(qg), full(kg), full(head_blocks)] + [full(a) for a in conv_params],
        out_specs=[
            pl.BlockSpec((N_HEADS, HEAD_DIM, tm), lambda i: (0, 0, i)),
            pl.BlockSpec((N_HEADS, tm, LANES), lambda i: (0, i, 0)),
            pl.BlockSpec((N_HEADS, LANES, tm), lambda i: (0, 0, i)),
            pl.BlockSpec((tm // SEL_TQ, IDX_HEADS // 2, 4 * IDX_DIM, 2 * SEL_TQ), lambda i: (i, 0, 0, 0)),
            row(4 * IDX_DIM), row(LANES), row(d), row(d),
        ],
        out_shape=[
            jax.ShapeDtypeStruct((N_HEADS, HEAD_DIM, s), BF16),
            jax.ShapeDtypeStruct((N_HEADS, s, LANES), BF16),
            jax.ShapeDtypeStruct((N_HEADS, LANES, s), BF16),
            jax.ShapeDtypeStruct((s // SEL_TQ, IDX_HEADS // 2, 4 * IDX_DIM, 2 * SEL_TQ), BF16),
            jax.ShapeDtypeStruct((s, 4 * IDX_DIM), BF16),
            jax.ShapeDtypeStruct((s, LANES), F32),
            jax.ShapeDtypeStruct((s, d), F32),
            jax.ShapeDtypeStruct((s, d), F32),
        ],
        scratch_shapes=[pltpu.VMEM((tm + CONV_HALO, cw), F32), pltpu.VMEM((SUBLANES - 1, span, cw), F32),
                        pltpu.VMEM((tm, cw), F32)],
        compiler_params=pltpu.CompilerParams(dimension_semantics=("arbitrary",), vmem_limit_bytes=VMEM_LIMIT),
        name="mix_in",
    )(h, g2, w_conv, w_qkv, w_idx_hi, w_gate, qg, kg, head_blocks, *conv_params)


def _conv_kernel(z_ref, halo_ref, dw_ref, db_ref, lg_ref, lb_ref, wo_ref, gc_ref, o_ref, zp_ref, zs_ref, acc_ref):
    tm = z_ref.shape[0]
    first = pl.program_id(0) == 0
    halo = halo_ref[...]
    zp_ref[0:CONV_HALO, :] = jnp.where(first, jnp.zeros_like(halo), halo)
    zp_ref[CONV_HALO:, :] = z_ref[...]
    off = CONV_HALO - (CONV_WIDTH - 1)
    span = zs_ref.shape[1]
    for b in range(1, SUBLANES):
        zs_ref[b - 1] = zp_ref[b:b + span, :]

    def chunk(c, carry):
        r0 = pl.multiple_of(c * CONV_CHUNK, CONV_CHUNK)
        acc = jnp.zeros((CONV_CHUNK, z_ref.shape[1]), F32) + db_ref[...]
        for j in range(CONV_WIDTH):
            a, b = divmod(off + j, SUBLANES)
            src = zp_ref if b == 0 else zs_ref.at[b - 1]
            acc = acc + dw_ref[j:j + 1, :] * src[pl.ds(r0 + a * SUBLANES, CONV_CHUNK), :]
        acc_ref[pl.ds(r0, CONV_CHUNK), :] = acc
        return carry

    lax.fori_loop(0, tm // CONV_CHUNK, chunk, 0)
    acc = acc_ref[...]
    mu = jnp.mean(acc, axis=-1, keepdims=True)
    xc = acc - mu
    y = xc * lax.rsqrt(jnp.mean(xc * xc, axis=-1, keepdims=True) + EPS)
    y = y * lg_ref[...] + lb_ref[...]
    y = (y * _sigmoid(y)).astype(BF16)
    o_ref[...] = gc_ref[...] * _dot(y, wo_ref[...])


def _conv(z, dw_w, dw_b, ln_g, ln_b, w_out, gate_c, tm=512):
    s, c = z.shape
    d = w_out.shape[1]
    assert tm % CONV_HALO == 0 and tm % CONV_CHUNK == 0
    r = tm // CONV_HALO
    span = tm + CONV_HALO - SUBLANES
    full = lambda a: pl.BlockSpec(a.shape, lambda i: (0,) * a.ndim)
    vecs = [dw_b.reshape(1, c), ln_g.reshape(1, c), ln_b.reshape(1, c)]
    return pl.pallas_call(
        _conv_kernel,
        grid=(s // tm,),
        in_specs=[
            pl.BlockSpec((tm, c), lambda i: (i, 0)),
            pl.BlockSpec((CONV_HALO, c), lambda i: (jnp.maximum(i * r - 1, 0), 0)),
            full(dw_w), full(vecs[0]), full(vecs[1]), full(vecs[2]), full(w_out),
            pl.BlockSpec((tm, d), lambda i: (i, 0)),
        ],
        out_specs=pl.BlockSpec((tm, d), lambda i: (i, 0)),
        out_shape=jax.ShapeDtypeStruct((s, d), F32),
        scratch_shapes=[pltpu.VMEM((tm + CONV_HALO, c), F32), pltpu.VMEM((SUBLANES - 1, span, c), F32),
                        pltpu.VMEM((tm, c), F32)],
        compiler_params=pltpu.CompilerParams(dimension_semantics=("arbitrary",), vmem_limit_bytes=VMEM_LIMIT),
        name="conv",
    )(z, z, dw_w, *vecs, w_out, gate_c)


def _sortable_key(x):
    bits = pltpu.bitcast(x + 0.0, I32)
    return bits ^ ((bits >> 31) & 0x7FFFFFFF)


def _key_to_float(key):
    return pltpu.bitcast(key ^ ((key >> 31) & 0x7FFFFFFF), F32)


def _sort_network(n):
    pairs = []

    def merge(lo, hi, r):
        step = r * 2
        if step < hi - lo:
            merge(lo, hi, step)
            merge(lo + r, hi, step)
            pairs.extend((i, i + r) for i in range(lo + r, hi - r, step))
        else:
            pairs.append((lo, lo + r))

    def sort(lo, hi):
        if hi > lo:
            mid = lo + (hi - lo) // 2
            sort(lo, mid)
            sort(mid + 1, hi)
            merge(lo, hi, 1)

    sort(0, n - 1)
    return pairs


def _compare_exchange(a, i, j):
    a[i], a[j] = jnp.maximum(a[i], a[j]), jnp.minimum(a[i], a[j])


def _merge_top(top, batch):
    n = len(top)
    out = [jnp.maximum(top[i], batch[n - 1 - i]) for i in range(n)]
    d = n // 2
    while d >= 1:
        for i in range(n):
            if i & d == 0:
                _compare_exchange(out, i, i + d)
        d //= 2
    return out


def _select_kernel(qi_ref, w_ref, ki_ref, mask_ref, sc_ref, cand_ref, ckey_ref, st_ref, *, k_top):
    qb = pl.program_id(0)
    n_rows = mask_ref.shape[1]
    n_tiles = n_rows // SEL_TK
    q0 = qb * SEL_TQ
    nkt = (q0 + SEL_TQ - 1) // SEL_TK + 1
    vt = SEL_TK // SUBLANES
    vshape = (SUBLANES, LANES)
    qpos = q0 + lax.broadcasted_iota(I32, vshape, 1)
    sub = lax.broadcasted_iota(I32, vshape, 0)
    w = w_ref[...] * ((IDX_HEADS ** -0.5) * (IDX_DIM ** -0.5))

    qpos_t = q0 + lax.broadcasted_iota(I32, (SEL_TK, LANES), 1)
    krow_t = lax.broadcasted_iota(I32, (SEL_TK, LANES), 0)
    zero_bits_t = (MIN_NORMAL_BITS + n_rows) - krow_t

    def score_tile(kt):
        k0 = pl.multiple_of(kt * SEL_TK, SEL_TK)
        ki = ki_ref[pl.ds(k0, SEL_TK), :]
        sc = None
        for pair in range(IDX_HEADS // 2):
            r = _dot(ki, qi_ref[0, pair])
            for j in range(2):
                h = 2 * pair + j
                term = w[h:h + 1, :] * jnp.maximum(r[:, j * LANES:(j + 1) * LANES], 0.0)
                sc = term if sc is None else sc + term
        sc_ref[pl.ds(k0, SEL_TK), :] = jnp.where(sc == 0.0, pltpu.bitcast(zero_bits_t - k0, F32), sc)

    def vreg(ref, row):
        return ref[pl.ds(pl.multiple_of(row, SUBLANES), SUBLANES), :]

    def col_sum(parts):
        tot = parts[0]
        for part in parts[1:]:
            tot = tot + part
        return jnp.broadcast_to(jnp.sum(tot, axis=0, keepdims=True), vshape)

    n_cand = SEL_GROUPS * SEL_DEPTH
    network = _sort_network(SEL_DEPTH)
    cand_ref[...] = jnp.full(cand_ref.shape, -jnp.inf, F32)

    def lists_tile(kt):
        for g in range(SEL_GROUPS):
            batch = [vreg(sc_ref, kt * SEL_TK + (g * SEL_DEPTH + j) * SUBLANES) for j in range(SEL_DEPTH)]
            for i, j in network:
                _compare_exchange(batch, i, j)
            rows = [slice((g * SEL_DEPTH + i) * SUBLANES, (g * SEL_DEPTH + i + 1) * SUBLANES)
                    for i in range(SEL_DEPTH)]
            top = _merge_top([cand_ref[r, :] for r in rows], batch)
            for r, t in zip(rows, top):
                cand_ref[r, :] = t

    sc_ref[pl.ds(n_rows, SEL_TK), :] = jnp.full((SEL_TK, LANES), -jnp.inf, F32)

    def score_and_lists(kt, carry):
        lists_tile(jnp.where(kt == 0, n_tiles, kt - 1))
        score_tile(kt)
        return carry

    lax.fori_loop(0, nkt, score_and_lists, 0)
    k_last = pl.multiple_of((nkt - 1) * SEL_TK, SEL_TK)
    sc_ref[pl.ds(k_last, SEL_TK), :] = jnp.where(
        k_last + krow_t <= qpos_t, sc_ref[pl.ds(k_last, SEL_TK), :], -jnp.inf)
    lists_tile(nkt - 1)
    ckey_ref[...] = _sortable_key(cand_ref[...])

    n_acc = 4

    def cand_bit(i, t):
        c = t ^ lax.shift_left(jnp.int32(1), 31 - i)
        acc = [jnp.zeros(vshape, I32) for _ in range(n_acc)]
        for v in range(n_cand):
            x = ckey_ref[v * SUBLANES:(v + 1) * SUBLANES, :]
            acc[v % n_acc] = acc[v % n_acc] + jnp.where(x >= c, 1, 0)
        return jnp.where(col_sum(acc) >= k_top, c, t)

    st_ref[0] = lax.fori_loop(0, 32, cand_bit, jnp.full(vshape, INT_MIN, I32))

    def count_scores(preds):
        def tile(kt, acc):
            acc = [list(a) for a in acc]
            for v in range(vt):
                x = vreg(sc_ref, kt * SEL_TK + v * SUBLANES)
                kpos = kt * SEL_TK + v * SUBLANES + sub
                for p, pred in enumerate(preds):
                    acc[p][v % n_acc] = acc[p][v % n_acc] + jnp.where(pred(x, kpos), 1, 0)
            return tuple(tuple(a) for a in acc)

        zero = tuple(tuple(jnp.zeros(vshape, I32) for _ in range(n_acc)) for _ in preds)
        return [col_sum(list(a)) for a in lax.fori_loop(0, nkt, tile, zero)]

    def count_around_threshold():
        tf = _key_to_float(st_ref[0])
        gt, ge = count_scores([lambda x, kpos: x > tf, lambda x, kpos: x >= tf])
        st_ref[1] = gt
        st_ref[2] = ge

    tf_fast = jnp.broadcast_to(_key_to_float(st_ref[0])[0:1, :], (SEL_TK, LANES))

    def fast_tile(kt, acc):
        k0 = pl.multiple_of(kt * SEL_TK, SEL_TK)
        m = jnp.where(sc_ref[pl.ds(k0, SEL_TK), :] >= tf_fast, 1.0, 0.0)
        mask_ref[0, pl.ds(k0, SEL_TK), :] = m.astype(BF16)
        acc = list(acc)
        for v in range(vt):
            acc[v % n_acc] = acc[v % n_acc] + m[v * SUBLANES:(v + 1) * SUBLANES, :]
        return tuple(acc)

    size = col_sum(list(lax.fori_loop(0, nkt, fast_tile, tuple(jnp.zeros(vshape, F32) for _ in range(n_acc)))))

    @pl.when(jnp.max(jnp.abs(size - k_top)) > 0.0)
    def _():
        count_around_threshold()

        @pl.when(jnp.max(st_ref[1]) >= k_top)
        def _():
            def full_bit(i, t):
                c = t ^ lax.shift_left(jnp.int32(1), 31 - i)
                cnt, = count_scores([lambda x, kpos: _sortable_key(x) >= c])
                return jnp.where(cnt >= k_top, c, t)

            st_ref[0] = lax.fori_loop(0, 32, full_bit, jnp.full(vshape, INT_MIN, I32))
            count_around_threshold()

        st_ref[3] = jnp.full(vshape, n_rows, I32)

        @pl.when(jnp.max(st_ref[2]) > k_top)
        def _():
            tf = _key_to_float(st_ref[0])
            need = k_top - st_ref[1]
            n_bits = n_rows.bit_length() - 1

            def pos_bit(i, p):
                c = p + lax.shift_left(jnp.int32(1), n_bits - 1 - i)
                f, = count_scores([lambda x, kpos: (x == tf) & (kpos < c)])
                return jnp.where(f < need, c, p)

            st_ref[3] = lax.fori_loop(0, n_bits, pos_bit, jnp.zeros(vshape, I32))

        tf_t = jnp.broadcast_to(_key_to_float(st_ref[0])[0:1, :], (SEL_TK, LANES))
        lim_t = jnp.broadcast_to(st_ref[3][0:1, :], (SEL_TK, LANES))

        def write_tile(kt, carry):
            k0 = pl.multiple_of(kt * SEL_TK, SEL_TK)
            x = sc_ref[pl.ds(k0, SEL_TK), :]
            kpos = k0 + krow_t
            sel = ((x > tf_t) | ((x == tf_t) & (kpos <= lim_t))) & (kpos <= qpos_t)
            mask_ref[0, pl.ds(k0, SEL_TK), :] = jnp.where(sel, 1.0, 0.0).astype(BF16)
            return carry

        lax.fori_loop(0, nkt, write_tile, 0)

    def fill_tile(kt, carry):
        mask_ref[0, pl.ds(pl.multiple_of(kt * SEL_TK, SEL_TK), SEL_TK), :] = jnp.zeros((SEL_TK, LANES), BF16)
        return carry

    lax.fori_loop(nkt, n_tiles, fill_tile, 0)


def _select(qi_pairs, w_t, ki_ext, k_top):
    nqb, npair, kw, _ = qi_pairs.shape
    s = ki_ext.shape[0]
    n_cand_rows = SEL_GROUPS * SEL_DEPTH * SUBLANES
    assert s % SEL_TK == 0 and SEL_TK == n_cand_rows and n_cand_rows >= k_top
    return pl.pallas_call(
        functools.partial(_select_kernel, k_top=k_top),
        grid=(nqb,),
        in_specs=[
            pl.BlockSpec((1, npair, kw, 2 * SEL_TQ), lambda i: (i, 0, 0, 0)),
            pl.BlockSpec((w_t.shape[0], SEL_TQ), lambda i: (0, i)),
            pl.BlockSpec((s, kw), lambda i: (0, 0)),
        ],
        out_specs=pl.BlockSpec((1, s, SEL_TQ), lambda i: (i, 0, 0)),
        out_shape=jax.ShapeDtypeStruct((nqb, s, SEL_TQ), BF16),
        scratch_shapes=[
            pltpu.VMEM((s + SEL_TK, SEL_TQ), F32),
            pltpu.VMEM((n_cand_rows, SEL_TQ), F32),
            pltpu.VMEM((n_cand_rows, SEL_TQ), I32),
            pltpu.VMEM((4, SUBLANES, LANES), I32),
        ],
        compiler_params=pltpu.CompilerParams(dimension_semantics=("arbitrary",), vmem_limit_bytes=VMEM_LIMIT),
        name="select",
    )(qi_pairs, w_t, ki_ext)


def _causal_pairs(s, tq, tk):
    pairs = [(qb, kt) for qb in range(s // tq) for kt in range((qb * tq + tq - 1) // tk + 1)]
    return jnp.asarray([p[0] for p in pairs], I32), jnp.asarray([p[1] for p in pairs], I32)


def _attn_kernel(qb_tab, kt_tab, qt_ref, kmax_ref, k_ref, vt_ref, mask_ref, bias_ref, o_ref, l_ref, qx_ref, acc_ref):
    step = pl.program_id(0)
    qb = qb_tab[step]
    kt = kt_tab[step]
    q0 = qb * ATT_TQ
    k0 = kt * ATT_TK
    nq = ATT_TQ // LANES
    nk = ATT_TK // LANES

    @pl.when(kt == 0)
    def _():
        row = lax.broadcasted_iota(I32, (HEAD_DIM, ATT_TQ), 0)
        for h in range(N_HEADS):
            q = qt_ref[h]
            qf = q.astype(F32)
            bound = jnp.sqrt(jnp.sum(qf * qf, axis=0, keepdims=True)) * kmax_ref[h] * BOUND_MARGIN
            qx_ref[h, 0:HEAD_DIM, :] = q
            qx_ref[h, HEAD_DIM:, :] = jnp.where(row == 0, -bound, 0.0).astype(BF16)
        acc_ref[...] = jnp.zeros(acc_ref.shape, F32)

    mask = jnp.concatenate([mask_ref[a] for a in range(nq)], axis=1)

    def heads(with_bias):
        for h in range(N_HEADS):
            s = _dot(k_ref[h], qx_ref[h])
            if with_bias:
                rows = []
                for c in range(nk):
                    tab = [jnp.clip((q0 + a * LANES - k0 - c * LANES) // LANES + 1, 0, 3) for a in range(nq)]
                    rows.append(jnp.concatenate([bias_ref[tab[a], h] for a in range(nq)], axis=1))
                s = s + jnp.concatenate(rows, axis=0)
            p = jnp.exp2(s).astype(BF16) * mask
            acc_ref[h] += _dot(vt_ref[h], p)

    near = k0 + ATT_TK + 2 * LANES > q0

    @pl.when(near)
    def _():
        heads(True)

    @pl.when(jnp.logical_not(near))
    def _():
        heads(False)

    @pl.when(kt == (q0 + ATT_TQ - 1) // ATT_TK)
    def _():
        for h in range(N_HEADS):
            acc = acc_ref[h]
            den = acc[HEAD_DIM:HEAD_DIM + 1, :]
            o_ref[h] = acc[:HEAD_DIM, :] / den
            l_ref[h] = den


def _attn(q_t, kmax_b, k_ext, v_t, mask, bias_tab):
    nh, hd, s = q_t.shape
    qb_tab, kt_tab = _causal_pairs(s, ATT_TQ, ATT_TK)
    grid_spec = pltpu.PrefetchScalarGridSpec(
        num_scalar_prefetch=2,
        grid=(qb_tab.shape[0],),
        in_specs=[
            pl.BlockSpec((nh, hd, ATT_TQ), lambda i, qb, kt: (0, 0, qb[i])),
            pl.BlockSpec((nh, 1, ATT_TQ), lambda i, qb, kt: (0, 0, 0)),
            pl.BlockSpec((nh, ATT_TK, LANES), lambda i, qb, kt: (0, kt[i], 0)),
            pl.BlockSpec((nh, LANES, ATT_TK), lambda i, qb, kt: (0, 0, kt[i])),
            pl.BlockSpec((ATT_TQ // LANES, ATT_TK, LANES), lambda i, qb, kt: (qb[i], kt[i], 0)),
            pl.BlockSpec(bias_tab.shape, lambda i, qb, kt: (0, 0, 0, 0)),
        ],
        out_specs=[
            pl.BlockSpec((nh, hd, ATT_TQ), lambda i, qb, kt: (0, 0, qb[i])),
            pl.BlockSpec((nh, 1, ATT_TQ), lambda i, qb, kt: (0, 0, qb[i])),
        ],
        scratch_shapes=[pltpu.VMEM((nh, LANES, ATT_TQ), BF16), pltpu.VMEM((nh, LANES, ATT_TQ), F32)],
    )
    return pl.pallas_call(
        _attn_kernel,
        grid_spec=grid_spec,
        out_shape=[jax.ShapeDtypeStruct((nh, hd, s), F32), jax.ShapeDtypeStruct((nh, 1, s), F32)],
        compiler_params=pltpu.CompilerParams(dimension_semantics=("arbitrary",), vmem_limit_bytes=VMEM_LIMIT),
        name="attn",
    )(qb_tab, kt_tab, q_t, kmax_b, k_ext, v_t, mask, bias_tab)


def _attn_safe_kernel(qb_tab, kt_tab, q_ref, k_ref, v_ref, mask_ref, bias_ref, o_ref, m_ref, acc_ref):
    step = pl.program_id(0)
    qb = qb_tab[step]
    kt = kt_tab[step]
    nq = SAFE_TQ // LANES
    nsub = ATT_TK // LANES

    @pl.when(kt == 0)
    def _():
        m_ref[...] = jnp.full(m_ref.shape, NEG, F32)
        acc_ref[...] = jnp.zeros(acc_ref.shape, F32)

    for a in range(nq):
        rows = slice(a * LANES, (a + 1) * LANES)
        q0 = qb * SAFE_TQ + a * LANES
        maskf = (mask_ref[a].astype(F32) - 1.0) * (-NEG)
        tab_idx = [jnp.clip((q0 - (kt * ATT_TK + c * LANES)) // LANES + 1, 0, 3) for c in range(nsub)]
        for h in range(N_HEADS):
            s = _dot_t(q_ref[h, rows, :], k_ref[h]) + maskf
            s = s + jnp.concatenate([bias_ref[tab_idx[c], h] for c in range(nsub)], axis=1)
            m_old = m_ref[h, rows, :]
            m_new = jnp.maximum(m_old, jnp.max(s, axis=1, keepdims=True))
            p = jnp.exp2(s - m_new[:, 0:1])
            alpha = jnp.exp2(m_old - m_new)
            acc_ref[h, rows, :] = alpha * acc_ref[h, rows, :] + _dot(p.astype(BF16), v_ref[h])
            m_ref[h, rows, :] = m_new

    @pl.when(kt == (qb * SAFE_TQ + SAFE_TQ - 1) // ATT_TK)
    def _():
        for h in range(N_HEADS):
            acc = acc_ref[h]
            o_ref[h] = acc[:, :HEAD_DIM] / acc[:, HEAD_DIM:HEAD_DIM + 1]


def _attn_safe(q, k, v_ext, mask_qk, bias_tab):
    nh, s, hd = q.shape
    qb_tab, kt_tab = _causal_pairs(s, SAFE_TQ, ATT_TK)
    grid_spec = pltpu.PrefetchScalarGridSpec(
        num_scalar_prefetch=2,
        grid=(qb_tab.shape[0],),
        in_specs=[
            pl.BlockSpec((nh, SAFE_TQ, hd), lambda i, qb, kt: (0, qb[i], 0)),
            pl.BlockSpec((nh, ATT_TK, hd), lambda i, qb, kt: (0, kt[i], 0)),
            pl.BlockSpec((nh, ATT_TK, LANES), lambda i, qb, kt: (0, kt[i], 0)),
            pl.BlockSpec((SAFE_TQ // LANES, LANES, ATT_TK), lambda i, qb, kt: (qb[i], 0, kt[i])),
            pl.BlockSpec(bias_tab.shape, lambda i, qb, kt: (0, 0, 0, 0)),
        ],
        out_specs=pl.BlockSpec((nh, SAFE_TQ, hd), lambda i, qb, kt: (0, qb[i], 0)),
        scratch_shapes=[pltpu.VMEM((nh, SAFE_TQ, LANES), F32), pltpu.VMEM((nh, SAFE_TQ, LANES), F32)],
    )
    return pl.pallas_call(
        _attn_safe_kernel,
        grid_spec=grid_spec,
        out_shape=jax.ShapeDtypeStruct((nh, s, hd), F32),
        compiler_params=pltpu.CompilerParams(dimension_semantics=("arbitrary",), vmem_limit_bytes=VMEM_LIMIT),
        name="attn_safe",
    )(qb_tab, kt_tab, q, k, v_ext, mask_qk, bias_tab)


def _tail_kernel(h_ref, cg_ref, ga_ref, at_ref, p_ref, wao_ref, wmo_ref, fg_ref, wi_ref, wo_ref, pg_ref, wpg_ref,
                 wpp_ref, o_ref):
    attn = at_ref[...].T.astype(BF16)
    merged = cg_ref[...] + ga_ref[...] * _dot(attn, wao_ref[...])
    h = h_ref[...] + _dot(merged.astype(BF16), wmo_ref[...])
    h = _half_step_ffn(h, fg_ref[...], wi_ref, wo_ref)
    gate = _sigmoid(_dot(_rms(h, pg_ref[...]).astype(BF16), wpg_ref[...]))
    o_ref[...] = h + gate * _dot(p_ref[...].astype(BF16), wpp_ref[...])


def _tail(h, conv_g, gate_a, attn_t, p, w_ao, w_mo, ffn_g, w_in, w_out, ple_g, w_pg, w_pp, tm=512):
    s, d = h.shape
    row = lambda n: pl.BlockSpec((tm, n), lambda i: (i, 0))
    fg2 = ffn_g.reshape(1, d)
    pg2 = ple_g.reshape(1, d)
    params = (w_ao, w_mo, fg2, w_in, w_out, pg2, w_pg, w_pp)
    return pl.pallas_call(
        _tail_kernel,
        grid=(s // tm,),
        in_specs=[row(d), row(d), row(d), pl.BlockSpec((attn_t.shape[0], tm), lambda i: (0, i)), row(p.shape[1])]
        + [_resident(a) for a in params],
        out_specs=row(d),
        out_shape=jax.ShapeDtypeStruct((s, d), F32),
        compiler_params=pltpu.CompilerParams(dimension_semantics=("arbitrary",), vmem_limit_bytes=VMEM_LIMIT),
        name="tail",
    )(h, conv_g, gate_a, attn_t, p, *params)


def _t5_bucket_table(n_dist):
    n = np.arange(n_dist)
    max_exact = NUM_BUCKETS // 2
    nf = np.maximum(n, 1).astype(np.float32)
    large = max_exact + (np.log(nf / max_exact) / math.log(MAX_DISTANCE / max_exact)
                         * (NUM_BUCKETS - max_exact)).astype(np.int32)
    large = np.minimum(large, NUM_BUCKETS - 1)
    return np.where(n < max_exact, n, large)


def _bias_tables(rel_bias):
    assert MAX_DISTANCE <= LANES
    n = LANES
    rel = (rel_bias.astype(F32) - rel_bias[NUM_BUCKETS - 1].astype(F32)[None, :]) * LOG2E
    by_dist = rel[_t5_bucket_table(2 * n)].T

    def toeplitz(a):
        skew = jnp.tile(a, (1, n))[:, :n * (2 * n - 1)].reshape(a.shape[0], n, 2 * n - 1)
        return skew[:, :, :n]

    d0 = toeplitz(jnp.concatenate([by_dist[:, :n], jnp.broadcast_to(by_dist[:, :1], by_dist[:, :n].shape)], axis=1))
    d1 = toeplitz(jnp.concatenate([by_dist[:, n:], by_dist[:, :n]], axis=1))
    zero = jnp.zeros_like(d0)
    return jnp.swapaxes(jnp.stack([zero, d0, d1, zero]), 2, 3)


def kernel(x, p, ffn1_norm, ffn1_w_in, ffn1_w_out, mix_norm, mix_w_in, conv_dw_w, conv_dw_b, conv_ln_g,
           conv_ln_b, conv_w_out, q_norm, k_norm, attn_w_out, mix_w_out, ffn2_norm, ffn2_w_in, ffn2_w_out,
           ple_norm, ple_w_gate, ple_w_proj, rel_bias):
    b, s, d = x.shape
    depth = ffn1_norm.shape[0]
    cw = conv_dw_w.shape[2]
    aw = N_HEADS * HEAD_DIM
    nqi = IDX_HEADS * IDX_DIM
    iw = nqi + IDX_DIM + IDX_HEADS
    iw_pad = -(-iw // LANES) * LANES
    k_top = min(TOPK_MAX, s // 4)
    assert b == 1 and s % ATT_TK == 0 and mix_w_in.shape[2] == 2 * cw + 3 * aw + iw + 2 * d

    head_blocks = jnp.asarray(np.kron(np.eye(N_HEADS), np.ones((HEAD_DIM, HEAD_DIM))), BF16)
    bias_qk = _bias_tables(rel_bias)
    bias_kq = jnp.swapaxes(bias_qk, 2, 3)

    h = x[0]
    for i in range(depth):
        h = _ffn(h, ffn1_norm[i], ffn1_w_in[i].astype(BF16), ffn1_w_out[i].astype(BF16))

        w = mix_w_in[i]
        o0 = 2 * cw
        o1 = o0 + 3 * aw
        o2 = o1 + iw
        w_idx = jnp.pad(w[:, o1:o2], ((0, 0), (0, iw_pad - iw)))
        w_idx_hi = w_idx.astype(BF16)
        qg = jnp.tile(q_norm[i], N_HEADS).reshape(1, aw)
        kg = jnp.tile(k_norm[i], N_HEADS).reshape(1, aw)
        q_t, k_ext, v_t, qi_pairs, ki_ext, idx_w, conv_g, gate_a = _mix_in(
            h, mix_norm[i], w[:, :o0].astype(BF16), w[:, o0:o1].astype(BF16), w_idx_hi,
            w[:, o2:].astype(BF16), qg, kg, head_blocks,
            conv_dw_w[i], conv_dw_b[i], conv_ln_g[i], conv_ln_b[i], conv_w_out[i].astype(BF16))

        w_t = idx_w[:, IDX_DIM:IDX_DIM + IDX_HEADS].T
        mask = _select(qi_pairs, w_t, ki_ext, k_top)

        kmax = math.sqrt(HEAD_DIM) * jnp.max(jnp.abs(k_norm[i]))
        kmax_b = jnp.full((N_HEADS, 1, ATT_TQ), kmax, F32)
        attn_t, den = _attn(q_t, kmax_b, k_ext, v_t, mask, bias_kq)

        def safe_attn():
            out = _attn_safe(jnp.swapaxes(q_t, 1, 2), k_ext[:, :, :HEAD_DIM], jnp.swapaxes(v_t, 1, 2),
                             jnp.swapaxes(mask, 1, 2), bias_qk)
            return jnp.swapaxes(out, 1, 2)

        underflow = jnp.logical_not(jnp.min(den) > L_MIN)
        attn_t = lax.cond(underflow, safe_attn, lambda: attn_t).reshape(aw, s)

        h = _tail(h, conv_g, gate_a, attn_t, p[i, 0], attn_w_out[i].astype(BF16), mix_w_out[i].astype(BF16),
                  ffn2_norm[i], ffn2_w_in[i].astype(BF16), ffn2_w_out[i].astype(BF16),
                  ple_norm[i], ple_w_gate[i].astype(BF16), ple_w_proj[i].astype(BF16))
    return h[None]
```
